```python
import math
import jax, jax.numpy as jnp
from jax import lax
import numpy as np

D_MODEL = 1024
BATCH = 16
SEQ = 256
DEPTH = 2
DEC_BATCH = 4
DEC_SEQ = 1024
PAST_LEN = 512

GRID_W = 64
MIX_W = D_MODEL // 2
N_RET_HEADS = 4
RET_DK = MIX_W // N_RET_HEADS
RET_DV = MIX_W // N_RET_HEADS
RET_CHUNK = 128
SSM_CH = MIX_W
SSM_GROUP = 16
SSM_GROUPS = SSM_CH // SSM_GROUP
SSM_STATE = 64
NA_HEADS = 8
NA_HEAD_DIM = MIX_W // NA_HEADS
NA_KR = 8
NA_KW = 16
N_BRANCH = 3
D_FF = ((8 * D_MODEL // 3 + 127) // 128) * 128
ROPE_BASE = 10000.0
LN_EPS = 1e-5
NEG_INF = -1e30
DEEPNORM_ALPHA = (2 * DEPTH) ** 0.25
DEEPNORM_BETA = (8 * DEPTH) ** -0.25
IN_SPLITS = (MIX_W, MIX_W, MIX_W, MIX_W, SSM_CH, MIX_W, MIX_W, MIX_W, N_BRANCH * D_MODEL)
IN_COLS = sum(IN_SPLITS)

kernel_name = 'hybrid_diffusion_retention_s5_natten_step'


def _layer_norm(x, g, b):
    xf = x.astype(jnp.float32)
    mu = jnp.mean(xf, -1, keepdims=True)
    var = jnp.mean(jnp.square(xf - mu), -1, keepdims=True)
    return ((xf - mu) * lax.rsqrt(var + LN_EPS) * g.astype(jnp.float32) + b.astype(jnp.float32)).astype(x.dtype)


def _head_norm(o):
    mu = jnp.mean(o, -1, keepdims=True)
    var = jnp.mean(jnp.square(o - mu), -1, keepdims=True)
    return (o - mu) * lax.rsqrt(var + LN_EPS)


def _ada(cond, w_ada, b_ada):
    p = jax.nn.silu(cond) @ w_ada + b_ada
    return jnp.split(p[:, None, :], 6, axis=-1)


def _split_in(h, w_in):
    z = h @ w_in
    cuts = [int(o) for o in np.cumsum(IN_SPLITS)[:-1]]
    return jnp.split(z, cuts, axis=-1)


def _axial_rope(x):
    B, L, H, Dh = x.shape
    pos = jnp.arange(L)
    row = (pos // GRID_W).astype(jnp.float32)
    col = (pos % GRID_W).astype(jnp.float32)
    half = Dh // 2
    quarter = half // 2
    inv_freq = ROPE_BASE ** (-jnp.arange(quarter, dtype=jnp.float32) / quarter)

    def rot(xa, p):
        ang = p[:, None] * inv_freq[None, :]
        cos = jnp.cos(ang)[None, :, None, :]
        sin = jnp.sin(ang)[None, :, None, :]
        x1, x2 = xa[..., :quarter], xa[..., quarter:]
        return jnp.concatenate([x1 * cos - x2 * sin, x1 * sin + x2 * cos], -1)

    xf = x.astype(jnp.float32)
    return jnp.concatenate([rot(xf[..., :half], row), rot(xf[..., half:], col)], -1)


def _retention_scan(q, k, v, s0, log_gamma):
    B, L, H, DK = q.shape
    DV = v.shape[-1]
    C = RET_CHUNK
    n = L // C
    idx = jnp.arange(C, dtype=jnp.float32)
    diff = idx[:, None] - idx[None, :]
    inner_decay = jnp.where(diff >= 0, jnp.exp(log_gamma[:, None, None] * jnp.maximum(diff, 0.0)), 0.0)
    q_decay = jnp.exp(log_gamma[:, None] * (idx + 1.0)).T[None, :, :, None]
    k_decay = jnp.exp(log_gamma[:, None] * (C - 1.0 - idx)).T[None, :, :, None]
    chunk_decay = jnp.exp(log_gamma * C)[None, :, None, None]
    qc = q.reshape(B, n, C, H, DK).swapaxes(0, 1)
    kc = k.reshape(B, n, C, H, DK).swapaxes(0, 1)
    vc = v.reshape(B, n, C, H, DV).swapaxes(0, 1)

    def step(s, inp):
        qi, ki, vi = inp
        att = jnp.einsum('bqhd,bkhd->bhqk', qi, ki) * inner_decay[None]
        o = jnp.einsum('bhqk,bkhe->bqhe', att, vi)
        o = o + jnp.einsum('bqhd,bhde->bqhe', qi, s) * q_decay
        s = s * chunk_decay + jnp.einsum('bkhd,bkhe->bhde', ki * k_decay, vi)
        return s, o

    s, o = lax.scan(step, s0, (qc, kc, vc))
    return o.swapaxes(0, 1).reshape(B, L, H, DV), s


def _retention(q, k, v, g, s0, decay_logit):
    B, L = q.shape[:2]
    log_gamma = jax.nn.log_sigmoid(decay_logit.astype(jnp.float32))
    k = k * RET_DK ** -0.5
    s0 = s0.astype(jnp.float32)
    o_f, s_f = _retention_scan(q, k, v, s0[:, 0], log_gamma[0])
    o_b, s_b = _retention_scan(q[:, ::-1], k[:, ::-1], v[:, ::-1], s0[:, 1], log_gamma[1])
    o = _head_norm(o_f + o_b[:, ::-1]).reshape(B, L, MIX_W)
    out = o * jax.nn.silu(g.astype(jnp.float32))
    return out.astype(g.dtype), jnp.stack([s_f, s_b], axis=1)


def _complex_scan(bu, a_bar, h0):
    bu = bu.at[:, 0].add(a_bar * h0)
    a = jnp.broadcast_to(a_bar, bu.shape)

    def combine(e1, e2):
        a1, b1 = e1
        a2, b2 = e2
        return a1 * a2, a2 * b1 + b2

    _, xs = lax.associative_scan(combine, (a, bu), axis=1)
    return xs


def _s5(u, h0, a_re, a_im, log_dt, b_re, b_im, c_re, c_im, d_skip, w_glu):
    f32 = jnp.float32
    B, L, _ = u.shape
    uf = u.astype(f32)
    ug = uf.reshape(B, L, SSM_GROUPS, SSM_GROUP).astype(jnp.complex64)
    lam = lax.complex(jnp.minimum(a_re.astype(f32), -1e-4), a_im.astype(f32))
    dt = jnp.exp(log_dt.astype(f32))[..., None]
    a_bar = jnp.exp(lam * dt)
    b = lax.complex(b_re.astype(f32), b_im.astype(f32))
    b_bar = ((a_bar - 1.0) / lam)[..., None] * b[None]
    c = lax.complex(c_re.astype(f32), c_im.astype(f32))
    h0c = lax.complex(h0[..., 0].astype(f32), h0[..., 1].astype(f32))
    y = d_skip.astype(f32) * uf
    finals = []
    for di in range(2):
        ud = ug if di == 0 else ug[:, ::-1]
        bu = jnp.einsum('blgc,gpc->blgp', ud, b_bar[di])
        xs = _complex_scan(bu, a_bar[di], h0c[:, di])
        yd = jnp.einsum('blgp,gcp->blgc', xs, c[di]).real
        if di == 1:
            yd = yd[:, ::-1]
        y = y + yd.reshape(B, L, SSM_CH)
        finals.append(xs[:, -1])
    y = jax.nn.gelu(y)
    y = y * jax.nn.sigmoid(y @ w_glu.astype(f32))
    fin = jnp.stack(finals, axis=1)
    return y.astype(u.dtype), jnp.stack([fin.real, fin.imag], axis=-1)


def _context_attention(q, k, v):
    s = jnp.einsum('bqhd,bkhd->bhqk', q, k).astype(jnp.float32) * NA_HEAD_DIM ** -0.5
    p = jax.nn.softmax(s, axis=-1).astype(v.dtype)
    return jnp.einsum('bhqk,bkhd->bqhd', p, v)


def _neighbourhood_attention(q, k, v, k_ctx, v_ctx, rpb):
    B, L, H, Dh = q.shape
    rows = L // GRID_W
    kr = min(NA_KR, rows)
    ncb = GRID_W // NA_KW
    span = 2 * NA_KW
    r = jnp.arange(rows)
    key_rows = jnp.clip(r - kr // 2, 0, rows - kr)[:, None] + jnp.arange(kr)[None, :]
    qcol = jnp.arange(GRID_W).reshape(ncb, NA_KW)
    win_start = jnp.clip(qcol - NA_KW // 2, 0, GRID_W - NA_KW)
    blk_start = jnp.clip(jnp.arange(ncb) * NA_KW - NA_KW // 2, 0, GRID_W - span)
    key_cols = blk_start[:, None] + jnp.arange(span)[None, :]
    ridx = key_rows[:, None, :, None]
    cidx = key_cols[None, :, None, :]
    kb = k.reshape(B, rows, GRID_W, H, Dh)[:, ridx, cidx]
    vb = v.reshape(B, rows, GRID_W, H, Dh)[:, ridx, cidx]
    qb = q.reshape(B, rows, ncb, NA_KW, H, Dh)
    scale = Dh ** -0.5
    s_loc = jnp.einsum('brjqhd,brjkshd->bhrjqks', qb, kb).astype(jnp.float32) * scale
    kc = key_cols[:, None, :]
    valid = (kc >= win_start[:, :, None]) & (kc < win_start[:, :, None] + NA_KW)
    roff = key_rows - r[:, None] + NA_KR - 1
    coff = jnp.clip(kc - qcol[:, :, None] + NA_KW - 1, 0, 2 * NA_KW - 2)
    bias = rpb.astype(jnp.float32)[:, roff[:, None, None, :, None], coff[None, :, :, None, :]]
    s_loc = jnp.where(valid[None, None, None, :, :, None, :], s_loc + bias[None], NEG_INF)
    n_loc = kr * span
    s_loc = s_loc.reshape(B, H, rows, ncb, NA_KW, n_loc)
    s_ctx = jnp.einsum('brjqhd,bchd->bhrjqc', qb, k_ctx).astype(jnp.float32) * scale
    p = jax.nn.softmax(jnp.concatenate([s_loc, s_ctx], axis=-1), axis=-1).astype(v.dtype)
    p_loc = p[..., :n_loc].reshape(B, H, rows, ncb, NA_KW, kr, span)
    out = (jnp.einsum('bhrjqks,brjkshd->brjqhd', p_loc, vb)
           + jnp.einsum('bhrjqc,bchd->brjqhd', p[..., n_loc:], v_ctx))
    return out.reshape(B, L, H * Dh)


def _merge_branches(r_out, s_out, n_out, gates, w_branch, w_o):
    g = jax.nn.sigmoid(gates.astype(jnp.float32)).astype(r_out.dtype)
    ga, gb, gc = jnp.split(g, 3, axis=-1)
    merged = ga * (r_out @ w_branch[0]) + gb * (s_out @ w_branch[1]) + gc * (n_out @ w_branch[2])
    return merged @ w_o


def _conv_ffn(h, w_up, conv_w, conv_b, w_down):
    z = h @ w_up
    zp = jnp.pad(z, ((0, 0), (1, 1), (0, 0)))
    z = zp[:, :-2] * conv_w[0] + zp[:, 1:-1] * conv_w[1] + zp[:, 2:] * conv_w[2] + conv_b
    a, b = jnp.split(z, 2, axis=-1)
    return (jax.nn.gelu(a) * b) @ w_down


def _s5_call(su, h0, lw):
    return _s5(su, h0, lw['ssm_a_re'], lw['ssm_a_im'], lw['ssm_log_dt'], lw['ssm_b_re'], lw['ssm_b_im'],
               lw['ssm_c_re'], lw['ssm_c_im'], lw['ssm_d'], lw['ssm_w_glu'])


def _context_mixer(h, lw):
    B, L, _ = h.shape
    f32 = jnp.float32
    rq, rk, rv, rg, su, nq, nk, nv, gates = _split_in(h, lw['w_in'])
    rq = rq.reshape(B, L, N_RET_HEADS, RET_DK).astype(f32)
    rk = rk.reshape(B, L, N_RET_HEADS, RET_DK).astype(f32)
    rv = rv.reshape(B, L, N_RET_HEADS, RET_DV).astype(f32)
    zero_ret = jnp.zeros((B, 2, N_RET_HEADS, RET_DK, RET_DV), f32)
    r_out, ret_state = _retention(rq, rk, rv, rg, zero_ret, lw['ret_decay'])
    zero_ssm = jnp.zeros((B, 2, SSM_GROUPS, SSM_STATE, 2), f32)
    s_out, ssm_state = _s5_call(su, zero_ssm, lw)
    nq = nq.reshape(B, L, NA_HEADS, NA_HEAD_DIM)
    nk = nk.reshape(B, L, NA_HEADS, NA_HEAD_DIM)
    nv = nv.reshape(B, L, NA_HEADS, NA_HEAD_DIM)
    n_out = _context_attention(nq, nk, nv).reshape(B, L, MIX_W)
    m = _merge_branches(r_out, s_out, n_out, gates, lw['w_branch'], lw['w_o'])
    return m, (ret_state, ssm_state, nk, nv)


def _latent_mixer(h, lw, s_ret, s_ssm, k_ctx, v_ctx):
    B, L, _ = h.shape
    rq, rk, rv, rg, su, nq, nk, nv, gates = _split_in(h, lw['w_in'])
    rq = _axial_rope(rq.reshape(B, L, N_RET_HEADS, RET_DK))
    rk = _axial_rope(rk.reshape(B, L, N_RET_HEADS, RET_DK))
    rv = rv.reshape(B, L, N_RET_HEADS, RET_DV).astype(jnp.float32)
    r_out, _ = _retention(rq, rk, rv, rg, s_ret, lw['ret_decay'])
    s_out, _ = _s5_call(su, s_ssm, lw)
    nq = nq.reshape(B, L, NA_HEADS, NA_HEAD_DIM)
    nk = nk.reshape(B, L, NA_HEADS, NA_HEAD_DIM)
    nv = nv.reshape(B, L, NA_HEADS, NA_HEAD_DIM)
    n_out = _neighbourhood_attention(nq, nk, nv, k_ctx.astype(nq.dtype), v_ctx.astype(nv.dtype), lw['na_rpb'])
    m = _merge_branches(r_out, s_out, n_out, gates, lw['w_branch'], lw['w_o'])
    return m, None


def _trunk_layer(x, cond, lw, mix):
    sh1, sc1, g1, sh2, sc2, g2 = _ada(cond, lw['w_ada'], lw['b_ada'])
    m, extras = mix(x * (1.0 + sc1) + sh1)
    x = _layer_norm(DEEPNORM_ALPHA * x + g1 * m, lw['ln1_g'], lw['ln1_b'])
    f = _conv_ffn(x * (1.0 + sc2) + sh2, lw['w_up'], lw['conv_w'], lw['conv_b'], lw['w_down'])
    x = _layer_norm(DEEPNORM_ALPHA * x + g2 * f, lw['ln2_g'], lw['ln2_b'])
    return x, extras


def setup_inputs(seed: int = 0) -> dict:
    key = jax.random.key(seed)
    ks = jax.random.split(key, 32)
    f32 = jnp.float32

    def nrm(k, shape, s):
        return jax.random.normal(k, shape, f32) * s

    gamma0 = 1.0 - 2.0 ** (-5.0 - jnp.arange(N_RET_HEADS, dtype=f32))
    ret_logit = jnp.log(gamma0) - jnp.log1p(-gamma0)
    return {
        'x_prompt': nrm(ks[0], (BATCH, SEQ, D_MODEL), 1.0),
        'x_sample': nrm(ks[1], (DEC_BATCH, DEC_SEQ, D_MODEL), 1.0),
        'state_ret': nrm(ks[2], (DEC_BATCH, DEPTH, 2, N_RET_HEADS, RET_DK, RET_DV), 0.5),
        'state_ssm': nrm(ks[3], (DEC_BATCH, DEPTH, 2, SSM_GROUPS, SSM_STATE, 2), 0.5),
        'cache_na_k': nrm(ks[4], (DEC_BATCH, DEPTH, PAST_LEN, NA_HEADS, NA_HEAD_DIM), 1.0),
        'cache_na_v': nrm(ks[5], (DEC_BATCH, DEPTH, PAST_LEN, NA_HEADS, NA_HEAD_DIM), 1.0),
        'c': nrm(ks[6], (DEC_BATCH, D_MODEL), 1.0),
        'c_ctx': nrm(ks[7], (D_MODEL,), 1.0),
        'w_ada': nrm(ks[8], (DEPTH, D_MODEL, 6 * D_MODEL), 0.5 * D_MODEL ** -0.5),
        'b_ada': nrm(ks[9], (DEPTH, 6 * D_MODEL), 0.02),
        'w_in': nrm(ks[10], (DEPTH, D_MODEL, IN_COLS), D_MODEL ** -0.5),
        'ret_decay': ret_logit + nrm(ks[11], (DEPTH, 2, N_RET_HEADS), 0.05),
        'ssm_a_re': -0.5 + nrm(ks[12], (DEPTH, 2, SSM_GROUPS, SSM_STATE), 0.01),
        'ssm_a_im': jnp.pi * jnp.arange(SSM_STATE, dtype=f32) + nrm(ks[13], (DEPTH, 2, SSM_GROUPS, SSM_STATE), 0.01),
        'ssm_log_dt': jax.random.uniform(ks[14], (DEPTH, 2, SSM_GROUPS), f32, math.log(1e-3), math.log(1e-1)),
        'ssm_b_re': nrm(ks[15], (DEPTH, SSM_GROUPS, SSM_STATE, SSM_GROUP), (2 * SSM_GROUP) ** -0.5),
        'ssm_b_im': nrm(ks[16], (DEPTH, SSM_GROUPS, SSM_STATE, SSM_GROUP), (2 * SSM_GROUP) ** -0.5),
        'ssm_c_re': nrm(ks[17], (DEPTH, 2, SSM_GROUPS, SSM_GROUP, SSM_STATE), (2 * SSM_STATE) ** -0.5),
        'ssm_c_im': nrm(ks[18], (DEPTH, 2, SSM_GROUPS, SSM_GROUP, SSM_STATE), (2 * SSM_STATE) ** -0.5),
        'ssm_d': nrm(ks[19], (DEPTH, SSM_CH), 1.0),
        'ssm_w_glu': nrm(ks[20], (DEPTH, SSM_CH, SSM_CH), SSM_CH ** -0.5),
        'na_rpb': nrm(ks[21], (DEPTH, NA_HEADS, 2 * NA_KR - 1, 2 * NA_KW - 1), 0.1),
        'w_branch': nrm(ks[22], (DEPTH, N_BRANCH, MIX_W, D_MODEL), DEEPNORM_BETA * MIX_W ** -0.5),
        'w_o': nrm(ks[23], (DEPTH, D_MODEL, D_MODEL), DEEPNORM_BETA * D_MODEL ** -0.5),
        'ln1_g': 1.0 + nrm(ks[24], (DEPTH, D_MODEL), 0.02),
        'ln1_b': nrm(ks[25], (DEPTH, D_MODEL), 0.02),
        'w_up': nrm(ks[26], (DEPTH, D_MODEL, 2 * D_FF), D_MODEL ** -0.5),
        'conv_w': nrm(ks[27], (DEPTH, 3, 2 * D_FF), 3 ** -0.5),
        'conv_b': nrm(ks[28], (DEPTH, 2 * D_FF), 0.02),
        'w_down': nrm(ks[29], (DEPTH, D_FF, D_MODEL), DEEPNORM_BETA * D_FF ** -0.5),
        'ln2_g': 1.0 + nrm(ks[30], (DEPTH, D_MODEL), 0.02),
        'ln2_b': nrm(ks[31], (DEPTH, D_MODEL), 0.02),
    }


def reference(x_prompt, x_sample, state_ret, state_ssm, cache_na_k, cache_na_v, c, c_ctx,
              w_ada, b_ada, w_in, ret_decay, ssm_a_re, ssm_a_im, ssm_log_dt, ssm_b_re, ssm_b_im,
              ssm_c_re, ssm_c_im, ssm_d, ssm_w_glu, na_rpb, w_branch, w_o, ln1_g, ln1_b,
              w_up, conv_w, conv_b, w_down, ln2_g, ln2_b):
    cond_ctx = jnp.broadcast_to(c_ctx[None, :], (x_prompt.shape[0], D_MODEL))
    xp = x_prompt
    xs = x_sample
    ret_states, ssm_states, na_ks, na_vs = [], [], [], []
    for l in range(DEPTH):
        lw = dict(w_ada=w_ada[l], b_ada=b_ada[l], w_in=w_in[l], ret_decay=ret_decay[l],
                  ssm_a_re=ssm_a_re[l], ssm_a_im=ssm_a_im[l], ssm_log_dt=ssm_log_dt[l],
                  ssm_b_re=ssm_b_re[l], ssm_b_im=ssm_b_im[l], ssm_c_re=ssm_c_re[l], ssm_c_im=ssm_c_im[l],
                  ssm_d=ssm_d[l], ssm_w_glu=ssm_w_glu[l], na_rpb=na_rpb[l], w_branch=w_branch[l],
                  w_o=w_o[l], ln1_g=ln1_g[l], ln1_b=ln1_b[l], w_up=w_up[l], conv_w=conv_w[l],
                  conv_b=conv_b[l], w_down=w_down[l], ln2_g=ln2_g[l], ln2_b=ln2_b[l])
        xp, (s_ret, s_ssm, k_ctx, v_ctx) = _trunk_layer(xp, cond_ctx, lw, lambda h: _context_mixer(h, lw))
        ret_states.append(s_ret)
        ssm_states.append(s_ssm)
        na_ks.append(k_ctx)
        na_vs.append(v_ctx)
        xs, _ = _trunk_layer(xs, c, lw, lambda h: _latent_mixer(h, lw, state_ret[:, l], state_ssm[:, l],
                                                                cache_na_k[:, l], cache_na_v[:, l]))
    new_state_ret = jnp.stack(ret_states, axis=1).astype(x_prompt.dtype)
    new_state_ssm = jnp.stack(ssm_states, axis=1).astype(x_prompt.dtype)
    new_cache_na_k = jnp.stack(na_ks, axis=1).astype(x_prompt.dtype)
    new_cache_na_v = jnp.stack(na_vs, axis=1).astype(x_prompt.dtype)
    return (xp, xs, new_state_ret, new_state_ssm, new_cache_na_k, new_cache_na_v)
```

```python
import functools

import jax
import jax.numpy as jnp
from jax import lax
from jax.experimental import pallas as pl
from jax.experimental.pallas import tpu as pltpu

F32 = jnp.float32
BF16 = jnp.bfloat16

D_MODEL = 1024
DEPTH = 2
GRID_W = 64
MIX_W = D_MODEL // 2
N_RET_HEADS = 4
RET_DK = MIX_W // N_RET_HEADS
SSM_GROUP = 16
SSM_GROUPS = MIX_W // SSM_GROUP
SSM_STATE = 64
NA_HEADS = 8
NA_HEAD_DIM = MIX_W // NA_HEADS
NA_KR = 8
NA_KW = 16
D_FF = ((8 * D_MODEL // 3 + 127) // 128) * 128
ROPE_BASE = 10000.0
LN_EPS = 1e-5
NEG_INF = -1e30
DEEPNORM_ALPHA = (2 * DEPTH) ** 0.25
IN_COLS = 8 * MIX_W + 3 * D_MODEL

VMEM_LIMIT_BYTES = 56 * 1024 * 1024

TOKEN_TILE = 1024
MERGE_TILE = 256
COL_TILE = 512
SU_TILE = 4
FF_TILE = 256
RET_CHUNK = 256
S5_PAIRS = 4
S5_ROWS = 1024
N_PAD_ROWS = 8


def _params(*sem):
    return pltpu.CompilerParams(dimension_semantics=sem, vmem_limit_bytes=VMEM_LIMIT_BYTES)


def _dot(a, b):
    return jnp.dot(a, b, preferred_element_type=F32)


def _dot_nt(a, b):
    return lax.dot_general(a, b, (((1,), (1,)), ((), ())), preferred_element_type=F32)


def _layer_norm(x, g, b):
    mu = jnp.mean(x, -1, keepdims=True)
    xc = x - mu
    var = jnp.mean(xc * xc, -1, keepdims=True)
    return xc * lax.rsqrt(var + LN_EPS) * g + b


def _ada_body(c_ref, w_ref, b_ref, o_ref):
    c = c_ref[...]
    s = c * jax.nn.sigmoid(c)
    o_ref[...] = _dot(s.astype(BF16), w_ref[...].astype(BF16)) + b_ref[...]


def _ada(cond, w_ada, b_ada):
    tn = 1024
    return pl.pallas_call(
        _ada_body,
        grid=(DEPTH, 6 * D_MODEL // tn),
        in_specs=[pl.BlockSpec((N_PAD_ROWS, D_MODEL), lambda l, j: (0, 0)),
                  pl.BlockSpec((None, D_MODEL, tn), lambda l, j: (l, 0, j)),
                  pl.BlockSpec((None, 1, tn), lambda l, j: (l, 0, j))],
        out_specs=pl.BlockSpec((None, N_PAD_ROWS, tn), lambda l, j: (l, 0, j)),
        out_shape=jax.ShapeDtypeStruct((DEPTH, N_PAD_ROWS, 6 * D_MODEL), F32),
        compiler_params=_params("arbitrary", "arbitrary"),
    )(cond, w_ada, b_ada.reshape(DEPTH, 1, 6 * D_MODEL))


def _mod_row(p_ref, row, k):
    return p_ref[pl.ds(row, 1), k * D_MODEL:(k + 1) * D_MODEL]


def _inproj_body(x_ref, p_ref, w_ref, z_ref, u_ref, h_scr, *, L, row0, rstride, pad):
    i = pl.program_id(0)
    j = pl.program_id(1)
    nb = x_ref.shape[0] // L

    @pl.when(j == 0)
    def _():
        for s in range(nb):
            row = row0 + rstride * (i * nb + s)
            sh = _mod_row(p_ref, row, 0)
            sc = _mod_row(p_ref, row, 1)
            h_scr[s * L:(s + 1) * L, :] = (x_ref[s * L:(s + 1) * L, :] * (1.0 + sc) + sh).astype(BF16)

    acc = _dot(h_scr[...], w_ref[...].astype(BF16))
    z_ref[...] = acc.astype(BF16)

    @pl.when(j == SU_TILE)
    def _():
        for s in range(nb):
            u_ref[:, (s * pad) * MIX_W:(s * pad + 1) * MIX_W] = acc[s * L:(s + 1) * L, :]
            for e in range(1, pad):
                u_ref[:, (s * pad + e) * MIX_W:(s * pad + e + 1) * MIX_W] = jnp.zeros((L, MIX_W), F32)


def _inproj(x, p, w_in, *, L, row0, rstride, pad):
    T = x.shape[0]
    tm = TOKEN_TILE
    nb = tm // L
    slots = (T // L) * pad
    body = functools.partial(_inproj_body, L=L, row0=row0, rstride=rstride, pad=pad)
    return pl.pallas_call(
        body,
        grid=(T // tm, IN_COLS // COL_TILE),
        in_specs=[pl.BlockSpec((tm, D_MODEL), lambda i, j: (i, 0)),
                  pl.BlockSpec((N_PAD_ROWS, 6 * D_MODEL), lambda i, j: (0, 0)),
                  pl.BlockSpec((D_MODEL, COL_TILE), lambda i, j: (0, j))],
        out_specs=[pl.BlockSpec((tm, COL_TILE), lambda i, j: (i, j)),
                   pl.BlockSpec((L, nb * pad * MIX_W), lambda i, j: (0, i))],
        out_shape=[jax.ShapeDtypeStruct((T, IN_COLS), BF16),
                   jax.ShapeDtypeStruct((L, slots * MIX_W), F32)],
        scratch_shapes=[pltpu.VMEM((tm, D_MODEL), BF16)],
        compiler_params=_params("arbitrary", "arbitrary"),
    )(x, p, w_in)


def _rope(x, cos, s_up, s_dn):
    return x * cos + pltpu.roll(x, 96, 1) * s_up + pltpu.roll(x, 32, 1) * s_dn


def _ret_body(*refs, n, rope, has_s0, want_state):
    refs = list(refs)
    lg_ref, q_ref, k_ref, v_ref, g_ref = refs[:5]
    refs = refs[5:]
    if rope:
        cos_ref, sup_ref, sdn_ref = refs[:3]
        refs = refs[3:]
    if has_s0:
        s0_ref = refs[0]
        refs = refs[1:]
    o_ref = refs[0]
    refs = refs[1:]
    if want_state:
        st_ref = refs[0]
        refs = refs[1:]
    q_scr, k_scr, sb_scr = refs

    C = RET_CHUNK
    h = pl.program_id(1)
    lf = lg_ref[0, h]
    lb = lg_ref[1, h]

    q = q_ref[...].astype(F32)
    k = k_ref[...].astype(F32)
    if rope:
        q = _rope(q, cos_ref[...], sup_ref[...], sdn_ref[...])
        k = _rope(k, cos_ref[...], sup_ref[...], sdn_ref[...])
    q_scr[...] = q
    k_scr[...] = k * (RET_DK ** -0.5)

    tcol = lax.broadcasted_iota(jnp.int32, (C, 1), 0).astype(F32)
    ti = lax.broadcasted_iota(jnp.int32, (C, C), 0)
    si = lax.broadcasted_iota(jnp.int32, (C, C), 1)
    dlt = (ti - si).astype(F32)
    decay = (jnp.where(dlt >= 0, jnp.exp(lf * jnp.maximum(dlt, 0.0)), 0.0)
             + jnp.where(dlt <= 0, jnp.exp(lb * jnp.maximum(-dlt, 0.0)), 0.0))
    qd_f = jnp.exp(lf * (tcol + 1.0))
    qd_b = jnp.exp(lb * (C - tcol))
    kd_f = jnp.exp(lf * (C - 1.0 - tcol))
    kd_b = jnp.exp(lb * tcol)
    cd_f = jnp.exp(lf * jnp.full((1, RET_DK), float(C), F32))
    cd_b = jnp.exp(lb * jnp.full((1, RET_DK), float(C), F32))

    def kv_outer(kc, vc, kd):
        return _dot((kc * kd).T.astype(BF16), vc)

    s_b = s0_ref[1] if has_s0 else jnp.zeros((RET_DK, RET_DK), F32)
    for i in reversed(range(n)):
        sb_scr[i] = s_b
        if i > 0 or want_state:
            s_b = s_b * cd_b + kv_outer(k_scr[i * C:(i + 1) * C, :], v_ref[i * C:(i + 1) * C, :], kd_b)

    s_f = s0_ref[0] if has_s0 else jnp.zeros((RET_DK, RET_DK), F32)
    for i in range(n):
        sl = slice(i * C, (i + 1) * C)
        qc = q_scr[sl, :]
        kc = k_scr[sl, :]
        vc = v_ref[sl, :]
        att = _dot_nt(qc.astype(BF16), kc.astype(BF16)) * decay
        o = _dot(att.astype(BF16), vc)
        o = o + _dot((qc * qd_f).astype(BF16), s_f.astype(BF16))
        o = o + _dot((qc * qd_b).astype(BF16), sb_scr[i].astype(BF16))
        mu = jnp.mean(o, -1, keepdims=True)
        oc = o - mu
        var = jnp.mean(oc * oc, -1, keepdims=True)
        gc = g_ref[sl, :].astype(F32)
        o_ref[sl, :] = (oc * lax.rsqrt(var + LN_EPS) * (gc * jax.nn.sigmoid(gc))).astype(BF16)
        if i < n - 1 or want_state:
            s_f = s_f * cd_f + kv_outer(kc, vc, kd_f)

    if want_state:
        st_ref[0] = s_f
        st_ref[1] = s_b


def _retention(z, log_gamma, *, B, L, rope_tabs=None, s0=None, layer=0, want_state):
    n = L // RET_CHUNK
    H = N_RET_HEADS
    nblk = MIX_W // RET_DK

    def sec(s):
        return pl.BlockSpec((None, L, RET_DK), lambda b, h: (b, 0, s * nblk + h))

    in_specs = [pl.BlockSpec(memory_space=pltpu.SMEM), sec(0), sec(1), sec(2), sec(3)]
    args = [log_gamma, z, z, z, z]
    if rope_tabs is not None:
        in_specs += [pl.BlockSpec((L, RET_DK), lambda b, h: (0, 0))] * 3
        args += list(rope_tabs)
    if s0 is not None:
        in_specs.append(pl.BlockSpec((None, None, 2, None, RET_DK, RET_DK), lambda b, h: (b, layer, 0, h, 0, 0)))
        args.append(s0)
    out_specs = [pl.BlockSpec((None, L, RET_DK), lambda b, h: (b, 0, h))]
    out_shape = [jax.ShapeDtypeStruct((B, L, MIX_W), BF16)]
    if want_state:
        out_specs.append(pl.BlockSpec((None, 2, None, RET_DK, RET_DK), lambda b, h: (b, 0, h, 0, 0)))
        out_shape.append(jax.ShapeDtypeStruct((B, 2, H, RET_DK, RET_DK), F32))
    body = functools.partial(_ret_body, n=n, rope=rope_tabs is not None, has_s0=s0 is not None,
                             want_state=want_state)
    return pl.pallas_call(
        body,
        grid=(B, H),
        in_specs=in_specs,
        out_specs=out_specs,
        out_shape=out_shape,
        scratch_shapes=[pltpu.VMEM((L, RET_DK), F32), pltpu.VMEM((L, RET_DK), F32),
                        pltpu.VMEM((n, RET_DK, RET_DK), F32)],
        compiler_params=_params("arbitrary", "arbitrary"),
    )(*args)


def _rope_tables(L):
    pos = jnp.arange(L)
    row = (pos // GRID_W).astype(F32)
    col = (pos % GRID_W).astype(F32)
    quarter = RET_DK // 4
    inv_freq = ROPE_BASE ** (-jnp.arange(quarter, dtype=F32) / quarter)
    ang_r = row[:, None] * inv_freq[None, :]
    ang_c = col[:, None] * inv_freq[None, :]
    zero = jnp.zeros_like(ang_r)
    cos = jnp.concatenate([jnp.cos(ang_r), jnp.cos(ang_r), jnp.cos(ang_c), jnp.cos(ang_c)], -1)
    s_up = jnp.concatenate([-jnp.sin(ang_r), zero, -jnp.sin(ang_c), zero], -1)
    s_dn = jnp.concatenate([zero, jnp.sin(ang_r), zero, jnp.sin(ang_c)], -1)
    return cos, s_up, s_dn


def _s5_body(u_ref, wb_ref, wc_ref, a_ref, h0_ref, y_ref, fin_ref, w_scr, st_scr, *, slots, tc):
    d = pl.program_id(0)
    c = pl.program_id(2)
    lanes = 128
    nrt = slots // 8

    @pl.when(c == 0)
    def _():
        st_scr[...] = h0_ref[...]

    w_scr[...] = _dot(u_ref[...].astype(BF16), wb_ref[...])

    a_re = [jnp.broadcast_to(a_ref[0, :, q * lanes:(q + 1) * lanes], (8, lanes)) for q in range(S5_PAIRS)]
    a_im = [jnp.broadcast_to(a_ref[1, :, q * lanes:(q + 1) * lanes], (8, lanes)) for q in range(S5_PAIRS)]

    init = []
    for rt in range(nrt):
        for q in range(S5_PAIRS):
            init.append(st_scr[0, rt * 8:(rt + 1) * 8, q * lanes:(q + 1) * lanes])
            init.append(st_scr[1, rt * 8:(rt + 1) * 8, q * lanes:(q + 1) * lanes])

    def step(jj, carry):
        t = jj + d * (tc - 1 - 2 * jj)
        new = []
        for rt in range(nrt):
            r0 = pl.multiple_of(t * slots + rt * 8, 8)
            for q in range(S5_PAIRS):
                xr = carry[2 * (rt * S5_PAIRS + q)]
                xi = carry[2 * (rt * S5_PAIRS + q) + 1]
                cr = slice(2 * q * lanes, (2 * q + 1) * lanes)
                ci = slice((2 * q + 1) * lanes, (2 * q + 2) * lanes)
                nr = a_re[q] * xr - a_im[q] * xi + w_scr[pl.ds(r0, 8), cr]
                ni = a_re[q] * xi + a_im[q] * xr + w_scr[pl.ds(r0, 8), ci]
                w_scr[pl.ds(r0, 8), cr] = nr
                w_scr[pl.ds(r0, 8), ci] = ni
                new += [nr, ni]
        return tuple(new)

    fin = lax.fori_loop(0, tc, step, tuple(init), unroll=2)
    for rt in range(nrt):
        for q in range(S5_PAIRS):
            st_scr[0, rt * 8:(rt + 1) * 8, q * lanes:(q + 1) * lanes] = fin[2 * (rt * S5_PAIRS + q)]
            st_scr[1, rt * 8:(rt + 1) * 8, q * lanes:(q + 1) * lanes] = fin[2 * (rt * S5_PAIRS + q) + 1]

    y_ref[...] = _dot(w_scr[...].astype(BF16), wc_ref[...])

    @pl.when(c == pl.num_programs(2) - 1)
    def _():
        fin_ref[...] = st_scr[...]


def _s5(u, wb, wc, a_bar, h0, *, L, slots):
    tc = S5_ROWS // slots
    nT = L // tc
    npb = SSM_GROUPS // (2 * S5_PAIRS)
    sw = S5_PAIRS * 128
    gp = SSM_GROUPS * SSM_STATE

    def tci(d, c):
        return c + d * (nT - 1 - 2 * c)

    body = functools.partial(_s5_body, slots=slots, tc=tc)
    return pl.pallas_call(
        body,
        grid=(2, npb, nT),
        in_specs=[pl.BlockSpec((S5_ROWS, 128), lambda d, pb, c: (tci(d, c), pb)),
                  pl.BlockSpec((None, None, 128, 2 * sw), lambda d, pb, c: (d, pb, 0, 0)),
                  pl.BlockSpec((None, None, 2 * sw, 128), lambda d, pb, c: (d, pb, 0, 0)),
                  pl.BlockSpec((None, None, 2, 1, sw), lambda d, pb, c: (d, pb, 0, 0, 0)),
                  pl.BlockSpec((None, 2, slots, sw), lambda d, pb, c: (d, 0, 0, pb))],
        out_specs=[pl.BlockSpec((None, S5_ROWS, 128), lambda d, pb, c: (d, tci(d, c), pb)),
                   pl.BlockSpec((None, 2, slots, sw), lambda d, pb, c: (d, 0, 0, pb))],
        out_shape=[jax.ShapeDtypeStruct((2, L * slots, MIX_W), F32),
                   jax.ShapeDtypeStruct((2, 2, slots, gp), F32)],
        scratch_shapes=[pltpu.VMEM((S5_ROWS, 2 * sw), F32), pltpu.VMEM((2, slots, sw), F32)],
        compiler_params=_params("arbitrary", "arbitrary", "arbitrary"),
    )(u, wb, wc, a_bar, h0)


def _s5_weights(a_re, a_im, log_dt, b_re, b_im, c_re, c_im):
    npb = SSM_GROUPS // (2 * S5_PAIRS)
    lr = jnp.minimum(a_re, -1e-4)
    li = a_im
    dt = jnp.exp(log_dt)[..., None]
    mag = jnp.exp(lr * dt)
    ar = mag * jnp.cos(li * dt)
    ai = mag * jnp.sin(li * dt)
    den = lr * lr + li * li
    sr = ((ar - 1.0) * lr + ai * li) / den
    si = (ai * lr - (ar - 1.0) * li) / den
    bbr = sr[..., None] * b_re[None] - si[..., None] * b_im[None]
    bbi = sr[..., None] * b_im[None] + si[..., None] * b_re[None]
    eye_q = jnp.eye(S5_PAIRS, dtype=F32)
    eye_e = jnp.eye(2, dtype=F32)
    shp_b = (2, 2, npb, S5_PAIRS, 2, SSM_STATE, SSM_GROUP)
    bval = jnp.stack([bbr, bbi], 0).reshape(shp_b)
    wb = jnp.einsum('tdbqepc,qQ,eE->dbqecQtEp', bval, eye_q, eye_e)
    wb = wb.reshape(2, npb, S5_PAIRS * 2 * SSM_GROUP, S5_PAIRS * 2 * 2 * SSM_STATE).astype(BF16)
    shp_c = (2, 2, npb, S5_PAIRS, 2, SSM_GROUP, SSM_STATE)
    cval = jnp.stack([c_re, -c_im], 0).reshape(shp_c)
    wc = jnp.einsum('tdbqeop,qQ,eE->dbQtEpqeo', cval, eye_q, eye_e)
    wc = wc.reshape(2, npb, S5_PAIRS * 2 * 2 * SSM_STATE, S5_PAIRS * 2 * SSM_GROUP).astype(BF16)
    sw = S5_PAIRS * 2 * SSM_STATE
    a_bar = jnp.stack([ar.reshape(2, npb, 1, sw), ai.reshape(2, npb, 1, sw)], 2)
    return wb, wc, a_bar


def _head_masks(shape):
    lane = lax.broadcasted_iota(jnp.int32, shape, 1)
    return lane < NA_HEAD_DIM


def _cattn_body(q_ref, k_ref, v_ref, o_ref):
    q = q_ref[...]
    k = k_ref[...]
    v = v_ref[...]
    first = _head_masks(q.shape)
    outs = []
    for e in range(2):
        qe = jnp.where(first if e == 0 else jnp.logical_not(first), q, jnp.zeros_like(q))
        s = _dot_nt(qe, k) * (NA_HEAD_DIM ** -0.5)
        m = jnp.max(s, -1, keepdims=True)
        p = jnp.exp(s - m)
        l = jnp.sum(p, -1, keepdims=True)
        outs.append(_dot(p.astype(BF16), v) / l)
    o_ref[...] = jnp.where(first, outs[0], outs[1]).astype(BF16)


def _context_attention(z, *, B, L):
    nblk = MIX_W // 128

    def sec(s):
        return pl.BlockSpec((None, L, 128), lambda b, hp: (b, 0, s * nblk + hp))

    return pl.pallas_call(
        _cattn_body,
        grid=(B, nblk),
        in_specs=[sec(5), sec(6), sec(7)],
        out_specs=pl.BlockSpec((None, L, 128), lambda b, hp: (b, 0, hp)),
        out_shape=jax.ShapeDtypeStruct((B, L, MIX_W), BF16),
        compiler_params=_params("arbitrary", "arbitrary"),
    )(z, z, z)


def _na_body(q_ref, k_ref, v_ref, kc_ref, vc_ref, bias_ref, o_ref, *, rows):
    kr = NA_KR
    win = kr * GRID_W
    scale = NA_HEAD_DIM ** -0.5
    kctx = kc_ref[...].astype(BF16)
    vctx = vc_ref[...].astype(BF16)
    first = _head_masks((GRID_W, 128))

    def row(r, carry):
        rs = jnp.clip(r - kr // 2, 0, rows - kr)
        pat = jnp.where(r < kr // 2, r, jnp.where(r > rows - kr // 2, r - (rows - kr), kr // 2))
        q0 = pl.multiple_of(r * GRID_W, GRID_W)
        k0 = pl.multiple_of(rs * GRID_W, GRID_W)
        qr = q_ref[pl.ds(q0, GRID_W), :]
        kw = k_ref[pl.ds(k0, win), :]
        vw = v_ref[pl.ds(k0, win), :]
        outs = []
        for e in range(2):
            qe = jnp.where(first if e == 0 else jnp.logical_not(first), qr, jnp.zeros_like(qr))
            s_loc = _dot_nt(qe, kw) * scale + bias_ref[e, pat]
            s_ctx = _dot_nt(qe, kctx) * scale
            m = jnp.maximum(jnp.max(s_loc, -1, keepdims=True), jnp.max(s_ctx, -1, keepdims=True))
            p_loc = jnp.exp(s_loc - m)
            p_ctx = jnp.exp(s_ctx - m)
            l = jnp.sum(p_loc, -1, keepdims=True) + jnp.sum(p_ctx, -1, keepdims=True)
            outs.append((_dot(p_loc.astype(BF16), vw) + _dot(p_ctx.astype(BF16), vctx)) / l)
        o_ref[pl.ds(q0, GRID_W), :] = jnp.where(first, outs[0], outs[1]).astype(BF16)
        return carry

    lax.fori_loop(0, rows, row, 0)


def _neighbourhood_attention(z, cache_k, cache_v, bias, *, B, L, layer):
    nblk = MIX_W // 128
    rows = L // GRID_W
    Lc = cache_k.shape[2]

    def sec(s):
        return pl.BlockSpec((None, L, 128), lambda hp, b: (b, 0, s * nblk + hp))

    ctx = pl.BlockSpec((None, None, Lc, 128), lambda hp, b: (b, layer, 0, hp))
    return pl.pallas_call(
        functools.partial(_na_body, rows=rows),
        grid=(nblk, B),
        in_specs=[sec(5), sec(6), sec(7), ctx, ctx,
                  pl.BlockSpec((2, NA_KR, GRID_W, NA_KR * GRID_W), lambda hp, b: (hp, 0, 0, 0))],
        out_specs=pl.BlockSpec((None, L, 128), lambda hp, b: (b, 0, hp)),
        out_shape=jax.ShapeDtypeStruct((B, L, MIX_W), BF16),
        compiler_params=_params("arbitrary", "arbitrary"),
    )(z, z, z, cache_k, cache_v, bias)


def _na_bias_strips(rpb, rows):
    kr = NA_KR
    half = kr // 2
    pat_row = jnp.array(list(range(half)) + [half] + list(range(rows - half + 1, rows)))
    key_row0 = jnp.clip(pat_row - half, 0, rows - kr)
    roff = key_row0[:, None] + jnp.arange(kr)[None, :] - pat_row[:, None] + NA_KR - 1
    qc = jnp.arange(GRID_W)
    kc = jnp.arange(GRID_W)
    ws = jnp.clip(qc - NA_KW // 2, 0, GRID_W - NA_KW)
    valid = (kc[None, :] >= ws[:, None]) & (kc[None, :] < ws[:, None] + NA_KW)
    coff = jnp.clip(kc[None, :] - qc[:, None] + NA_KW - 1, 0, 2 * NA_KW - 2)
    b = rpb.astype(F32)[:, roff[:, :, None, None], coff[None, None, :, :]]
    b = jnp.where(valid[None, None, None], b, NEG_INF)
    b = jnp.transpose(b, (0, 1, 3, 2, 4))
    return b.reshape(rpb.shape[0], kr, GRID_W, kr * GRID_W)


def _merge_body(x_ref, p_ref, r_ref, u_ref, yf_ref, yb_ref, n_ref, ga_ref, gb_ref, gc_ref,
                d_ref, wglu_ref, wbr_ref, wo_ref, lg_ref, lb_ref, o_ref,
                wglu_s, wbr_s, wo_s, *, L, row0, rstride):
    i = pl.program_id(0)
    tm = x_ref.shape[0]

    @pl.when(i == 0)
    def _():
        wglu_s[...] = wglu_ref[...].astype(BF16)
        wbr_s[...] = wbr_ref[...].astype(BF16)
        wo_s[...] = wo_ref[...].astype(BF16)

    row = row0 + rstride * ((i * tm) // L)
    g1 = _mod_row(p_ref, row, 2)

    y = d_ref[...] * u_ref[...].astype(F32) + yf_ref[...] + yb_ref[...]
    y = jax.nn.gelu(y)
    s_out = y * jax.nn.sigmoid(_dot(y.astype(BF16), wglu_s[...]))

    def gate(ref):
        return jax.nn.sigmoid(ref[...].astype(F32))

    merged = (gate(ga_ref) * _dot(r_ref[...], wbr_s[0])
              + gate(gb_ref) * _dot(s_out.astype(BF16), wbr_s[1])
              + gate(gc_ref) * _dot(n_ref[...], wbr_s[2]))
    m = _dot(merged.astype(BF16), wo_s[...])
    o_ref[...] = _layer_norm(DEEPNORM_ALPHA * x_ref[...] + g1 * m, lg_ref[...], lb_ref[...])


def _merge(x, p, z, r_out, y, n_out, ssm_d, w_glu, w_branch, w_o, ln_g, ln_b, *, L, row0, rstride, pad):
    T = x.shape[0]
    tm = MERGE_TILE
    nt = L // tm
    gate0 = 8 * MIX_W // D_MODEL

    def tok(w):
        return pl.BlockSpec((tm, w), lambda i: (i, 0))

    def ysec(d):
        return pl.BlockSpec((None, tm, MIX_W), lambda i: (d, i % nt, (i // nt) * pad))

    def full(shape):
        return pl.BlockSpec(shape, lambda i: (0,) * len(shape))

    body = functools.partial(_merge_body, L=L, row0=row0, rstride=rstride)
    return pl.pallas_call(
        body,
        grid=(T // tm,),
        in_specs=[tok(D_MODEL), full((N_PAD_ROWS, 6 * D_MODEL)), tok(MIX_W),
                  pl.BlockSpec((tm, MIX_W), lambda i: (i, SU_TILE)), ysec(0), ysec(1), tok(MIX_W),
                  pl.BlockSpec((tm, D_MODEL), lambda i: (i, gate0)),
                  pl.BlockSpec((tm, D_MODEL), lambda i: (i, gate0 + 1)),
                  pl.BlockSpec((tm, D_MODEL), lambda i: (i, gate0 + 2)),
                  full((1, MIX_W)), full((MIX_W, MIX_W)), full((3, MIX_W, D_MODEL)),
                  full((D_MODEL, D_MODEL)), full((1, D_MODEL)), full((1, D_MODEL))],
        out_specs=tok(D_MODEL),
        out_shape=jax.ShapeDtypeStruct((T, D_MODEL), F32),
        scratch_shapes=[pltpu.VMEM((MIX_W, MIX_W), BF16), pltpu.VMEM((3, MIX_W, D_MODEL), BF16),
                        pltpu.VMEM((D_MODEL, D_MODEL), BF16)],
        compiler_params=_params("arbitrary"),
    )(x, p, r_out, z, y, y, n_out, z, z, z, ssm_d.reshape(1, MIX_W), w_glu, w_branch, w_o,
      ln_g.reshape(1, D_MODEL), ln_b.reshape(1, D_MODEL))


def _ffn_body(x_ref, p_ref, wa_ref, wb_ref, cwa_ref, cwb_ref, cba_ref, cbb_ref, wd_ref, lg_ref, lb_ref,
              o_ref, h_scr, acc_scr, *, L, row0, rstride):
    i = pl.program_id(0)
    j = pl.program_id(1)
    tm = x_ref.shape[0]
    nb = tm // L

    @pl.when(j == 0)
    def _():
        for s in range(nb):
            row = row0 + rstride * (i * nb + s)
            sh = _mod_row(p_ref, row, 3)
            sc = _mod_row(p_ref, row, 4)
            h_scr[s * L:(s + 1) * L, :] = (x_ref[s * L:(s + 1) * L, :] * (1.0 + sc) + sh).astype(BF16)
        acc_scr[...] = jnp.zeros_like(acc_scr)

    t = lax.broadcasted_iota(jnp.int32, (tm, 1), 0) % L
    has_prev = t != 0
    has_next = t != L - 1

    def conv(w_ref, cw_ref, cb_ref):
        zc = _dot(h_scr[...], w_ref[...].astype(BF16))
        zp = jnp.where(has_prev, pltpu.roll(zc, 1, 0), 0.0)
        zn = jnp.where(has_next, pltpu.roll(zc, tm - 1, 0), 0.0)
        return zp * cw_ref[0:1, :] + zc * cw_ref[1:2, :] + zn * cw_ref[2:3, :] + cb_ref[...]

    a = conv(wa_ref, cwa_ref, cba_ref)
    b = conv(wb_ref, cwb_ref, cbb_ref)
    acc_scr[...] += _dot((jax.nn.gelu(a) * b).astype(BF16), wd_ref[...].astype(BF16))

    @pl.when(j == pl.num_programs(1) - 1)
    def _():
        for s in range(nb):
            row = row0 + rstride * (i * nb + s)
            g2 = _mod_row(p_ref, row, 5)
            sl = slice(s * L, (s + 1) * L)
            o_ref[sl, :] = _layer_norm(DEEPNORM_ALPHA * x_ref[sl, :] + g2 * acc_scr[sl, :],
                                       lg_ref[...], lb_ref[...])


def _conv_ffn(x, p, w_up, conv_w, conv_b, w_down, ln_g, ln_b, *, L, row0, rstride):
    T = x.shape[0]
    tm = TOKEN_TILE
    nff = D_FF // FF_TILE
    body = functools.partial(_ffn_body, L=L, row0=row0, rstride=rstride)
    conv_b = conv_b.reshape(1, 2 * D_FF)
    return pl.pallas_call(
        body,
        grid=(T // tm, nff),
        in_specs=[pl.BlockSpec((tm, D_MODEL), lambda i, j: (i, 0)),
                  pl.BlockSpec((N_PAD_ROWS, 6 * D_MODEL), lambda i, j: (0, 0)),
                  pl.BlockSpec((D_MODEL, FF_TILE), lambda i, j: (0, j)),
                  pl.BlockSpec((D_MODEL, FF_TILE), lambda i, j: (0, nff + j)),
                  pl.BlockSpec((3, FF_TILE), lambda i, j: (0, j)),
                  pl.BlockSpec((3, FF_TILE), lambda i, j: (0, nff + j)),
                  pl.BlockSpec((1, FF_TILE), lambda i, j: (0, j)),
                  pl.BlockSpec((1, FF_TILE), lambda i, j: (0, nff + j)),
                  pl.BlockSpec((FF_TILE, D_MODEL), lambda i, j: (j, 0)),
                  pl.BlockSpec((1, D_MODEL), lambda i, j: (0, 0)),
                  pl.BlockSpec((1, D_MODEL), lambda i, j: (0, 0))],
        out_specs=pl.BlockSpec((tm, D_MODEL), lambda i, j: (i, 0)),
        out_shape=jax.ShapeDtypeStruct((T, D_MODEL), F32),
        scratch_shapes=[pltpu.VMEM((tm, D_MODEL), BF16), pltpu.VMEM((tm, D_MODEL), F32)],
        compiler_params=_params("arbitrary", "arbitrary"),
    )(x, p, w_up, w_up, conv_w, conv_w, conv_b, conv_b, w_down, ln_g.reshape(1, D_MODEL),
      ln_b.reshape(1, D_MODEL))


def _layer(x, p, lw, *, B, L, row0, rstride, pad, latent, layer, extra):
    T = B * L
    slots = B * pad
    z, u = _inproj(x, p, lw['w_in'], L=L, row0=row0, rstride=rstride, pad=pad)
    z3 = z.reshape(B, L, IN_COLS)
    log_gamma = jax.nn.log_sigmoid(lw['ret_decay'].astype(F32))
    wb, wc, a_bar = lw['s5']
    gp = SSM_GROUPS * SSM_STATE
    if latent:
        r_out = _retention(z3, log_gamma, B=B, L=L, rope_tabs=extra['rope'], s0=extra['state_ret'],
                           layer=layer, want_state=False)[0]
        h0 = jnp.transpose(extra['state_ssm'][:, layer], (1, 4, 0, 2, 3)).reshape(2, 2, B, gp)
        h0 = jnp.stack([h0] + [jnp.zeros_like(h0)] * (pad - 1), 3).reshape(2, 2, slots, gp)
        n_out = _neighbourhood_attention(z3, extra['cache_k'], extra['cache_v'], extra['bias'][layer],
                                         B=B, L=L, layer=layer)
        states = None
    else:
        r_out, ret_state = _retention(z3, log_gamma, B=B, L=L, want_state=True)
        h0 = jnp.zeros((2, 2, slots, gp), F32)
        n_out = _context_attention(z3, B=B, L=L)
    y, fin = _s5(u.reshape(L * slots, MIX_W), wb, wc, a_bar, h0, L=L, slots=slots)
    if not latent:
        ssm_state = jnp.transpose(fin.reshape(2, 2, B, SSM_GROUPS, SSM_STATE), (2, 0, 3, 4, 1))
        nk = z3[:, :, 6 * MIX_W:7 * MIX_W].astype(F32).reshape(B, L, NA_HEADS, NA_HEAD_DIM)
        nv = z3[:, :, 7 * MIX_W:8 * MIX_W].astype(F32).reshape(B, L, NA_HEADS, NA_HEAD_DIM)
        states = (ret_state, ssm_state, nk, nv)
    x = _merge(x, p, z, r_out.reshape(T, MIX_W), y.reshape(2, L, slots * MIX_W), n_out.reshape(T, MIX_W),
               lw['ssm_d'], lw['ssm_w_glu'], lw['w_branch'], lw['w_o'], lw['ln1_g'], lw['ln1_b'],
               L=L, row0=row0, rstride=rstride, pad=pad)
    x = _conv_ffn(x, p, lw['w_up'], lw['conv_w'], lw['conv_b'], lw['w_down'], lw['ln2_g'], lw['ln2_b'],
                  L=L, row0=row0, rstride=rstride)
    return x, states


def kernel(x_prompt, x_sample, state_ret, state_ssm, cache_na_k, cache_na_v, c, c_ctx, w_ada, b_ada, w_in, ret_decay, ssm_a_re, ssm_a_im, ssm_log_dt, ssm_b_re, ssm_b_im, ssm_c_re, ssm_c_im, ssm_d, ssm_w_glu, na_rpb, w_branch, w_o, ln1_g, ln1_b, w_up, conv_w, conv_b, w_down, ln2_g, ln2_b):
    B, L, _ = x_prompt.shape
    Bd, Ld, _ = x_sample.shape
    Lc = cache_na_k.shape[2]

    cond = jnp.concatenate([c_ctx[None, :], c, jnp.zeros((N_PAD_ROWS - 1 - Bd, D_MODEL), F32)], 0)
    p_all = _ada(cond, w_ada, b_ada)

    extra = dict(rope=_rope_tables(Ld), state_ret=state_ret, state_ssm=state_ssm,
                 cache_k=cache_na_k.reshape(Bd, DEPTH, Lc, MIX_W),
                 cache_v=cache_na_v.reshape(Bd, DEPTH, Lc, MIX_W),
                 bias=jnp.stack([_na_bias_strips(na_rpb[l], Ld // GRID_W) for l in range(DEPTH)], 0))

    xp = x_prompt.reshape(B * L, D_MODEL)
    xs = x_sample.reshape(Bd * Ld, D_MODEL)
    ret_states, ssm_states, na_ks, na_vs = [], [], [], []
    for l in range(DEPTH):
        lw = dict(w_in=w_in[l], ret_decay=ret_decay[l], ssm_d=ssm_d[l], ssm_w_glu=ssm_w_glu[l],
                  w_branch=w_branch[l], w_o=w_o[l], ln1_g=ln1_g[l], ln1_b=ln1_b[l], w_up=w_up[l],
                  conv_w=conv_w[l], conv_b=conv_b[l], w_down=w_down[l], ln2_g=ln2_g[l], ln2_b=ln2_b[l],
                  s5=_s5_weights(ssm_a_re[l], ssm_a_im[l], ssm_log_dt[l], ssm_b_re[l], ssm_b_im[l],
                                 ssm_c_re[l], ssm_c_im[l]))
        xp, (s_ret, s_ssm, nk, nv) = _layer(xp, p_all[l], lw, B=B, L=L, row0=0, rstride=0, pad=1,
                                            latent=False, layer=l, extra=None)
        ret_states.append(s_ret)
        ssm_states.append(s_ssm)
        na_ks.append(nk)
        na_vs.append(nv)
        xs, _ = _layer(xs, p_all[l], lw, B=Bd, L=Ld, row0=1, rstride=1, pad=N_PAD_ROWS // Bd,
                       latent=True, layer=l, extra=extra)
    return (xp.reshape(B, L, D_MODEL), xs.reshape(Bd, Ld, D_MODEL),
            jnp.stack(ret_states, 1), jnp.stack(ssm_states, 1), jnp.stack(na_ks, 1), jnp.stack(na_vs, 1))
```

```python
import functools

import jax
import jax.numpy as jnp
import numpy as np
from jax import lax
from jax.experimental import pallas as pl
from jax.experimental.pallas import tpu as pltpu

F32 = jnp.float32
BF16 = jnp.bfloat16

D_MODEL = 1024
DEPTH = 2
GRID_W = 64
MIX_W = D_MODEL // 2
N_RET_HEADS = 4
RET_DK = MIX_W // N_RET_HEADS
SSM_GROUP = 16
SSM_GROUPS = MIX_W // SSM_GROUP
SSM_STATE = 64
NA_HEADS = 8
NA_HEAD_DIM = MIX_W // NA_HEADS
NA_KR = 8
NA_KW = 16
D_FF = ((8 * D_MODEL // 3 + 127) // 128) * 128
ROPE_BASE = 10000.0
LN_EPS = 1e-5
NEG_INF = -1e30
DEEPNORM_ALPHA = (2 * DEPTH) ** 0.25
IN_COLS = 8 * MIX_W + 3 * D_MODEL

VMEM_LIMIT_BYTES = 56 * 1024 * 1024

TOKEN_TILE = 1024
MERGE_TILE = 256
COL_TILE = 512
SU_TILE = 4
FF_TILE = 256
RET_CHUNK = 256
S5_PAIRS = 4
S5_ROWS = 1024
N_PAD_ROWS = 8


def _params(*sem):
    return pltpu.CompilerParams(dimension_semantics=sem, vmem_limit_bytes=VMEM_LIMIT_BYTES)


def _dot(a, b):
    return jnp.dot(a, b, preferred_element_type=F32)


def _dot_nt(a, b):
    return lax.dot_general(a, b, (((1,), (1,)), ((), ())), preferred_element_type=F32)


def _layer_norm(x, g, b):
    mu = jnp.mean(x, -1, keepdims=True)
    xc = x - mu
    var = jnp.mean(xc * xc, -1, keepdims=True)
    return xc * lax.rsqrt(var + LN_EPS) * g + b


def _ada_body(c_ref, w_ref, b_ref, o_ref):
    c = c_ref[...]
    s = c * jax.nn.sigmoid(c)
    o_ref[...] = _dot(s.astype(BF16), w_ref[...].astype(BF16)) + b_ref[...]


def _ada(cond, w_ada, b_ada):
    tn = 1024
    return pl.pallas_call(
        _ada_body,
        grid=(DEPTH, 6 * D_MODEL // tn),
        in_specs=[pl.BlockSpec((N_PAD_ROWS, D_MODEL), lambda l, j: (0, 0)),
                  pl.BlockSpec((None, D_MODEL, tn), lambda l, j: (l, 0, j)),
                  pl.BlockSpec((None, 1, tn), lambda l, j: (l, 0, j))],
        out_specs=pl.BlockSpec((None, N_PAD_ROWS, tn), lambda l, j: (l, 0, j)),
        out_shape=jax.ShapeDtypeStruct((DEPTH, N_PAD_ROWS, 6 * D_MODEL), F32),
        compiler_params=_params("arbitrary", "arbitrary"),
    )(cond, w_ada, b_ada.reshape(DEPTH, 1, 6 * D_MODEL))


def _mod_row(p_ref, row, k):
    return p_ref[pl.ds(row, 1), k * D_MODEL:(k + 1) * D_MODEL]


def _inproj_body(x_ref, p_ref, w_ref, z_ref, u_ref, h_scr, *, L, row0, rstride, pad):
    i = pl.program_id(0)
    j = pl.program_id(1)
    nb = x_ref.shape[0] // L

    @pl.when(j == 0)
    def _():
        for s in range(nb):
            row = row0 + rstride * (i * nb + s)
            sh = _mod_row(p_ref, row, 0)
            sc = _mod_row(p_ref, row, 1)
            h_scr[s * L:(s + 1) * L, :] = (x_ref[s * L:(s + 1) * L, :] * (1.0 + sc) + sh).astype(BF16)

    acc = _dot(h_scr[...], w_ref[...].astype(BF16))
    z_ref[...] = acc.astype(BF16)

    @pl.when(j == SU_TILE)
    def _():
        for s in range(nb):
            u_ref[:, (s * pad) * MIX_W:(s * pad + 1) * MIX_W] = acc[s * L:(s + 1) * L, :]
            for e in range(1, pad):
                u_ref[:, (s * pad + e) * MIX_W:(s * pad + e + 1) * MIX_W] = jnp.zeros((L, MIX_W), F32)


def _inproj(x, p, w_in, *, L, row0, rstride, pad):
    T = x.shape[0]
    tm = TOKEN_TILE
    nb = tm // L
    slots = (T // L) * pad
    body = functools.partial(_inproj_body, L=L, row0=row0, rstride=rstride, pad=pad)
    return pl.pallas_call(
        body,
        grid=(T // tm, IN_COLS // COL_TILE),
        in_specs=[pl.BlockSpec((tm, D_MODEL), lambda i, j: (i, 0)),
                  pl.BlockSpec((N_PAD_ROWS, 6 * D_MODEL), lambda i, j: (0, 0)),
                  pl.BlockSpec((D_MODEL, COL_TILE), lambda i, j: (0, j))],
        out_specs=[pl.BlockSpec((tm, COL_TILE), lambda i, j: (i, j)),
                   pl.BlockSpec((L, nb * pad * MIX_W), lambda i, j: (0, i))],
        out_shape=[jax.ShapeDtypeStruct((T, IN_COLS), BF16),
                   jax.ShapeDtypeStruct((L, slots * MIX_W), F32)],
        scratch_shapes=[pltpu.VMEM((tm, D_MODEL), BF16)],
        compiler_params=_params("arbitrary", "arbitrary"),
    )(x, p, w_in)


def _rope(x, cos, s_up, s_dn):
    return x * cos + pltpu.roll(x, 96, 1) * s_up + pltpu.roll(x, 32, 1) * s_dn


def _ret_body(*refs, n, rope, has_s0, want_state):
    refs = list(refs)
    lg_ref, q_ref, k_ref, v_ref, g_ref = refs[:5]
    refs = refs[5:]
    if rope:
        cos_ref, sup_ref, sdn_ref = refs[:3]
        refs = refs[3:]
    if has_s0:
        s0_ref = refs[0]
        refs = refs[1:]
    o_ref = refs[0]
    refs = refs[1:]
    if want_state:
        st_ref = refs[0]
        refs = refs[1:]
    q_scr, k_scr, sb_scr = refs

    C = RET_CHUNK
    h = pl.program_id(1)
    lf = lg_ref[0, h]
    lb = lg_ref[1, h]

    q = q_ref[...].astype(F32)
    k = k_ref[...].astype(F32)
    if rope:
        q = _rope(q, cos_ref[...], sup_ref[...], sdn_ref[...])
        k = _rope(k, cos_ref[...], sup_ref[...], sdn_ref[...])
    q_scr[...] = q
    k_scr[...] = k * (RET_DK ** -0.5)

    tcol = lax.broadcasted_iota(jnp.int32, (C, 1), 0).astype(F32)
    ti = lax.broadcasted_iota(jnp.int32, (C, C), 0)
    si = lax.broadcasted_iota(jnp.int32, (C, C), 1)
    dlt = (ti - si).astype(F32)
    decay = (jnp.where(dlt >= 0, jnp.exp(lf * jnp.maximum(dlt, 0.0)), 0.0)
             + jnp.where(dlt <= 0, jnp.exp(lb * jnp.maximum(-dlt, 0.0)), 0.0))
    qd_f = jnp.exp(lf * (tcol + 1.0))
    qd_b = jnp.exp(lb * (C - tcol))
    kd_f = jnp.exp(lf * (C - 1.0 - tcol))
    kd_b = jnp.exp(lb * tcol)
    cd_f = jnp.exp(lf * jnp.full((1, RET_DK), float(C), F32))
    cd_b = jnp.exp(lb * jnp.full((1, RET_DK), float(C), F32))

    def kv_outer(kc, vc, kd):
        return _dot((kc * kd).T.astype(BF16), vc)

    s_b = s0_ref[1] if has_s0 else jnp.zeros((RET_DK, RET_DK), F32)
    for i in reversed(range(n)):
        sb_scr[i] = s_b
        if i > 0 or want_state:
            s_b = s_b * cd_b + kv_outer(k_scr[i * C:(i + 1) * C, :], v_ref[i * C:(i + 1) * C, :], kd_b)

    s_f = s0_ref[0] if has_s0 else jnp.zeros((RET_DK, RET_DK), F32)
    for i in range(n):
        sl = slice(i * C, (i + 1) * C)
        qc = q_scr[sl, :]
        kc = k_scr[sl, :]
        vc = v_ref[sl, :]
        att = _dot_nt(qc.astype(BF16), kc.astype(BF16)) * decay
        o = _dot(att.astype(BF16), vc)
        o = o + _dot((qc * qd_f).astype(BF16), s_f.astype(BF16))
        o = o + _dot((qc * qd_b).astype(BF16), sb_scr[i].astype(BF16))
        mu = jnp.mean(o, -1, keepdims=True)
        oc = o - mu
        var = jnp.mean(oc * oc, -1, keepdims=True)
        gc = g_ref[sl, :].astype(F32)
        o_ref[sl, :] = (oc * lax.rsqrt(var + LN_EPS) * (gc * jax.nn.sigmoid(gc))).astype(BF16)
        if i < n - 1 or want_state:
            s_f = s_f * cd_f + kv_outer(kc, vc, kd_f)

    if want_state:
        st_ref[0] = s_f
        st_ref[1] = s_b


def _retention(z, log_gamma, *, B, L, rope_tabs=None, s0=None, layer=0, want_state):
    n = L // RET_CHUNK
    H = N_RET_HEADS
    nblk = MIX_W // RET_DK

    def sec(s):
        return pl.BlockSpec((None, L, RET_DK), lambda b, h: (b, 0, s * nblk + h))

    in_specs = [pl.BlockSpec(memory_space=pltpu.SMEM), sec(0), sec(1), sec(2), sec(3)]
    args = [log_gamma, z, z, z, z]
    if rope_tabs is not None:
        in_specs += [pl.BlockSpec((L, RET_DK), lambda b, h: (0, 0))] * 3
        args += list(rope_tabs)
    if s0 is not None:
        in_specs.append(pl.BlockSpec((None, None, 2, None, RET_DK, RET_DK), lambda b, h: (b, layer, 0, h, 0, 0)))
        args.append(s0)
    out_specs = [pl.BlockSpec((None, L, RET_DK), lambda b, h: (b, 0, h))]
    out_shape = [jax.ShapeDtypeStruct((B, L, MIX_W), BF16)]
    if want_state:
        out_specs.append(pl.BlockSpec((None, 2, None, RET_DK, RET_DK), lambda b, h: (b, 0, h, 0, 0)))
        out_shape.append(jax.ShapeDtypeStruct((B, 2, H, RET_DK, RET_DK), F32))
    body = functools.partial(_ret_body, n=n, rope=rope_tabs is not None, has_s0=s0 is not None,
                             want_state=want_state)
    return pl.pallas_call(
        body,
        grid=(B, H),
        in_specs=in_specs,
        out_specs=out_specs,
        out_shape=out_shape,
        scratch_shapes=[pltpu.VMEM((L, RET_DK), F32), pltpu.VMEM((L, RET_DK), F32),
                        pltpu.VMEM((n, RET_DK, RET_DK), F32)],
        compiler_params=_params("arbitrary", "arbitrary"),
    )(*args)


def _rope_tables(L):
    pos = jnp.arange(L)
    row = (pos // GRID_W).astype(F32)
    col = (pos % GRID_W).astype(F32)
    quarter = RET_DK // 4
    inv_freq = ROPE_BASE ** (-jnp.arange(quarter, dtype=F32) / quarter)
    ang_r = row[:, None] * inv_freq[None, :]
    ang_c = col[:, None] * inv_freq[None, :]
    zero = jnp.zeros_like(ang_r)
    cos = jnp.concatenate([jnp.cos(ang_r), jnp.cos(ang_r), jnp.cos(ang_c), jnp.cos(ang_c)], -1)
    s_up = jnp.concatenate([-jnp.sin(ang_r), zero, -jnp.sin(ang_c), zero], -1)
    s_dn = jnp.concatenate([zero, jnp.sin(ang_r), zero, jnp.sin(ang_c)], -1)
    return cos, s_up, s_dn


def _s5_body(u_ref, wb_ref, wc_ref, a_ref, h0_ref, y_ref, fin_ref, w_scr, st_scr, *, slots, tc):
    d = pl.program_id(0)
    c = pl.program_id(2)
    lanes = 128
    nrt = slots // 8

    @pl.when(c == 0)
    def _():
        st_scr[...] = h0_ref[...]

    w_scr[...] = _dot(u_ref[...].astype(BF16), wb_ref[...])

    a_re = [jnp.broadcast_to(a_ref[0, :, q * lanes:(q + 1) * lanes], (8, lanes)) for q in range(S5_PAIRS)]
    a_im = [jnp.broadcast_to(a_ref[1, :, q * lanes:(q + 1) * lanes], (8, lanes)) for q in range(S5_PAIRS)]

    init = []
    for rt in range(nrt):
        for q in range(S5_PAIRS):
            init.append(st_scr[0, rt * 8:(rt + 1) * 8, q * lanes:(q + 1) * lanes])
            init.append(st_scr[1, rt * 8:(rt + 1) * 8, q * lanes:(q + 1) * lanes])

    def step(jj, carry):
        t = jj + d * (tc - 1 - 2 * jj)
        new = []
        for rt in range(nrt):
            r0 = pl.multiple_of(t * slots + rt * 8, 8)
            for q in range(S5_PAIRS):
                xr = carry[2 * (rt * S5_PAIRS + q)]
                xi = carry[2 * (rt * S5_PAIRS + q) + 1]
                cr = slice(2 * q * lanes, (2 * q + 1) * lanes)
                ci = slice((2 * q + 1) * lanes, (2 * q + 2) * lanes)
                nr = a_re[q] * xr - a_im[q] * xi + w_scr[pl.ds(r0, 8), cr]
                ni = a_re[q] * xi + a_im[q] * xr + w_scr[pl.ds(r0, 8), ci]
                w_scr[pl.ds(r0, 8), cr] = nr
                w_scr[pl.ds(r0, 8), ci] = ni
                new += [nr, ni]
        return tuple(new)

    fin = lax.fori_loop(0, tc, step, tuple(init), unroll=2)
    for rt in range(nrt):
        for q in range(S5_PAIRS):
            st_scr[0, rt * 8:(rt + 1) * 8, q * lanes:(q + 1) * lanes] = fin[2 * (rt * S5_PAIRS + q)]
            st_scr[1, rt * 8:(rt + 1) * 8, q * lanes:(q + 1) * lanes] = fin[2 * (rt * S5_PAIRS + q) + 1]

    y_ref[...] = _dot(w_scr[...].astype(BF16), wc_ref[...])

    @pl.when(c == pl.num_programs(2) - 1)
    def _():
        fin_ref[...] = st_scr[...]


def _s5(u, wb, wc, a_bar, h0, *, L, slots):
    tc = S5_ROWS // slots
    nT = L // tc
    npb = SSM_GROUPS // (2 * S5_PAIRS)
    sw = S5_PAIRS * 128
    gp = SSM_GROUPS * SSM_STATE

    def tci(d, c):
        return c + d * (nT - 1 - 2 * c)

    body = functools.partial(_s5_body, slots=slots, tc=tc)
    return pl.pallas_call(
        body,
        grid=(2, npb, nT),
        in_specs=[pl.BlockSpec((S5_ROWS, 128), lambda d, pb, c: (tci(d, c), pb)),
                  pl.BlockSpec((None, None, 128, 2 * sw), lambda d, pb, c: (d, pb, 0, 0)),
                  pl.BlockSpec((None, None, 2 * sw, 128), lambda d, pb, c: (d, pb, 0, 0)),
                  pl.BlockSpec((None, None, 2, 1, sw), lambda d, pb, c: (d, pb, 0, 0, 0)),
                  pl.BlockSpec((None, 2, slots, sw), lambda d, pb, c: (d, 0, 0, pb))],
        out_specs=[pl.BlockSpec((None, S5_ROWS, 128), lambda d, pb, c: (d, tci(d, c), pb)),
                   pl.BlockSpec((None, 2, slots, sw), lambda d, pb, c: (d, 0, 0, pb))],
        out_shape=[jax.ShapeDtypeStruct((2, L * slots, MIX_W), F32),
                   jax.ShapeDtypeStruct((2, 2, slots, gp), F32)],
        scratch_shapes=[pltpu.VMEM((S5_ROWS, 2 * sw), F32), pltpu.VMEM((2, slots, sw), F32)],
        compiler_params=_params("arbitrary", "arbitrary", "arbitrary"),
    )(u, wb, wc, a_bar, h0)


def _s5_weights(a_re, a_im, log_dt, b_re, b_im, c_re, c_im):
    npb = SSM_GROUPS // (2 * S5_PAIRS)
    lr = jnp.minimum(a_re, -1e-4)
    li = a_im
    dt = jnp.exp(log_dt)[..., None]
    mag = jnp.exp(lr * dt)
    ar = mag * jnp.cos(li * dt)
    ai = mag * jnp.sin(li * dt)
    den = lr * lr + li * li
    sr = ((ar - 1.0) * lr + ai * li) / den
    si = (ai * lr - (ar - 1.0) * li) / den
    bbr = sr[..., None] * b_re[None] - si[..., None] * b_im[None]
    bbi = sr[..., None] * b_im[None] + si[..., None] * b_re[None]
    eye_q = jnp.eye(S5_PAIRS, dtype=F32)
    eye_e = jnp.eye(2, dtype=F32)
    shp_b = (2, 2, npb, S5_PAIRS, 2, SSM_STATE, SSM_GROUP)
    bval = jnp.stack([bbr, bbi], 0).reshape(shp_b)
    wb = jnp.einsum('tdbqepc,qQ,eE->dbqecQtEp', bval, eye_q, eye_e)
    wb = wb.reshape(2, npb, S5_PAIRS * 2 * SSM_GROUP, S5_PAIRS * 2 * 2 * SSM_STATE).astype(BF16)
    shp_c = (2, 2, npb, S5_PAIRS, 2, SSM_GROUP, SSM_STATE)
    cval = jnp.stack([c_re, -c_im], 0).reshape(shp_c)
    wc = jnp.einsum('tdbqeop,qQ,eE->dbQtEpqeo', cval, eye_q, eye_e)
    wc = wc.reshape(2, npb, S5_PAIRS * 2 * 2 * SSM_STATE, S5_PAIRS * 2 * SSM_GROUP).astype(BF16)
    sw = S5_PAIRS * 2 * SSM_STATE
    a_bar = jnp.stack([ar.reshape(2, npb, 1, sw), ai.reshape(2, npb, 1, sw)], 2)
    return wb, wc, a_bar


def _head_masks(shape):
    lane = lax.broadcasted_iota(jnp.int32, shape, 1)
    return lane < NA_HEAD_DIM


def _cattn_body(q_ref, k_ref, v_ref, o_ref):
    q = q_ref[...]
    k = k_ref[...]
    v = v_ref[...]
    first = _head_masks(q.shape)
    outs = []
    for e in range(2):
        qe = jnp.where(first if e == 0 else jnp.logical_not(first), q, jnp.zeros_like(q))
        s = _dot_nt(qe, k) * (NA_HEAD_DIM ** -0.5)
        m = jnp.max(s, -1, keepdims=True)
        p = jnp.exp(s - m)
        l = jnp.sum(p, -1, keepdims=True)
        outs.append(_dot(p.astype(BF16), v) / l)
    o_ref[...] = jnp.where(first, outs[0], outs[1]).astype(BF16)


def _context_attention(z, *, B, L):
    nblk = MIX_W // 128

    def sec(s):
        return pl.BlockSpec((None, L, 128), lambda b, hp: (b, 0, s * nblk + hp))

    return pl.pallas_call(
        _cattn_body,
        grid=(B, nblk),
        in_specs=[sec(5), sec(6), sec(7)],
        out_specs=pl.BlockSpec((None, L, 128), lambda b, hp: (b, 0, hp)),
        out_shape=jax.ShapeDtypeStruct((B, L, MIX_W), BF16),
        compiler_params=_params("arbitrary", "arbitrary"),
    )(z, z, z)


def _na_body(q_ref, k_ref, v_ref, kc_ref, vc_ref, bias_ref, o_ref, *, rows):
    kr = NA_KR
    win = kr * GRID_W
    scale = NA_HEAD_DIM ** -0.5
    kctx = kc_ref[...].astype(BF16)
    vctx = vc_ref[...].astype(BF16)
    first = _head_masks((GRID_W, 128))

    def row(r, carry):
        rs = jnp.clip(r - kr // 2, 0, rows - kr)
        pat = jnp.where(r < kr // 2, r, jnp.where(r > rows - kr // 2, r - (rows - kr), kr // 2))
        q0 = pl.multiple_of(r * GRID_W, GRID_W)
        k0 = pl.multiple_of(rs * GRID_W, GRID_W)
        qr = q_ref[pl.ds(q0, GRID_W), :]
        kw = k_ref[pl.ds(k0, win), :]
        vw = v_ref[pl.ds(k0, win), :]
        outs = []
        for e in range(2):
            qe = jnp.where(first if e == 0 else jnp.logical_not(first), qr, jnp.zeros_like(qr))
            s_loc = _dot_nt(qe, kw) * scale + bias_ref[e, pat]
            s_ctx = _dot_nt(qe, kctx) * scale
            m = jnp.maximum(jnp.max(s_loc, -1, keepdims=True), jnp.max(s_ctx, -1, keepdims=True))
            p_loc = jnp.exp(s_loc - m)
            p_ctx = jnp.exp(s_ctx - m)
            l = jnp.sum(p_loc, -1, keepdims=True) + jnp.sum(p_ctx, -1, keepdims=True)
            outs.append((_dot(p_loc.astype(BF16), vw) + _dot(p_ctx.astype(BF16), vctx)) / l)
        o_ref[pl.ds(q0, GRID_W), :] = jnp.where(first, outs[0], outs[1]).astype(BF16)
        return carry

    lax.fori_loop(0, rows, row, 0)


def _neighbourhood_attention(z, cache_k, cache_v, bias, *, B, L, layer):
    nblk = MIX_W // 128
    rows = L // GRID_W
    Lc = cache_k.shape[2]

    def sec(s):
        return pl.BlockSpec((None, L, 128), lambda hp, b: (b, 0, s * nblk + hp))

    ctx = pl.BlockSpec((None, None, Lc, 128), lambda hp, b: (b, layer, 0, hp))
    return pl.pallas_call(
        functools.partial(_na_body, rows=rows),
        grid=(nblk, B),
        in_specs=[sec(5), sec(6), sec(7), ctx, ctx,
                  pl.BlockSpec((2, NA_KR, GRID_W, NA_KR * GRID_W), lambda hp, b: (hp, 0, 0, 0))],
        out_specs=pl.BlockSpec((None, L, 128), lambda hp, b: (b, 0, hp)),
        out_shape=jax.ShapeDtypeStruct((B, L, MIX_W), BF16),
        compiler_params=_params("arbitrary", "arbitrary"),
    )(z, z, z, cache_k, cache_v, bias)


def _na_bias_strips(rpb, rows):
    kr = NA_KR
    half = kr // 2
    pat_row = np.array(list(range(half)) + [half] + list(range(rows - half + 1, rows)))
    key_row0 = np.clip(pat_row - half, 0, rows - kr)
    roff = key_row0[:, None] + np.arange(kr)[None, :] - pat_row[:, None] + NA_KR - 1
    qc = np.arange(GRID_W)
    kc = np.arange(GRID_W)
    ws = np.clip(qc - NA_KW // 2, 0, GRID_W - NA_KW)
    valid = (kc[None, :] >= ws[:, None]) & (kc[None, :] < ws[:, None] + NA_KW)
    coff = np.clip(kc[None, :] - qc[:, None] + NA_KW - 1, 0, 2 * NA_KW - 2)
    sel_r = (roff[:, :, None] == np.arange(2 * NA_KR - 1)[None, None, :]).astype(np.float32)
    sel_c = ((coff[None] == np.arange(2 * NA_KW - 1)[:, None, None]) & valid[None]).astype(np.float32)
    b = jnp.einsum('hrc,pjr,cqk->hpqjk', rpb.astype(F32), sel_r, sel_c, precision=lax.Precision.HIGHEST)
    b = jnp.where(valid[None, None, :, None, :], b, NEG_INF)
    return b.reshape(rpb.shape[0], kr, GRID_W, kr * GRID_W)


def _merge_body(x_ref, p_ref, r_ref, u_ref, yf_ref, yb_ref, n_ref, ga_ref, gb_ref, gc_ref,
                d_ref, wglu_ref, wbr_ref, wo_ref, lg_ref, lb_ref, o_ref,
                wglu_s, wbr_s, wo_s, *, L, row0, rstride):
    i = pl.program_id(0)
    tm = x_ref.shape[0]

    @pl.when(i == 0)
    def _():
        wglu_s[...] = wglu_ref[...].astype(BF16)
        wbr_s[...] = wbr_ref[...].astype(BF16)
        wo_s[...] = wo_ref[...].astype(BF16)

    row = row0 + rstride * ((i * tm) // L)
    g1 = _mod_row(p_ref, row, 2)

    y = d_ref[...] * u_ref[...].astype(F32) + yf_ref[...] + yb_ref[...]
    y = jax.nn.gelu(y)
    s_out = y * jax.nn.sigmoid(_dot(y.astype(BF16), wglu_s[...]))

    def gate(ref):
        return jax.nn.sigmoid(ref[...].astype(F32))

    merged = (gate(ga_ref) * _dot(r_ref[...], wbr_s[0])
              + gate(gb_ref) * _dot(s_out.astype(BF16), wbr_s[1])
              + gate(gc_ref) * _dot(n_ref[...], wbr_s[2]))
    m = _dot(merged.astype(BF16), wo_s[...])
    o_ref[...] = _layer_norm(DEEPNORM_ALPHA * x_ref[...] + g1 * m, lg_ref[...], lb_ref[...])


def _merge(x, p, z, r_out, y, n_out, ssm_d, w_glu, w_branch, w_o, ln_g, ln_b, *, L, row0, rstride, pad):
    T = x.shape[0]
    tm = MERGE_TILE
    nt = L // tm
    gate0 = 8 * MIX_W // D_MODEL

    def tok(w):
        return pl.BlockSpec((tm, w), lambda i: (i, 0))

    def ysec(d):
        return pl.BlockSpec((None, tm, MIX_W), lambda i: (d, i % nt, (i // nt) * pad))

    def full(shape):
        return pl.BlockSpec(shape, lambda i: (0,) * len(shape))

    body = functools.partial(_merge_body, L=L, row0=row0, rstride=rstride)
    return pl.pallas_call(
        body,
        grid=(T // tm,),
        in_specs=[tok(D_MODEL), full((N_PAD_ROWS, 6 * D_MODEL)), tok(MIX_W),
                  pl.BlockSpec((tm, MIX_W), lambda i: (i, SU_TILE)), ysec(0), ysec(1), tok(MIX_W),
                  pl.BlockSpec((tm, D_MODEL), lambda i: (i, gate0)),
                  pl.BlockSpec((tm, D_MODEL), lambda i: (i, gate0 + 1)),
                  pl.BlockSpec((tm, D_MODEL), lambda i: (i, gate0 + 2)),
                  full((1, MIX_W)), full((MIX_W, MIX_W)), full((3, MIX_W, D_MODEL)),
                  full((D_MODEL, D_MODEL)), full((1, D_MODEL)), full((1, D_MODEL))],
        out_specs=tok(D_MODEL),
        out_shape=jax.ShapeDtypeStruct((T, D_MODEL), F32),
        scratch_shapes=[pltpu.VMEM((MIX_W, MIX_W), BF16), pltpu.VMEM((3, MIX_W, D_MODEL), BF16),
                        pltpu.VMEM((D_MODEL, D_MODEL), BF16)],
        compiler_params=_params("arbitrary"),
    )(x, p, r_out, z, y, y, n_out, z, z, z, ssm_d.reshape(1, MIX_W), w_glu, w_branch, w_o,
      ln_g.reshape(1, D_MODEL), ln_b.reshape(1, D_MODEL))


def _ffn_body(x_ref, p_ref, wa_ref, wb_ref, cwa_ref, cwb_ref, cba_ref, cbb_ref, wd_ref, lg_ref, lb_ref,
              o_ref, h_scr, acc_scr, *, L, row0, rstride):
    i = pl.program_id(0)
    j = pl.program_id(1)
    tm = x_ref.shape[0]
    nb = tm // L

    @pl.when(j == 0)
    def _():
        for s in range(nb):
            row = row0 + rstride * (i * nb + s)
            sh = _mod_row(p_ref, row, 3)
            sc = _mod_row(p_ref, row, 4)
            h_scr[s * L:(s + 1) * L, :] = (x_ref[s * L:(s + 1) * L, :] * (1.0 + sc) + sh).astype(BF16)
        acc_scr[...] = jnp.zeros_like(acc_scr)

    t = lax.broadcasted_iota(jnp.int32, (tm, 1), 0) % L
    has_prev = t != 0
    has_next = t != L - 1

    def conv(w_ref, cw_ref, cb_ref):
        zc = _dot(h_scr[...], w_ref[...].astype(BF16))
        zp = jnp.where(has_prev, pltpu.roll(zc, 1, 0), 0.0)
        zn = jnp.where(has_next, pltpu.roll(zc, tm - 1, 0), 0.0)
        return zp * cw_ref[0:1, :] + zc * cw_ref[1:2, :] + zn * cw_ref[2:3, :] + cb_ref[...]

    a = conv(wa_ref, cwa_ref, cba_ref)
    b = conv(wb_ref, cwb_ref, cbb_ref)
    acc_scr[...] += _dot((jax.nn.gelu(a) * b).astype(BF16), wd_ref[...].astype(BF16))

    @pl.when(j == pl.num_programs(1) - 1)
    def _():
        for s in range(nb):
            row = row0 + rstride * (i * nb + s)
            g2 = _mod_row(p_ref, row, 5)
            sl = slice(s * L, (s + 1) * L)
            o_ref[sl, :] = _layer_norm(DEEPNORM_ALPHA * x_ref[sl, :] + g2 * acc_scr[sl, :],
                                       lg_ref[...], lb_ref[...])


def _conv_ffn(x, p, w_up, conv_w, conv_b, w_down, ln_g, ln_b, *, L, row0, rstride):
    T = x.shape[0]
    tm = TOKEN_TILE
    nff = D_FF // FF_TILE
    body = functools.partial(_ffn_body, L=L, row0=row0, rstride=rstride)
    conv_b = conv_b.reshape(1, 2 * D_FF)
    return pl.pallas_call(
        body,
        grid=(T // tm, nff),
        in_specs=[pl.BlockSpec((tm, D_MODEL), lambda i, j: (i, 0)),
                  pl.BlockSpec((N_PAD_ROWS, 6 * D_MODEL), lambda i, j: (0, 0)),
                  pl.BlockSpec((D_MODEL, FF_TILE), lambda i, j: (0, j)),
                  pl.BlockSpec((D_MODEL, FF_TILE), lambda i, j: (0, nff + j)),
                  pl.BlockSpec((3, FF_TILE), lambda i, j: (0, j)),
                  pl.BlockSpec((3, FF_TILE), lambda i, j: (0, nff + j)),
                  pl.BlockSpec((1, FF_TILE), lambda i, j: (0, j)),
                  pl.BlockSpec((1, FF_TILE), lambda i, j: (0, nff + j)),
                  pl.BlockSpec((FF_TILE, D_MODEL), lambda i, j: (j, 0)),
                  pl.BlockSpec((1, D_MODEL), lambda i, j: (0, 0)),
                  pl.BlockSpec((1, D_MODEL), lambda i, j: (0, 0))],
        out_specs=pl.BlockSpec((tm, D_MODEL), lambda i, j: (i, 0)),
        out_shape=jax.ShapeDtypeStruct((T, D_MODEL), F32),
        scratch_shapes=[pltpu.VMEM((tm, D_MODEL), BF16), pltpu.VMEM((tm, D_MODEL), F32)],
        compiler_params=_params("arbitrary", "arbitrary"),
    )(x, p, w_up, w_up, conv_w, conv_w, conv_b, conv_b, w_down, ln_g.reshape(1, D_MODEL),
      ln_b.reshape(1, D_MODEL))


def _layer(x, p, lw, *, B, L, row0, rstride, pad, latent, layer, extra):
    T = B * L
    slots = B * pad
    z, u = _inproj(x, p, lw['w_in'], L=L, row0=row0, rstride=rstride, pad=pad)
    z3 = z.reshape(B, L, IN_COLS)
    log_gamma = jax.nn.log_sigmoid(lw['ret_decay'].astype(F32))
    wb, wc, a_bar = lw['s5']
    gp = SSM_GROUPS * SSM_STATE
    if latent:
        r_out = _retention(z3, log_gamma, B=B, L=L, rope_tabs=extra['rope'], s0=extra['state_ret'],
                           layer=layer, want_state=False)[0]
        h0 = jnp.transpose(extra['state_ssm'][:, layer], (1, 4, 0, 2, 3)).reshape(2, 2, B, gp)
        h0 = jnp.stack([h0] + [jnp.zeros_like(h0)] * (pad - 1), 3).reshape(2, 2, slots, gp)
        n_out = _neighbourhood_attention(z3, extra['cache_k'], extra['cache_v'], extra['bias'][layer],
                                         B=B, L=L, layer=layer)
        states = None
    else:
        r_out, ret_state = _retention(z3, log_gamma, B=B, L=L, want_state=True)
        h0 = jnp.zeros((2, 2, slots, gp), F32)
        n_out = _context_attention(z3, B=B, L=L)
    y, fin = _s5(u.reshape(L * slots, MIX_W), wb, wc, a_bar, h0, L=L, slots=slots)
    if not latent:
        ssm_state = jnp.transpose(fin.reshape(2, 2, B, SSM_GROUPS, SSM_STATE), (2, 0, 3, 4, 1))
        nk = z3[:, :, 6 * MIX_W:7 * MIX_W].astype(F32).reshape(B, L, NA_HEADS, NA_HEAD_DIM)
        nv = z3[:, :, 7 * MIX_W:8 * MIX_W].astype(F32).reshape(B, L, NA_HEADS, NA_HEAD_DIM)
        states = (ret_state, ssm_state, nk, nv)
    x = _merge(x, p, z, r_out.reshape(T, MIX_W), y.reshape(2, L, slots * MIX_W), n_out.reshape(T, MIX_W),
               lw['ssm_d'], lw['ssm_w_glu'], lw['w_branch'], lw['w_o'], lw['ln1_g'], lw['ln1_b'],
               L=L, row0=row0, rstride=rstride, pad=pad)
    x = _conv_ffn(x, p, lw['w_up'], lw['conv_w'], lw['conv_b'], lw['w_down'], lw['ln2_g'], lw['ln2_b'],
                  L=L, row0=row0, rstride=rstride)
    return x, states


def kernel(x_prompt, x_sample, state_ret, state_ssm, cache_na_k, cache_na_v, c, c_ctx, w_ada, b_ada, w_in, ret_decay, ssm_a_re, ssm_a_im, ssm_log_dt, ssm_b_re, ssm_b_im, ssm_c_re, ssm_c_im, ssm_d, ssm_w_glu, na_rpb, w_branch, w_o, ln1_g, ln1_b, w_up, conv_w, conv_b, w_down, ln2_g, ln2_b):
    B, L, _ = x_prompt.shape
    Bd, Ld, _ = x_sample.shape
    Lc = cache_na_k.shape[2]

    cond = jnp.concatenate([c_ctx[None, :], c, jnp.zeros((N_PAD_ROWS - 1 - Bd, D_MODEL), F32)], 0)
    p_all = _ada(cond, w_ada, b_ada)

    extra = dict(rope=_rope_tables(Ld), state_ret=state_ret, state_ssm=state_ssm,
                 cache_k=cache_na_k.reshape(Bd, DEPTH, Lc, MIX_W),
                 cache_v=cache_na_v.reshape(Bd, DEPTH, Lc, MIX_W),
                 bias=jnp.stack([_na_bias_strips(na_rpb[l], Ld // GRID_W) for l in range(DEPTH)], 0))

    xp = x_prompt.reshape(B * L, D_MODEL)
    xs = x_sample.reshape(Bd * Ld, D_MODEL)
    ret_states, ssm_states, na_ks, na_vs = [], [], [], []
    for l in range(DEPTH):
        lw = dict(w_in=w_in[l], ret_decay=ret_decay[l], ssm_d=ssm_d[l], ssm_w_glu=ssm_w_glu[l],
                  w_branch=w_branch[l], w_o=w_o[l], ln1_g=ln1_g[l], ln1_b=ln1_b[l], w_up=w_up[l],
                  conv_w=conv_w[l], conv_b=conv_b[l], w_down=w_down[l], ln2_g=ln2_g[l], ln2_b=ln2_b[l],
                  s5=_s5_weights(ssm_a_re[l], ssm_a_im[l], ssm_log_dt[l], ssm_b_re[l], ssm_b_im[l],
                                 ssm_c_re[l], ssm_c_im[l]))
        xp, (s_ret, s_ssm, nk, nv) = _layer(xp, p_all[l], lw, B=B, L=L, row0=0, rstride=0, pad=1,
                                            latent=False, layer=l, extra=None)
        ret_states.append(s_ret)
        ssm_states.append(s_ssm)
        na_ks.append(nk)
        na_vs.append(nv)
        xs, _ = _layer(xs, p_all[l], lw, B=Bd, L=Ld, row0=1, rstride=1, pad=N_PAD_ROWS // Bd,
                       latent=True, layer=l, extra=extra)
    return (xp.reshape(B, L, D_MODEL), xs.reshape(Bd, Ld, D_MODEL),
            jnp.stack(ret_states, 1), jnp.stack(ssm_states, 1), jnp.stack(na_ks, 1), jnp.stack(na_vs, 1))
```

```python
import functools

import jax
import jax.numpy as jnp
import numpy as np
from jax import lax
from jax.experimental import pallas as pl
from jax.experimental.pallas import tpu as pltpu

F32 = jnp.float32
BF16 = jnp.bfloat16

D_MODEL = 1024
DEPTH = 2
GRID_W = 64
MIX_W = D_MODEL // 2
N_RET_HEADS = 4
RET_DK = MIX_W // N_RET_HEADS
SSM_GROUP = 16
SSM_GROUPS = MIX_W // SSM_GROUP
SSM_STATE = 64
NA_HEADS = 8
NA_HEAD_DIM = MIX_W // NA_HEADS
NA_KR = 8
NA_KW = 16
D_FF = ((8 * D_MODEL // 3 + 127) // 128) * 128
ROPE_BASE = 10000.0
LN_EPS = 1e-5
NEG_INF = -1e30
DEEPNORM_ALPHA = (2 * DEPTH) ** 0.25
IN_COLS = 8 * MIX_W + 3 * D_MODEL

VMEM_LIMIT_BYTES = 56 * 1024 * 1024

TOKEN_TILE = 1024
MERGE_TILE = 256
COL_TILE = 512
SU_TILE = 4
NK_TILE = 6
FF_TILE = 256
RET_CHUNK = 256
S5_PAIRS = 4
S5_ROWS = 1024
N_PAD_ROWS = 8
NA_ROW_UNROLL = 2
CTX_ATTN_BATCH = 4


def _params(*sem):
    return pltpu.CompilerParams(dimension_semantics=sem, vmem_limit_bytes=VMEM_LIMIT_BYTES)


def _dot(a, b):
    return jnp.dot(a, b, preferred_element_type=F32)


def _dot_nt(a, b):
    return lax.dot_general(a, b, (((1,), (1,)), ((), ())), preferred_element_type=F32)


def _layer_norm(x, g, b):
    mu = jnp.mean(x, -1, keepdims=True)
    xc = x - mu
    var = jnp.mean(xc * xc, -1, keepdims=True)
    return xc * lax.rsqrt(var + LN_EPS) * g + b


def _ada_body(c_ref, w_ref, b_ref, o_ref):
    c = c_ref[...]
    s = c * jax.nn.sigmoid(c)
    o_ref[...] = _dot(s.astype(BF16), w_ref[...].astype(BF16)) + b_ref[...]


def _ada(cond, w_ada, b_ada):
    tn = 1024
    return pl.pallas_call(
        _ada_body,
        grid=(DEPTH, 6 * D_MODEL // tn),
        in_specs=[pl.BlockSpec((N_PAD_ROWS, D_MODEL), lambda l, j: (0, 0)),
                  pl.BlockSpec((None, D_MODEL, tn), lambda l, j: (l, 0, j)),
                  pl.BlockSpec((None, 1, tn), lambda l, j: (l, 0, j))],
        out_specs=pl.BlockSpec((None, N_PAD_ROWS, tn), lambda l, j: (l, 0, j)),
        out_shape=jax.ShapeDtypeStruct((DEPTH, N_PAD_ROWS, 6 * D_MODEL), F32),
        name="ada",
        compiler_params=_params("arbitrary", "arbitrary"),
    )(cond, w_ada, b_ada.reshape(DEPTH, 1, 6 * D_MODEL))


def _mod_row(p_ref, row, k):
    return p_ref[pl.ds(row, 1), k * D_MODEL:(k + 1) * D_MODEL]


def _inproj_body(x_ref, p_ref, w_ref, z_ref, u_ref, *rest, L, row0, rstride, pad):
    h_scr, w_scr = rest[-2:]
    kv_refs = rest[:-2]
    j = pl.program_id(0)
    i = pl.program_id(1)
    nb = x_ref.shape[0] // L

    @pl.when(j == 0)
    def _():
        for s in range(nb):
            row = row0 + rstride * (i * nb + s)
            sh = _mod_row(p_ref, row, 0)
            sc = _mod_row(p_ref, row, 1)
            h_scr[i, s * L:(s + 1) * L, :] = (x_ref[s * L:(s + 1) * L, :] * (1.0 + sc) + sh).astype(BF16)

    @pl.when(i == 0)
    def _():
        w_scr[...] = w_ref[...].astype(BF16)

    acc = _dot(h_scr[i], w_scr[...])
    z_ref[...] = acc.astype(BF16)

    @pl.when(j == SU_TILE)
    def _():
        for s in range(nb):
            u_ref[:, (s * pad) * MIX_W:(s * pad + 1) * MIX_W] = acc[s * L:(s + 1) * L, :]
            for e in range(1, pad):
                u_ref[:, (s * pad + e) * MIX_W:(s * pad + e + 1) * MIX_W] = jnp.zeros((L, MIX_W), F32)

    for n, ref in enumerate(kv_refs):
        @pl.when(j == NK_TILE + n)
        def _(ref=ref):
            ref[...] = acc


def _inproj(x, p, w_in, *, L, row0, rstride, pad, want_kv):
    T = x.shape[0]
    tm = TOKEN_TILE
    nb = tm // L
    n_i = T // tm
    slots = (T // L) * pad
    body = functools.partial(_inproj_body, L=L, row0=row0, rstride=rstride, pad=pad)
    n_kv = 2 if want_kv else 0

    def only_at(tile):
        return lambda j, i: jnp.where(j < tile, 0, jnp.where(j > tile, n_i - 1, i))

    su_i = only_at(SU_TILE)
    kv_i = [only_at(NK_TILE + n) for n in range(n_kv)]
    return pl.pallas_call(
        body,
        grid=(IN_COLS // COL_TILE, n_i),
        in_specs=[pl.BlockSpec((tm, D_MODEL), lambda j, i: (jnp.where(j == 0, i, n_i - 1), 0)),
                  pl.BlockSpec((N_PAD_ROWS, 6 * D_MODEL), lambda j, i: (0, 0)),
                  pl.BlockSpec((D_MODEL, COL_TILE), lambda j, i: (0, j))],
        out_specs=[pl.BlockSpec((tm, COL_TILE), lambda j, i: (i, j)),
                   pl.BlockSpec((L, nb * pad * MIX_W), lambda j, i: (0, su_i(j, i)))]
        + [pl.BlockSpec((tm, MIX_W), lambda j, i, f=f: (f(j, i), 0)) for f in kv_i],
        out_shape=[jax.ShapeDtypeStruct((T, IN_COLS), BF16),
                   jax.ShapeDtypeStruct((L, slots * MIX_W), F32)]
        + [jax.ShapeDtypeStruct((T, MIX_W), F32)] * n_kv,
        scratch_shapes=[pltpu.VMEM((n_i, tm, D_MODEL), BF16), pltpu.VMEM((D_MODEL, COL_TILE), BF16)],
        name="inproj",
        compiler_params=_params("arbitrary", "arbitrary"),
    )(x, p, w_in)


def _rope(x, cos, s_up, s_dn):
    return x * cos + pltpu.roll(x, 96, 1) * s_up + pltpu.roll(x, 32, 1) * s_dn


def _ret_body(*refs, n, rope, has_s0, want_state):
    refs = list(refs)
    lg_ref, q_ref, k_ref, v_ref, g_ref = refs[:5]
    refs = refs[5:]
    if rope:
        cos_ref, sup_ref, sdn_ref = refs[:3]
        refs = refs[3:]
    if has_s0:
        s0_ref = refs[0]
        refs = refs[1:]
    o_ref = refs[0]
    refs = refs[1:]
    if want_state:
        st_ref = refs[0]
        refs = refs[1:]
    q_scr, k_scr, sb_scr = refs

    C = RET_CHUNK
    h = pl.program_id(1)
    lf = lg_ref[0, h]
    lb = lg_ref[1, h]

    q = q_ref[...].astype(F32)
    k = k_ref[...].astype(F32)
    if rope:
        q = _rope(q, cos_ref[...], sup_ref[...], sdn_ref[...])
        k = _rope(k, cos_ref[...], sup_ref[...], sdn_ref[...])
    q_scr[...] = q
    k_scr[...] = k * (RET_DK ** -0.5)

    tcol = lax.broadcasted_iota(jnp.int32, (C, 1), 0).astype(F32)
    ti = lax.broadcasted_iota(jnp.int32, (C, C), 0)
    si = lax.broadcasted_iota(jnp.int32, (C, C), 1)
    dlt = (ti - si).astype(F32)
    decay = (jnp.where(dlt >= 0, jnp.exp(lf * jnp.maximum(dlt, 0.0)), 0.0)
             + jnp.where(dlt <= 0, jnp.exp(lb * jnp.maximum(-dlt, 0.0)), 0.0))
    qd_f = jnp.exp(lf * (tcol + 1.0))
    qd_b = jnp.exp(lb * (C - tcol))
    kd_f = jnp.exp(lf * (C - 1.0 - tcol))
    kd_b = jnp.exp(lb * tcol)
    cd_f = jnp.exp(lf * jnp.full((1, RET_DK), float(C), F32))
    cd_b = jnp.exp(lb * jnp.full((1, RET_DK), float(C), F32))

    def kv_outer(kc, vc, kd):
        return _dot((kc * kd).T.astype(BF16), vc)

    s_b = s0_ref[1] if has_s0 else jnp.zeros((RET_DK, RET_DK), F32)
    for i in reversed(range(n)):
        sb_scr[i] = s_b
        if i > 0 or want_state:
            s_b = s_b * cd_b + kv_outer(k_scr[i * C:(i + 1) * C, :], v_ref[i * C:(i + 1) * C, :], kd_b)

    s_f = s0_ref[0] if has_s0 else jnp.zeros((RET_DK, RET_DK), F32)
    for i in range(n):
        sl = slice(i * C, (i + 1) * C)
        qc = q_scr[sl, :]
        kc = k_scr[sl, :]
        vc = v_ref[sl, :]
        att = _dot_nt(qc.astype(BF16), kc.astype(BF16)) * decay
        o = _dot(att.astype(BF16), vc)
        o = o + _dot((qc * qd_f).astype(BF16), s_f.astype(BF16))
        o = o + _dot((qc * qd_b).astype(BF16), sb_scr[i].astype(BF16))
        mu = jnp.mean(o, -1, keepdims=True)
        oc = o - mu
        var = jnp.mean(oc * oc, -1, keepdims=True)
        gc = g_ref[sl, :].astype(F32)
        o_ref[sl, :] = (oc * lax.rsqrt(var + LN_EPS) * (gc * jax.nn.sigmoid(gc))).astype(BF16)
        if i < n - 1 or want_state:
            s_f = s_f * cd_f + kv_outer(kc, vc, kd_f)

    if want_state:
        st_ref[0] = s_f
        st_ref[1] = s_b


def _retention(z, log_gamma, *, B, L, rope_tabs=None, s0=None, layer=0, want_state):
    n = L // RET_CHUNK
    H = N_RET_HEADS
    nblk = MIX_W // RET_DK

    def sec(s):
        return pl.BlockSpec((None, L, RET_DK), lambda b, h: (b, 0, s * nblk + h))

    in_specs = [pl.BlockSpec(memory_space=pltpu.SMEM), sec(0), sec(1), sec(2), sec(3)]
    args = [log_gamma, z, z, z, z]
    if rope_tabs is not None:
        in_specs += [pl.BlockSpec((L, RET_DK), lambda b, h: (0, 0))] * 3
        args += list(rope_tabs)
    if s0 is not None:
        in_specs.append(pl.BlockSpec((None, None, 2, None, RET_DK, RET_DK), lambda b, h: (b, layer, 0, h, 0, 0)))
        args.append(s0)
    out_specs = [pl.BlockSpec((None, L, RET_DK), lambda b, h: (b, 0, h))]
    out_shape = [jax.ShapeDtypeStruct((B, L, MIX_W), BF16)]
    if want_state:
        out_specs.append(pl.BlockSpec((None, 2, None, RET_DK, RET_DK), lambda b, h: (b, 0, h, 0, 0)))
        out_shape.append(jax.ShapeDtypeStruct((B, 2, H, RET_DK, RET_DK), F32))
    body = functools.partial(_ret_body, n=n, rope=rope_tabs is not None, has_s0=s0 is not None,
                             want_state=want_state)
    return pl.pallas_call(
        body,
        grid=(B, H),
        in_specs=in_specs,
        out_specs=out_specs,
        out_shape=out_shape,
        scratch_shapes=[pltpu.VMEM((L, RET_DK), F32), pltpu.VMEM((L, RET_DK), F32),
                        pltpu.VMEM((n, RET_DK, RET_DK), F32)],
        name="retention",
        compiler_params=_params("arbitrary", "arbitrary"),
    )(*args)


def _rope_tables(L):
    pos = jnp.arange(L)
    row = (pos // GRID_W).astype(F32)
    col = (pos % GRID_W).astype(F32)
    quarter = RET_DK // 4
    inv_freq = ROPE_BASE ** (-jnp.arange(quarter, dtype=F32) / quarter)
    ang_r = row[:, None] * inv_freq[None, :]
    ang_c = col[:, None] * inv_freq[None, :]
    zero = jnp.zeros_like(ang_r)
    cos = jnp.concatenate([jnp.cos(ang_r), jnp.cos(ang_r), jnp.cos(ang_c), jnp.cos(ang_c)], -1)
    s_up = jnp.concatenate([-jnp.sin(ang_r), zero, -jnp.sin(ang_c), zero], -1)
    s_dn = jnp.concatenate([zero, jnp.sin(ang_r), zero, jnp.sin(ang_c)], -1)
    return cos, s_up, s_dn


def _s5_body(u_ref, wb_ref, wc_ref, a_ref, h0_ref, y_ref, fin_ref, w_scr, st_scr, *, slots, tc):
    d = pl.program_id(0)
    c = pl.program_id(2)
    lanes = 128
    nrt = slots // 8

    @pl.when(c == 0)
    def _():
        st_scr[...] = h0_ref[...]

    w_scr[...] = _dot(u_ref[...].astype(BF16), wb_ref[...])

    a_re = [jnp.broadcast_to(a_ref[0, :, q * lanes:(q + 1) * lanes], (8, lanes)) for q in range(S5_PAIRS)]
    a_im = [jnp.broadcast_to(a_ref[1, :, q * lanes:(q + 1) * lanes], (8, lanes)) for q in range(S5_PAIRS)]

    init = []
    for rt in range(nrt):
        for q in range(S5_PAIRS):
            init.append(st_scr[0, rt * 8:(rt + 1) * 8, q * lanes:(q + 1) * lanes])
            init.append(st_scr[1, rt * 8:(rt + 1) * 8, q * lanes:(q + 1) * lanes])

    def step(jj, carry):
        t = jj + d * (tc - 1 - 2 * jj)
        new = []
        for rt in range(nrt):
            r0 = pl.multiple_of(t * slots + rt * 8, 8)
            for q in range(S5_PAIRS):
                xr = carry[2 * (rt * S5_PAIRS + q)]
                xi = carry[2 * (rt * S5_PAIRS + q) + 1]
                cr = slice(2 * q * lanes, (2 * q + 1) * lanes)
                ci = slice((2 * q + 1) * lanes, (2 * q + 2) * lanes)
                nr = a_re[q] * xr - a_im[q] * xi + w_scr[pl.ds(r0, 8), cr]
                ni = a_re[q] * xi + a_im[q] * xr + w_scr[pl.ds(r0, 8), ci]
                w_scr[pl.ds(r0, 8), cr] = nr
                w_scr[pl.ds(r0, 8), ci] = ni
                new += [nr, ni]
        return tuple(new)

    fin = lax.fori_loop(0, tc, step, tuple(init), unroll=2)
    for rt in range(nrt):
        for q in range(S5_PAIRS):
            st_scr[0, rt * 8:(rt + 1) * 8, q * lanes:(q + 1) * lanes] = fin[2 * (rt * S5_PAIRS + q)]
            st_scr[1, rt * 8:(rt + 1) * 8, q * lanes:(q + 1) * lanes] = fin[2 * (rt * S5_PAIRS + q) + 1]

    y_ref[...] = _dot(w_scr[...].astype(BF16), wc_ref[...])

    @pl.when(c == pl.num_programs(2) - 1)
    def _():
        fin_ref[...] = st_scr[...]


def _s5(u, wb, wc, a_bar, h0, *, L, slots):
    tc = S5_ROWS // slots
    nT = L // tc
    npb = SSM_GROUPS // (2 * S5_PAIRS)
    sw = S5_PAIRS * 128
    gp = SSM_GROUPS * SSM_STATE

    def tci(d, c):
        return c + d * (nT - 1 - 2 * c)

    body = functools.partial(_s5_body, slots=slots, tc=tc)
    return pl.pallas_call(
        body,
        grid=(2, npb, nT),
        in_specs=[pl.BlockSpec((S5_ROWS, 128), lambda d, pb, c: (tci(d, c), pb)),
                  pl.BlockSpec((None, None, 128, 2 * sw), lambda d, pb, c: (d, pb, 0, 0)),
                  pl.BlockSpec((None, None, 2 * sw, 128), lambda d, pb, c: (d, pb, 0, 0)),
                  pl.BlockSpec((None, None, 2, 1, sw), lambda d, pb, c: (d, pb, 0, 0, 0)),
                  pl.BlockSpec((None, 2, slots, sw), lambda d, pb, c: (d, 0, 0, pb))],
        out_specs=[pl.BlockSpec((None, S5_ROWS, 128), lambda d, pb, c: (d, tci(d, c), pb)),
                   pl.BlockSpec((None, 2, slots, sw), lambda d, pb, c: (d, 0, 0, pb))],
        out_shape=[jax.ShapeDtypeStruct((2, L * slots, MIX_W), F32),
                   jax.ShapeDtypeStruct((2, 2, slots, gp), F32)],
        scratch_shapes=[pltpu.VMEM((S5_ROWS, 2 * sw), F32), pltpu.VMEM((2, slots, sw), F32)],
        name="s5",
        compiler_params=_params("arbitrary", "arbitrary", "arbitrary"),
    )(u, wb, wc, a_bar, h0)


def _s5_weights(a_re, a_im, log_dt, b_re, b_im, c_re, c_im):
    npb = SSM_GROUPS // (2 * S5_PAIRS)
    lr = jnp.minimum(a_re, -1e-4)
    li = a_im
    dt = jnp.exp(log_dt)[..., None]
    mag = jnp.exp(lr * dt)
    ar = mag * jnp.cos(li * dt)
    ai = mag * jnp.sin(li * dt)
    den = lr * lr + li * li
    sr = ((ar - 1.0) * lr + ai * li) / den
    si = (ai * lr - (ar - 1.0) * li) / den
    bbr = sr[..., None] * b_re[None] - si[..., None] * b_im[None]
    bbi = sr[..., None] * b_im[None] + si[..., None] * b_re[None]
    eye_q = jnp.eye(S5_PAIRS, dtype=F32)
    eye_e = jnp.eye(2, dtype=F32)
    shp_b = (2, 2, npb, S5_PAIRS, 2, SSM_STATE, SSM_GROUP)
    bval = jnp.stack([bbr, bbi], 0).reshape(shp_b)
    wb = jnp.einsum('tdbqepc,qQ,eE->dbqecQtEp', bval, eye_q, eye_e)
    wb = wb.reshape(2, npb, S5_PAIRS * 2 * SSM_GROUP, S5_PAIRS * 2 * 2 * SSM_STATE).astype(BF16)
    shp_c = (2, 2, npb, S5_PAIRS, 2, SSM_GROUP, SSM_STATE)
    cval = jnp.stack([c_re, -c_im], 0).reshape(shp_c)
    wc = jnp.einsum('tdbqeop,qQ,eE->dbQtEpqeo', cval, eye_q, eye_e)
    wc = wc.reshape(2, npb, S5_PAIRS * 2 * 2 * SSM_STATE, S5_PAIRS * 2 * SSM_GROUP).astype(BF16)
    sw = S5_PAIRS * 2 * SSM_STATE
    a_bar = jnp.stack([ar.reshape(2, npb, 1, sw), ai.reshape(2, npb, 1, sw)], 2)
    return wb, wc, a_bar


def _head_masks(shape):
    lane = lax.broadcasted_iota(jnp.int32, shape, 1)
    return lane < NA_HEAD_DIM


def _cattn_body(q_ref, k_ref, v_ref, o_ref):
    first = _head_masks(q_ref.shape[1:])
    for bb in range(q_ref.shape[0]):
        q = q_ref[bb]
        k = k_ref[bb]
        v = v_ref[bb]
        outs = []
        for e in range(2):
            qe = jnp.where(first if e == 0 else jnp.logical_not(first), q, jnp.zeros_like(q))
            s = _dot_nt(qe, k) * (NA_HEAD_DIM ** -0.5)
            m = jnp.max(s, -1, keepdims=True)
            p = jnp.exp(s - m)
            l = jnp.sum(p, -1, keepdims=True)
            outs.append(_dot(p.astype(BF16), v) / l)
        o_ref[bb] = jnp.where(first, outs[0], outs[1]).astype(BF16)


def _context_attention(z, *, B, L):
    nblk = MIX_W // 128
    nb = CTX_ATTN_BATCH

    def sec(s):
        return pl.BlockSpec((nb, L, 128), lambda b, hp: (b, 0, s * nblk + hp))

    return pl.pallas_call(
        _cattn_body,
        grid=(B // nb, nblk),
        in_specs=[sec(5), sec(6), sec(7)],
        out_specs=pl.BlockSpec((nb, L, 128), lambda b, hp: (b, 0, hp)),
        out_shape=jax.ShapeDtypeStruct((B, L, MIX_W), BF16),
        name="ctx_attention",
        compiler_params=_params("arbitrary", "arbitrary"),
    )(z, z, z)


def _na_body(q_ref, k_ref, v_ref, kc_ref, vc_ref, bias_ref, o_ref, *, rows):
    kr = NA_KR
    win = kr * GRID_W
    scale = NA_HEAD_DIM ** -0.5
    kctx = kc_ref[...].astype(BF16)
    vctx = vc_ref[...].astype(BF16)
    first = _head_masks((GRID_W, 128))

    def row(r, carry):
        rs = jnp.clip(r - kr // 2, 0, rows - kr)
        pat = jnp.where(r < kr // 2, r, jnp.where(r > rows - kr // 2, r - (rows - kr), kr // 2))
        q0 = pl.multiple_of(r * GRID_W, GRID_W)
        k0 = pl.multiple_of(rs * GRID_W, GRID_W)
        qr = q_ref[pl.ds(q0, GRID_W), :]
        kw = k_ref[pl.ds(k0, win), :]
        vw = v_ref[pl.ds(k0, win), :]
        outs = []
        for e in range(2):
            qe = jnp.where(first if e == 0 else jnp.logical_not(first), qr, jnp.zeros_like(qr))
            s_loc = _dot_nt(qe, kw) * scale + bias_ref[e, pat]
            s_ctx = _dot_nt(qe, kctx) * scale
            m = jnp.maximum(jnp.max(s_loc, -1, keepdims=True), jnp.max(s_ctx, -1, keepdims=True))
            p_loc = jnp.exp(s_loc - m)
            p_ctx = jnp.exp(s_ctx - m)
            l = jnp.sum(p_loc, -1, keepdims=True) + jnp.sum(p_ctx, -1, keepdims=True)
            outs.append((_dot(p_loc.astype(BF16), vw) + _dot(p_ctx.astype(BF16), vctx)) / l)
        o_ref[pl.ds(q0, GRID_W), :] = jnp.where(first, outs[0], outs[1]).astype(BF16)
        return carry

    lax.fori_loop(0, rows, row, 0, unroll=NA_ROW_UNROLL)


def _neighbourhood_attention(z, cache_k, cache_v, bias, *, B, L, layer):
    nblk = MIX_W // 128
    rows = L // GRID_W
    Lc = cache_k.shape[2]

    def sec(s):
        return pl.BlockSpec((None, L, 128), lambda hp, b: (b, 0, s * nblk + hp))

    ctx = pl.BlockSpec((None, None, Lc, 128), lambda hp, b: (b, layer, 0, hp))
    return pl.pallas_call(
        functools.partial(_na_body, rows=rows),
        grid=(nblk, B),
        in_specs=[sec(5), sec(6), sec(7), ctx, ctx,
                  pl.BlockSpec((2, NA_KR, GRID_W, NA_KR * GRID_W), lambda hp, b: (hp, 0, 0, 0))],
        out_specs=pl.BlockSpec((None, L, 128), lambda hp, b: (b, 0, hp)),
        out_shape=jax.ShapeDtypeStruct((B, L, MIX_W), BF16),
        name="nbr_attention",
        compiler_params=_params("arbitrary", "arbitrary"),
    )(z, z, z, cache_k, cache_v, bias)


def _na_bias_strips(rpb, rows):
    kr = NA_KR
    half = kr // 2
    pat_row = np.array(list(range(half)) + [half] + list(range(rows - half + 1, rows)))
    key_row0 = np.clip(pat_row - half, 0, rows - kr)
    roff = key_row0[:, None] + np.arange(kr)[None, :] - pat_row[:, None] + NA_KR - 1
    qc = np.arange(GRID_W)
    kc = np.arange(GRID_W)
    ws = np.clip(qc - NA_KW // 2, 0, GRID_W - NA_KW)
    valid = (kc[None, :] >= ws[:, None]) & (kc[None, :] < ws[:, None] + NA_KW)
    coff = np.clip(kc[None, :] - qc[:, None] + NA_KW - 1, 0, 2 * NA_KW - 2)
    sel_r = (roff[:, :, None] == np.arange(2 * NA_KR - 1)[None, None, :]).astype(np.float32)
    sel_c = ((coff[None] == np.arange(2 * NA_KW - 1)[:, None, None]) & valid[None]).astype(np.float32)
    b = jnp.einsum('hrc,pjr,cqk->hpqjk', rpb.astype(F32), sel_r, sel_c, precision=lax.Precision.HIGHEST)
    b = jnp.where(valid[None, None, :, None, :], b, NEG_INF)
    return b.reshape(rpb.shape[0], kr, GRID_W, kr * GRID_W)


def _merge_body(x_ref, p_ref, r_ref, u_ref, yf_ref, yb_ref, n_ref, ga_ref, gb_ref, gc_ref,
                d_ref, wglu_ref, wbr_ref, wo_ref, lg_ref, lb_ref, o_ref,
                wglu_s, wbr_s, wo_s, *, L, row0, rstride):
    i = pl.program_id(0)
    tm = x_ref.shape[0]

    @pl.when(i == 0)
    def _():
        wglu_s[...] = wglu_ref[...].astype(BF16)
        wbr_s[...] = wbr_ref[...].astype(BF16)
        wo_s[...] = wo_ref[...].astype(BF16)

    row = row0 + rstride * ((i * tm) // L)
    g1 = _mod_row(p_ref, row, 2)

    y = d_ref[...] * u_ref[...].astype(F32) + yf_ref[...] + yb_ref[...]
    y = jax.nn.gelu(y)
    s_out = y * jax.nn.sigmoid(_dot(y.astype(BF16), wglu_s[...]))

    def gate(ref):
        return jax.nn.sigmoid(ref[...].astype(F32))

    merged = (gate(ga_ref) * _dot(r_ref[...], wbr_s[0])
              + gate(gb_ref) * _dot(s_out.astype(BF16), wbr_s[1])
              + gate(gc_ref) * _dot(n_ref[...], wbr_s[2]))
    m = _dot(merged.astype(BF16), wo_s[...])
    o_ref[...] = _layer_norm(DEEPNORM_ALPHA * x_ref[...] + g1 * m, lg_ref[...], lb_ref[...])


def _merge(x, p, z, r_out, y, n_out, ssm_d, w_glu, w_branch, w_o, ln_g, ln_b, *, L, row0, rstride, pad):
    T = x.shape[0]
    tm = MERGE_TILE
    nt = L // tm
    gate0 = 8 * MIX_W // D_MODEL

    def tok(w):
        return pl.BlockSpec((tm, w), lambda i: (i, 0))

    def ysec(d):
        return pl.BlockSpec((None, tm, MIX_W), lambda i: (d, i % nt, (i // nt) * pad))

    def full(shape):
        return pl.BlockSpec(shape, lambda i: (0,) * len(shape))

    body = functools.partial(_merge_body, L=L, row0=row0, rstride=rstride)
    return pl.pallas_call(
        body,
        grid=(T // tm,),
        in_specs=[tok(D_MODEL), full((N_PAD_ROWS, 6 * D_MODEL)), tok(MIX_W),
                  pl.BlockSpec((tm, MIX_W), lambda i: (i, SU_TILE)), ysec(0), ysec(1), tok(MIX_W),
                  pl.BlockSpec((tm, D_MODEL), lambda i: (i, gate0)),
                  pl.BlockSpec((tm, D_MODEL), lambda i: (i, gate0 + 1)),
                  pl.BlockSpec((tm, D_MODEL), lambda i: (i, gate0 + 2)),
                  full((1, MIX_W)), full((MIX_W, MIX_W)), full((3, MIX_W, D_MODEL)),
                  full((D_MODEL, D_MODEL)), full((1, D_MODEL)), full((1, D_MODEL))],
        out_specs=tok(D_MODEL),
        out_shape=jax.ShapeDtypeStruct((T, D_MODEL), F32),
        scratch_shapes=[pltpu.VMEM((MIX_W, MIX_W), BF16), pltpu.VMEM((3, MIX_W, D_MODEL), BF16),
                        pltpu.VMEM((D_MODEL, D_MODEL), BF16)],
        name="merge",
        compiler_params=_params("arbitrary"),
    )(x, p, r_out, z, y, y, n_out, z, z, z, ssm_d.reshape(1, MIX_W), w_glu, w_branch, w_o,
      ln_g.reshape(1, D_MODEL), ln_b.reshape(1, D_MODEL))


def _ffn_body(x_ref, p_ref, wa_ref, wb_ref, cwa_ref, cwb_ref, cba_ref, cbb_ref, wd_ref, lg_ref, lb_ref,
              o_ref, h_scr, acc_scr, *, L, row0, rstride):
    i = pl.program_id(0)
    j = pl.program_id(1)
    tm = x_ref.shape[0]
    nb = tm // L

    @pl.when(j == 0)
    def _():
        for s in range(nb):
            row = row0 + rstride * (i * nb + s)
            sh = _mod_row(p_ref, row, 3)
            sc = _mod_row(p_ref, row, 4)
            h_scr[s * L:(s + 1) * L, :] = (x_ref[s * L:(s + 1) * L, :] * (1.0 + sc) + sh).astype(BF16)
        acc_scr[...] = jnp.zeros_like(acc_scr)

    t = lax.broadcasted_iota(jnp.int32, (tm, 1), 0) % L
    has_prev = t != 0
    has_next = t != L - 1

    def conv(w_ref, cw_ref, cb_ref):
        zc = _dot(h_scr[...], w_ref[...].astype(BF16))
        zp = jnp.where(has_prev, pltpu.roll(zc, 1, 0), 0.0)
        zn = jnp.where(has_next, pltpu.roll(zc, tm - 1, 0), 0.0)
        return zp * cw_ref[0:1, :] + zc * cw_ref[1:2, :] + zn * cw_ref[2:3, :] + cb_ref[...]

    a = conv(wa_ref, cwa_ref, cba_ref)
    b = conv(wb_ref, cwb_ref, cbb_ref)
    acc_scr[...] += _dot((jax.nn.gelu(a) * b).astype(BF16), wd_ref[...].astype(BF16))

    @pl.when(j == pl.num_programs(1) - 1)
    def _():
        for s in range(nb):
            row = row0 + rstride * (i * nb + s)
            g2 = _mod_row(p_ref, row, 5)
            sl = slice(s * L, (s + 1) * L)
            o_ref[sl, :] = _layer_norm(DEEPNORM_ALPHA * x_ref[sl, :] + g2 * acc_scr[sl, :],
                                       lg_ref[...], lb_ref[...])


def _conv_ffn(x, p, w_up, conv_w, conv_b, w_down, ln_g, ln_b, *, L, row0, rstride):
    T = x.shape[0]
    tm = TOKEN_TILE
    nff = D_FF // FF_TILE
    body = functools.partial(_ffn_body, L=L, row0=row0, rstride=rstride)
    conv_b = conv_b.reshape(1, 2 * D_FF)
    return pl.pallas_call(
        body,
        grid=(T // tm, nff),
        in_specs=[pl.BlockSpec((tm, D_MODEL), lambda i, j: (i, 0)),
                  pl.BlockSpec((N_PAD_ROWS, 6 * D_MODEL), lambda i, j: (0, 0)),
                  pl.BlockSpec((D_MODEL, FF_TILE), lambda i, j: (0, j)),
                  pl.BlockSpec((D_MODEL, FF_TILE), lambda i, j: (0, nff + j)),
                  pl.BlockSpec((3, FF_TILE), lambda i, j: (0, j)),
                  pl.BlockSpec((3, FF_TILE), lambda i, j: (0, nff + j)),
                  pl.BlockSpec((1, FF_TILE), lambda i, j: (0, j)),
                  pl.BlockSpec((1, FF_TILE), lambda i, j: (0, nff + j)),
                  pl.BlockSpec((FF_TILE, D_MODEL), lambda i, j: (j, 0)),
                  pl.BlockSpec((1, D_MODEL), lambda i, j: (0, 0)),
                  pl.BlockSpec((1, D_MODEL), lambda i, j: (0, 0))],
        out_specs=pl.BlockSpec((tm, D_MODEL), lambda i, j: (i, 0)),
        out_shape=jax.ShapeDtypeStruct((T, D_MODEL), F32),
        scratch_shapes=[pltpu.VMEM((tm, D_MODEL), BF16), pltpu.VMEM((tm, D_MODEL), F32)],
        name="conv_ffn",
        compiler_params=_params("arbitrary", "arbitrary"),
    )(x, p, w_up, w_up, conv_w, conv_w, conv_b, conv_b, w_down, ln_g.reshape(1, D_MODEL),
      ln_b.reshape(1, D_MODEL))


def _layer(x, p, lw, *, B, L, row0, rstride, pad, latent, layer, extra):
    T = B * L
    slots = B * pad
    z, u, *kv = _inproj(x, p, lw['w_in'], L=L, row0=row0, rstride=rstride, pad=pad, want_kv=not latent)
    z3 = z.reshape(B, L, IN_COLS)
    log_gamma = jax.nn.log_sigmoid(lw['ret_decay'].astype(F32))
    wb, wc, a_bar = lw['s5']
    gp = SSM_GROUPS * SSM_STATE
    if latent:
        r_out = _retention(z3, log_gamma, B=B, L=L, rope_tabs=extra['rope'], s0=extra['state_ret'],
                           layer=layer, want_state=False)[0]
        h0 = jnp.transpose(extra['state_ssm'][:, layer], (1, 4, 0, 2, 3)).reshape(2, 2, B, gp)
        h0 = jnp.stack([h0] + [jnp.zeros_like(h0)] * (pad - 1), 3).reshape(2, 2, slots, gp)
        n_out = _neighbourhood_attention(z3, extra['cache_k'], extra['cache_v'], extra['bias'][layer],
                                         B=B, L=L, layer=layer)
        states = None
    else:
        r_out, ret_state = _retention(z3, log_gamma, B=B, L=L, want_state=True)
        h0 = jnp.zeros((2, 2, slots, gp), F32)
        n_out = _context_attention(z3, B=B, L=L)
    y, fin = _s5(u.reshape(L * slots, MIX_W), wb, wc, a_bar, h0, L=L, slots=slots)
    if not latent:
        ssm_state = jnp.transpose(fin.reshape(2, 2, B, SSM_GROUPS, SSM_STATE), (2, 0, 3, 4, 1))
        nk = kv[0].reshape(B, L, NA_HEADS, NA_HEAD_DIM)
        nv = kv[1].reshape(B, L, NA_HEADS, NA_HEAD_DIM)
        states = (ret_state, ssm_state, nk, nv)
    x = _merge(x, p, z, r_out.reshape(T, MIX_W), y.reshape(2, L, slots * MIX_W), n_out.reshape(T, MIX_W),
               lw['ssm_d'], lw['ssm_w_glu'], lw['w_branch'], lw['w_o'], lw['ln1_g'], lw['ln1_b'],
               L=L, row0=row0, rstride=rstride, pad=pad)
    x = _conv_ffn(x, p, lw['w_up'], lw['conv_w'], lw['conv_b'], lw['w_down'], lw['ln2_g'], lw['ln2_b'],
                  L=L, row0=row0, rstride=rstride)
    return x, states


def kernel(x_prompt, x_sample, state_ret, state_ssm, cache_na_k, cache_na_v, c, c_ctx, w_ada, b_ada, w_in, ret_decay, ssm_a_re, ssm_a_im, ssm_log_dt, ssm_b_re, ssm_b_im, ssm_c_re, ssm_c_im, ssm_d, ssm_w_glu, na_rpb, w_branch, w_o, ln1_g, ln1_b, w_up, conv_w, conv_b, w_down, ln2_g, ln2_b):
    B, L, _ = x_prompt.shape
    Bd, Ld, _ = x_sample.shape
    Lc = cache_na_k.shape[2]

    cond = jnp.concatenate([c_ctx[None, :], c, jnp.zeros((N_PAD_ROWS - 1 - Bd, D_MODEL), F32)], 0)
    p_all = _ada(cond, w_ada, b_ada)

    extra = dict(rope=_rope_tables(Ld), state_ret=state_ret, state_ssm=state_ssm,
                 cache_k=cache_na_k.reshape(Bd, DEPTH, Lc, MIX_W),
                 cache_v=cache_na_v.reshape(Bd, DEPTH, Lc, MIX_W),
                 bias=jnp.stack([_na_bias_strips(na_rpb[l], Ld // GRID_W) for l in range(DEPTH)], 0))

    xp = x_prompt.reshape(B * L, D_MODEL)
    xs = x_sample.reshape(Bd * Ld, D_MODEL)
    ret_states, ssm_states, na_ks, na_vs = [], [], [], []
    for l in range(DEPTH):
        lw = dict(w_in=w_in[l], ret_decay=ret_decay[l], ssm_d=ssm_d[l], ssm_w_glu=ssm_w_glu[l],
                  w_branch=w_branch[l], w_o=w_o[l], ln1_g=ln1_g[l], ln1_b=ln1_b[l], w_up=w_up[l],
                  conv_w=conv_w[l], conv_b=conv_b[l], w_down=w_down[l], ln2_g=ln2_g[l], ln2_b=ln2_b[l],
                  s5=_s5_weights(ssm_a_re[l], ssm_a_im[l], ssm_log_dt[l], ssm_b_re[l], ssm_b_im[l],
                                 ssm_c_re[l], ssm_c_im[l]))
        xp, (s_ret, s_ssm, nk, nv) = _layer(xp, p_all[l], lw, B=B, L=L, row0=0, rstride=0, pad=1,
                                            latent=False, layer=l, extra=None)
        ret_states.append(s_ret)
        ssm_states.append(s_ssm)
        na_ks.append(nk)
        na_vs.append(nv)
        xs, _ = _layer(xs, p_all[l], lw, B=Bd, L=Ld, row0=1, rstride=1, pad=N_PAD_ROWS // Bd,
                       latent=True, layer=l, extra=extra)
    return (xp.reshape(B, L, D_MODEL), xs.reshape(Bd, Ld, D_MODEL),
            jnp.stack(ret_states, 1), jnp.stack(ssm_states, 1), jnp.stack(na_ks, 1), jnp.stack(na_vs, 1))
```

```python
import functools

import jax
import jax.numpy as jnp
import numpy as np
from jax import lax
from jax.experimental import pallas as pl
from jax.experimental.pallas import tpu as pltpu

F32 = jnp.float32
BF16 = jnp.bfloat16

D_MODEL = 1024
DEPTH = 2
GRID_W = 64
MIX_W = D_MODEL // 2
N_RET_HEADS = 4
RET_DK = MIX_W // N_RET_HEADS
SSM_GROUP = 16
SSM_GROUPS = MIX_W // SSM_GROUP
SSM_STATE = 64
NA_HEADS = 8
NA_HEAD_DIM = MIX_W // NA_HEADS
NA_KR = 8
NA_KW = 16
D_FF = ((8 * D_MODEL // 3 + 127) // 128) * 128
ROPE_BASE = 10000.0
LN_EPS = 1e-5
NEG_INF = -1e30
DEEPNORM_ALPHA = (2 * DEPTH) ** 0.25
IN_COLS = 8 * MIX_W + 3 * D_MODEL

VMEM_LIMIT_BYTES = 56 * 1024 * 1024

TOKEN_TILE = 1024
MERGE_TILE = 256
COL_TILE = 512
SU_TILE = 4
NK_TILE = 6
FF_TILE = 256
RET_CHUNK = 256
S5_PAIRS = 4
S5_ROWS = 1024
N_PAD_ROWS = 8
NA_CHUNK_ROWS = 4
NA_WIN_ROWS = 12
CTX_ATTN_BATCH = 4


def _params(*sem):
    return pltpu.CompilerParams(dimension_semantics=sem, vmem_limit_bytes=VMEM_LIMIT_BYTES)


def _dot(a, b):
    return jnp.dot(a, b, preferred_element_type=F32)


def _dot_nt(a, b):
    return lax.dot_general(a, b, (((1,), (1,)), ((), ())), preferred_element_type=F32)


def _layer_norm(x, g, b):
    mu = jnp.mean(x, -1, keepdims=True)
    xc = x - mu
    var = jnp.mean(xc * xc, -1, keepdims=True)
    return xc * lax.rsqrt(var + LN_EPS) * g + b


def _ada_body(c_ref, w_ref, b_ref, o_ref):
    c = c_ref[...]
    s = c * jax.nn.sigmoid(c)
    o_ref[...] = _dot(s.astype(BF16), w_ref[...].astype(BF16)) + b_ref[...]


def _ada(cond, w_ada, b_ada):
    tn = 1024
    return pl.pallas_call(
        _ada_body,
        grid=(DEPTH, 6 * D_MODEL // tn),
        in_specs=[pl.BlockSpec((N_PAD_ROWS, D_MODEL), lambda l, j: (0, 0)),
                  pl.BlockSpec((None, D_MODEL, tn), lambda l, j: (l, 0, j)),
                  pl.BlockSpec((None, 1, tn), lambda l, j: (l, 0, j))],
        out_specs=pl.BlockSpec((None, N_PAD_ROWS, tn), lambda l, j: (l, 0, j)),
        out_shape=jax.ShapeDtypeStruct((DEPTH, N_PAD_ROWS, 6 * D_MODEL), F32),
        name="ada",
        compiler_params=_params("arbitrary", "arbitrary"),
    )(cond, w_ada, b_ada.reshape(DEPTH, 1, 6 * D_MODEL))


def _mod_row(p_ref, row, k):
    return p_ref[pl.ds(row, 1), k * D_MODEL:(k + 1) * D_MODEL]


def _inproj_body(x_ref, p_ref, w_ref, z_ref, u_ref, *rest, L, row0, rstride, pad):
    h_scr, w_scr = rest[-2:]
    kv_refs = rest[:-2]
    j = pl.program_id(0)
    i = pl.program_id(1)
    nb = x_ref.shape[0] // L

    @pl.when(j == 0)
    def _():
        for s in range(nb):
            row = row0 + rstride * (i * nb + s)
            sh = _mod_row(p_ref, row, 0)
            sc = _mod_row(p_ref, row, 1)
            h_scr[i, s * L:(s + 1) * L, :] = (x_ref[s * L:(s + 1) * L, :] * (1.0 + sc) + sh).astype(BF16)

    @pl.when(i == 0)
    def _():
        w_scr[...] = w_ref[...].astype(BF16)

    acc = _dot(h_scr[i], w_scr[...])
    z_ref[...] = acc.astype(BF16)

    @pl.when(j == SU_TILE)
    def _():
        for s in range(nb):
            u_ref[:, (s * pad) * MIX_W:(s * pad + 1) * MIX_W] = acc[s * L:(s + 1) * L, :]
            for e in range(1, pad):
                u_ref[:, (s * pad + e) * MIX_W:(s * pad + e + 1) * MIX_W] = jnp.zeros((L, MIX_W), F32)

    for n, ref in enumerate(kv_refs):
        @pl.when(j == NK_TILE + n)
        def _(ref=ref):
            ref[...] = acc


def _inproj(x, p, w_in, *, layer, L, row0, rstride, pad, want_kv):
    T = x.shape[0]
    tm = TOKEN_TILE
    nb = tm // L
    n_i = T // tm
    slots = (T // L) * pad
    body = functools.partial(_inproj_body, L=L, row0=row0, rstride=rstride, pad=pad)
    n_kv = 2 if want_kv else 0

    def only_at(tile):
        return lambda j, i: jnp.where(j < tile, 0, jnp.where(j > tile, n_i - 1, i))

    su_i = only_at(SU_TILE)
    kv_i = [only_at(NK_TILE + n) for n in range(n_kv)]
    return pl.pallas_call(
        body,
        grid=(IN_COLS // COL_TILE, n_i),
        in_specs=[pl.BlockSpec((tm, D_MODEL), lambda j, i: (jnp.where(j == 0, i, n_i - 1), 0)),
                  pl.BlockSpec((None, N_PAD_ROWS, 6 * D_MODEL), lambda j, i: (layer, 0, 0)),
                  pl.BlockSpec((None, D_MODEL, COL_TILE), lambda j, i: (layer, 0, j))],
        out_specs=[pl.BlockSpec((tm, COL_TILE), lambda j, i: (i, j)),
                   pl.BlockSpec((L, nb * pad * MIX_W), lambda j, i: (0, su_i(j, i)))]
        + [pl.BlockSpec((tm, MIX_W), lambda j, i, f=f: (f(j, i), 0)) for f in kv_i],
        out_shape=[jax.ShapeDtypeStruct((T, IN_COLS), BF16),
                   jax.ShapeDtypeStruct((L, slots * MIX_W), F32)]
        + [jax.ShapeDtypeStruct((T, MIX_W), F32)] * n_kv,
        scratch_shapes=[pltpu.VMEM((n_i, tm, D_MODEL), BF16), pltpu.VMEM((D_MODEL, COL_TILE), BF16)],
        name="inproj",
        compiler_params=_params("arbitrary", "arbitrary"),
    )(x, p, w_in)


def _rope(x, cos, s_up, s_dn):
    return x * cos + pltpu.roll(x, 96, 1) * s_up + pltpu.roll(x, 32, 1) * s_dn


def _ret_body(*refs, n, rope, has_s0, want_state):
    refs = list(refs)
    lg_ref, q_ref, k_ref, v_ref, g_ref = refs[:5]
    refs = refs[5:]
    if rope:
        cos_ref, sup_ref, sdn_ref = refs[:3]
        refs = refs[3:]
    if has_s0:
        s0_ref = refs[0]
        refs = refs[1:]
    o_ref = refs[0]
    refs = refs[1:]
    if want_state:
        st_ref = refs[0]
        refs = refs[1:]
    q_scr, k_scr, sb_scr = refs

    C = RET_CHUNK
    h = pl.program_id(1)
    lf = lg_ref[0, h]
    lb = lg_ref[1, h]

    q = q_ref[...].astype(F32)
    k = k_ref[...].astype(F32)
    if rope:
        q = _rope(q, cos_ref[...], sup_ref[...], sdn_ref[...])
        k = _rope(k, cos_ref[...], sup_ref[...], sdn_ref[...])
    q_scr[...] = q
    k_scr[...] = k * (RET_DK ** -0.5)

    tcol = lax.broadcasted_iota(jnp.int32, (C, 1), 0).astype(F32)
    ti = lax.broadcasted_iota(jnp.int32, (C, C), 0)
    si = lax.broadcasted_iota(jnp.int32, (C, C), 1)
    dlt = (ti - si).astype(F32)
    decay = (jnp.where(dlt >= 0, jnp.exp(lf * jnp.maximum(dlt, 0.0)), 0.0)
             + jnp.where(dlt <= 0, jnp.exp(lb * jnp.maximum(-dlt, 0.0)), 0.0))
    qd_f = jnp.exp(lf * (tcol + 1.0))
    qd_b = jnp.exp(lb * (C - tcol))
    kd_f = jnp.exp(lf * (C - 1.0 - tcol))
    kd_b = jnp.exp(lb * tcol)
    cd_f = jnp.exp(lf * jnp.full((1, RET_DK), float(C), F32))
    cd_b = jnp.exp(lb * jnp.full((1, RET_DK), float(C), F32))

    def kv_outer(kc, vc, kd):
        return _dot((kc * kd).T.astype(BF16), vc)

    s_b = s0_ref[1] if has_s0 else jnp.zeros((RET_DK, RET_DK), F32)
    for i in reversed(range(n)):
        sb_scr[i] = s_b
        if i > 0 or want_state:
            s_b = s_b * cd_b + kv_outer(k_scr[i * C:(i + 1) * C, :], v_ref[i * C:(i + 1) * C, :], kd_b)

    s_f = s0_ref[0] if has_s0 else jnp.zeros((RET_DK, RET_DK), F32)
    for i in range(n):
        sl = slice(i * C, (i + 1) * C)
        qc = q_scr[sl, :]
        kc = k_scr[sl, :]
        vc = v_ref[sl, :]
        att = _dot_nt(qc.astype(BF16), kc.astype(BF16)) * decay
        o = _dot(att.astype(BF16), vc)
        o = o + _dot((qc * qd_f).astype(BF16), s_f.astype(BF16))
        o = o + _dot((qc * qd_b).astype(BF16), sb_scr[i].astype(BF16))
        mu = jnp.mean(o, -1, keepdims=True)
        oc = o - mu
        var = jnp.mean(oc * oc, -1, keepdims=True)
        gc = g_ref[sl, :].astype(F32)
        o_ref[sl, :] = (oc * lax.rsqrt(var + LN_EPS) * (gc * jax.nn.sigmoid(gc))).astype(BF16)
        if i < n - 1 or want_state:
            s_f = s_f * cd_f + kv_outer(kc, vc, kd_f)

    if want_state:
        st_ref[0] = s_f
        st_ref[1] = s_b


def _retention(z, log_gamma, *, B, L, rope_tabs=None, s0=None, layer=0, want_state):
    n = L // RET_CHUNK
    H = N_RET_HEADS
    nblk = MIX_W // RET_DK

    def sec(s):
        return pl.BlockSpec((None, L, RET_DK), lambda b, h: (b, 0, s * nblk + h))

    in_specs = [pl.BlockSpec(memory_space=pltpu.SMEM), sec(0), sec(1), sec(2), sec(3)]
    args = [log_gamma, z, z, z, z]
    if rope_tabs is not None:
        in_specs += [pl.BlockSpec((L, RET_DK), lambda b, h: (0, 0))] * 3
        args += list(rope_tabs)
    if s0 is not None:
        in_specs.append(pl.BlockSpec((None, None, 2, None, RET_DK, RET_DK), lambda b, h: (b, layer, 0, h, 0, 0)))
        args.append(s0)
    out_specs = [pl.BlockSpec((None, L, RET_DK), lambda b, h: (b, 0, h))]
    out_shape = [jax.ShapeDtypeStruct((B, L, MIX_W), BF16)]
    if want_state:
        out_specs.append(pl.BlockSpec((None, 2, None, RET_DK, RET_DK), lambda b, h: (b, 0, h, 0, 0)))
        out_shape.append(jax.ShapeDtypeStruct((B, 2, H, RET_DK, RET_DK), F32))
    body = functools.partial(_ret_body, n=n, rope=rope_tabs is not None, has_s0=s0 is not None,
                             want_state=want_state)
    return pl.pallas_call(
        body,
        grid=(B, H),
        in_specs=in_specs,
        out_specs=out_specs,
        out_shape=out_shape,
        scratch_shapes=[pltpu.VMEM((L, RET_DK), F32), pltpu.VMEM((L, RET_DK), F32),
                        pltpu.VMEM((n, RET_DK, RET_DK), F32)],
        name="retention",
        compiler_params=_params("arbitrary", "arbitrary"),
    )(*args)


def _rope_tables(L):
    pos = jnp.arange(L)
    row = (pos // GRID_W).astype(F32)
    col = (pos % GRID_W).astype(F32)
    quarter = RET_DK // 4
    inv_freq = ROPE_BASE ** (-jnp.arange(quarter, dtype=F32) / quarter)
    ang_r = row[:, None] * inv_freq[None, :]
    ang_c = col[:, None] * inv_freq[None, :]
    zero = jnp.zeros_like(ang_r)
    cos = jnp.concatenate([jnp.cos(ang_r), jnp.cos(ang_r), jnp.cos(ang_c), jnp.cos(ang_c)], -1)
    s_up = jnp.concatenate([-jnp.sin(ang_r), zero, -jnp.sin(ang_c), zero], -1)
    s_dn = jnp.concatenate([zero, jnp.sin(ang_r), zero, jnp.sin(ang_c)], -1)
    return cos, s_up, s_dn


def _s5_body(u_ref, wb_ref, wc_ref, a_ref, h0_ref, y_ref, fin_ref, w_scr, st_scr, *, slots, tc):
    d = pl.program_id(0)
    c = pl.program_id(2)
    lanes = 128
    nrt = slots // 8

    @pl.when(c == 0)
    def _():
        st_scr[...] = h0_ref[...]

    w_scr[...] = _dot(u_ref[...].astype(BF16), wb_ref[...])

    a_re = [jnp.broadcast_to(a_ref[0, :, q * lanes:(q + 1) * lanes], (8, lanes)) for q in range(S5_PAIRS)]
    a_im = [jnp.broadcast_to(a_ref[1, :, q * lanes:(q + 1) * lanes], (8, lanes)) for q in range(S5_PAIRS)]

    init = []
    for rt in range(nrt):
        for q in range(S5_PAIRS):
            init.append(st_scr[0, rt * 8:(rt + 1) * 8, q * lanes:(q + 1) * lanes])
            init.append(st_scr[1, rt * 8:(rt + 1) * 8, q * lanes:(q + 1) * lanes])

    def step(jj, carry):
        t = jj + d * (tc - 1 - 2 * jj)
        new = []
        for rt in range(nrt):
            r0 = pl.multiple_of(t * slots + rt * 8, 8)
            for q in range(S5_PAIRS):
                xr = carry[2 * (rt * S5_PAIRS + q)]
                xi = carry[2 * (rt * S5_PAIRS + q) + 1]
                cr = slice(2 * q * lanes, (2 * q + 1) * lanes)
                ci = slice((2 * q + 1) * lanes, (2 * q + 2) * lanes)
                nr = a_re[q] * xr - a_im[q] * xi + w_scr[pl.ds(r0, 8), cr]
                ni = a_re[q] * xi + a_im[q] * xr + w_scr[pl.ds(r0, 8), ci]
                w_scr[pl.ds(r0, 8), cr] = nr
                w_scr[pl.ds(r0, 8), ci] = ni
                new += [nr, ni]
        return tuple(new)

    fin = lax.fori_loop(0, tc, step, tuple(init), unroll=2)
    for rt in range(nrt):
        for q in range(S5_PAIRS):
            st_scr[0, rt * 8:(rt + 1) * 8, q * lanes:(q + 1) * lanes] = fin[2 * (rt * S5_PAIRS + q)]
            st_scr[1, rt * 8:(rt + 1) * 8, q * lanes:(q + 1) * lanes] = fin[2 * (rt * S5_PAIRS + q) + 1]

    y_ref[...] = _dot(w_scr[...].astype(BF16), wc_ref[...])

    @pl.when(c == pl.num_programs(2) - 1)
    def _():
        fin_ref[...] = st_scr[...]


def _s5(u, wb, wc, a_bar, h0, *, L, slots):
    tc = S5_ROWS // slots
    nT = L // tc
    npb = SSM_GROUPS // (2 * S5_PAIRS)
    sw = S5_PAIRS * 128
    gp = SSM_GROUPS * SSM_STATE

    def tci(d, c):
        return c + d * (nT - 1 - 2 * c)

    body = functools.partial(_s5_body, slots=slots, tc=tc)
    return pl.pallas_call(
        body,
        grid=(2, npb, nT),
        in_specs=[pl.BlockSpec((S5_ROWS, 128), lambda d, pb, c: (tci(d, c), pb)),
                  pl.BlockSpec((None, None, 128, 2 * sw), lambda d, pb, c: (d, pb, 0, 0)),
                  pl.BlockSpec((None, None, 2 * sw, 128), lambda d, pb, c: (d, pb, 0, 0)),
                  pl.BlockSpec((None, None, 2, 1, sw), lambda d, pb, c: (d, pb, 0, 0, 0)),
                  pl.BlockSpec((None, 2, slots, sw), lambda d, pb, c: (d, 0, 0, pb))],
        out_specs=[pl.BlockSpec((None, S5_ROWS, 128), lambda d, pb, c: (d, tci(d, c), pb)),
                   pl.BlockSpec((None, 2, slots, sw), lambda d, pb, c: (d, 0, 0, pb))],
        out_shape=[jax.ShapeDtypeStruct((2, L * slots, MIX_W), F32),
                   jax.ShapeDtypeStruct((2, 2, slots, gp), F32)],
        scratch_shapes=[pltpu.VMEM((S5_ROWS, 2 * sw), F32), pltpu.VMEM((2, slots, sw), F32)],
        name="s5",
        compiler_params=_params("arbitrary", "arbitrary", "arbitrary"),
    )(u, wb, wc, a_bar, h0)


def _s5_weights(a_re, a_im, log_dt, b_re, b_im, c_re, c_im):
    npb = SSM_GROUPS // (2 * S5_PAIRS)
    lr = jnp.minimum(a_re, -1e-4)
    li = a_im
    dt = jnp.exp(log_dt)[..., None]
    mag = jnp.exp(lr * dt)
    ar = mag * jnp.cos(li * dt)
    ai = mag * jnp.sin(li * dt)
    den = lr * lr + li * li
    sr = ((ar - 1.0) * lr + ai * li) / den
    si = (ai * lr - (ar - 1.0) * li) / den
    bbr = sr[..., None] * b_re[None] - si[..., None] * b_im[None]
    bbi = sr[..., None] * b_im[None] + si[..., None] * b_re[None]
    eye_q = jnp.eye(S5_PAIRS, dtype=F32)
    eye_e = jnp.eye(2, dtype=F32)
    shp_b = (2, 2, npb, S5_PAIRS, 2, SSM_STATE, SSM_GROUP)
    bval = jnp.stack([bbr, bbi], 0).reshape(shp_b)
    wb = jnp.einsum('tdbqepc,qQ,eE->dbqecQtEp', bval, eye_q, eye_e)
    wb = wb.reshape(2, npb, S5_PAIRS * 2 * SSM_GROUP, S5_PAIRS * 2 * 2 * SSM_STATE).astype(BF16)
    shp_c = (2, 2, npb, S5_PAIRS, 2, SSM_GROUP, SSM_STATE)
    cval = jnp.stack([c_re, -c_im], 0).reshape(shp_c)
    wc = jnp.einsum('tdbqeop,qQ,eE->dbQtEpqeo', cval, eye_q, eye_e)
    wc = wc.reshape(2, npb, S5_PAIRS * 2 * 2 * SSM_STATE, S5_PAIRS * 2 * SSM_GROUP).astype(BF16)
    sw = S5_PAIRS * 2 * SSM_STATE
    a_bar = jnp.stack([ar.reshape(2, npb, 1, sw), ai.reshape(2, npb, 1, sw)], 2)
    return wb, wc, a_bar


def _head_masks(shape):
    lane = lax.broadcasted_iota(jnp.int32, shape, 1)
    return lane < NA_HEAD_DIM


def _cattn_body(q_ref, k_ref, v_ref, o_ref):
    first = _head_masks(q_ref.shape[1:])
    for bb in range(q_ref.shape[0]):
        q = q_ref[bb]
        k = k_ref[bb]
        v = v_ref[bb]
        outs = []
        for e in range(2):
            qe = jnp.where(first if e == 0 else jnp.logical_not(first), q, jnp.zeros_like(q))
            s = _dot_nt(qe, k) * (NA_HEAD_DIM ** -0.5)
            m = jnp.max(s, -1, keepdims=True)
            p = jnp.exp(s - m)
            l = jnp.sum(p, -1, keepdims=True)
            outs.append(_dot(p.astype(BF16), v) / l)
        o_ref[bb] = jnp.where(first, outs[0], outs[1]).astype(BF16)


def _context_attention(z, *, B, L):
    nblk = MIX_W // 128
    nb = CTX_ATTN_BATCH

    def sec(s):
        return pl.BlockSpec((nb, L, 128), lambda b, hp: (b, 0, s * nblk + hp))

    return pl.pallas_call(
        _cattn_body,
        grid=(B // nb, nblk),
        in_specs=[sec(5), sec(6), sec(7)],
        out_specs=pl.BlockSpec((nb, L, 128), lambda b, hp: (b, 0, hp)),
        out_shape=jax.ShapeDtypeStruct((B, L, MIX_W), BF16),
        name="ctx_attention",
        compiler_params=_params("arbitrary", "arbitrary"),
    )(z, z, z)


def _na_chunks(rows):
    half = NA_KR // 2
    plan, kinds = [], []
    for r0 in range(0, rows, NA_CHUNK_ROWS):
        rs = [min(max(r - half, 0), rows - NA_KR) for r in range(r0, r0 + NA_CHUNK_ROWS)]
        ws = min(rs[0], rows - NA_WIN_ROWS)
        assert rs[-1] + NA_KR <= ws + NA_WIN_ROWS
        kind = tuple((r0 + n - ws, rs[n] - ws) for n in range(NA_CHUNK_ROWS))
        if kind not in kinds:
            kinds.append(kind)
        plan.append((ws, kinds.index(kind)))
    return plan, kinds


def _na_body(q_ref, k_ref, v_ref, kc_ref, vc_ref, bias_ref, o_ref, *, rows):
    scale = NA_HEAD_DIM ** -0.5
    nq = NA_CHUNK_ROWS * GRID_W
    kctx = kc_ref[...].astype(BF16)
    vctx = vc_ref[...].astype(BF16)
    first = _head_masks((nq, 128))
    plan, _ = _na_chunks(rows)
    for c, (ws, kind) in enumerate(plan):
        qc = q_ref[c * nq:(c + 1) * nq, :]
        qs = jnp.concatenate([jnp.where(first, qc, jnp.zeros_like(qc)),
                              jnp.where(first, jnp.zeros_like(qc), qc)], 0)
        kw = k_ref[ws * GRID_W:(ws + NA_WIN_ROWS) * GRID_W, :]
        vw = v_ref[ws * GRID_W:(ws + NA_WIN_ROWS) * GRID_W, :]
        s_loc = _dot_nt(qs, kw) * scale + bias_ref[kind]
        s_ctx = _dot_nt(qs, kctx) * scale
        m = jnp.maximum(jnp.max(s_loc, -1, keepdims=True), jnp.max(s_ctx, -1, keepdims=True))
        p_loc = jnp.exp(s_loc - m)
        p_ctx = jnp.exp(s_ctx - m)
        l = jnp.sum(p_loc, -1, keepdims=True) + jnp.sum(p_ctx, -1, keepdims=True)
        o = (_dot(p_loc.astype(BF16), vw) + _dot(p_ctx.astype(BF16), vctx)) / l
        o_ref[c * nq:(c + 1) * nq, :] = jnp.where(first, o[:nq], o[nq:]).astype(BF16)


def _neighbourhood_attention(z, cache_k, cache_v, bias, *, B, L, layer):
    nblk = MIX_W // 128
    rows = L // GRID_W
    Lc = cache_k.shape[2]
    nkind = bias.shape[1]

    def sec(s):
        return pl.BlockSpec((None, L, 128), lambda hp, b: (b, 0, s * nblk + hp))

    ctx = pl.BlockSpec((None, None, Lc, 128), lambda hp, b: (b, layer, 0, hp))
    return pl.pallas_call(
        functools.partial(_na_body, rows=rows),
        grid=(nblk, B),
        in_specs=[sec(5), sec(6), sec(7), ctx, ctx,
                  pl.BlockSpec((None, nkind, 2 * NA_CHUNK_ROWS * GRID_W, NA_WIN_ROWS * GRID_W),
                               lambda hp, b: (hp, 0, 0, 0))],
        out_specs=pl.BlockSpec((None, L, 128), lambda hp, b: (b, 0, hp)),
        out_shape=jax.ShapeDtypeStruct((B, L, MIX_W), BF16),
        name="nbr_attention",
        compiler_params=_params("arbitrary", "arbitrary"),
    )(z, z, z, cache_k, cache_v, bias)


def _na_bias_table(rpb, rows):
    _, kinds = _na_chunks(rows)
    nr, nc = 2 * NA_KR - 1, 2 * NA_KW - 1
    sel_r = np.zeros((len(kinds), NA_CHUNK_ROWS, NA_WIN_ROWS, nr), np.float32)
    for t, kind in enumerate(kinds):
        for n, (r_rel, rs_rel) in enumerate(kind):
            for kj in range(rs_rel, rs_rel + NA_KR):
                sel_r[t, n, kj, kj - r_rel + NA_KR - 1] = 1.0
    row_ok = sel_r.sum(-1) > 0
    qc = np.arange(GRID_W)
    kc = np.arange(GRID_W)
    ws = np.clip(qc - NA_KW // 2, 0, GRID_W - NA_KW)
    col_ok = (kc[None, :] >= ws[:, None]) & (kc[None, :] < ws[:, None] + NA_KW)
    coff = np.clip(kc[None, :] - qc[:, None] + NA_KW - 1, 0, nc - 1)
    sel_c = ((coff[None] == np.arange(nc)[:, None, None]) & col_ok[None]).astype(np.float32)
    H = rpb.shape[0]
    b = jnp.einsum('xerc,tnjr,cqk->xtenqjk', rpb.astype(F32).reshape(H // 2, 2, nr, nc), sel_r, sel_c,
                   precision=lax.Precision.HIGHEST)
    ok = row_ok[None, :, None, :, None, :, None] & col_ok[None, None, None, None, :, None, :]
    b = jnp.where(ok, b, NEG_INF)
    return b.reshape(H // 2, len(kinds), 2 * NA_CHUNK_ROWS * GRID_W, NA_WIN_ROWS * GRID_W)


def _merge_body(x_ref, p_ref, r_ref, u_ref, yf_ref, yb_ref, n_ref, ga_ref, gb_ref, gc_ref,
                d_ref, wglu_ref, wbr_ref, wo_ref, lg_ref, lb_ref, o_ref,
                wglu_s, wbr_s, wo_s, *, L, row0, rstride):
    i = pl.program_id(0)
    tm = x_ref.shape[0]

    @pl.when(i == 0)
    def _():
        wglu_s[...] = wglu_ref[...].astype(BF16)
        wbr_s[...] = wbr_ref[...].astype(BF16)
        wo_s[...] = wo_ref[...].astype(BF16)

    row = row0 + rstride * ((i * tm) // L)
    g1 = _mod_row(p_ref, row, 2)

    y = d_ref[...] * u_ref[...].astype(F32) + yf_ref[...] + yb_ref[...]
    y = jax.nn.gelu(y)
    s_out = y * jax.nn.sigmoid(_dot(y.astype(BF16), wglu_s[...]))

    def gate(ref):
        return jax.nn.sigmoid(ref[...].astype(F32))

    merged = (gate(ga_ref) * _dot(r_ref[...], wbr_s[0])
              + gate(gb_ref) * _dot(s_out.astype(BF16), wbr_s[1])
              + gate(gc_ref) * _dot(n_ref[...], wbr_s[2]))
    m = _dot(merged.astype(BF16), wo_s[...])
    o_ref[...] = _layer_norm(DEEPNORM_ALPHA * x_ref[...] + g1 * m, lg_ref[...], lb_ref[...])


def _merge(x, p, z, r_out, y, n_out, ssm_d, w_glu, w_branch, w_o, ln_g, ln_b, *, layer, L, row0, rstride, pad):
    T = x.shape[0]
    tm = MERGE_TILE
    nt = L // tm
    gate0 = 8 * MIX_W // D_MODEL

    def tok(w):
        return pl.BlockSpec((tm, w), lambda i: (i, 0))

    def ysec(d):
        return pl.BlockSpec((None, tm, MIX_W), lambda i: (d, i % nt, (i // nt) * pad))

    def full(shape):
        return pl.BlockSpec((None,) + shape, lambda i: (layer,) + (0,) * len(shape))

    body = functools.partial(_merge_body, L=L, row0=row0, rstride=rstride)
    return pl.pallas_call(
        body,
        grid=(T // tm,),
        in_specs=[tok(D_MODEL), full((N_PAD_ROWS, 6 * D_MODEL)), tok(MIX_W),
                  pl.BlockSpec((tm, MIX_W), lambda i: (i, SU_TILE)), ysec(0), ysec(1), tok(MIX_W),
                  pl.BlockSpec((tm, D_MODEL), lambda i: (i, gate0)),
                  pl.BlockSpec((tm, D_MODEL), lambda i: (i, gate0 + 1)),
                  pl.BlockSpec((tm, D_MODEL), lambda i: (i, gate0 + 2)),
                  full((1, MIX_W)), full((MIX_W, MIX_W)), full((3, MIX_W, D_MODEL)),
                  full((D_MODEL, D_MODEL)), full((1, D_MODEL)), full((1, D_MODEL))],
        out_specs=tok(D_MODEL),
        out_shape=jax.ShapeDtypeStruct((T, D_MODEL), F32),
        scratch_shapes=[pltpu.VMEM((MIX_W, MIX_W), BF16), pltpu.VMEM((3, MIX_W, D_MODEL), BF16),
                        pltpu.VMEM((D_MODEL, D_MODEL), BF16)],
        name="merge",
        compiler_params=_params("arbitrary"),
    )(x, p, r_out, z, y, y, n_out, z, z, z, ssm_d.reshape(DEPTH, 1, MIX_W), w_glu, w_branch, w_o,
      ln_g.reshape(DEPTH, 1, D_MODEL), ln_b.reshape(DEPTH, 1, D_MODEL))


def _ffn_body(x_ref, p_ref, wa_ref, wb_ref, cwa_ref, cwb_ref, cba_ref, cbb_ref, wd_ref, lg_ref, lb_ref,
              o_ref, h_scr, acc_scr, *, L, row0, rstride):
    i = pl.program_id(0)
    j = pl.program_id(1)
    tm = x_ref.shape[0]
    nb = tm // L

    @pl.when(j == 0)
    def _():
        for s in range(nb):
            row = row0 + rstride * (i * nb + s)
            sh = _mod_row(p_ref, row, 3)
            sc = _mod_row(p_ref, row, 4)
            h_scr[s * L:(s + 1) * L, :] = (x_ref[s * L:(s + 1) * L, :] * (1.0 + sc) + sh).astype(BF16)
        acc_scr[...] = jnp.zeros_like(acc_scr)

    t = lax.broadcasted_iota(jnp.int32, (tm, 1), 0) % L
    has_prev = t != 0
    has_next = t != L - 1

    def conv(w_ref, cw_ref, cb_ref):
        zc = _dot(h_scr[...], w_ref[...].astype(BF16))
        zp = jnp.where(has_prev, pltpu.roll(zc, 1, 0), 0.0)
        zn = jnp.where(has_next, pltpu.roll(zc, tm - 1, 0), 0.0)
        return zp * cw_ref[0:1, :] + zc * cw_ref[1:2, :] + zn * cw_ref[2:3, :] + cb_ref[...]

    a = conv(wa_ref, cwa_ref, cba_ref)
    b = conv(wb_ref, cwb_ref, cbb_ref)
    acc_scr[...] += _dot((jax.nn.gelu(a) * b).astype(BF16), wd_ref[...].astype(BF16))

    @pl.when(j == pl.num_programs(1) - 1)
    def _():
        for s in range(nb):
            row = row0 + rstride * (i * nb + s)
            g2 = _mod_row(p_ref, row, 5)
            sl = slice(s * L, (s + 1) * L)
            o_ref[sl, :] = _layer_norm(DEEPNORM_ALPHA * x_ref[sl, :] + g2 * acc_scr[sl, :],
                                       lg_ref[...], lb_ref[...])


def _conv_ffn(x, p, w_up, conv_w, conv_b, w_down, ln_g, ln_b, *, layer, L, row0, rstride):
    T = x.shape[0]
    tm = TOKEN_TILE
    nff = D_FF // FF_TILE
    body = functools.partial(_ffn_body, L=L, row0=row0, rstride=rstride)
    conv_b = conv_b.reshape(DEPTH, 1, 2 * D_FF)
    return pl.pallas_call(
        body,
        grid=(T // tm, nff),
        in_specs=[pl.BlockSpec((tm, D_MODEL), lambda i, j: (i, 0)),
                  pl.BlockSpec((None, N_PAD_ROWS, 6 * D_MODEL), lambda i, j: (layer, 0, 0)),
                  pl.BlockSpec((None, D_MODEL, FF_TILE), lambda i, j: (layer, 0, j)),
                  pl.BlockSpec((None, D_MODEL, FF_TILE), lambda i, j: (layer, 0, nff + j)),
                  pl.BlockSpec((None, 3, FF_TILE), lambda i, j: (layer, 0, j)),
                  pl.BlockSpec((None, 3, FF_TILE), lambda i, j: (layer, 0, nff + j)),
                  pl.BlockSpec((None, 1, FF_TILE), lambda i, j: (layer, 0, j)),
                  pl.BlockSpec((None, 1, FF_TILE), lambda i, j: (layer, 0, nff + j)),
                  pl.BlockSpec((None, FF_TILE, D_MODEL), lambda i, j: (layer, j, 0)),
                  pl.BlockSpec((None, 1, D_MODEL), lambda i, j: (layer, 0, 0)),
                  pl.BlockSpec((None, 1, D_MODEL), lambda i, j: (layer, 0, 0))],
        out_specs=pl.BlockSpec((tm, D_MODEL), lambda i, j: (i, 0)),
        out_shape=jax.ShapeDtypeStruct((T, D_MODEL), F32),
        scratch_shapes=[pltpu.VMEM((tm, D_MODEL), BF16), pltpu.VMEM((tm, D_MODEL), F32)],
        name="conv_ffn",
        compiler_params=_params("arbitrary", "arbitrary"),
    )(x, p, w_up, w_up, conv_w, conv_w, conv_b, conv_b, w_down, ln_g.reshape(DEPTH, 1, D_MODEL),
      ln_b.reshape(DEPTH, 1, D_MODEL))


def _layer(x, p, lw, *, B, L, row0, rstride, pad, latent, layer, extra):
    T = B * L
    slots = B * pad
    z, u, *kv = _inproj(x, p, lw['w_in'], layer=layer, L=L, row0=row0, rstride=rstride, pad=pad,
                        want_kv=not latent)
    z3 = z.reshape(B, L, IN_COLS)
    log_gamma = jax.nn.log_sigmoid(lw['ret_decay'].astype(F32))
    wb, wc, a_bar = lw['s5']
    gp = SSM_GROUPS * SSM_STATE
    if latent:
        r_out = _retention(z3, log_gamma, B=B, L=L, rope_tabs=extra['rope'], s0=extra['state_ret'],
                           layer=layer, want_state=False)[0]
        h0 = jnp.transpose(extra['state_ssm'][:, layer], (1, 4, 0, 2, 3)).reshape(2, 2, B, gp)
        h0 = jnp.stack([h0] + [jnp.zeros_like(h0)] * (pad - 1), 3).reshape(2, 2, slots, gp)
        n_out = _neighbourhood_attention(z3, extra['cache_k'], extra['cache_v'], extra['bias'][layer],
                                         B=B, L=L, layer=layer)
        states = None
    else:
        r_out, ret_state = _retention(z3, log_gamma, B=B, L=L, want_state=True)
        h0 = jnp.zeros((2, 2, slots, gp), F32)
        n_out = _context_attention(z3, B=B, L=L)
    y, fin = _s5(u.reshape(L * slots, MIX_W), wb, wc, a_bar, h0, L=L, slots=slots)
    if not latent:
        ssm_state = jnp.transpose(fin.reshape(2, 2, B, SSM_GROUPS, SSM_STATE), (2, 0, 3, 4, 1))
        nk = kv[0].reshape(B, L, NA_HEADS, NA_HEAD_DIM)
        nv = kv[1].reshape(B, L, NA_HEADS, NA_HEAD_DIM)
        states = (ret_state, ssm_state, nk, nv)
    x = _merge(x, p, z, r_out.reshape(T, MIX_W), y.reshape(2, L, slots * MIX_W), n_out.reshape(T, MIX_W),
               lw['ssm_d'], lw['ssm_w_glu'], lw['w_branch'], lw['w_o'], lw['ln1_g'], lw['ln1_b'],
               layer=layer, L=L, row0=row0, rstride=rstride, pad=pad)
    x = _conv_ffn(x, p, lw['w_up'], lw['conv_w'], lw['conv_b'], lw['w_down'], lw['ln2_g'], lw['ln2_b'],
                  layer=layer, L=L, row0=row0, rstride=rstride)
    return x, states


def kernel(x_prompt, x_sample, state_ret, state_ssm, cache_na_k, cache_na_v, c, c_ctx, w_ada, b_ada, w_in, ret_decay, ssm_a_re, ssm_a_im, ssm_log_dt, ssm_b_re, ssm_b_im, ssm_c_re, ssm_c_im, ssm_d, ssm_w_glu, na_rpb, w_branch, w_o, ln1_g, ln1_b, w_up, conv_w, conv_b, w_down, ln2_g, ln2_b):
    B, L, _ = x_prompt.shape
    Bd, Ld, _ = x_sample.shape
    Lc = cache_na_k.shape[2]

    cond = jnp.concatenate([c_ctx[None, :], c, jnp.zeros((N_PAD_ROWS - 1 - Bd, D_MODEL), F32)], 0)
    p_all = _ada(cond, w_ada, b_ada)

    extra = dict(rope=_rope_tables(Ld), state_ret=state_ret, state_ssm=state_ssm,
                 cache_k=cache_na_k.reshape(Bd, DEPTH, Lc, MIX_W),
                 cache_v=cache_na_v.reshape(Bd, DEPTH, Lc, MIX_W),
                 bias=[_na_bias_table(na_rpb[l], Ld // GRID_W) for l in range(DEPTH)])

    xp = x_prompt.reshape(B * L, D_MODEL)
    xs = x_sample.reshape(Bd * Ld, D_MODEL)
    ret_states, ssm_states, na_ks, na_vs = [], [], [], []
    for l in range(DEPTH):
        lw = dict(w_in=w_in, ret_decay=ret_decay[l], ssm_d=ssm_d, ssm_w_glu=ssm_w_glu,
                  w_branch=w_branch, w_o=w_o, ln1_g=ln1_g, ln1_b=ln1_b, w_up=w_up,
                  conv_w=conv_w, conv_b=conv_b, w_down=w_down, ln2_g=ln2_g, ln2_b=ln2_b,
                  s5=_s5_weights(ssm_a_re[l], ssm_a_im[l], ssm_log_dt[l], ssm_b_re[l], ssm_b_im[l],
                                 ssm_c_re[l], ssm_c_im[l]))
        xp, (s_ret, s_ssm, nk, nv) = _layer(xp, p_all, lw, B=B, L=L, row0=0, rstride=0, pad=1,
                                            latent=False, layer=l, extra=None)
        ret_states.append(s_ret)
        ssm_states.append(s_ssm)
        na_ks.append(nk)
        na_vs.append(nv)
        xs, _ = _layer(xs, p_all, lw, B=Bd, L=Ld, row0=1, rstride=1, pad=N_PAD_ROWS // Bd,
                       latent=True, layer=l, extra=extra)
    return (xp.reshape(B, L, D_MODEL), xs.reshape(Bd, Ld, D_MODEL),
            jnp.stack(ret_states, 1), jnp.stack(ssm_states, 1), jnp.stack(na_ks, 1), jnp.stack(na_vs, 1))
```

```python
import functools

import jax
import jax.numpy as jnp
import numpy as np
from jax import lax
from jax.experimental import pallas as pl
from jax.experimental.pallas import tpu as pltpu

F32 = jnp.float32
BF16 = jnp.bfloat16

D_MODEL = 1024
DEPTH = 2
GRID_W = 64
MIX_W = D_MODEL // 2
N_RET_HEADS = 4
RET_DK = MIX_W // N_RET_HEADS
SSM_GROUP = 16
SSM_GROUPS = MIX_W // SSM_GROUP
SSM_STATE = 64
NA_HEADS = 8
NA_HEAD_DIM = MIX_W // NA_HEADS
NA_KR = 8
NA_KW = 16
D_FF = ((8 * D_MODEL // 3 + 127) // 128) * 128
ROPE_BASE = 10000.0
LN_EPS = 1e-5
NEG_INF = -1e30
DEEPNORM_ALPHA = (2 * DEPTH) ** 0.25
IN_COLS = 8 * MIX_W + 3 * D_MODEL

VMEM_LIMIT_BYTES = 56 * 1024 * 1024
LANES = 128

TOKEN_TILE = 1024
MERGE_TILE = 256
COL_TILE = 512
SU_SECTION = 4
NK_TILE = 6
FF_TILE = 256
RET_CHUNK = 256
S5_CHUNK = 8
S5_PITCH_PAD = 8
S5_LBLK_GROUPS = LANES // SSM_GROUP
N_PAD_ROWS = 8
CTX_ATTN_BATCH = 4
NA_CHUNK_ROWS = 4
NA_WIN_ROWS = 12


def _params(*sem):
    return pltpu.CompilerParams(dimension_semantics=sem, vmem_limit_bytes=VMEM_LIMIT_BYTES)


def _dot(a, b):
    return jnp.dot(a, b, preferred_element_type=F32)


def _dot_nt(a, b):
    return lax.dot_general(a, b, (((1,), (1,)), ((), ())), preferred_element_type=F32)


def _layer_norm(x, g, b):
    mu = jnp.mean(x, -1, keepdims=True)
    xc = x - mu
    var = jnp.mean(xc * xc, -1, keepdims=True)
    return xc * lax.rsqrt(var + LN_EPS) * g + b


def _ada_body(c_ref, w_ref, b_ref, o_ref):
    c = c_ref[...]
    s = c * jax.nn.sigmoid(c)
    o_ref[...] = _dot(s.astype(BF16), w_ref[...].astype(BF16)) + b_ref[...]


def _ada(cond, w_ada, b_ada):
    tn = 1024
    return pl.pallas_call(
        _ada_body,
        grid=(DEPTH, 6 * D_MODEL // tn),
        in_specs=[pl.BlockSpec((N_PAD_ROWS, D_MODEL), lambda l, j: (0, 0)),
                  pl.BlockSpec((None, D_MODEL, tn), lambda l, j: (l, 0, j)),
                  pl.BlockSpec((None, 1, tn), lambda l, j: (l, 0, j))],
        out_specs=pl.BlockSpec((None, N_PAD_ROWS, tn), lambda l, j: (l, 0, j)),
        out_shape=jax.ShapeDtypeStruct((DEPTH, N_PAD_ROWS, 6 * D_MODEL), F32),
        name="ada",
        compiler_params=_params("arbitrary", "arbitrary"),
    )(cond, w_ada, b_ada.reshape(DEPTH, 1, 6 * D_MODEL))


def _mod_row(p_ref, row, k):
    return p_ref[pl.ds(row, 1), k * D_MODEL:(k + 1) * D_MODEL]


def _inproj_body(x_ref, p_ref, w_ref, z_ref, *rest, L, row0, rstride):
    h_scr, w_scr = rest[-2:]
    kv_refs = rest[:-2]
    j = pl.program_id(0)
    i = pl.program_id(1)
    nb = x_ref.shape[0] // L

    @pl.when(j == 0)
    def _():
        for s in range(nb):
            row = row0 + rstride * (i * nb + s)
            sh = _mod_row(p_ref, row, 0)
            sc = _mod_row(p_ref, row, 1)
            h_scr[i, s * L:(s + 1) * L, :] = (x_ref[s * L:(s + 1) * L, :] * (1.0 + sc) + sh).astype(BF16)

    @pl.when(i == 0)
    def _():
        w_scr[...] = w_ref[...].astype(BF16)

    acc = _dot(h_scr[i], w_scr[...])
    z_ref[...] = acc.astype(BF16)

    for n, ref in enumerate(kv_refs):
        @pl.when(j == NK_TILE + n)
        def _(ref=ref):
            ref[...] = acc


def _inproj(x, p, w_in, *, layer, L, row0, rstride, want_kv):
    T = x.shape[0]
    tm = TOKEN_TILE
    n_i = T // tm
    body = functools.partial(_inproj_body, L=L, row0=row0, rstride=rstride)
    n_kv = 2 if want_kv else 0

    def only_at(tile):
        return lambda j, i: jnp.where(j < tile, 0, jnp.where(j > tile, n_i - 1, i))

    kv_i = [only_at(NK_TILE + n) for n in range(n_kv)]
    return pl.pallas_call(
        body,
        grid=(IN_COLS // COL_TILE, n_i),
        in_specs=[pl.BlockSpec((tm, D_MODEL), lambda j, i: (jnp.where(j == 0, i, n_i - 1), 0)),
                  pl.BlockSpec((None, N_PAD_ROWS, 6 * D_MODEL), lambda j, i: (layer, 0, 0)),
                  pl.BlockSpec((None, D_MODEL, COL_TILE), lambda j, i: (layer, 0, j))],
        out_specs=[pl.BlockSpec((tm, COL_TILE), lambda j, i: (i, j))]
        + [pl.BlockSpec((tm, MIX_W), lambda j, i, f=f: (f(j, i), 0)) for f in kv_i],
        out_shape=[jax.ShapeDtypeStruct((T, IN_COLS), BF16)] + [jax.ShapeDtypeStruct((T, MIX_W), F32)] * n_kv,
        scratch_shapes=[pltpu.VMEM((n_i, tm, D_MODEL), BF16), pltpu.VMEM((D_MODEL, COL_TILE), BF16)],
        name="inproj",
        compiler_params=_params("arbitrary", "arbitrary"),
    )(x, p, w_in)


def _rope(x, cos, s_up, s_dn):
    return x * cos + pltpu.roll(x, 96, 1) * s_up + pltpu.roll(x, 32, 1) * s_dn


def _ret_body(*refs, n, rope, has_s0, want_state):
    refs = list(refs)
    lg_ref, q_ref, k_ref, v_ref, g_ref = refs[:5]
    refs = refs[5:]
    if rope:
        cos_ref, sup_ref, sdn_ref = refs[:3]
        refs = refs[3:]
    if has_s0:
        s0_ref = refs[0]
        refs = refs[1:]
    o_ref = refs[0]
    refs = refs[1:]
    if want_state:
        st_ref = refs[0]
        refs = refs[1:]
    q_scr, k_scr, sb_scr = refs

    C = RET_CHUNK
    h = pl.program_id(1)
    lf = lg_ref[0, h]
    lb = lg_ref[1, h]

    q = q_ref[...].astype(F32)
    k = k_ref[...].astype(F32)
    if rope:
        q = _rope(q, cos_ref[...], sup_ref[...], sdn_ref[...])
        k = _rope(k, cos_ref[...], sup_ref[...], sdn_ref[...])
    q_scr[...] = q
    k_scr[...] = k * (RET_DK ** -0.5)

    tcol = lax.broadcasted_iota(jnp.int32, (C, 1), 0).astype(F32)
    ti = lax.broadcasted_iota(jnp.int32, (C, C), 0)
    si = lax.broadcasted_iota(jnp.int32, (C, C), 1)
    dlt = (ti - si).astype(F32)
    decay = (jnp.where(dlt >= 0, jnp.exp(lf * jnp.maximum(dlt, 0.0)), 0.0)
             + jnp.where(dlt <= 0, jnp.exp(lb * jnp.maximum(-dlt, 0.0)), 0.0))
    qd_f = jnp.exp(lf * (tcol + 1.0))
    qd_b = jnp.exp(lb * (C - tcol))
    kd_f = jnp.exp(lf * (C - 1.0 - tcol))
    kd_b = jnp.exp(lb * tcol)
    cd_f = jnp.exp(lf * jnp.full((1, RET_DK), float(C), F32))
    cd_b = jnp.exp(lb * jnp.full((1, RET_DK), float(C), F32))

    def kv_outer(kc, vc, kd):
        return _dot((kc * kd).T.astype(BF16), vc)

    s_b = s0_ref[1] if has_s0 else jnp.zeros((RET_DK, RET_DK), F32)
    for i in reversed(range(n)):
        sb_scr[i] = s_b
        if i > 0 or want_state:
            s_b = s_b * cd_b + kv_outer(k_scr[i * C:(i + 1) * C, :], v_ref[i * C:(i + 1) * C, :], kd_b)

    s_f = s0_ref[0] if has_s0 else jnp.zeros((RET_DK, RET_DK), F32)
    for i in range(n):
        sl = slice(i * C, (i + 1) * C)
        qc = q_scr[sl, :]
        kc = k_scr[sl, :]
        vc = v_ref[sl, :]
        att = _dot_nt(qc.astype(BF16), kc.astype(BF16)) * decay
        o = _dot(att.astype(BF16), vc)
        o = o + _dot((qc * qd_f).astype(BF16), s_f.astype(BF16))
        o = o + _dot((qc * qd_b).astype(BF16), sb_scr[i].astype(BF16))
        mu = jnp.mean(o, -1, keepdims=True)
        oc = o - mu
        var = jnp.mean(oc * oc, -1, keepdims=True)
        gc = g_ref[sl, :].astype(F32)
        o_ref[sl, :] = (oc * lax.rsqrt(var + LN_EPS) * (gc * jax.nn.sigmoid(gc))).astype(BF16)
        if i < n - 1 or want_state:
            s_f = s_f * cd_f + kv_outer(kc, vc, kd_f)

    if want_state:
        st_ref[0] = s_f
        st_ref[1] = s_b


def _retention(z, log_gamma, *, B, L, rope_tabs=None, s0=None, layer=0, want_state):
    n = L // RET_CHUNK
    H = N_RET_HEADS
    nblk = MIX_W // RET_DK

    def sec(s):
        return pl.BlockSpec((None, L, RET_DK), lambda b, h: (b, 0, s * nblk + h))

    in_specs = [pl.BlockSpec(memory_space=pltpu.SMEM), sec(0), sec(1), sec(2), sec(3)]
    args = [log_gamma, z, z, z, z]
    if rope_tabs is not None:
        in_specs += [pl.BlockSpec((L, RET_DK), lambda b, h: (0, 0))] * 3
        args += list(rope_tabs)
    if s0 is not None:
        in_specs.append(pl.BlockSpec((None, None, 2, None, RET_DK, RET_DK), lambda b, h: (b, layer, 0, h, 0, 0)))
        args.append(s0)
    out_specs = [pl.BlockSpec((None, L, RET_DK), lambda b, h: (b, 0, h))]
    out_shape = [jax.ShapeDtypeStruct((B, L, MIX_W), BF16)]
    if want_state:
        out_specs.append(pl.BlockSpec((None, 2, None, RET_DK, RET_DK), lambda b, h: (b, 0, h, 0, 0)))
        out_shape.append(jax.ShapeDtypeStruct((B, 2, H, RET_DK, RET_DK), F32))
    body = functools.partial(_ret_body, n=n, rope=rope_tabs is not None, has_s0=s0 is not None,
                             want_state=want_state)
    return pl.pallas_call(
        body,
        grid=(B, H),
        in_specs=in_specs,
        out_specs=out_specs,
        out_shape=out_shape,
        scratch_shapes=[pltpu.VMEM((L, RET_DK), F32), pltpu.VMEM((L, RET_DK), F32),
                        pltpu.VMEM((n, RET_DK, RET_DK), F32)],
        name="retention",
        compiler_params=_params("arbitrary", "arbitrary"),
    )(*args)


def _rope_tables(L):
    pos = jnp.arange(L)
    row = (pos // GRID_W).astype(F32)
    col = (pos % GRID_W).astype(F32)
    quarter = RET_DK // 4
    inv_freq = ROPE_BASE ** (-jnp.arange(quarter, dtype=F32) / quarter)
    ang_r = row[:, None] * inv_freq[None, :]
    ang_c = col[:, None] * inv_freq[None, :]
    zero = jnp.zeros_like(ang_r)
    cos = jnp.concatenate([jnp.cos(ang_r), jnp.cos(ang_r), jnp.cos(ang_c), jnp.cos(ang_c)], -1)
    s_up = jnp.concatenate([-jnp.sin(ang_r), zero, -jnp.sin(ang_c), zero], -1)
    s_dn = jnp.concatenate([zero, jnp.sin(ang_r), zero, jnp.sin(ang_c)], -1)
    return cos, s_up, s_dn


def _s5_body(*refs, B, nC, has_h0):
    refs = list(refs)
    u_ref, dk_ref, bs_ref, cp_ref, a8_ref = refs[:5]
    refs = refs[5:]
    if has_h0:
        h0_ref = refs[0]
        refs = refs[1:]
    y_ref, fin_ref, u_scr, a_scr, m_scr, s_scr, x_scr, y_scr = refs

    TC = S5_CHUNK
    R = B * nC
    P = nC + S5_PITCH_PAD
    nsl = s_scr.shape[0]
    half = nsl // 2

    u_scr[...] = u_ref[...].astype(F32)
    for s in range(TC):
        a_scr[:, s * LANES:(s + 1) * LANES] = u_scr[pl.ds(s, R, stride=TC), :].astype(BF16)

    for d in range(2):
        for s in range(TC):
            for t in range(TC):
                k = (t - s) if d == 0 else (s - t)
                blk = dk_ref[d, k] if k >= 0 else jnp.zeros((LANES, LANES), BF16)
                m_scr[s * LANES:(s + 1) * LANES, t * LANES:(t + 1) * LANES] = blk

        a = a_scr[...]
        yd = _dot(a, m_scr[...])
        if d == 0:
            y_scr[...] = yd
        else:
            y_scr[...] += yd

        sm = _dot(a, bs_ref[d])
        for b in range(B):
            for sl in range(nsl):
                s_scr[sl, b * P:b * P + nC, :] = sm[b * nC:(b + 1) * nC, sl * LANES:(sl + 1) * LANES]

        a_r = [jnp.broadcast_to(a8_ref[d, 0, :, q * LANES:(q + 1) * LANES], (B, LANES)) for q in range(half)]
        a_i = [jnp.broadcast_to(a8_ref[d, 1, :, q * LANES:(q + 1) * LANES], (B, LANES)) for q in range(half)]
        if has_h0:
            init = tuple(h0_ref[d, :, sl * LANES:(sl + 1) * LANES] for sl in range(nsl))
        else:
            init = tuple(jnp.zeros((B, LANES), F32) for _ in range(nsl))

        def step(j, carry, d=d, a_r=a_r, a_i=a_i):
            c = j if d == 0 else nC - 1 - j
            rows = pl.ds(c, B, stride=P)
            new_r, new_i = [], []
            for q in range(half):
                xr, xi = carry[q], carry[half + q]
                sr = s_scr[q, rows, :]
                si = s_scr[half + q, rows, :]
                s_scr[q, rows, :] = xr
                s_scr[half + q, rows, :] = xi
                new_r.append(a_r[q] * xr - a_i[q] * xi + sr)
                new_i.append(a_r[q] * xi + a_i[q] * xr + si)
            return tuple(new_r + new_i)

        fin = lax.fori_loop(0, nC, step, init)
        for sl in range(nsl):
            fin_ref[d, :, sl * LANES:(sl + 1) * LANES] = fin[sl]

        for b in range(B):
            for sl in range(nsl):
                x_scr[b * nC:(b + 1) * nC, sl * LANES:(sl + 1) * LANES] = \
                    s_scr[sl, b * P:b * P + nC, :].astype(BF16)
        y_scr[...] += _dot(x_scr[...], cp_ref[d])

    for t in range(TC):
        y_ref[pl.ds(t, R, stride=TC), :] = y_scr[:, t * LANES:(t + 1) * LANES]


def _s5(z, ops, h0, *, B, L):
    dk, bs, cp, a8 = ops
    T = B * L
    nC = L // S5_CHUNK
    nlb = MIX_W // LANES
    sc = 2 * S5_LBLK_GROUPS * SSM_STATE
    kc = S5_CHUNK * LANES
    su0 = SU_SECTION * MIX_W // LANES

    in_specs = [pl.BlockSpec((T, LANES), lambda lb: (0, su0 + lb)),
                pl.BlockSpec((2, None, S5_CHUNK, LANES, LANES), lambda lb: (0, lb, 0, 0, 0)),
                pl.BlockSpec((2, None, kc, sc), lambda lb: (0, lb, 0, 0)),
                pl.BlockSpec((2, None, sc, kc), lambda lb: (0, lb, 0, 0)),
                pl.BlockSpec((2, None, 2, 1, sc // 2), lambda lb: (0, lb, 0, 0, 0))]
    args = [z, dk, bs, cp, a8]
    if h0 is not None:
        in_specs.append(pl.BlockSpec((2, None, B, sc), lambda lb: (0, lb, 0, 0)))
        args.append(h0)
    body = functools.partial(_s5_body, B=B, nC=nC, has_h0=h0 is not None)
    return pl.pallas_call(
        body,
        grid=(nlb,),
        in_specs=in_specs,
        out_specs=[pl.BlockSpec((T, LANES), lambda lb: (0, lb)),
                   pl.BlockSpec((2, None, B, sc), lambda lb: (0, lb, 0, 0))],
        out_shape=[jax.ShapeDtypeStruct((T, MIX_W), F32),
                   jax.ShapeDtypeStruct((2, nlb, B, sc), F32)],
        scratch_shapes=[pltpu.VMEM((T, LANES), F32), pltpu.VMEM((B * nC, kc), BF16),
                        pltpu.VMEM((kc, kc), BF16),
                        pltpu.VMEM((sc // LANES, B * (nC + S5_PITCH_PAD), LANES), F32),
                        pltpu.VMEM((B * nC, sc), BF16), pltpu.VMEM((B * nC, kc), F32)],
        name="s5",
        compiler_params=_params("arbitrary"),
    )(*args)


def _s5_operators(a_re, a_im, log_dt, b_re, b_im, c_re, c_im):
    TC = S5_CHUNK
    nlb = MIX_W // LANES
    ng = S5_LBLK_GROUPS
    hi = lax.Precision.HIGHEST
    lr = jnp.minimum(a_re, -1e-4)
    li = a_im
    dt = jnp.exp(log_dt)[..., None]
    k = jnp.arange(TC + 1, dtype=F32)[:, None, None, None]
    mag = jnp.exp(k * (lr * dt)[None])
    pr = mag * jnp.cos(k * (li * dt)[None])
    pi = mag * jnp.sin(k * (li * dt)[None])
    ar, ai = pr[1], pi[1]
    den = lr * lr + li * li
    sr = ((ar - 1.0) * lr + ai * li) / den
    si = (ai * lr - (ar - 1.0) * li) / den
    bbr = sr[..., None] * b_re[None] - si[..., None] * b_im[None]
    bbi = sr[..., None] * b_im[None] + si[..., None] * b_re[None]
    car = c_re[None] * pr[:, :, :, None, :] - c_im[None] * pi[:, :, :, None, :]
    cai = c_re[None] * pi[:, :, :, None, :] + c_im[None] * pr[:, :, :, None, :]
    eye = jnp.eye(ng, dtype=F32)

    kern = (jnp.einsum('kdgop,dgpc->kdgoc', car[:TC], bbr, precision=hi)
            - jnp.einsum('kdgop,dgpc->kdgoc', cai[:TC], bbi, precision=hi))
    dk = jnp.einsum('kdbgoc,gG->dbkgcGo', kern.reshape(TC, 2, nlb, ng, SSM_GROUP, SSM_GROUP), eye)
    dk = dk.reshape(2, nlb, TC, LANES, LANES).astype(BF16)

    abr = pr[:TC, ..., None] * bbr[None] - pi[:TC, ..., None] * bbi[None]
    abi = pr[:TC, ..., None] * bbi[None] + pi[:TC, ..., None] * bbr[None]

    def by_step(x):
        return jnp.stack([jnp.flip(x[:, 0], 0), x[:, 1]], 1)

    bval = jnp.stack([by_step(abr), by_step(abi)], 0).reshape(2, TC, 2, nlb, ng, SSM_STATE, SSM_GROUP)
    bs = jnp.einsum('tsdbgpc,gG->dbsgctGp', bval, eye)
    bs = bs.reshape(2, nlb, TC * LANES, 2 * ng * SSM_STATE).astype(BF16)

    def by_out(x):
        return jnp.stack([x[1:, 0], jnp.flip(x[1:, 1], 0)], 1)

    cval = jnp.stack([by_out(car), -by_out(cai)], 0).reshape(2, TC, 2, nlb, ng, SSM_GROUP, SSM_STATE)
    cp = jnp.einsum('txdbgop,gG->dbtgpxGo', cval, eye)
    cp = cp.reshape(2, nlb, 2 * ng * SSM_STATE, TC * LANES).astype(BF16)

    sw = ng * SSM_STATE
    a8 = jnp.stack([pr[TC].reshape(2, nlb, 1, sw), pi[TC].reshape(2, nlb, 1, sw)], 2)
    return dk, bs, cp, a8


def _head_masks(shape):
    lane = lax.broadcasted_iota(jnp.int32, shape, 1)
    return lane < NA_HEAD_DIM


def _cattn_body(q_ref, k_ref, v_ref, o_ref):
    first = _head_masks(q_ref.shape[1:])
    for bb in range(q_ref.shape[0]):
        q = q_ref[bb]
        k = k_ref[bb]
        v = v_ref[bb]
        outs = []
        for e in range(2):
            qe = jnp.where(first if e == 0 else jnp.logical_not(first), q, jnp.zeros_like(q))
            s = _dot_nt(qe, k) * (NA_HEAD_DIM ** -0.5)
            m = jnp.max(s, -1, keepdims=True)
            p = jnp.exp(s - m)
            l = jnp.sum(p, -1, keepdims=True)
            outs.append(_dot(p.astype(BF16), v) / l)
        o_ref[bb] = jnp.where(first, outs[0], outs[1]).astype(BF16)


def _context_attention(z, *, B, L):
    nblk = MIX_W // LANES
    nb = CTX_ATTN_BATCH

    def sec(s):
        return pl.BlockSpec((nb, L, LANES), lambda b, hp: (b, 0, s * nblk + hp))

    return pl.pallas_call(
        _cattn_body,
        grid=(B // nb, nblk),
        in_specs=[sec(5), sec(6), sec(7)],
        out_specs=pl.BlockSpec((nb, L, LANES), lambda b, hp: (b, 0, hp)),
        out_shape=jax.ShapeDtypeStruct((B, L, MIX_W), BF16),
        name="ctx_attention",
        compiler_params=_params("arbitrary", "arbitrary"),
    )(z, z, z)


def _na_chunks(rows):
    half = NA_KR // 2
    plan, kinds = [], []
    for r0 in range(0, rows, NA_CHUNK_ROWS):
        rs = [min(max(r - half, 0), rows - NA_KR) for r in range(r0, r0 + NA_CHUNK_ROWS)]
        ws = min(rs[0], rows - NA_WIN_ROWS)
        assert rs[-1] + NA_KR <= ws + NA_WIN_ROWS
        kind = tuple((r0 + n - ws, rs[n] - ws) for n in range(NA_CHUNK_ROWS))
        if kind not in kinds:
            kinds.append(kind)
        plan.append((ws, kinds.index(kind)))
    return plan, kinds


def _na_body(q_ref, k_ref, v_ref, kc_ref, vc_ref, tb_ref, o_ref, bias_scr, *, rows):
    scale = NA_HEAD_DIM ** -0.5
    nq = NA_CHUNK_ROWS * GRID_W
    plan, kinds = _na_chunks(rows)
    n_off = 2 * NA_KR - 1

    @pl.when(pl.program_id(1) == 0)
    def _():
        for t, kind in enumerate(kinds):
            for e in range(2):
                for n, (r_rel, rs_rel) in enumerate(kind):
                    for kj in range(NA_WIN_ROWS):
                        off = kj - r_rel + NA_KR - 1 if rs_rel <= kj < rs_rel + NA_KR else n_off
                        lo = (kj % 2) * GRID_W
                        bias_scr[t, e * nq + n * GRID_W:e * nq + (n + 1) * GRID_W, kj * GRID_W:(kj + 1) * GRID_W] = \
                            tb_ref[e, off, :, lo:lo + GRID_W]

    kctx = kc_ref[...].astype(BF16)
    vctx = vc_ref[...].astype(BF16)
    first = _head_masks((nq, LANES))
    for c, (ws, kind) in enumerate(plan):
        qc = q_ref[c * nq:(c + 1) * nq, :]
        qs = jnp.concatenate([jnp.where(first, qc, jnp.zeros_like(qc)),
                              jnp.where(first, jnp.zeros_like(qc), qc)], 0)
        kw = k_ref[ws * GRID_W:(ws + NA_WIN_ROWS) * GRID_W, :]
        vw = v_ref[ws * GRID_W:(ws + NA_WIN_ROWS) * GRID_W, :]
        s_loc = _dot_nt(qs, kw) * scale + bias_scr[kind]
        s_ctx = _dot_nt(qs, kctx) * scale
        m = jnp.maximum(jnp.max(s_loc, -1, keepdims=True), jnp.max(s_ctx, -1, keepdims=True))
        p_loc = jnp.exp(s_loc - m)
        p_ctx = jnp.exp(s_ctx - m)
        l = jnp.sum(p_loc, -1, keepdims=True) + jnp.sum(p_ctx, -1, keepdims=True)
        o = (_dot(p_loc.astype(BF16), vw) + _dot(p_ctx.astype(BF16), vctx)) / l
        o_ref[c * nq:(c + 1) * nq, :] = jnp.where(first, o[:nq], o[nq:]).astype(BF16)


def _neighbourhood_attention(z, cache_k, cache_v, blocks, *, B, L, layer):
    nblk = MIX_W // LANES
    rows = L // GRID_W
    Lc = cache_k.shape[2]
    _, kinds = _na_chunks(rows)

    def sec(s):
        return pl.BlockSpec((None, L, LANES), lambda hp, b: (b, 0, s * nblk + hp))

    ctx = pl.BlockSpec((None, None, Lc, LANES), lambda hp, b: (b, layer, 0, hp))
    return pl.pallas_call(
        functools.partial(_na_body, rows=rows),
        grid=(nblk, B),
        in_specs=[sec(5), sec(6), sec(7), ctx, ctx,
                  pl.BlockSpec((None, 2, 2 * NA_KR, GRID_W, 2 * GRID_W), lambda hp, b: (hp, 0, 0, 0, 0))],
        out_specs=pl.BlockSpec((None, L, LANES), lambda hp, b: (b, 0, hp)),
        out_shape=jax.ShapeDtypeStruct((B, L, MIX_W), BF16),
        scratch_shapes=[pltpu.VMEM((len(kinds), 2 * NA_CHUNK_ROWS * GRID_W, NA_WIN_ROWS * GRID_W), F32)],
        name="nbr_attention",
        compiler_params=_params("arbitrary", "arbitrary"),
    )(z, z, z, cache_k, cache_v, blocks)


def _na_bias_blocks(rpb):
    nr, nc = 2 * NA_KR - 1, 2 * NA_KW - 1
    qc = np.arange(GRID_W)
    kc = np.arange(GRID_W)
    ws = np.clip(qc - NA_KW // 2, 0, GRID_W - NA_KW)
    col_ok = (kc[None, :] >= ws[:, None]) & (kc[None, :] < ws[:, None] + NA_KW)
    coff = np.clip(kc[None, :] - qc[:, None] + NA_KW - 1, 0, nc - 1)
    sel_c = ((coff[None] == np.arange(nc)[:, None, None]) & col_ok[None]).astype(np.float32)
    H = rpb.shape[0]
    t = jnp.einsum('hrc,cqk->hrqk', rpb.astype(F32), sel_c, precision=lax.Precision.HIGHEST)
    t = jnp.where(col_ok[None, None], t, NEG_INF)
    t = jnp.concatenate([t, jnp.full((H, 1, GRID_W, GRID_W), NEG_INF, F32)], 1)
    return jnp.concatenate([t, t], -1).reshape(H // 2, 2, nr + 1, GRID_W, 2 * GRID_W)


def _merge_body(x_ref, p_ref, r_ref, u_ref, y_ref, n_ref, ga_ref, gb_ref, gc_ref,
                d_ref, wglu_ref, wbr_ref, wo_ref, lg_ref, lb_ref, o_ref,
                wglu_s, wbr_s, wo_s, *, L, row0, rstride):
    i = pl.program_id(0)
    tm = x_ref.shape[0]

    @pl.when(i == 0)
    def _():
        wglu_s[...] = wglu_ref[...].astype(BF16)
        wbr_s[...] = wbr_ref[...].astype(BF16)
        wo_s[...] = wo_ref[...].astype(BF16)

    row = row0 + rstride * ((i * tm) // L)
    g1 = _mod_row(p_ref, row, 2)

    y = d_ref[...] * u_ref[...].astype(F32) + y_ref[...]
    y = jax.nn.gelu(y)
    s_out = y * jax.nn.sigmoid(_dot(y.astype(BF16), wglu_s[...]))

    def gate(ref):
        return jax.nn.sigmoid(ref[...].astype(F32))

    merged = (gate(ga_ref) * _dot(r_ref[...], wbr_s[0])
              + gate(gb_ref) * _dot(s_out.astype(BF16), wbr_s[1])
              + gate(gc_ref) * _dot(n_ref[...], wbr_s[2]))
    m = _dot(merged.astype(BF16), wo_s[...])
    o_ref[...] = _layer_norm(DEEPNORM_ALPHA * x_ref[...] + g1 * m, lg_ref[...], lb_ref[...])


def _merge(x, p, z, r_out, y, n_out, ssm_d, w_glu, w_branch, w_o, ln_g, ln_b, *, layer, L, row0, rstride):
    T = x.shape[0]
    tm = MERGE_TILE
    gate0 = 8 * MIX_W // D_MODEL

    def tok(w):
        return pl.BlockSpec((tm, w), lambda i: (i, 0))

    def full(shape):
        return pl.BlockSpec((None,) + shape, lambda i: (layer,) + (0,) * len(shape))

    body = functools.partial(_merge_body, L=L, row0=row0, rstride=rstride)
    return pl.pallas_call(
        body,
        grid=(T // tm,),
        in_specs=[tok(D_MODEL), full((N_PAD_ROWS, 6 * D_MODEL)), tok(MIX_W),
                  pl.BlockSpec((tm, MIX_W), lambda i: (i, SU_SECTION)), tok(MIX_W), tok(MIX_W),
                  pl.BlockSpec((tm, D_MODEL), lambda i: (i, gate0)),
                  pl.BlockSpec((tm, D_MODEL), lambda i: (i, gate0 + 1)),
                  pl.BlockSpec((tm, D_MODEL), lambda i: (i, gate0 + 2)),
                  full((1, MIX_W)), full((MIX_W, MIX_W)), full((3, MIX_W, D_MODEL)),
                  full((D_MODEL, D_MODEL)), full((1, D_MODEL)), full((1, D_MODEL))],
        out_specs=tok(D_MODEL),
        out_shape=jax.ShapeDtypeStruct((T, D_MODEL), F32),
        scratch_shapes=[pltpu.VMEM((MIX_W, MIX_W), BF16), pltpu.VMEM((3, MIX_W, D_MODEL), BF16),
                        pltpu.VMEM((D_MODEL, D_MODEL), BF16)],
        name="merge",
        compiler_params=_params("arbitrary"),
    )(x, p, r_out, z, y, n_out, z, z, z, ssm_d.reshape(DEPTH, 1, MIX_W), w_glu, w_branch, w_o,
      ln_g.reshape(DEPTH, 1, D_MODEL), ln_b.reshape(DEPTH, 1, D_MODEL))


def _ffn_body(x_ref, p_ref, wa_ref, wb_ref, cwa_ref, cwb_ref, cba_ref, cbb_ref, wd_ref, lg_ref, lb_ref,
              o_ref, h_scr, acc_scr, *, L, row0, rstride):
    i = pl.program_id(0)
    j = pl.program_id(1)
    tm = x_ref.shape[0]
    nb = tm // L

    @pl.when(j == 0)
    def _():
        for s in range(nb):
            row = row0 + rstride * (i * nb + s)
            sh = _mod_row(p_ref, row, 3)
            sc = _mod_row(p_ref, row, 4)
            h_scr[s * L:(s + 1) * L, :] = (x_ref[s * L:(s + 1) * L, :] * (1.0 + sc) + sh).astype(BF16)
        acc_scr[...] = jnp.zeros_like(acc_scr)

    t = lax.broadcasted_iota(jnp.int32, (tm, 1), 0) % L
    has_prev = t != 0
    has_next = t != L - 1

    def conv(w_ref, cw_ref, cb_ref):
        zc = _dot(h_scr[...], w_ref[...].astype(BF16))
        zp = jnp.where(has_prev, pltpu.roll(zc, 1, 0), 0.0)
        zn = jnp.where(has_next, pltpu.roll(zc, tm - 1, 0), 0.0)
        return zp * cw_ref[0:1, :] + zc * cw_ref[1:2, :] + zn * cw_ref[2:3, :] + cb_ref[...]

    a = conv(wa_ref, cwa_ref, cba_ref)
    b = conv(wb_ref, cwb_ref, cbb_ref)
    acc_scr[...] += _dot((jax.nn.gelu(a) * b).astype(BF16), wd_ref[...].astype(BF16))

    @pl.when(j == pl.num_programs(1) - 1)
    def _():
        for s in range(nb):
            row = row0 + rstride * (i * nb + s)
            g2 = _mod_row(p_ref, row, 5)
            sl = slice(s * L, (s + 1) * L)
            o_ref[sl, :] = _layer_norm(DEEPNORM_ALPHA * x_ref[sl, :] + g2 * acc_scr[sl, :],
                                       lg_ref[...], lb_ref[...])


def _conv_ffn(x, p, w_up, conv_w, conv_b, w_down, ln_g, ln_b, *, layer, L, row0, rstride):
    T = x.shape[0]
    tm = TOKEN_TILE
    nff = D_FF // FF_TILE
    body = functools.partial(_ffn_body, L=L, row0=row0, rstride=rstride)
    conv_b = conv_b.reshape(DEPTH, 1, 2 * D_FF)
    return pl.pallas_call(
        body,
        grid=(T // tm, nff),
        in_specs=[pl.BlockSpec((tm, D_MODEL), lambda i, j: (i, 0)),
                  pl.BlockSpec((None, N_PAD_ROWS, 6 * D_MODEL), lambda i, j: (layer, 0, 0)),
                  pl.BlockSpec((None, D_MODEL, FF_TILE), lambda i, j: (layer, 0, j)),
                  pl.BlockSpec((None, D_MODEL, FF_TILE), lambda i, j: (layer, 0, nff + j)),
                  pl.BlockSpec((None, 3, FF_TILE), lambda i, j: (layer, 0, j)),
                  pl.BlockSpec((None, 3, FF_TILE), lambda i, j: (layer, 0, nff + j)),
                  pl.BlockSpec((None, 1, FF_TILE), lambda i, j: (layer, 0, j)),
                  pl.BlockSpec((None, 1, FF_TILE), lambda i, j: (layer, 0, nff + j)),
                  pl.BlockSpec((None, FF_TILE, D_MODEL), lambda i, j: (layer, j, 0)),
                  pl.BlockSpec((None, 1, D_MODEL), lambda i, j: (layer, 0, 0)),
                  pl.BlockSpec((None, 1, D_MODEL), lambda i, j: (layer, 0, 0))],
        out_specs=pl.BlockSpec((tm, D_MODEL), lambda i, j: (i, 0)),
        out_shape=jax.ShapeDtypeStruct((T, D_MODEL), F32),
        scratch_shapes=[pltpu.VMEM((tm, D_MODEL), BF16), pltpu.VMEM((tm, D_MODEL), F32)],
        name="conv_ffn",
        compiler_params=_params("arbitrary", "arbitrary"),
    )(x, p, w_up, w_up, conv_w, conv_w, conv_b, conv_b, w_down, ln_g.reshape(DEPTH, 1, D_MODEL),
      ln_b.reshape(DEPTH, 1, D_MODEL))


def _s5_states_in(state_ssm, layer):
    B = state_ssm.shape[0]
    nlb = MIX_W // LANES
    h = state_ssm[:, layer].reshape(B, 2, nlb, S5_LBLK_GROUPS, SSM_STATE, 2)
    return jnp.transpose(h, (1, 2, 0, 5, 3, 4)).reshape(2, nlb, B, 2 * S5_LBLK_GROUPS * SSM_STATE)


def _s5_states_out(fin):
    nlb, B = fin.shape[1], fin.shape[2]
    h = fin.reshape(2, nlb, B, 2, S5_LBLK_GROUPS, SSM_STATE)
    return jnp.transpose(h, (2, 0, 1, 4, 5, 3)).reshape(B, 2, SSM_GROUPS, SSM_STATE, 2)


def _layer(x, p, lw, *, B, L, row0, rstride, latent, layer, extra):
    T = B * L
    z, *kv = _inproj(x, p, lw['w_in'], layer=layer, L=L, row0=row0, rstride=rstride, want_kv=not latent)
    z3 = z.reshape(B, L, IN_COLS)
    log_gamma = jax.nn.log_sigmoid(lw['ret_decay'].astype(F32))
    if latent:
        r_out = _retention(z3, log_gamma, B=B, L=L, rope_tabs=extra['rope'], s0=extra['state_ret'],
                           layer=layer, want_state=False)[0]
        n_out = _neighbourhood_attention(z3, extra['cache_k'], extra['cache_v'], extra['bias'][layer],
                                         B=B, L=L, layer=layer)
        y, _ = _s5(z, lw['s5'], _s5_states_in(extra['state_ssm'], layer), B=B, L=L)
        states = None
    else:
        r_out, ret_state = _retention(z3, log_gamma, B=B, L=L, want_state=True)
        n_out = _context_attention(z3, B=B, L=L)
        y, fin = _s5(z, lw['s5'], None, B=B, L=L)
        nk = kv[0].reshape(B, L, NA_HEADS, NA_HEAD_DIM)
        nv = kv[1].reshape(B, L, NA_HEADS, NA_HEAD_DIM)
        states = (ret_state, _s5_states_out(fin), nk, nv)
    x = _merge(x, p, z, r_out.reshape(T, MIX_W), y, n_out.reshape(T, MIX_W),
               lw['ssm_d'], lw['ssm_w_glu'], lw['w_branch'], lw['w_o'], lw['ln1_g'], lw['ln1_b'],
               layer=layer, L=L, row0=row0, rstride=rstride)
    x = _conv_ffn(x, p, lw['w_up'], lw['conv_w'], lw['conv_b'], lw['w_down'], lw['ln2_g'], lw['ln2_b'],
                  layer=layer, L=L, row0=row0, rstride=rstride)
    return x, states


def kernel(x_prompt, x_sample, state_ret, state_ssm, cache_na_k, cache_na_v, c, c_ctx, w_ada, b_ada, w_in, ret_decay, ssm_a_re, ssm_a_im, ssm_log_dt, ssm_b_re, ssm_b_im, ssm_c_re, ssm_c_im, ssm_d, ssm_w_glu, na_rpb, w_branch, w_o, ln1_g, ln1_b, w_up, conv_w, conv_b, w_down, ln2_g, ln2_b):
    B, L, _ = x_prompt.shape
    Bd, Ld, _ = x_sample.shape
    Lc = cache_na_k.shape[2]

    cond = jnp.concatenate([c_ctx[None, :], c, jnp.zeros((N_PAD_ROWS - 1 - Bd, D_MODEL), F32)], 0)
    p_all = _ada(cond, w_ada, b_ada)

    extra = dict(rope=_rope_tables(Ld), state_ret=state_ret, state_ssm=state_ssm,
                 cache_k=cache_na_k.reshape(Bd, DEPTH, Lc, MIX_W),
                 cache_v=cache_na_v.reshape(Bd, DEPTH, Lc, MIX_W),
                 bias=[_na_bias_blocks(na_rpb[l]) for l in range(DEPTH)])

    xp = x_prompt.reshape(B * L, D_MODEL)
    xs = x_sample.reshape(Bd * Ld, D_MODEL)
    ret_states, ssm_states, na_ks, na_vs = [], [], [], []
    for l in range(DEPTH):
        lw = dict(w_in=w_in, ret_decay=ret_decay[l], ssm_d=ssm_d, ssm_w_glu=ssm_w_glu,
                  w_branch=w_branch, w_o=w_o, ln1_g=ln1_g, ln1_b=ln1_b, w_up=w_up,
                  conv_w=conv_w, conv_b=conv_b, w_down=w_down, ln2_g=ln2_g, ln2_b=ln2_b,
                  s5=_s5_operators(ssm_a_re[l], ssm_a_im[l], ssm_log_dt[l], ssm_b_re[l], ssm_b_im[l],
                                   ssm_c_re[l], ssm_c_im[l]))
        xp, (s_ret, s_ssm, nk, nv) = _layer(xp, p_all, lw, B=B, L=L, row0=0, rstride=0,
                                            latent=False, layer=l, extra=None)
        ret_states.append(s_ret)
        ssm_states.append(s_ssm)
        na_ks.append(nk)
        na_vs.append(nv)
        xs, _ = _layer(xs, p_all, lw, B=Bd, L=Ld, row0=1, rstride=1, latent=True, layer=l, extra=extra)
    return (xp.reshape(B, L, D_MODEL), xs.reshape(Bd, Ld, D_MODEL),
            jnp.stack(ret_states, 1), jnp.stack(ssm_states, 1), jnp.stack(na_ks, 1), jnp.stack(na_vs, 1))
```

```python
import functools

import jax
import jax.numpy as jnp
import numpy as np
from jax import lax
from jax.experimental import pallas as pl
from jax.experimental.pallas import tpu as pltpu

F32 = jnp.float32
BF16 = jnp.bfloat16

D_MODEL = 1024
DEPTH = 2
GRID_W = 64
MIX_W = D_MODEL // 2
N_RET_HEADS = 4
RET_DK = MIX_W // N_RET_HEADS
SSM_GROUP = 16
SSM_GROUPS = MIX_W // SSM_GROUP
SSM_STATE = 64
NA_HEADS = 8
NA_HEAD_DIM = MIX_W // NA_HEADS
NA_KR = 8
NA_KW = 16
D_FF = ((8 * D_MODEL // 3 + 127) // 128) * 128
ROPE_BASE = 10000.0
LN_EPS = 1e-5
NEG_INF = -1e30
DEEPNORM_ALPHA = (2 * DEPTH) ** 0.25
IN_COLS = 8 * MIX_W + 3 * D_MODEL

VMEM_LIMIT_BYTES = 56 * 1024 * 1024
LANES = 128

TOKEN_TILE = 1024
MERGE_TILE = 256
COL_TILE = 512
SU_SECTION = 4
NK_TILE = 6
FF_TILE = 256
RET_CHUNK = 256
S5_CHUNK = 8
S5_PITCH_PAD = 8
S5_LBLK_GROUPS = LANES // SSM_GROUP
N_PAD_ROWS = 8
CTX_ATTN_BATCH = 4
NA_CHUNK_ROWS = 4
NA_WIN_ROWS = 12


def _params(*sem):
    return pltpu.CompilerParams(dimension_semantics=sem, vmem_limit_bytes=VMEM_LIMIT_BYTES)


def _dot(a, b):
    return jnp.dot(a, b, preferred_element_type=F32)


def _dot_nt(a, b):
    return lax.dot_general(a, b, (((1,), (1,)), ((), ())), preferred_element_type=F32)


def _layer_norm(x, g, b):
    mu = jnp.mean(x, -1, keepdims=True)
    xc = x - mu
    var = jnp.mean(xc * xc, -1, keepdims=True)
    return xc * lax.rsqrt(var + LN_EPS) * g + b


def _ada_body(c_ref, w_ref, b_ref, o_ref):
    c = c_ref[...]
    s = c * jax.nn.sigmoid(c)
    o_ref[...] = _dot(s.astype(BF16), w_ref[...].astype(BF16)) + b_ref[...]


def _ada(cond, w_ada, b_ada):
    tn = 1024
    return pl.pallas_call(
        _ada_body,
        grid=(DEPTH, 6 * D_MODEL // tn),
        in_specs=[pl.BlockSpec((N_PAD_ROWS, D_MODEL), lambda l, j: (0, 0)),
                  pl.BlockSpec((None, D_MODEL, tn), lambda l, j: (l, 0, j)),
                  pl.BlockSpec((None, 1, tn), lambda l, j: (l, 0, j))],
        out_specs=pl.BlockSpec((None, N_PAD_ROWS, tn), lambda l, j: (l, 0, j)),
        out_shape=jax.ShapeDtypeStruct((DEPTH, N_PAD_ROWS, 6 * D_MODEL), F32),
        name="ada",
        compiler_params=_params("arbitrary", "arbitrary"),
    )(cond, w_ada, b_ada.reshape(DEPTH, 1, 6 * D_MODEL))


def _mod_row(p_ref, row, k):
    return p_ref[pl.ds(row, 1), k * D_MODEL:(k + 1) * D_MODEL]


def _inproj_body(x_ref, p_ref, w_ref, z_ref, *rest, L, row0, rstride):
    h_scr, w_scr = rest[-2:]
    kv_refs = rest[:-2]
    j = pl.program_id(0)
    i = pl.program_id(1)
    nb = x_ref.shape[0] // L

    @pl.when(j == 0)
    def _():
        for s in range(nb):
            row = row0 + rstride * (i * nb + s)
            sh = _mod_row(p_ref, row, 0)
            sc = _mod_row(p_ref, row, 1)
            h_scr[i, s * L:(s + 1) * L, :] = (x_ref[s * L:(s + 1) * L, :] * (1.0 + sc) + sh).astype(BF16)

    @pl.when(i == 0)
    def _():
        w_scr[...] = w_ref[...].astype(BF16)

    acc = _dot(h_scr[i], w_scr[...])
    z_ref[...] = acc.astype(BF16)

    for n, ref in enumerate(kv_refs):
        @pl.when(j == NK_TILE + n)
        def _(ref=ref):
            ref[...] = acc


def _inproj(x, p, w_in, *, layer, L, row0, rstride, want_kv):
    T = x.shape[0]
    tm = TOKEN_TILE
    n_i = T // tm
    body = functools.partial(_inproj_body, L=L, row0=row0, rstride=rstride)
    n_kv = 2 if want_kv else 0

    def only_at(tile):
        return lambda j, i: jnp.where(j < tile, 0, jnp.where(j > tile, n_i - 1, i))

    kv_i = [only_at(NK_TILE + n) for n in range(n_kv)]
    return pl.pallas_call(
        body,
        grid=(IN_COLS // COL_TILE, n_i),
        in_specs=[pl.BlockSpec((tm, D_MODEL), lambda j, i: (jnp.where(j == 0, i, n_i - 1), 0)),
                  pl.BlockSpec((None, N_PAD_ROWS, 6 * D_MODEL), lambda j, i: (layer, 0, 0)),
                  pl.BlockSpec((None, D_MODEL, COL_TILE), lambda j, i: (layer, 0, j))],
        out_specs=[pl.BlockSpec((tm, COL_TILE), lambda j, i: (i, j))]
        + [pl.BlockSpec((tm, MIX_W), lambda j, i, f=f: (f(j, i), 0)) for f in kv_i],
        out_shape=[jax.ShapeDtypeStruct((T, IN_COLS), BF16)] + [jax.ShapeDtypeStruct((T, MIX_W), F32)] * n_kv,
        scratch_shapes=[pltpu.VMEM((n_i, tm, D_MODEL), BF16), pltpu.VMEM((D_MODEL, COL_TILE), BF16)],
        name="inproj",
        compiler_params=_params("arbitrary", "arbitrary"),
    )(x, p, w_in)


def _rope(x, cos, s_up, s_dn):
    return x * cos + pltpu.roll(x, 96, 1) * s_up + pltpu.roll(x, 32, 1) * s_dn


def _ret_body(*refs, n, rope, has_s0, want_state):
    refs = list(refs)
    lg_ref, q_ref, k_ref, v_ref, g_ref = refs[:5]
    refs = refs[5:]
    if rope:
        cos_ref, sup_ref, sdn_ref = refs[:3]
        refs = refs[3:]
    if has_s0:
        s0_ref = refs[0]
        refs = refs[1:]
    o_ref = refs[0]
    refs = refs[1:]
    if want_state:
        st_ref = refs[0]
        refs = refs[1:]
    q_scr, k_scr, sb_scr, decay_scr = refs

    C = RET_CHUNK
    h = pl.program_id(0)
    lf = lg_ref[0, h]
    lb = lg_ref[1, h]

    q = q_ref[...].astype(F32)
    k = k_ref[...].astype(F32)
    if rope:
        q = _rope(q, cos_ref[...], sup_ref[...], sdn_ref[...])
        k = _rope(k, cos_ref[...], sup_ref[...], sdn_ref[...])
    q_scr[...] = q
    k_scr[...] = k * (RET_DK ** -0.5)

    @pl.when(pl.program_id(1) == 0)
    def _():
        ti = lax.broadcasted_iota(jnp.int32, (C, C), 0)
        si = lax.broadcasted_iota(jnp.int32, (C, C), 1)
        dlt = (ti - si).astype(F32)
        decay_scr[...] = (jnp.where(dlt >= 0, jnp.exp(lf * jnp.maximum(dlt, 0.0)), 0.0)
                          + jnp.where(dlt <= 0, jnp.exp(lb * jnp.maximum(-dlt, 0.0)), 0.0))

    tcol = lax.broadcasted_iota(jnp.int32, (C, 1), 0).astype(F32)
    qd_f = jnp.exp(lf * (tcol + 1.0))
    qd_b = jnp.exp(lb * (C - tcol))
    kd_f = jnp.exp(lf * (C - 1.0 - tcol))
    kd_b = jnp.exp(lb * tcol)
    cd_f = jnp.exp(lf * jnp.full((1, RET_DK), float(C), F32))
    cd_b = jnp.exp(lb * jnp.full((1, RET_DK), float(C), F32))

    def kv_outer(kc, vc, kd):
        return _dot((kc * kd).T.astype(BF16), vc)

    s_b = s0_ref[1] if has_s0 else jnp.zeros((RET_DK, RET_DK), F32)
    for i in reversed(range(n)):
        sb_scr[i] = s_b
        if i > 0 or want_state:
            s_b = s_b * cd_b + kv_outer(k_scr[i * C:(i + 1) * C, :], v_ref[i * C:(i + 1) * C, :], kd_b)

    s_f = s0_ref[0] if has_s0 else jnp.zeros((RET_DK, RET_DK), F32)
    for i in range(n):
        sl = slice(i * C, (i + 1) * C)
        qc = q_scr[sl, :]
        kc = k_scr[sl, :]
        vc = v_ref[sl, :]
        att = _dot_nt(qc.astype(BF16), kc.astype(BF16)) * decay_scr[...]
        o = _dot(att.astype(BF16), vc)
        o = o + _dot((qc * qd_f).astype(BF16), s_f.astype(BF16))
        o = o + _dot((qc * qd_b).astype(BF16), sb_scr[i].astype(BF16))
        mu = jnp.mean(o, -1, keepdims=True)
        oc = o - mu
        var = jnp.mean(oc * oc, -1, keepdims=True)
        gc = g_ref[sl, :].astype(F32)
        o_ref[sl, :] = (oc * lax.rsqrt(var + LN_EPS) * (gc * jax.nn.sigmoid(gc))).astype(BF16)
        if i < n - 1 or want_state:
            s_f = s_f * cd_f + kv_outer(kc, vc, kd_f)

    if want_state:
        st_ref[0] = s_f
        st_ref[1] = s_b


def _retention(z, log_gamma, *, B, L, rope_tabs=None, s0=None, layer=0, want_state):
    n = L // RET_CHUNK
    H = N_RET_HEADS
    nblk = MIX_W // RET_DK

    def sec(s):
        return pl.BlockSpec((None, L, RET_DK), lambda h, b: (b, 0, s * nblk + h))

    in_specs = [pl.BlockSpec(memory_space=pltpu.SMEM), sec(0), sec(1), sec(2), sec(3)]
    args = [log_gamma, z, z, z, z]
    if rope_tabs is not None:
        in_specs += [pl.BlockSpec((L, RET_DK), lambda h, b: (0, 0))] * 3
        args += list(rope_tabs)
    if s0 is not None:
        in_specs.append(pl.BlockSpec((None, None, 2, None, RET_DK, RET_DK), lambda h, b: (b, layer, 0, h, 0, 0)))
        args.append(s0)
    out_specs = [pl.BlockSpec((None, L, RET_DK), lambda h, b: (b, 0, h))]
    out_shape = [jax.ShapeDtypeStruct((B, L, MIX_W), BF16)]
    if want_state:
        out_specs.append(pl.BlockSpec((None, 2, None, RET_DK, RET_DK), lambda h, b: (b, 0, h, 0, 0)))
        out_shape.append(jax.ShapeDtypeStruct((B, 2, H, RET_DK, RET_DK), F32))
    body = functools.partial(_ret_body, n=n, rope=rope_tabs is not None, has_s0=s0 is not None,
                             want_state=want_state)
    return pl.pallas_call(
        body,
        grid=(H, B),
        in_specs=in_specs,
        out_specs=out_specs,
        out_shape=out_shape,
        scratch_shapes=[pltpu.VMEM((L, RET_DK), F32), pltpu.VMEM((L, RET_DK), F32),
                        pltpu.VMEM((n, RET_DK, RET_DK), F32), pltpu.VMEM((RET_CHUNK, RET_CHUNK), F32)],
        name="retention",
        compiler_params=_params("arbitrary", "arbitrary"),
    )(*args)


def _rope_tables(L):
    pos = jnp.arange(L)
    row = (pos // GRID_W).astype(F32)
    col = (pos % GRID_W).astype(F32)
    quarter = RET_DK // 4
    inv_freq = ROPE_BASE ** (-jnp.arange(quarter, dtype=F32) / quarter)
    ang_r = row[:, None] * inv_freq[None, :]
    ang_c = col[:, None] * inv_freq[None, :]
    zero = jnp.zeros_like(ang_r)
    cos = jnp.concatenate([jnp.cos(ang_r), jnp.cos(ang_r), jnp.cos(ang_c), jnp.cos(ang_c)], -1)
    s_up = jnp.concatenate([-jnp.sin(ang_r), zero, -jnp.sin(ang_c), zero], -1)
    s_dn = jnp.concatenate([zero, jnp.sin(ang_r), zero, jnp.sin(ang_c)], -1)
    return cos, s_up, s_dn


def _s5_body(*refs, B, nC, has_h0):
    refs = list(refs)
    u_ref, dk_ref, bs_ref, cp_ref, a8_ref = refs[:5]
    refs = refs[5:]
    if has_h0:
        h0_ref = refs[0]
        refs = refs[1:]
    y_ref, fin_ref, u_scr, a_scr, m_scr, bs_scr, cp_scr, s_scr, x_scr, y_scr = refs

    TC = S5_CHUNK
    R = B * nC
    P = nC + S5_PITCH_PAD
    nsl = s_scr.shape[0]
    half = nsl // 2
    ng = S5_LBLK_GROUPS
    sw = ng * SSM_STATE

    u_scr[...] = u_ref[...].astype(F32)
    for s in range(TC):
        a_scr[:, s * LANES:(s + 1) * LANES] = u_scr[pl.ds(s, R, stride=TC), :].astype(BF16)

    bs_scr[...] = jnp.zeros_like(bs_scr)
    cp_scr[...] = jnp.zeros_like(cp_scr)

    for d in range(2):
        for s in range(TC):
            for t in range(TC):
                k = (t - s) if d == 0 else (s - t)
                blk = dk_ref[d, k] if k >= 0 else jnp.zeros((LANES, LANES), BF16)
                m_scr[s * LANES:(s + 1) * LANES, t * LANES:(t + 1) * LANES] = blk
        for s in range(TC):
            for g in range(ng):
                for part in range(2):
                    rows = slice(s * LANES + g * SSM_GROUP, s * LANES + (g + 1) * SSM_GROUP)
                    cols = slice(part * sw + g * SSM_STATE, part * sw + (g + 1) * SSM_STATE)
                    lo = (g % 2) * SSM_STATE
                    bs_scr[rows, cols] = bs_ref[d, s, g, part, :, lo:lo + SSM_STATE]
                    cp_scr[rows, cols] = cp_ref[d, s, g, part, :, lo:lo + SSM_STATE]

        a = a_scr[...]
        yd = _dot(a, m_scr[...])
        if d == 0:
            y_scr[...] = yd
        else:
            y_scr[...] += yd

        sm = _dot(a, bs_scr[...])
        for b in range(B):
            for sl in range(nsl):
                s_scr[sl, b * P:b * P + nC, :] = sm[b * nC:(b + 1) * nC, sl * LANES:(sl + 1) * LANES]

        a_r = [jnp.broadcast_to(a8_ref[d, 0, :, q * LANES:(q + 1) * LANES], (B, LANES)) for q in range(half)]
        a_i = [jnp.broadcast_to(a8_ref[d, 1, :, q * LANES:(q + 1) * LANES], (B, LANES)) for q in range(half)]
        if has_h0:
            init = tuple(h0_ref[d, :, sl * LANES:(sl + 1) * LANES] for sl in range(nsl))
        else:
            init = tuple(jnp.zeros((B, LANES), F32) for _ in range(nsl))

        def step(j, carry, d=d, a_r=a_r, a_i=a_i):
            c = j if d == 0 else nC - 1 - j
            rows = pl.ds(c, B, stride=P)
            new_r, new_i = [], []
            for q in range(half):
                xr, xi = carry[q], carry[half + q]
                sr = s_scr[q, rows, :]
                si = s_scr[half + q, rows, :]
                s_scr[q, rows, :] = xr
                s_scr[half + q, rows, :] = xi
                new_r.append(a_r[q] * xr - a_i[q] * xi + sr)
                new_i.append(a_r[q] * xi + a_i[q] * xr + si)
            return tuple(new_r + new_i)

        fin = lax.fori_loop(0, nC, step, init)
        for sl in range(nsl):
            fin_ref[d, :, sl * LANES:(sl + 1) * LANES] = fin[sl]

        for b in range(B):
            for sl in range(nsl):
                x_scr[b * nC:(b + 1) * nC, sl * LANES:(sl + 1) * LANES] = \
                    s_scr[sl, b * P:b * P + nC, :].astype(BF16)
        y_scr[...] += _dot_nt(x_scr[...], cp_scr[...])

    for t in range(TC):
        y_ref[pl.ds(t, R, stride=TC), :] = y_scr[:, t * LANES:(t + 1) * LANES]


def _s5(z, ops, h0, *, B, L):
    dk, bs, cp, a8 = ops
    T = B * L
    nC = L // S5_CHUNK
    nlb = MIX_W // LANES
    sc = 2 * S5_LBLK_GROUPS * SSM_STATE
    kc = S5_CHUNK * LANES
    su0 = SU_SECTION * MIX_W // LANES

    blocks = pl.BlockSpec((2, None, S5_CHUNK, S5_LBLK_GROUPS, 2, SSM_GROUP, LANES),
                          lambda lb: (0, lb, 0, 0, 0, 0, 0))
    in_specs = [pl.BlockSpec((T, LANES), lambda lb: (0, su0 + lb)),
                pl.BlockSpec((2, None, S5_CHUNK, LANES, LANES), lambda lb: (0, lb, 0, 0, 0)),
                blocks, blocks,
                pl.BlockSpec((2, None, 2, 1, sc // 2), lambda lb: (0, lb, 0, 0, 0))]
    args = [z, dk, bs, cp, a8]
    if h0 is not None:
        in_specs.append(pl.BlockSpec((2, None, B, sc), lambda lb: (0, lb, 0, 0)))
        args.append(h0)
    body = functools.partial(_s5_body, B=B, nC=nC, has_h0=h0 is not None)
    return pl.pallas_call(
        body,
        grid=(nlb,),
        in_specs=in_specs,
        out_specs=[pl.BlockSpec((T, LANES), lambda lb: (0, lb)),
                   pl.BlockSpec((2, None, B, sc), lambda lb: (0, lb, 0, 0))],
        out_shape=[jax.ShapeDtypeStruct((T, MIX_W), F32),
                   jax.ShapeDtypeStruct((2, nlb, B, sc), F32)],
        scratch_shapes=[pltpu.VMEM((T, LANES), F32), pltpu.VMEM((B * nC, kc), BF16),
                        pltpu.VMEM((kc, kc), BF16), pltpu.VMEM((kc, sc), BF16), pltpu.VMEM((kc, sc), BF16),
                        pltpu.VMEM((sc // LANES, B * (nC + S5_PITCH_PAD), LANES), F32),
                        pltpu.VMEM((B * nC, sc), BF16), pltpu.VMEM((B * nC, kc), F32)],
        name="s5",
        compiler_params=_params("arbitrary"),
    )(*args)


def _s5_operators(a_re, a_im, log_dt, b_re, b_im, c_re, c_im):
    TC = S5_CHUNK
    nlb = MIX_W // LANES
    ng = S5_LBLK_GROUPS
    hi = lax.Precision.HIGHEST
    lr = jnp.minimum(a_re, -1e-4)
    li = a_im
    dt = jnp.exp(log_dt)[..., None]
    k = jnp.arange(TC + 1, dtype=F32)[:, None, None, None]
    mag = jnp.exp(k * (lr * dt)[None])
    pr = mag * jnp.cos(k * (li * dt)[None])
    pi = mag * jnp.sin(k * (li * dt)[None])
    ar, ai = pr[1], pi[1]
    den = lr * lr + li * li
    sr = ((ar - 1.0) * lr + ai * li) / den
    si = (ai * lr - (ar - 1.0) * li) / den
    bbr = sr[..., None] * b_re[None] - si[..., None] * b_im[None]
    bbi = sr[..., None] * b_im[None] + si[..., None] * b_re[None]
    car = c_re[None] * pr[:, :, :, None, :] - c_im[None] * pi[:, :, :, None, :]
    cai = c_re[None] * pi[:, :, :, None, :] + c_im[None] * pr[:, :, :, None, :]
    eye = jnp.eye(ng, dtype=F32)

    kern = (jnp.einsum('kdgop,dgpc->kdgoc', car[:TC], bbr, precision=hi)
            - jnp.einsum('kdgop,dgpc->kdgoc', cai[:TC], bbi, precision=hi))
    dk = jnp.einsum('kdbgoc,gG->dbkgcGo', kern.reshape(TC, 2, nlb, ng, SSM_GROUP, SSM_GROUP), eye)
    dk = dk.reshape(2, nlb, TC, LANES, LANES).astype(BF16)

    abr = pr[:TC, ..., None] * bbr[None] - pi[:TC, ..., None] * bbi[None]
    abi = pr[:TC, ..., None] * bbi[None] + pi[:TC, ..., None] * bbr[None]

    def by_step(x):
        return jnp.stack([jnp.flip(x[:, 0], 0), x[:, 1]], 1)

    def blocks(x):
        x = jnp.transpose(x, (2, 3, 1, 4, 0, 5, 6))
        return jnp.concatenate([x, x], -1).astype(BF16)

    bval = jnp.stack([by_step(abr), by_step(abi)], 0).reshape(2, TC, 2, nlb, ng, SSM_STATE, SSM_GROUP)
    bs = blocks(jnp.swapaxes(bval, -1, -2))

    def by_out(x):
        return jnp.stack([x[1:, 0], jnp.flip(x[1:, 1], 0)], 1)

    cval = jnp.stack([by_out(car), -by_out(cai)], 0).reshape(2, TC, 2, nlb, ng, SSM_GROUP, SSM_STATE)
    cp = blocks(cval)

    sw = ng * SSM_STATE
    a8 = jnp.stack([pr[TC].reshape(2, nlb, 1, sw), pi[TC].reshape(2, nlb, 1, sw)], 2)
    return dk, bs, cp, a8


def _head_masks(shape):
    lane = lax.broadcasted_iota(jnp.int32, shape, 1)
    return lane < NA_HEAD_DIM


def _cattn_body(q_ref, k_ref, v_ref, o_ref):
    first = _head_masks(q_ref.shape[1:])
    for bb in range(q_ref.shape[0]):
        q = q_ref[bb]
        k = k_ref[bb]
        v = v_ref[bb]
        outs = []
        for e in range(2):
            qe = jnp.where(first if e == 0 else jnp.logical_not(first), q, jnp.zeros_like(q))
            s = _dot_nt(qe, k) * (NA_HEAD_DIM ** -0.5)
            m = jnp.max(s, -1, keepdims=True)
            p = jnp.exp(s - m)
            l = jnp.sum(p, -1, keepdims=True)
            outs.append(_dot(p.astype(BF16), v) / l)
        o_ref[bb] = jnp.where(first, outs[0], outs[1]).astype(BF16)


def _context_attention(z, *, B, L):
    nblk = MIX_W // LANES
    nb = CTX_ATTN_BATCH

    def sec(s):
        return pl.BlockSpec((nb, L, LANES), lambda b, hp: (b, 0, s * nblk + hp))

    return pl.pallas_call(
        _cattn_body,
        grid=(B // nb, nblk),
        in_specs=[sec(5), sec(6), sec(7)],
        out_specs=pl.BlockSpec((nb, L, LANES), lambda b, hp: (b, 0, hp)),
        out_shape=jax.ShapeDtypeStruct((B, L, MIX_W), BF16),
        name="ctx_attention",
        compiler_params=_params("arbitrary", "arbitrary"),
    )(z, z, z)


def _na_chunks(rows):
    half = NA_KR // 2
    plan, kinds = [], []
    for r0 in range(0, rows, NA_CHUNK_ROWS):
        rs = [min(max(r - half, 0), rows - NA_KR) for r in range(r0, r0 + NA_CHUNK_ROWS)]
        ws = min(rs[0], rows - NA_WIN_ROWS)
        assert rs[-1] + NA_KR <= ws + NA_WIN_ROWS
        kind = tuple((r0 + n - ws, rs[n] - ws) for n in range(NA_CHUNK_ROWS))
        if kind not in kinds:
            kinds.append(kind)
        plan.append((ws, kinds.index(kind)))
    return plan, kinds


def _na_body(q_ref, k_ref, v_ref, kc_ref, vc_ref, tb_ref, o_ref, bias_scr, *, rows):
    scale = NA_HEAD_DIM ** -0.5
    nq = NA_CHUNK_ROWS * GRID_W
    plan, kinds = _na_chunks(rows)
    n_off = 2 * NA_KR - 1

    @pl.when(pl.program_id(1) == 0)
    def _():
        for t, kind in enumerate(kinds):
            for e in range(2):
                for n, (r_rel, rs_rel) in enumerate(kind):
                    for kj in range(NA_WIN_ROWS):
                        off = kj - r_rel + NA_KR - 1 if rs_rel <= kj < rs_rel + NA_KR else n_off
                        lo = (kj % 2) * GRID_W
                        bias_scr[t, e * nq + n * GRID_W:e * nq + (n + 1) * GRID_W, kj * GRID_W:(kj + 1) * GRID_W] = \
                            tb_ref[e, off, :, lo:lo + GRID_W]

    kctx = kc_ref[...].astype(BF16)
    vctx = vc_ref[...].astype(BF16)
    first = _head_masks((nq, LANES))
    for c, (ws, kind) in enumerate(plan):
        qc = q_ref[c * nq:(c + 1) * nq, :]
        qs = jnp.concatenate([jnp.where(first, qc, jnp.zeros_like(qc)),
                              jnp.where(first, jnp.zeros_like(qc), qc)], 0)
        kw = k_ref[ws * GRID_W:(ws + NA_WIN_ROWS) * GRID_W, :]
        vw = v_ref[ws * GRID_W:(ws + NA_WIN_ROWS) * GRID_W, :]
        s_loc = _dot_nt(qs, kw) * scale + bias_scr[kind]
        s_ctx = _dot_nt(qs, kctx) * scale
        m = jnp.maximum(jnp.max(s_loc, -1, keepdims=True), jnp.max(s_ctx, -1, keepdims=True))
        p_loc = jnp.exp(s_loc - m)
        p_ctx = jnp.exp(s_ctx - m)
        l = jnp.sum(p_loc, -1, keepdims=True) + jnp.sum(p_ctx, -1, keepdims=True)
        o = (_dot(p_loc.astype(BF16), vw) + _dot(p_ctx.astype(BF16), vctx)) / l
        o_ref[c * nq:(c + 1) * nq, :] = jnp.where(first, o[:nq], o[nq:]).astype(BF16)


def _neighbourhood_attention(z, cache_k, cache_v, blocks, *, B, L, layer):
    nblk = MIX_W // LANES
    rows = L // GRID_W
    Lc = cache_k.shape[2]
    _, kinds = _na_chunks(rows)

    def sec(s):
        return pl.BlockSpec((None, L, LANES), lambda hp, b: (b, 0, s * nblk + hp))

    ctx = pl.BlockSpec((None, None, Lc, LANES), lambda hp, b: (b, layer, 0, hp))
    return pl.pallas_call(
        functools.partial(_na_body, rows=rows),
        grid=(nblk, B),
        in_specs=[sec(5), sec(6), sec(7), ctx, ctx,
                  pl.BlockSpec((None, 2, 2 * NA_KR, GRID_W, 2 * GRID_W), lambda hp, b: (hp, 0, 0, 0, 0))],
        out_specs=pl.BlockSpec((None, L, LANES), lambda hp, b: (b, 0, hp)),
        out_shape=jax.ShapeDtypeStruct((B, L, MIX_W), BF16),
        scratch_shapes=[pltpu.VMEM((len(kinds), 2 * NA_CHUNK_ROWS * GRID_W, NA_WIN_ROWS * GRID_W), F32)],
        name="nbr_attention",
        compiler_params=_params("arbitrary", "arbitrary"),
    )(z, z, z, cache_k, cache_v, blocks)


def _na_bias_blocks(rpb):
    nr, nc = 2 * NA_KR - 1, 2 * NA_KW - 1
    qc = np.arange(GRID_W)
    kc = np.arange(GRID_W)
    ws = np.clip(qc - NA_KW // 2, 0, GRID_W - NA_KW)
    col_ok = (kc[None, :] >= ws[:, None]) & (kc[None, :] < ws[:, None] + NA_KW)
    coff = np.clip(kc[None, :] - qc[:, None] + NA_KW - 1, 0, nc - 1)
    sel_c = ((coff[None] == np.arange(nc)[:, None, None]) & col_ok[None]).astype(np.float32)
    H = rpb.shape[0]
    t = jnp.einsum('hrc,cqk->hrqk', rpb.astype(F32), sel_c, precision=lax.Precision.HIGHEST)
    t = jnp.where(col_ok[None, None], t, NEG_INF)
    t = jnp.concatenate([t, jnp.full((H, 1, GRID_W, GRID_W), NEG_INF, F32)], 1)
    return jnp.concatenate([t, t], -1).reshape(H // 2, 2, nr + 1, GRID_W, 2 * GRID_W)


def _merge_body(x_ref, p_ref, r_ref, u_ref, y_ref, n_ref, ga_ref, gb_ref, gc_ref,
                d_ref, wglu_ref, wbr_ref, wo_ref, lg_ref, lb_ref, o_ref,
                wglu_s, wbr_s, wo_s, *, L, row0, rstride):
    i = pl.program_id(0)
    tm = x_ref.shape[0]

    @pl.when(i == 0)
    def _():
        wglu_s[...] = wglu_ref[...].astype(BF16)
        wbr_s[...] = wbr_ref[...].astype(BF16)
        wo_s[...] = wo_ref[...].astype(BF16)

    row = row0 + rstride * ((i * tm) // L)
    g1 = _mod_row(p_ref, row, 2)

    y = d_ref[...] * u_ref[...].astype(F32) + y_ref[...]
    y = jax.nn.gelu(y)
    s_out = y * jax.nn.sigmoid(_dot(y.astype(BF16), wglu_s[...]))

    def gate(ref):
        return jax.nn.sigmoid(ref[...].astype(F32))

    merged = (gate(ga_ref) * _dot(r_ref[...], wbr_s[0])
              + gate(gb_ref) * _dot(s_out.astype(BF16), wbr_s[1])
              + gate(gc_ref) * _dot(n_ref[...], wbr_s[2]))
    m = _dot(merged.astype(BF16), wo_s[...])
    o_ref[...] = _layer_norm(DEEPNORM_ALPHA * x_ref[...] + g1 * m, lg_ref[...], lb_ref[...])


def _merge(x, p, z, r_out, y, n_out, ssm_d, w_glu, w_branch, w_o, ln_g, ln_b, *, layer, L, row0, rstride):
    T = x.shape[0]
    tm = MERGE_TILE
    gate0 = 8 * MIX_W // D_MODEL

    def tok(w):
        return pl.BlockSpec((tm, w), lambda i: (i, 0))

    def full(shape):
        return pl.BlockSpec((None,) + shape, lambda i: (layer,) + (0,) * len(shape))

    body = functools.partial(_merge_body, L=L, row0=row0, rstride=rstride)
    return pl.pallas_call(
        body,
        grid=(T // tm,),
        in_specs=[tok(D_MODEL), full((N_PAD_ROWS, 6 * D_MODEL)), tok(MIX_W),
                  pl.BlockSpec((tm, MIX_W), lambda i: (i, SU_SECTION)), tok(MIX_W), tok(MIX_W),
                  pl.BlockSpec((tm, D_MODEL), lambda i: (i, gate0)),
                  pl.BlockSpec((tm, D_MODEL), lambda i: (i, gate0 + 1)),
                  pl.BlockSpec((tm, D_MODEL), lambda i: (i, gate0 + 2)),
                  full((1, MIX_W)), full((MIX_W, MIX_W)), full((3, MIX_W, D_MODEL)),
                  full((D_MODEL, D_MODEL)), full((1, D_MODEL)), full((1, D_MODEL))],
        out_specs=tok(D_MODEL),
        out_shape=jax.ShapeDtypeStruct((T, D_MODEL), F32),
        scratch_shapes=[pltpu.VMEM((MIX_W, MIX_W), BF16), pltpu.VMEM((3, MIX_W, D_MODEL), BF16),
                        pltpu.VMEM((D_MODEL, D_MODEL), BF16)],
        name="merge",
        compiler_params=_params("arbitrary"),
    )(x, p, r_out, z, y, n_out, z, z, z, ssm_d.reshape(DEPTH, 1, MIX_W), w_glu, w_branch, w_o,
      ln_g.reshape(DEPTH, 1, D_MODEL), ln_b.reshape(DEPTH, 1, D_MODEL))


def _ffn_body(x_ref, p_ref, wa_ref, wb_ref, cwa_ref, cwb_ref, cba_ref, cbb_ref, wd_ref, lg_ref, lb_ref,
              o_ref, h_scr, acc_scr, *, L, row0, rstride):
    i = pl.program_id(0)
    j = pl.program_id(1)
    tm = x_ref.shape[0]
    nb = tm // L

    @pl.when(j == 0)
    def _():
        for s in range(nb):
            row = row0 + rstride * (i * nb + s)
            sh = _mod_row(p_ref, row, 3)
            sc = _mod_row(p_ref, row, 4)
            h_scr[s * L:(s + 1) * L, :] = (x_ref[s * L:(s + 1) * L, :] * (1.0 + sc) + sh).astype(BF16)
        acc_scr[...] = jnp.zeros_like(acc_scr)

    t = lax.broadcasted_iota(jnp.int32, (tm, 1), 0) % L
    has_prev = t != 0
    has_next = t != L - 1

    def conv(w_ref, cw_ref, cb_ref):
        zc = _dot(h_scr[...], w_ref[...].astype(BF16))
        zp = jnp.where(has_prev, pltpu.roll(zc, 1, 0), 0.0)
        zn = jnp.where(has_next, pltpu.roll(zc, tm - 1, 0), 0.0)
        return zp * cw_ref[0:1, :] + zc * cw_ref[1:2, :] + zn * cw_ref[2:3, :] + cb_ref[...]

    a = conv(wa_ref, cwa_ref, cba_ref)
    b = conv(wb_ref, cwb_ref, cbb_ref)
    acc_scr[...] += _dot((jax.nn.gelu(a) * b).astype(BF16), wd_ref[...].astype(BF16))

    @pl.when(j == pl.num_programs(1) - 1)
    def _():
        for s in range(nb):
            row = row0 + rstride * (i * nb + s)
            g2 = _mod_row(p_ref, row, 5)
            sl = slice(s * L, (s + 1) * L)
            o_ref[sl, :] = _layer_norm(DEEPNORM_ALPHA * x_ref[sl, :] + g2 * acc_scr[sl, :],
                                       lg_ref[...], lb_ref[...])


def _conv_ffn(x, p, w_up, conv_w, conv_b, w_down, ln_g, ln_b, *, layer, L, row0, rstride):
    T = x.shape[0]
    tm = TOKEN_TILE
    nff = D_FF // FF_TILE
    body = functools.partial(_ffn_body, L=L, row0=row0, rstride=rstride)
    conv_b = conv_b.reshape(DEPTH, 1, 2 * D_FF)
    return pl.pallas_call(
        body,
        grid=(T // tm, nff),
        in_specs=[pl.BlockSpec((tm, D_MODEL), lambda i, j: (i, 0)),
                  pl.BlockSpec((None, N_PAD_ROWS, 6 * D_MODEL), lambda i, j: (layer, 0, 0)),
                  pl.BlockSpec((None, D_MODEL, FF_TILE), lambda i, j: (layer, 0, j)),
                  pl.BlockSpec((None, D_MODEL, FF_TILE), lambda i, j: (layer, 0, nff + j)),
                  pl.BlockSpec((None, 3, FF_TILE), lambda i, j: (layer, 0, j)),
                  pl.BlockSpec((None, 3, FF_TILE), lambda i, j: (layer, 0, nff + j)),
                  pl.BlockSpec((None, 1, FF_TILE), lambda i, j: (layer, 0, j)),
                  pl.BlockSpec((None, 1, FF_TILE), lambda i, j: (layer, 0, nff + j)),
                  pl.BlockSpec((None, FF_TILE, D_MODEL), lambda i, j: (layer, j, 0)),
                  pl.BlockSpec((None, 1, D_MODEL), lambda i, j: (layer, 0, 0)),
                  pl.BlockSpec((None, 1, D_MODEL), lambda i, j: (layer, 0, 0))],
        out_specs=pl.BlockSpec((tm, D_MODEL), lambda i, j: (i, 0)),
        out_shape=jax.ShapeDtypeStruct((T, D_MODEL), F32),
        scratch_shapes=[pltpu.VMEM((tm, D_MODEL), BF16), pltpu.VMEM((tm, D_MODEL), F32)],
        name="conv_ffn",
        compiler_params=_params("arbitrary", "arbitrary"),
    )(x, p, w_up, w_up, conv_w, conv_w, conv_b, conv_b, w_down, ln_g.reshape(DEPTH, 1, D_MODEL),
      ln_b.reshape(DEPTH, 1, D_MODEL))


def _s5_states_in(state_ssm, layer):
    B = state_ssm.shape[0]
    nlb = MIX_W // LANES
    h = state_ssm[:, layer].reshape(B, 2, nlb, S5_LBLK_GROUPS, SSM_STATE, 2)
    return jnp.transpose(h, (1, 2, 0, 5, 3, 4)).reshape(2, nlb, B, 2 * S5_LBLK_GROUPS * SSM_STATE)


def _s5_states_out(fin):
    nlb, B = fin.shape[1], fin.shape[2]
    h = fin.reshape(2, nlb, B, 2, S5_LBLK_GROUPS, SSM_STATE)
    return jnp.transpose(h, (2, 0, 1, 4, 5, 3)).reshape(B, 2, SSM_GROUPS, SSM_STATE, 2)


def _layer(x, p, lw, *, B, L, row0, rstride, latent, layer, extra):
    T = B * L
    z, *kv = _inproj(x, p, lw['w_in'], layer=layer, L=L, row0=row0, rstride=rstride, want_kv=not latent)
    z3 = z.reshape(B, L, IN_COLS)
    log_gamma = jax.nn.log_sigmoid(lw['ret_decay'].astype(F32))
    if latent:
        r_out = _retention(z3, log_gamma, B=B, L=L, rope_tabs=extra['rope'], s0=extra['state_ret'],
                           layer=layer, want_state=False)[0]
        n_out = _neighbourhood_attention(z3, extra['cache_k'], extra['cache_v'], extra['bias'][layer],
                                         B=B, L=L, layer=layer)
        y, _ = _s5(z, lw['s5'], _s5_states_in(extra['state_ssm'], layer), B=B, L=L)
        states = None
    else:
        r_out, ret_state = _retention(z3, log_gamma, B=B, L=L, want_state=True)
        n_out = _context_attention(z3, B=B, L=L)
        y, fin = _s5(z, lw['s5'], None, B=B, L=L)
        nk = kv[0].reshape(B, L, NA_HEADS, NA_HEAD_DIM)
        nv = kv[1].reshape(B, L, NA_HEADS, NA_HEAD_DIM)
        states = (ret_state, _s5_states_out(fin), nk, nv)
    x = _merge(x, p, z, r_out.reshape(T, MIX_W), y, n_out.reshape(T, MIX_W),
               lw['ssm_d'], lw['ssm_w_glu'], lw['w_branch'], lw['w_o'], lw['ln1_g'], lw['ln1_b'],
               layer=layer, L=L, row0=row0, rstride=rstride)
    x = _conv_ffn(x, p, lw['w_up'], lw['conv_w'], lw['conv_b'], lw['w_down'], lw['ln2_g'], lw['ln2_b'],
                  layer=layer, L=L, row0=row0, rstride=rstride)
    return x, states


def kernel(x_prompt, x_sample, state_ret, state_ssm, cache_na_k, cache_na_v, c, c_ctx, w_ada, b_ada, w_in, ret_decay, ssm_a_re, ssm_a_im, ssm_log_dt, ssm_b_re, ssm_b_im, ssm_c_re, ssm_c_im, ssm_d, ssm_w_glu, na_rpb, w_branch, w_o, ln1_g, ln1_b, w_up, conv_w, conv_b, w_down, ln2_g, ln2_b):
    B, L, _ = x_prompt.shape
    Bd, Ld, _ = x_sample.shape
    Lc = cache_na_k.shape[2]

    cond = jnp.concatenate([c_ctx[None, :], c, jnp.zeros((N_PAD_ROWS - 1 - Bd, D_MODEL), F32)], 0)
    p_all = _ada(cond, w_ada, b_ada)

    extra = dict(rope=_rope_tables(Ld), state_ret=state_ret, state_ssm=state_ssm,
                 cache_k=cache_na_k.reshape(Bd, DEPTH, Lc, MIX_W),
                 cache_v=cache_na_v.reshape(Bd, DEPTH, Lc, MIX_W),
                 bias=[_na_bias_blocks(na_rpb[l]) for l in range(DEPTH)])

    xp = x_prompt.reshape(B * L, D_MODEL)
    xs = x_sample.reshape(Bd * Ld, D_MODEL)
    ret_states, ssm_states, na_ks, na_vs = [], [], [], []
    for l in range(DEPTH):
        lw = dict(w_in=w_in, ret_decay=ret_decay[l], ssm_d=ssm_d, ssm_w_glu=ssm_w_glu,
                  w_branch=w_branch, w_o=w_o, ln1_g=ln1_g, ln1_b=ln1_b, w_up=w_up,
                  conv_w=conv_w, conv_b=conv_b, w_down=w_down, ln2_g=ln2_g, ln2_b=ln2_b,
                  s5=_s5_operators(ssm_a_re[l], ssm_a_im[l], ssm_log_dt[l], ssm_b_re[l], ssm_b_im[l],
                                   ssm_c_re[l], ssm_c_im[l]))
        xp, (s_ret, s_ssm, nk, nv) = _layer(xp, p_all, lw, B=B, L=L, row0=0, rstride=0,
                                            latent=False, layer=l, extra=None)
        ret_states.append(s_ret)
        ssm_states.append(s_ssm)
        na_ks.append(nk)
        na_vs.append(nv)
        xs, _ = _layer(xs, p_all, lw, B=Bd, L=Ld, row0=1, rstride=1, latent=True, layer=l, extra=extra)
    return (xp.reshape(B, L, D_MODEL), xs.reshape(Bd, Ld, D_MODEL),
            jnp.stack(ret_states, 1), jnp.stack(ssm_states, 1), jnp.stack(na_ks, 1), jnp.stack(na_vs, 1))
```

```python
import functools

import jax
import jax.numpy as jnp
import numpy as np
from jax import lax
from jax.experimental import pallas as pl
from jax.experimental.pallas import tpu as pltpu

F32 = jnp.float32
BF16 = jnp.bfloat16

D_MODEL = 1024
DEPTH = 2
GRID_W = 64
MIX_W = D_MODEL // 2
N_RET_HEADS = 4
RET_DK = MIX_W // N_RET_HEADS
SSM_GROUP = 16
SSM_GROUPS = MIX_W // SSM_GROUP
SSM_STATE = 64
NA_HEADS = 8
NA_HEAD_DIM = MIX_W // NA_HEADS
NA_KR = 8
NA_KW = 16
D_FF = ((8 * D_MODEL // 3 + 127) // 128) * 128
ROPE_BASE = 10000.0
LN_EPS = 1e-5
NEG_INF = -1e30
DEEPNORM_ALPHA = (2 * DEPTH) ** 0.25
IN_COLS = 8 * MIX_W + 3 * D_MODEL

VMEM_LIMIT_BYTES = 56 * 1024 * 1024
LANES = 128

TOKEN_TILE = 1024
MERGE_TILE = 256
COL_TILE = 512
SU_SECTION = 4
NK_TILE = 6
FF_TILE = 256
RET_CHUNK = 256
RET_ROWS = 1024
S5_CHUNK = 8
S5_PITCH_PAD = 8
S5_LBLK_GROUPS = LANES // SSM_GROUP
N_PAD_ROWS = 8
CTX_ATTN_BATCH = 4
NA_CHUNK_ROWS = 4
NA_WIN_ROWS = 12


def _params(*sem):
    return pltpu.CompilerParams(dimension_semantics=sem, vmem_limit_bytes=VMEM_LIMIT_BYTES)


def _dot(a, b):
    return jnp.dot(a, b, preferred_element_type=F32)


def _dot_nt(a, b):
    return lax.dot_general(a, b, (((1,), (1,)), ((), ())), preferred_element_type=F32)


def _layer_norm(x, g, b):
    mu = jnp.mean(x, -1, keepdims=True)
    xc = x - mu
    var = jnp.mean(xc * xc, -1, keepdims=True)
    return xc * lax.rsqrt(var + LN_EPS) * g + b


def _ada_body(c_ref, w_ref, b_ref, o_ref):
    c = c_ref[...]
    s = c * jax.nn.sigmoid(c)
    o_ref[...] = _dot(s.astype(BF16), w_ref[...].astype(BF16)) + b_ref[...]


def _ada(cond, w_ada, b_ada):
    tn = 1024
    return pl.pallas_call(
        _ada_body,
        grid=(DEPTH, 6 * D_MODEL // tn),
        in_specs=[pl.BlockSpec((N_PAD_ROWS, D_MODEL), lambda l, j: (0, 0)),
                  pl.BlockSpec((None, D_MODEL, tn), lambda l, j: (l, 0, j)),
                  pl.BlockSpec((None, 1, tn), lambda l, j: (l, 0, j))],
        out_specs=pl.BlockSpec((None, N_PAD_ROWS, tn), lambda l, j: (l, 0, j)),
        out_shape=jax.ShapeDtypeStruct((DEPTH, N_PAD_ROWS, 6 * D_MODEL), F32),
        name="ada",
        compiler_params=_params("arbitrary", "arbitrary"),
    )(cond, w_ada, b_ada.reshape(DEPTH, 1, 6 * D_MODEL))


def _mod_row(p_ref, row, k):
    return p_ref[pl.ds(row, 1), k * D_MODEL:(k + 1) * D_MODEL]


def _inproj_body(x_ref, p_ref, w_ref, z_ref, *rest, L, row0, rstride):
    h_scr, w_scr = rest[-2:]
    kv_refs = rest[:-2]
    j = pl.program_id(0)
    i = pl.program_id(1)
    nb = x_ref.shape[0] // L

    @pl.when(j == 0)
    def _():
        for s in range(nb):
            row = row0 + rstride * (i * nb + s)
            sh = _mod_row(p_ref, row, 0)
            sc = _mod_row(p_ref, row, 1)
            h_scr[i, s * L:(s + 1) * L, :] = (x_ref[s * L:(s + 1) * L, :] * (1.0 + sc) + sh).astype(BF16)

    @pl.when(i == 0)
    def _():
        w_scr[...] = w_ref[...].astype(BF16)

    acc = _dot(h_scr[i], w_scr[...])
    z_ref[...] = acc.astype(BF16)

    for n, ref in enumerate(kv_refs):
        @pl.when(j == NK_TILE + n)
        def _(ref=ref):
            ref[...] = acc


def _inproj(x, p, w_in, *, layer, L, row0, rstride, want_kv):
    T = x.shape[0]
    tm = TOKEN_TILE
    n_i = T // tm
    body = functools.partial(_inproj_body, L=L, row0=row0, rstride=rstride)
    n_kv = 2 if want_kv else 0

    def only_at(tile):
        return lambda j, i: jnp.where(j < tile, 0, jnp.where(j > tile, n_i - 1, i))

    kv_i = [only_at(NK_TILE + n) for n in range(n_kv)]
    return pl.pallas_call(
        body,
        grid=(IN_COLS // COL_TILE, n_i),
        in_specs=[pl.BlockSpec((tm, D_MODEL), lambda j, i: (jnp.where(j == 0, i, n_i - 1), 0)),
                  pl.BlockSpec((None, N_PAD_ROWS, 6 * D_MODEL), lambda j, i: (layer, 0, 0)),
                  pl.BlockSpec((None, D_MODEL, COL_TILE), lambda j, i: (layer, 0, j))],
        out_specs=[pl.BlockSpec((tm, COL_TILE), lambda j, i: (i, j))]
        + [pl.BlockSpec((tm, MIX_W), lambda j, i, f=f: (f(j, i), 0)) for f in kv_i],
        out_shape=[jax.ShapeDtypeStruct((T, IN_COLS), BF16)] + [jax.ShapeDtypeStruct((T, MIX_W), F32)] * n_kv,
        scratch_shapes=[pltpu.VMEM((n_i, tm, D_MODEL), BF16), pltpu.VMEM((D_MODEL, COL_TILE), BF16)],
        name="inproj",
        compiler_params=_params("arbitrary", "arbitrary"),
    )(x, p, w_in)


def _rope(x, cos, s_up, s_dn):
    return x * cos + pltpu.roll(x, 96, 1) * s_up + pltpu.roll(x, 32, 1) * s_dn


def _ret_body(*refs, n, rope, has_s0, want_state):
    refs = list(refs)
    lg_ref, q_ref, k_ref, v_ref, g_ref = refs[:5]
    refs = refs[5:]
    if rope:
        cos_ref, sup_ref, sdn_ref = refs[:3]
        refs = refs[3:]
    if has_s0:
        s0_ref = refs[0]
        refs = refs[1:]
    o_ref = refs[0]
    refs = refs[1:]
    if want_state:
        st_ref = refs[0]
        refs = refs[1:]
    q_scr, k_scr, sb_scr, decay_scr = refs

    C = RET_CHUNK
    h = pl.program_id(0)
    lf = lg_ref[0, h]
    lb = lg_ref[1, h]

    @pl.when(pl.program_id(1) == 0)
    def _():
        ti = lax.broadcasted_iota(jnp.int32, (C, C), 0)
        si = lax.broadcasted_iota(jnp.int32, (C, C), 1)
        dlt = (ti - si).astype(F32)
        decay_scr[...] = (jnp.where(dlt >= 0, jnp.exp(lf * jnp.maximum(dlt, 0.0)), 0.0)
                          + jnp.where(dlt <= 0, jnp.exp(lb * jnp.maximum(-dlt, 0.0)), 0.0))

    tcol = lax.broadcasted_iota(jnp.int32, (C, 1), 0).astype(F32)
    qd_f = jnp.exp(lf * (tcol + 1.0))
    qd_b = jnp.exp(lb * (C - tcol))
    kd_f = jnp.exp(lf * (C - 1.0 - tcol))
    kd_b = jnp.exp(lb * tcol)
    cd_f = jnp.exp(lf * jnp.full((1, RET_DK), float(C), F32))
    cd_b = jnp.exp(lb * jnp.full((1, RET_DK), float(C), F32))

    def kv_outer(kc, vc, kd):
        return _dot((kc * kd).T.astype(BF16), vc)

    for bb in range(q_ref.shape[0]):
        q = q_ref[bb].astype(F32)
        k = k_ref[bb].astype(F32)
        if rope:
            q = _rope(q, cos_ref[...], sup_ref[...], sdn_ref[...])
            k = _rope(k, cos_ref[...], sup_ref[...], sdn_ref[...])
        q_scr[bb] = q
        k_scr[bb] = k * (RET_DK ** -0.5)

        s_b = s0_ref[bb, 1] if has_s0 else jnp.zeros((RET_DK, RET_DK), F32)
        for i in reversed(range(n)):
            sb_scr[bb, i] = s_b
            if i > 0 or want_state:
                s_b = s_b * cd_b + kv_outer(k_scr[bb, i * C:(i + 1) * C, :], v_ref[bb, i * C:(i + 1) * C, :], kd_b)

        s_f = s0_ref[bb, 0] if has_s0 else jnp.zeros((RET_DK, RET_DK), F32)
        for i in range(n):
            sl = slice(i * C, (i + 1) * C)
            qc = q_scr[bb, sl, :]
            kc = k_scr[bb, sl, :]
            vc = v_ref[bb, sl, :]
            att = _dot_nt(qc.astype(BF16), kc.astype(BF16)) * decay_scr[...]
            o = _dot(att.astype(BF16), vc)
            o = o + _dot((qc * qd_f).astype(BF16), s_f.astype(BF16))
            o = o + _dot((qc * qd_b).astype(BF16), sb_scr[bb, i].astype(BF16))
            mu = jnp.mean(o, -1, keepdims=True)
            oc = o - mu
            var = jnp.mean(oc * oc, -1, keepdims=True)
            gc = g_ref[bb, sl, :].astype(F32)
            o_ref[bb, sl, :] = (oc * lax.rsqrt(var + LN_EPS) * (gc * jax.nn.sigmoid(gc))).astype(BF16)
            if i < n - 1 or want_state:
                s_f = s_f * cd_f + kv_outer(kc, vc, kd_f)

        if want_state:
            st_ref[bb, 0] = s_f
            st_ref[bb, 1] = s_b


def _retention(z, log_gamma, *, B, L, rope_tabs=None, s0=None, layer=0, want_state):
    n = L // RET_CHUNK
    H = N_RET_HEADS
    nblk = MIX_W // RET_DK

    nbb = max(1, RET_ROWS // L)
    assert B % nbb == 0

    def sec(s):
        return pl.BlockSpec((nbb, L, RET_DK), lambda h, b: (b, 0, s * nblk + h))

    in_specs = [pl.BlockSpec(memory_space=pltpu.SMEM), sec(0), sec(1), sec(2), sec(3)]
    args = [log_gamma, z, z, z, z]
    if rope_tabs is not None:
        in_specs += [pl.BlockSpec((L, RET_DK), lambda h, b: (0, 0))] * 3
        args += list(rope_tabs)
    if s0 is not None:
        in_specs.append(pl.BlockSpec((nbb, None, 2, None, RET_DK, RET_DK), lambda h, b: (b, layer, 0, h, 0, 0)))
        args.append(s0)
    out_specs = [pl.BlockSpec((nbb, L, RET_DK), lambda h, b: (b, 0, h))]
    out_shape = [jax.ShapeDtypeStruct((B, L, MIX_W), BF16)]
    if want_state:
        out_specs.append(pl.BlockSpec((nbb, 2, None, RET_DK, RET_DK), lambda h, b: (b, 0, h, 0, 0)))
        out_shape.append(jax.ShapeDtypeStruct((B, 2, H, RET_DK, RET_DK), F32))
    body = functools.partial(_ret_body, n=n, rope=rope_tabs is not None, has_s0=s0 is not None,
                             want_state=want_state)
    return pl.pallas_call(
        body,
        grid=(H, B // nbb),
        in_specs=in_specs,
        out_specs=out_specs,
        out_shape=out_shape,
        scratch_shapes=[pltpu.VMEM((nbb, L, RET_DK), F32), pltpu.VMEM((nbb, L, RET_DK), F32),
                        pltpu.VMEM((nbb, n, RET_DK, RET_DK), F32), pltpu.VMEM((RET_CHUNK, RET_CHUNK), F32)],
        name="retention",
        compiler_params=_params("arbitrary", "arbitrary"),
    )(*args)


def _rope_tables(L):
    pos = jnp.arange(L)
    row = (pos // GRID_W).astype(F32)
    col = (pos % GRID_W).astype(F32)
    quarter = RET_DK // 4
    inv_freq = ROPE_BASE ** (-jnp.arange(quarter, dtype=F32) / quarter)
    ang_r = row[:, None] * inv_freq[None, :]
    ang_c = col[:, None] * inv_freq[None, :]
    zero = jnp.zeros_like(ang_r)
    cos = jnp.concatenate([jnp.cos(ang_r), jnp.cos(ang_r), jnp.cos(ang_c), jnp.cos(ang_c)], -1)
    s_up = jnp.concatenate([-jnp.sin(ang_r), zero, -jnp.sin(ang_c), zero], -1)
    s_dn = jnp.concatenate([zero, jnp.sin(ang_r), zero, jnp.sin(ang_c)], -1)
    return cos, s_up, s_dn


def _s5_body(*refs, B, nC, has_h0):
    refs = list(refs)
    u_ref, dk_ref, bs_ref, cp_ref, a8_ref = refs[:5]
    refs = refs[5:]
    if has_h0:
        h0_ref = refs[0]
        refs = refs[1:]
    y_ref, fin_ref, u_scr, a_scr, m_scr, bs_scr, cp_scr, s_scr, x_scr, y_scr = refs

    TC = S5_CHUNK
    R = B * nC
    P = nC + S5_PITCH_PAD
    nsl = s_scr.shape[0]
    half = nsl // 2
    ng = S5_LBLK_GROUPS
    sw = ng * SSM_STATE

    u_scr[...] = u_ref[...].astype(F32)
    for s in range(TC):
        a_scr[:, s * LANES:(s + 1) * LANES] = u_scr[pl.ds(s, R, stride=TC), :].astype(BF16)

    bs_scr[...] = jnp.zeros_like(bs_scr)
    cp_scr[...] = jnp.zeros_like(cp_scr)

    for d in range(2):
        for s in range(TC):
            for t in range(TC):
                k = (t - s) if d == 0 else (s - t)
                blk = dk_ref[d, k] if k >= 0 else jnp.zeros((LANES, LANES), BF16)
                m_scr[s * LANES:(s + 1) * LANES, t * LANES:(t + 1) * LANES] = blk
        for s in range(TC):
            for g in range(ng):
                for part in range(2):
                    rows = slice(s * LANES + g * SSM_GROUP, s * LANES + (g + 1) * SSM_GROUP)
                    cols = slice(part * sw + g * SSM_STATE, part * sw + (g + 1) * SSM_STATE)
                    lo = (g % 2) * SSM_STATE
                    bs_scr[rows, cols] = bs_ref[d, s, g, part, :, lo:lo + SSM_STATE]
                    cp_scr[rows, cols] = cp_ref[d, s, g, part, :, lo:lo + SSM_STATE]

        a = a_scr[...]
        yd = _dot(a, m_scr[...])
        if d == 0:
            y_scr[...] = yd
        else:
            y_scr[...] += yd

        sm = _dot(a, bs_scr[...])
        for b in range(B):
            for sl in range(nsl):
                s_scr[sl, b * P:b * P + nC, :] = sm[b * nC:(b + 1) * nC, sl * LANES:(sl + 1) * LANES]

        a_r = [jnp.broadcast_to(a8_ref[d, 0, :, q * LANES:(q + 1) * LANES], (B, LANES)) for q in range(half)]
        a_i = [jnp.broadcast_to(a8_ref[d, 1, :, q * LANES:(q + 1) * LANES], (B, LANES)) for q in range(half)]
        if has_h0:
            init = tuple(h0_ref[d, :, sl * LANES:(sl + 1) * LANES] for sl in range(nsl))
        else:
            init = tuple(jnp.zeros((B, LANES), F32) for _ in range(nsl))

        def step(j, carry, d=d, a_r=a_r, a_i=a_i):
            c = j if d == 0 else nC - 1 - j
            rows = pl.ds(c, B, stride=P)
            new_r, new_i = [], []
            for q in range(half):
                xr, xi = carry[q], carry[half + q]
                sr = s_scr[q, rows, :]
                si = s_scr[half + q, rows, :]
                s_scr[q, rows, :] = xr
                s_scr[half + q, rows, :] = xi
                new_r.append(a_r[q] * xr - a_i[q] * xi + sr)
                new_i.append(a_r[q] * xi + a_i[q] * xr + si)
            return tuple(new_r + new_i)

        fin = lax.fori_loop(0, nC, step, init)
        for sl in range(nsl):
            fin_ref[d, :, sl * LANES:(sl + 1) * LANES] = fin[sl]

        for b in range(B):
            for sl in range(nsl):
                x_scr[b * nC:(b + 1) * nC, sl * LANES:(sl + 1) * LANES] = \
                    s_scr[sl, b * P:b * P + nC, :].astype(BF16)
        y_scr[...] += _dot_nt(x_scr[...], cp_scr[...])

    for t in range(TC):
        y_ref[pl.ds(t, R, stride=TC), :] = y_scr[:, t * LANES:(t + 1) * LANES]


def _s5(z, ops, h0, *, B, L):
    dk, bs, cp, a8 = ops
    T = B * L
    nC = L // S5_CHUNK
    nlb = MIX_W // LANES
    sc = 2 * S5_LBLK_GROUPS * SSM_STATE
    kc = S5_CHUNK * LANES
    su0 = SU_SECTION * MIX_W // LANES

    blocks = pl.BlockSpec((2, None, S5_CHUNK, S5_LBLK_GROUPS, 2, SSM_GROUP, LANES),
                          lambda lb: (0, lb, 0, 0, 0, 0, 0))
    in_specs = [pl.BlockSpec((T, LANES), lambda lb: (0, su0 + lb)),
                pl.BlockSpec((2, None, S5_CHUNK, LANES, LANES), lambda lb: (0, lb, 0, 0, 0)),
                blocks, blocks,
                pl.BlockSpec((2, None, 2, 1, sc // 2), lambda lb: (0, lb, 0, 0, 0))]
    args = [z, dk, bs, cp, a8]
    if h0 is not None:
        in_specs.append(pl.BlockSpec((2, None, B, sc), lambda lb: (0, lb, 0, 0)))
        args.append(h0)
    body = functools.partial(_s5_body, B=B, nC=nC, has_h0=h0 is not None)
    return pl.pallas_call(
        body,
        grid=(nlb,),
        in_specs=in_specs,
        out_specs=[pl.BlockSpec((T, LANES), lambda lb: (0, lb)),
                   pl.BlockSpec((2, None, B, sc), lambda lb: (0, lb, 0, 0))],
        out_shape=[jax.ShapeDtypeStruct((T, MIX_W), F32),
                   jax.ShapeDtypeStruct((2, nlb, B, sc), F32)],
        scratch_shapes=[pltpu.VMEM((T, LANES), F32), pltpu.VMEM((B * nC, kc), BF16),
                        pltpu.VMEM((kc, kc), BF16), pltpu.VMEM((kc, sc), BF16), pltpu.VMEM((kc, sc), BF16),
                        pltpu.VMEM((sc // LANES, B * (nC + S5_PITCH_PAD), LANES), F32),
                        pltpu.VMEM((B * nC, sc), BF16), pltpu.VMEM((B * nC, kc), F32)],
        name="s5",
        compiler_params=_params("arbitrary"),
    )(*args)


def _s5_operators(a_re, a_im, log_dt, b_re, b_im, c_re, c_im):
    TC = S5_CHUNK
    nlb = MIX_W // LANES
    ng = S5_LBLK_GROUPS
    lr = jnp.minimum(a_re, -1e-4)
    li = a_im
    dt = jnp.exp(log_dt)[..., None]
    k = jnp.arange(TC + 1, dtype=F32)[:, None, None, None]
    mag = jnp.exp(k * (lr * dt)[None])
    pr = mag * jnp.cos(k * (li * dt)[None])
    pi = mag * jnp.sin(k * (li * dt)[None])
    ar, ai = pr[1], pi[1]
    den = lr * lr + li * li
    sr = ((ar - 1.0) * lr + ai * li) / den
    si = (ai * lr - (ar - 1.0) * li) / den
    bbr = sr[..., None] * b_re[None] - si[..., None] * b_im[None]
    bbi = sr[..., None] * b_im[None] + si[..., None] * b_re[None]
    car = c_re[None] * pr[:, :, :, None, :] - c_im[None] * pi[:, :, :, None, :]
    cai = c_re[None] * pi[:, :, :, None, :] + c_im[None] * pr[:, :, :, None, :]
    eye = jnp.eye(ng, dtype=F32)

    kern = jnp.sum(car[:TC, :, :, :, None, :] * jnp.swapaxes(bbr, -1, -2)[None, :, :, None]
                   - cai[:TC, :, :, :, None, :] * jnp.swapaxes(bbi, -1, -2)[None, :, :, None], axis=-1)
    dk = jnp.einsum('kdbgoc,gG->dbkgcGo', kern.reshape(TC, 2, nlb, ng, SSM_GROUP, SSM_GROUP), eye)
    dk = dk.reshape(2, nlb, TC, LANES, LANES).astype(BF16)

    abr = pr[:TC, ..., None] * bbr[None] - pi[:TC, ..., None] * bbi[None]
    abi = pr[:TC, ..., None] * bbi[None] + pi[:TC, ..., None] * bbr[None]

    def by_step(x):
        return jnp.stack([jnp.flip(x[:, 0], 0), x[:, 1]], 1)

    def blocks(x):
        x = jnp.transpose(x, (2, 3, 1, 4, 0, 5, 6)).astype(BF16)
        return jnp.broadcast_to(x[..., None, :], x.shape[:-1] + (2, SSM_STATE)).reshape(x.shape[:-1] + (LANES,))

    bval = jnp.stack([by_step(abr), by_step(abi)], 0).reshape(2, TC, 2, nlb, ng, SSM_STATE, SSM_GROUP)
    bs = blocks(jnp.swapaxes(bval, -1, -2))

    def by_out(x):
        return jnp.stack([x[1:, 0], jnp.flip(x[1:, 1], 0)], 1)

    cval = jnp.stack([by_out(car), -by_out(cai)], 0).reshape(2, TC, 2, nlb, ng, SSM_GROUP, SSM_STATE)
    cp = blocks(cval)

    sw = ng * SSM_STATE
    a8 = jnp.stack([pr[TC].reshape(2, nlb, 1, sw), pi[TC].reshape(2, nlb, 1, sw)], 2)
    return dk, bs, cp, a8


def _head_masks(shape):
    lane = lax.broadcasted_iota(jnp.int32, shape, 1)
    return lane < NA_HEAD_DIM


def _cattn_body(q_ref, k_ref, v_ref, o_ref):
    first = _head_masks(q_ref.shape[1:])
    for bb in range(q_ref.shape[0]):
        q = q_ref[bb]
        k = k_ref[bb]
        v = v_ref[bb]
        outs = []
        for e in range(2):
            qe = jnp.where(first if e == 0 else jnp.logical_not(first), q, jnp.zeros_like(q))
            s = _dot_nt(qe, k) * (NA_HEAD_DIM ** -0.5)
            m = jnp.max(s, -1, keepdims=True)
            p = jnp.exp(s - m)
            l = jnp.sum(p, -1, keepdims=True)
            outs.append(_dot(p.astype(BF16), v) / l)
        o_ref[bb] = jnp.where(first, outs[0], outs[1]).astype(BF16)


def _context_attention(z, *, B, L):
    nblk = MIX_W // LANES
    nb = CTX_ATTN_BATCH

    def sec(s):
        return pl.BlockSpec((nb, L, LANES), lambda b, hp: (b, 0, s * nblk + hp))

    return pl.pallas_call(
        _cattn_body,
        grid=(B // nb, nblk),
        in_specs=[sec(5), sec(6), sec(7)],
        out_specs=pl.BlockSpec((nb, L, LANES), lambda b, hp: (b, 0, hp)),
        out_shape=jax.ShapeDtypeStruct((B, L, MIX_W), BF16),
        name="ctx_attention",
        compiler_params=_params("arbitrary", "arbitrary"),
    )(z, z, z)


def _na_chunks(rows):
    half = NA_KR // 2
    plan, kinds = [], []
    for r0 in range(0, rows, NA_CHUNK_ROWS):
        rs = [min(max(r - half, 0), rows - NA_KR) for r in range(r0, r0 + NA_CHUNK_ROWS)]
        ws = min(rs[0], rows - NA_WIN_ROWS)
        assert rs[-1] + NA_KR <= ws + NA_WIN_ROWS
        kind = tuple((r0 + n - ws, rs[n] - ws) for n in range(NA_CHUNK_ROWS))
        if kind not in kinds:
            kinds.append(kind)
        plan.append((ws, kinds.index(kind)))
    return plan, kinds


def _na_body(q_ref, k_ref, v_ref, kc_ref, vc_ref, tb_ref, o_ref, bias_scr, *, rows):
    scale = NA_HEAD_DIM ** -0.5
    nq = NA_CHUNK_ROWS * GRID_W
    plan, kinds = _na_chunks(rows)
    n_off = 2 * NA_KR - 1

    @pl.when(pl.program_id(1) == 0)
    def _():
        for t, kind in enumerate(kinds):
            for e in range(2):
                for n, (r_rel, rs_rel) in enumerate(kind):
                    for kj in range(NA_WIN_ROWS):
                        off = kj - r_rel + NA_KR - 1 if rs_rel <= kj < rs_rel + NA_KR else n_off
                        lo = (kj % 2) * GRID_W
                        bias_scr[t, e * nq + n * GRID_W:e * nq + (n + 1) * GRID_W, kj * GRID_W:(kj + 1) * GRID_W] = \
                            tb_ref[e, off, :, lo:lo + GRID_W]

    kctx = kc_ref[...].astype(BF16)
    vctx = vc_ref[...].astype(BF16)
    first = _head_masks((nq, LANES))
    for c, (ws, kind) in enumerate(plan):
        qc = q_ref[c * nq:(c + 1) * nq, :]
        qs = jnp.concatenate([jnp.where(first, qc, jnp.zeros_like(qc)),
                              jnp.where(first, jnp.zeros_like(qc), qc)], 0)
        kw = k_ref[ws * GRID_W:(ws + NA_WIN_ROWS) * GRID_W, :]
        vw = v_ref[ws * GRID_W:(ws + NA_WIN_ROWS) * GRID_W, :]
        s_loc = _dot_nt(qs, kw) * scale + bias_scr[kind]
        s_ctx = _dot_nt(qs, kctx) * scale
        m = jnp.maximum(jnp.max(s_loc, -1, keepdims=True), jnp.max(s_ctx, -1, keepdims=True))
        p_loc = jnp.exp(s_loc - m)
        p_ctx = jnp.exp(s_ctx - m)
        l = jnp.sum(p_loc, -1, keepdims=True) + jnp.sum(p_ctx, -1, keepdims=True)
        o = (_dot(p_loc.astype(BF16), vw) + _dot(p_ctx.astype(BF16), vctx)) / l
        o_ref[c * nq:(c + 1) * nq, :] = jnp.where(first, o[:nq], o[nq:]).astype(BF16)


def _neighbourhood_attention(z, cache_k, cache_v, blocks, *, B, L, layer):
    nblk = MIX_W // LANES
    rows = L // GRID_W
    Lc = cache_k.shape[2]
    _, kinds = _na_chunks(rows)

    def sec(s):
        return pl.BlockSpec((None, L, LANES), lambda hp, b: (b, 0, s * nblk + hp))

    ctx = pl.BlockSpec((None, None, Lc, LANES), lambda hp, b: (b, layer, 0, hp))
    return pl.pallas_call(
        functools.partial(_na_body, rows=rows),
        grid=(nblk, B),
        in_specs=[sec(5), sec(6), sec(7), ctx, ctx,
                  pl.BlockSpec((None, 2, 2 * NA_KR, GRID_W, 2 * GRID_W), lambda hp, b: (hp, 0, 0, 0, 0))],
        out_specs=pl.BlockSpec((None, L, LANES), lambda hp, b: (b, 0, hp)),
        out_shape=jax.ShapeDtypeStruct((B, L, MIX_W), BF16),
        scratch_shapes=[pltpu.VMEM((len(kinds), 2 * NA_CHUNK_ROWS * GRID_W, NA_WIN_ROWS * GRID_W), F32)],
        name="nbr_attention",
        compiler_params=_params("arbitrary", "arbitrary"),
    )(z, z, z, cache_k, cache_v, blocks)


def _na_bias_blocks(rpb):
    nr, nc = 2 * NA_KR - 1, 2 * NA_KW - 1
    qc = np.arange(GRID_W)
    kc = np.arange(GRID_W)
    ws = np.clip(qc - NA_KW // 2, 0, GRID_W - NA_KW)
    col_ok = (kc[None, :] >= ws[:, None]) & (kc[None, :] < ws[:, None] + NA_KW)
    coff = np.clip(kc[None, :] - qc[:, None] + NA_KW - 1, 0, nc - 1)
    sel_c = ((coff[None] == np.arange(nc)[:, None, None]) & col_ok[None]).astype(np.float32)
    H = rpb.shape[0]
    t = jnp.einsum('hrc,cqk->hrqk', rpb.astype(F32), sel_c, precision=lax.Precision.HIGHEST)
    t = jnp.where(col_ok[None, None], t, NEG_INF)
    t = jnp.concatenate([t, jnp.full((H, 1, GRID_W, GRID_W), NEG_INF, F32)], 1)
    return jnp.concatenate([t, t], -1).reshape(H // 2, 2, nr + 1, GRID_W, 2 * GRID_W)


def _merge_body(x_ref, p_ref, r_ref, u_ref, y_ref, n_ref, ga_ref, gb_ref, gc_ref,
                d_ref, wglu_ref, wbr_ref, wo_ref, lg_ref, lb_ref, o_ref,
                wglu_s, wbr_s, wo_s, *, L, row0, rstride):
    i = pl.program_id(0)
    tm = x_ref.shape[0]

    @pl.when(i == 0)
    def _():
        wglu_s[...] = wglu_ref[...].astype(BF16)
        wbr_s[...] = wbr_ref[...].astype(BF16)
        wo_s[...] = wo_ref[...].astype(BF16)

    row = row0 + rstride * ((i * tm) // L)
    g1 = _mod_row(p_ref, row, 2)

    y = d_ref[...] * u_ref[...].astype(F32) + y_ref[...]
    y = jax.nn.gelu(y)
    s_out = y * jax.nn.sigmoid(_dot(y.astype(BF16), wglu_s[...]))

    def gate(ref):
        return jax.nn.sigmoid(ref[...].astype(F32))

    merged = (gate(ga_ref) * _dot(r_ref[...], wbr_s[0])
              + gate(gb_ref) * _dot(s_out.astype(BF16), wbr_s[1])
              + gate(gc_ref) * _dot(n_ref[...], wbr_s[2]))
    m = _dot(merged.astype(BF16), wo_s[...])
    o_ref[...] = _layer_norm(DEEPNORM_ALPHA * x_ref[...] + g1 * m, lg_ref[...], lb_ref[...])


def _merge(x, p, z, r_out, y, n_out, ssm_d, w_glu, w_branch, w_o, ln_g, ln_b, *, layer, L, row0, rstride):
    T = x.shape[0]
    tm = MERGE_TILE
    gate0 = 8 * MIX_W // D_MODEL

    def tok(w):
        return pl.BlockSpec((tm, w), lambda i: (i, 0))

    def full(shape):
        return pl.BlockSpec((None,) + shape, lambda i: (layer,) + (0,) * len(shape))

    body = functools.partial(_merge_body, L=L, row0=row0, rstride=rstride)
    return pl.pallas_call(
        body,
        grid=(T // tm,),
        in_specs=[tok(D_MODEL), full((N_PAD_ROWS, 6 * D_MODEL)), tok(MIX_W),
                  pl.BlockSpec((tm, MIX_W), lambda i: (i, SU_SECTION)), tok(MIX_W), tok(MIX_W),
                  pl.BlockSpec((tm, D_MODEL), lambda i: (i, gate0)),
                  pl.BlockSpec((tm, D_MODEL), lambda i: (i, gate0 + 1)),
                  pl.BlockSpec((tm, D_MODEL), lambda i: (i, gate0 + 2)),
                  full((1, MIX_W)), full((MIX_W, MIX_W)), full((3, MIX_W, D_MODEL)),
                  full((D_MODEL, D_MODEL)), full((1, D_MODEL)), full((1, D_MODEL))],
        out_specs=tok(D_MODEL),
        out_shape=jax.ShapeDtypeStruct((T, D_MODEL), F32),
        scratch_shapes=[pltpu.VMEM((MIX_W, MIX_W), BF16), pltpu.VMEM((3, MIX_W, D_MODEL), BF16),
                        pltpu.VMEM((D_MODEL, D_MODEL), BF16)],
        name="merge",
        compiler_params=_params("arbitrary"),
    )(x, p, r_out, z, y, n_out, z, z, z, ssm_d.reshape(DEPTH, 1, MIX_W), w_glu, w_branch, w_o,
      ln_g.reshape(DEPTH, 1, D_MODEL), ln_b.reshape(DEPTH, 1, D_MODEL))


def _ffn_body(x_ref, p_ref, wa_ref, wb_ref, cwa_ref, cwb_ref, cba_ref, cbb_ref, wd_ref, lg_ref, lb_ref,
              o_ref, h_scr, acc_scr, *, L, row0, rstride):
    i = pl.program_id(0)
    j = pl.program_id(1)
    tm = x_ref.shape[0]
    nb = tm // L

    @pl.when(j == 0)
    def _():
        for s in range(nb):
            row = row0 + rstride * (i * nb + s)
            sh = _mod_row(p_ref, row, 3)
            sc = _mod_row(p_ref, row, 4)
            h_scr[s * L:(s + 1) * L, :] = (x_ref[s * L:(s + 1) * L, :] * (1.0 + sc) + sh).astype(BF16)
        acc_scr[...] = jnp.zeros_like(acc_scr)

    t = lax.broadcasted_iota(jnp.int32, (tm, 1), 0) % L
    has_prev = t != 0
    has_next = t != L - 1

    def conv(w_ref, cw_ref, cb_ref):
        zc = _dot(h_scr[...], w_ref[...].astype(BF16))
        zp = jnp.where(has_prev, pltpu.roll(zc, 1, 0), 0.0)
        zn = jnp.where(has_next, pltpu.roll(zc, tm - 1, 0), 0.0)
        return zp * cw_ref[0:1, :] + zc * cw_ref[1:2, :] + zn * cw_ref[2:3, :] + cb_ref[...]

    a = conv(wa_ref, cwa_ref, cba_ref)
    b = conv(wb_ref, cwb_ref, cbb_ref)
    acc_scr[...] += _dot((jax.nn.gelu(a) * b).astype(BF16), wd_ref[...].astype(BF16))

    @pl.when(j == pl.num_programs(1) - 1)
    def _():
        for s in range(nb):
            row = row0 + rstride * (i * nb + s)
            g2 = _mod_row(p_ref, row, 5)
            sl = slice(s * L, (s + 1) * L)
            o_ref[sl, :] = _layer_norm(DEEPNORM_ALPHA * x_ref[sl, :] + g2 * acc_scr[sl, :],
                                       lg_ref[...], lb_ref[...])


def _conv_ffn(x, p, w_up, conv_w, conv_b, w_down, ln_g, ln_b, *, layer, L, row0, rstride):
    T = x.shape[0]
    tm = TOKEN_TILE
    nff = D_FF // FF_TILE
    body = functools.partial(_ffn_body, L=L, row0=row0, rstride=rstride)
    conv_b = conv_b.reshape(DEPTH, 1, 2 * D_FF)
    return pl.pallas_call(
        body,
        grid=(T // tm, nff),
        in_specs=[pl.BlockSpec((tm, D_MODEL), lambda i, j: (i, 0)),
                  pl.BlockSpec((None, N_PAD_ROWS, 6 * D_MODEL), lambda i, j: (layer, 0, 0)),
                  pl.BlockSpec((None, D_MODEL, FF_TILE), lambda i, j: (layer, 0, j)),
                  pl.BlockSpec((None, D_MODEL, FF_TILE), lambda i, j: (layer, 0, nff + j)),
                  pl.BlockSpec((None, 3, FF_TILE), lambda i, j: (layer, 0, j)),
                  pl.BlockSpec((None, 3, FF_TILE), lambda i, j: (layer, 0, nff + j)),
                  pl.BlockSpec((None, 1, FF_TILE), lambda i, j: (layer, 0, j)),
                  pl.BlockSpec((None, 1, FF_TILE), lambda i, j: (layer, 0, nff + j)),
                  pl.BlockSpec((None, FF_TILE, D_MODEL), lambda i, j: (layer, j, 0)),
                  pl.BlockSpec((None, 1, D_MODEL), lambda i, j: (layer, 0, 0)),
                  pl.BlockSpec((None, 1, D_MODEL), lambda i, j: (layer, 0, 0))],
        out_specs=pl.BlockSpec((tm, D_MODEL), lambda i, j: (i, 0)),
        out_shape=jax.ShapeDtypeStruct((T, D_MODEL), F32),
        scratch_shapes=[pltpu.VMEM((tm, D_MODEL), BF16), pltpu.VMEM((tm, D_MODEL), F32)],
        name="conv_ffn",
        compiler_params=_params("arbitrary", "arbitrary"),
    )(x, p, w_up, w_up, conv_w, conv_w, conv_b, conv_b, w_down, ln_g.reshape(DEPTH, 1, D_MODEL),
      ln_b.reshape(DEPTH, 1, D_MODEL))


def _s5_states_in(state_ssm, layer):
    B = state_ssm.shape[0]
    nlb = MIX_W // LANES
    h = state_ssm[:, layer].reshape(B, 2, nlb, S5_LBLK_GROUPS, SSM_STATE, 2)
    return jnp.transpose(h, (1, 2, 0, 5, 3, 4)).reshape(2, nlb, B, 2 * S5_LBLK_GROUPS * SSM_STATE)


def _s5_states_out(fin):
    nlb, B = fin.shape[1], fin.shape[2]
    h = fin.reshape(2, nlb, B, 2, S5_LBLK_GROUPS, SSM_STATE)
    return jnp.transpose(h, (2, 0, 1, 4, 5, 3)).reshape(B, 2, SSM_GROUPS, SSM_STATE, 2)


def _layer(x, p, lw, *, B, L, row0, rstride, latent, layer, extra):
    T = B * L
    z, *kv = _inproj(x, p, lw['w_in'], layer=layer, L=L, row0=row0, rstride=rstride, want_kv=not latent)
    z3 = z.reshape(B, L, IN_COLS)
    log_gamma = jax.nn.log_sigmoid(lw['ret_decay'].astype(F32))
    if latent:
        r_out = _retention(z3, log_gamma, B=B, L=L, rope_tabs=extra['rope'], s0=extra['state_ret'],
                           layer=layer, want_state=False)[0]
        n_out = _neighbourhood_attention(z3, extra['cache_k'], extra['cache_v'], extra['bias'][layer],
                                         B=B, L=L, layer=layer)
        y, _ = _s5(z, lw['s5'], _s5_states_in(extra['state_ssm'], layer), B=B, L=L)
        states = None
    else:
        r_out, ret_state = _retention(z3, log_gamma, B=B, L=L, want_state=True)
        n_out = _context_attention(z3, B=B, L=L)
        y, fin = _s5(z, lw['s5'], None, B=B, L=L)
        nk = kv[0].reshape(B, L, NA_HEADS, NA_HEAD_DIM)
        nv = kv[1].reshape(B, L, NA_HEADS, NA_HEAD_DIM)
        states = (ret_state, _s5_states_out(fin), nk, nv)
    x = _merge(x, p, z, r_out.reshape(T, MIX_W), y, n_out.reshape(T, MIX_W),
               lw['ssm_d'], lw['ssm_w_glu'], lw['w_branch'], lw['w_o'], lw['ln1_g'], lw['ln1_b'],
               layer=layer, L=L, row0=row0, rstride=rstride)
    x = _conv_ffn(x, p, lw['w_up'], lw['conv_w'], lw['conv_b'], lw['w_down'], lw['ln2_g'], lw['ln2_b'],
                  layer=layer, L=L, row0=row0, rstride=rstride)
    return x, states


def kernel(x_prompt, x_sample, state_ret, state_ssm, cache_na_k, cache_na_v, c, c_ctx, w_ada, b_ada, w_in, ret_decay, ssm_a_re, ssm_a_im, ssm_log_dt, ssm_b_re, ssm_b_im, ssm_c_re, ssm_c_im, ssm_d, ssm_w_glu, na_rpb, w_branch, w_o, ln1_g, ln1_b, w_up, conv_w, conv_b, w_down, ln2_g, ln2_b):
    B, L, _ = x_prompt.shape
    Bd, Ld, _ = x_sample.shape
    Lc = cache_na_k.shape[2]

    cond = jnp.concatenate([c_ctx[None, :], c, jnp.zeros((N_PAD_ROWS - 1 - Bd, D_MODEL), F32)], 0)
    p_all = _ada(cond, w_ada, b_ada)

    extra = dict(rope=_rope_tables(Ld), state_ret=state_ret, state_ssm=state_ssm,
                 cache_k=cache_na_k.reshape(Bd, DEPTH, Lc, MIX_W),
                 cache_v=cache_na_v.reshape(Bd, DEPTH, Lc, MIX_W),
                 bias=[_na_bias_blocks(na_rpb[l]) for l in range(DEPTH)])

    xp = x_prompt.reshape(B * L, D_MODEL)
    xs = x_sample.reshape(Bd * Ld, D_MODEL)
    ret_states, ssm_states, na_ks, na_vs = [], [], [], []
    for l in range(DEPTH):
        lw = dict(w_in=w_in, ret_decay=ret_decay[l], ssm_d=ssm_d, ssm_w_glu=ssm_w_glu,
                  w_branch=w_branch, w_o=w_o, ln1_g=ln1_g, ln1_b=ln1_b, w_up=w_up,
                  conv_w=conv_w, conv_b=conv_b, w_down=w_down, ln2_g=ln2_g, ln2_b=ln2_b,
                  s5=_s5_operators(ssm_a_re[l], ssm_a_im[l], ssm_log_dt[l], ssm_b_re[l], ssm_b_im[l],
                                   ssm_c_re[l], ssm_c_im[l]))
        xp, (s_ret, s_ssm, nk, nv) = _layer(xp, p_all, lw, B=B, L=L, row0=0, rstride=0,
                                            latent=False, layer=l, extra=None)
        ret_states.append(s_ret)
        ssm_states.append(s_ssm)
        na_ks.append(nk)
        na_vs.append(nv)
        xs, _ = _layer(xs, p_all, lw, B=Bd, L=Ld, row0=1, rstride=1, latent=True, layer=l, extra=extra)
    return (xp.reshape(B, L, D_MODEL), xs.reshape(Bd, Ld, D_MODEL),
            jnp.stack(ret_states, 1), jnp.stack(ssm_states, 1), jnp.stack(na_ks, 1), jnp.stack(na_vs, 1))
```

```python
import functools

import jax
import jax.numpy as jnp
import numpy as np
from jax import lax
from jax.experimental import pallas as pl
from jax.experimental.pallas import tpu as pltpu

F32 = jnp.float32
BF16 = jnp.bfloat16

D_MODEL = 1024
DEPTH = 2
GRID_W = 64
MIX_W = D_MODEL // 2
N_RET_HEADS = 4
RET_DK = MIX_W // N_RET_HEADS
SSM_GROUP = 16
SSM_GROUPS = MIX_W // SSM_GROUP
SSM_STATE = 64
NA_HEADS = 8
NA_HEAD_DIM = MIX_W // NA_HEADS
NA_KR = 8
NA_KW = 16
D_FF = ((8 * D_MODEL // 3 + 127) // 128) * 128
ROPE_BASE = 10000.0
LN_EPS = 1e-5
NEG_INF = -1e30
DEEPNORM_ALPHA = (2 * DEPTH) ** 0.25
IN_COLS = 8 * MIX_W + 3 * D_MODEL

VMEM_LIMIT_BYTES = 56 * 1024 * 1024
LANES = 128

TOKEN_TILE = 1024
MERGE_TILE = 256
COL_TILE = 1024
SU_SECTION = 4
NK_SECTION = 6
FF_TILE = 256
RET_CHUNK = 256
RET_ROWS = 1024
S5_CHUNK = 8
S5_PITCH_PAD = 8
S5_LBLK_GROUPS = LANES // SSM_GROUP
N_PAD_ROWS = 8
CTX_ATTN_BATCH = 4
NA_CHUNK_ROWS = 4
NA_WIN_ROWS = 12


def _params(*sem):
    return pltpu.CompilerParams(dimension_semantics=sem, vmem_limit_bytes=VMEM_LIMIT_BYTES)


def _dot(a, b):
    return jnp.dot(a, b, preferred_element_type=F32)


def _dot_nt(a, b):
    return lax.dot_general(a, b, (((1,), (1,)), ((), ())), preferred_element_type=F32)


def _layer_norm(x, g, b):
    mu = jnp.mean(x, -1, keepdims=True)
    xc = x - mu
    var = jnp.mean(xc * xc, -1, keepdims=True)
    return xc * lax.rsqrt(var + LN_EPS) * g + b


def _ada_body(c_ref, w_ref, b_ref, o_ref):
    c = c_ref[...]
    s = c * jax.nn.sigmoid(c)
    o_ref[...] = _dot(s.astype(BF16), w_ref[...].astype(BF16)) + b_ref[...]


def _ada(cond, w_ada, b_ada):
    tn = 1024
    return pl.pallas_call(
        _ada_body,
        grid=(DEPTH, 6 * D_MODEL // tn),
        in_specs=[pl.BlockSpec((N_PAD_ROWS, D_MODEL), lambda l, j: (0, 0)),
                  pl.BlockSpec((None, D_MODEL, tn), lambda l, j: (l, 0, j)),
                  pl.BlockSpec((None, 1, tn), lambda l, j: (l, 0, j))],
        out_specs=pl.BlockSpec((None, N_PAD_ROWS, tn), lambda l, j: (l, 0, j)),
        out_shape=jax.ShapeDtypeStruct((DEPTH, N_PAD_ROWS, 6 * D_MODEL), F32),
        name="ada",
        compiler_params=_params("arbitrary", "arbitrary"),
    )(cond, w_ada, b_ada.reshape(DEPTH, 1, 6 * D_MODEL))


def _mod_row(p_ref, row, k):
    return p_ref[pl.ds(row, 1), k * D_MODEL:(k + 1) * D_MODEL]


def _kv_tile(n):
    col = (NK_SECTION + n) * MIX_W
    return col // COL_TILE, col % COL_TILE


def _inproj_body(x_ref, p_ref, w_ref, z_ref, *rest, L, row0, rstride):
    h_scr, w_scr = rest[-2:]
    kv_refs = rest[:-2]
    j = pl.program_id(0)
    i = pl.program_id(1)
    nb = x_ref.shape[0] // L

    @pl.when(j == 0)
    def _():
        for s in range(nb):
            row = row0 + rstride * (i * nb + s)
            sh = _mod_row(p_ref, row, 0)
            sc = _mod_row(p_ref, row, 1)
            h_scr[i, s * L:(s + 1) * L, :] = (x_ref[s * L:(s + 1) * L, :] * (1.0 + sc) + sh).astype(BF16)

    @pl.when(i == 0)
    def _():
        w_scr[...] = w_ref[...].astype(BF16)

    acc = _dot(h_scr[i], w_scr[...])
    z_ref[...] = acc.astype(BF16)

    for n, ref in enumerate(kv_refs):
        tile, off = _kv_tile(n)

        @pl.when(j == tile)
        def _(ref=ref, off=off):
            ref[...] = acc[:, off:off + MIX_W]


def _inproj(x, p, w_in, *, layer, L, row0, rstride, want_kv):
    T = x.shape[0]
    tm = TOKEN_TILE
    n_i = T // tm
    body = functools.partial(_inproj_body, L=L, row0=row0, rstride=rstride)
    n_kv = 2 if want_kv else 0

    def only_at(tile):
        return lambda j, i: jnp.where(j < tile, 0, jnp.where(j > tile, n_i - 1, i))

    kv_i = [only_at(_kv_tile(n)[0]) for n in range(n_kv)]
    return pl.pallas_call(
        body,
        grid=(IN_COLS // COL_TILE, n_i),
        in_specs=[pl.BlockSpec((tm, D_MODEL), lambda j, i: (jnp.where(j == 0, i, n_i - 1), 0)),
                  pl.BlockSpec((None, N_PAD_ROWS, 6 * D_MODEL), lambda j, i: (layer, 0, 0)),
                  pl.BlockSpec((None, D_MODEL, COL_TILE), lambda j, i: (layer, 0, j))],
        out_specs=[pl.BlockSpec((tm, COL_TILE), lambda j, i: (i, j))]
        + [pl.BlockSpec((tm, MIX_W), lambda j, i, f=f: (f(j, i), 0)) for f in kv_i],
        out_shape=[jax.ShapeDtypeStruct((T, IN_COLS), BF16)] + [jax.ShapeDtypeStruct((T, MIX_W), F32)] * n_kv,
        scratch_shapes=[pltpu.VMEM((n_i, tm, D_MODEL), BF16), pltpu.VMEM((D_MODEL, COL_TILE), BF16)],
        name="inproj",
        compiler_params=_params("arbitrary", "arbitrary"),
    )(x, p, w_in)


def _rope(x, cos, s_up, s_dn):
    return x * cos + pltpu.roll(x, 96, 1) * s_up + pltpu.roll(x, 32, 1) * s_dn


def _ret_body(*refs, n, rope, has_s0, want_state):
    refs = list(refs)
    lg_ref, q_ref, k_ref, v_ref, g_ref = refs[:5]
    refs = refs[5:]
    if rope:
        cos_ref, sup_ref, sdn_ref = refs[:3]
        refs = refs[3:]
    if has_s0:
        s0_ref = refs[0]
        refs = refs[1:]
    o_ref = refs[0]
    refs = refs[1:]
    if want_state:
        st_ref = refs[0]
        refs = refs[1:]
    q_scr, k_scr, sb_scr, decay_scr = refs

    C = RET_CHUNK
    h = pl.program_id(0)
    lf = lg_ref[0, h]
    lb = lg_ref[1, h]

    @pl.when(pl.program_id(1) == 0)
    def _():
        ti = lax.broadcasted_iota(jnp.int32, (C, C), 0)
        si = lax.broadcasted_iota(jnp.int32, (C, C), 1)
        dlt = (ti - si).astype(F32)
        decay_scr[...] = (jnp.where(dlt >= 0, jnp.exp(lf * jnp.maximum(dlt, 0.0)), 0.0)
                          + jnp.where(dlt <= 0, jnp.exp(lb * jnp.maximum(-dlt, 0.0)), 0.0))

    tcol = lax.broadcasted_iota(jnp.int32, (C, 1), 0).astype(F32)
    qd_f = jnp.exp(lf * (tcol + 1.0))
    qd_b = jnp.exp(lb * (C - tcol))
    kd_f = jnp.exp(lf * (C - 1.0 - tcol))
    kd_b = jnp.exp(lb * tcol)
    cd_f = jnp.exp(lf * jnp.full((1, RET_DK), float(C), F32))
    cd_b = jnp.exp(lb * jnp.full((1, RET_DK), float(C), F32))

    def kv_outer(kc, vc, kd):
        return _dot((kc * kd).T.astype(BF16), vc)

    for bb in range(q_ref.shape[0]):
        q = q_ref[bb].astype(F32)
        k = k_ref[bb].astype(F32)
        if rope:
            q = _rope(q, cos_ref[...], sup_ref[...], sdn_ref[...])
            k = _rope(k, cos_ref[...], sup_ref[...], sdn_ref[...])
        q_scr[bb] = q
        k_scr[bb] = k * (RET_DK ** -0.5)

        s_b = s0_ref[bb, 1] if has_s0 else jnp.zeros((RET_DK, RET_DK), F32)
        for i in reversed(range(n)):
            sb_scr[bb, i] = s_b
            if i > 0 or want_state:
                s_b = s_b * cd_b + kv_outer(k_scr[bb, i * C:(i + 1) * C, :], v_ref[bb, i * C:(i + 1) * C, :], kd_b)

        s_f = s0_ref[bb, 0] if has_s0 else jnp.zeros((RET_DK, RET_DK), F32)
        for i in range(n):
            sl = slice(i * C, (i + 1) * C)
            qc = q_scr[bb, sl, :]
            kc = k_scr[bb, sl, :]
            vc = v_ref[bb, sl, :]
            att = _dot_nt(qc.astype(BF16), kc.astype(BF16)) * decay_scr[...]
            o = _dot(att.astype(BF16), vc)
            o = o + _dot((qc * qd_f).astype(BF16), s_f.astype(BF16))
            o = o + _dot((qc * qd_b).astype(BF16), sb_scr[bb, i].astype(BF16))
            mu = jnp.mean(o, -1, keepdims=True)
            oc = o - mu
            var = jnp.mean(oc * oc, -1, keepdims=True)
            gc = g_ref[bb, sl, :].astype(F32)
            o_ref[bb, sl, :] = (oc * lax.rsqrt(var + LN_EPS) * (gc * jax.nn.sigmoid(gc))).astype(BF16)
            if i < n - 1 or want_state:
                s_f = s_f * cd_f + kv_outer(kc, vc, kd_f)

        if want_state:
            st_ref[bb, 0] = s_f
            st_ref[bb, 1] = s_b


def _retention(z, log_gamma, *, B, L, rope_tabs=None, s0=None, layer=0, want_state):
    n = L // RET_CHUNK
    H = N_RET_HEADS
    nblk = MIX_W // RET_DK

    nbb = max(1, RET_ROWS // L)
    assert B % nbb == 0

    def sec(s):
        return pl.BlockSpec((nbb, L, RET_DK), lambda h, b: (b, 0, s * nblk + h))

    in_specs = [pl.BlockSpec(memory_space=pltpu.SMEM), sec(0), sec(1), sec(2), sec(3)]
    args = [log_gamma, z, z, z, z]
    if rope_tabs is not None:
        in_specs += [pl.BlockSpec((L, RET_DK), lambda h, b: (0, 0))] * 3
        args += list(rope_tabs)
    if s0 is not None:
        in_specs.append(pl.BlockSpec((nbb, None, 2, None, RET_DK, RET_DK), lambda h, b: (b, layer, 0, h, 0, 0)))
        args.append(s0)
    out_specs = [pl.BlockSpec((nbb, L, RET_DK), lambda h, b: (b, 0, h))]
    out_shape = [jax.ShapeDtypeStruct((B, L, MIX_W), BF16)]
    if want_state:
        out_specs.append(pl.BlockSpec((nbb, 2, None, RET_DK, RET_DK), lambda h, b: (b, 0, h, 0, 0)))
        out_shape.append(jax.ShapeDtypeStruct((B, 2, H, RET_DK, RET_DK), F32))
    body = functools.partial(_ret_body, n=n, rope=rope_tabs is not None, has_s0=s0 is not None,
                             want_state=want_state)
    return pl.pallas_call(
        body,
        grid=(H, B // nbb),
        in_specs=in_specs,
        out_specs=out_specs,
        out_shape=out_shape,
        scratch_shapes=[pltpu.VMEM((nbb, L, RET_DK), F32), pltpu.VMEM((nbb, L, RET_DK), F32),
                        pltpu.VMEM((nbb, n, RET_DK, RET_DK), F32), pltpu.VMEM((RET_CHUNK, RET_CHUNK), F32)],
        name="retention",
        compiler_params=_params("arbitrary", "arbitrary"),
    )(*args)


def _rope_tables(L):
    pos = jnp.arange(L)
    row = (pos // GRID_W).astype(F32)
    col = (pos % GRID_W).astype(F32)
    quarter = RET_DK // 4
    inv_freq = ROPE_BASE ** (-jnp.arange(quarter, dtype=F32) / quarter)
    ang_r = row[:, None] * inv_freq[None, :]
    ang_c = col[:, None] * inv_freq[None, :]
    zero = jnp.zeros_like(ang_r)
    cos = jnp.concatenate([jnp.cos(ang_r), jnp.cos(ang_r), jnp.cos(ang_c), jnp.cos(ang_c)], -1)
    s_up = jnp.concatenate([-jnp.sin(ang_r), zero, -jnp.sin(ang_c), zero], -1)
    s_dn = jnp.concatenate([zero, jnp.sin(ang_r), zero, jnp.sin(ang_c)], -1)
    return cos, s_up, s_dn


def _s5_body(*refs, B, nC, has_h0):
    refs = list(refs)
    u_ref, dk_ref, bs_ref, cp_ref, a8_ref = refs[:5]
    refs = refs[5:]
    if has_h0:
        h0_ref = refs[0]
        refs = refs[1:]
    y_ref, fin_ref, u_scr, a_scr, m_scr, bs_scr, cp_scr, s_scr, x_scr, y_scr = refs

    TC = S5_CHUNK
    R = B * nC
    P = nC + S5_PITCH_PAD
    nsl = s_scr.shape[0]
    half = nsl // 2
    ng = S5_LBLK_GROUPS
    sw = ng * SSM_STATE

    u_scr[...] = u_ref[...].astype(F32)
    for s in range(TC):
        a_scr[:, s * LANES:(s + 1) * LANES] = u_scr[pl.ds(s, R, stride=TC), :].astype(BF16)

    bs_scr[...] = jnp.zeros_like(bs_scr)
    cp_scr[...] = jnp.zeros_like(cp_scr)

    for d in range(2):
        for s in range(TC):
            for t in range(TC):
                k = (t - s) if d == 0 else (s - t)
                blk = dk_ref[d, k] if k >= 0 else jnp.zeros((LANES, LANES), BF16)
                m_scr[s * LANES:(s + 1) * LANES, t * LANES:(t + 1) * LANES] = blk
        for s in range(TC):
            for g in range(ng):
                for part in range(2):
                    rows = slice(s * LANES + g * SSM_GROUP, s * LANES + (g + 1) * SSM_GROUP)
                    cols = slice(part * sw + g * SSM_STATE, part * sw + (g + 1) * SSM_STATE)
                    lo = (g % 2) * SSM_STATE
                    bs_scr[rows, cols] = bs_ref[d, s, g, part, :, lo:lo + SSM_STATE]
                    cp_scr[rows, cols] = cp_ref[d, s, g, part, :, lo:lo + SSM_STATE]

        a = a_scr[...]
        yd = _dot(a, m_scr[...])
        if d == 0:
            y_scr[...] = yd
        else:
            y_scr[...] += yd

        sm = _dot(a, bs_scr[...])
        for b in range(B):
            for sl in range(nsl):
                s_scr[sl, b * P:b * P + nC, :] = sm[b * nC:(b + 1) * nC, sl * LANES:(sl + 1) * LANES]

        a_r = [jnp.broadcast_to(a8_ref[d, 0, :, q * LANES:(q + 1) * LANES], (B, LANES)) for q in range(half)]
        a_i = [jnp.broadcast_to(a8_ref[d, 1, :, q * LANES:(q + 1) * LANES], (B, LANES)) for q in range(half)]
        if has_h0:
            init = tuple(h0_ref[d, :, sl * LANES:(sl + 1) * LANES] for sl in range(nsl))
        else:
            init = tuple(jnp.zeros((B, LANES), F32) for _ in range(nsl))

        def step(j, carry, d=d, a_r=a_r, a_i=a_i):
            c = j if d == 0 else nC - 1 - j
            rows = pl.ds(c, B, stride=P)
            new_r, new_i = [], []
            for q in range(half):
                xr, xi = carry[q], carry[half + q]
                sr = s_scr[q, rows, :]
                si = s_scr[half + q, rows, :]
                s_scr[q, rows, :] = xr
                s_scr[half + q, rows, :] = xi
                new_r.append(a_r[q] * xr - a_i[q] * xi + sr)
                new_i.append(a_r[q] * xi + a_i[q] * xr + si)
            return tuple(new_r + new_i)

        fin = lax.fori_loop(0, nC, step, init)
        for sl in range(nsl):
            fin_ref[d, :, sl * LANES:(sl + 1) * LANES] = fin[sl]

        for b in range(B):
            for sl in range(nsl):
                x_scr[b * nC:(b + 1) * nC, sl * LANES:(sl + 1) * LANES] = \
                    s_scr[sl, b * P:b * P + nC, :].astype(BF16)
        y_scr[...] += _dot_nt(x_scr[...], cp_scr[...])

    for t in range(TC):
        y_ref[pl.ds(t, R, stride=TC), :] = y_scr[:, t * LANES:(t + 1) * LANES]


def _s5(z, ops, h0, *, B, L):
    dk, bs, cp, a8 = ops
    T = B * L
    nC = L // S5_CHUNK
    nlb = MIX_W // LANES
    sc = 2 * S5_LBLK_GROUPS * SSM_STATE
    kc = S5_CHUNK * LANES
    su0 = SU_SECTION * MIX_W // LANES

    blocks = pl.BlockSpec((2, None, S5_CHUNK, S5_LBLK_GROUPS, 2, SSM_GROUP, LANES),
                          lambda lb: (0, lb, 0, 0, 0, 0, 0))
    in_specs = [pl.BlockSpec((T, LANES), lambda lb: (0, su0 + lb)),
                pl.BlockSpec((2, None, S5_CHUNK, LANES, LANES), lambda lb: (0, lb, 0, 0, 0)),
                blocks, blocks,
                pl.BlockSpec((2, None, 2, 1, sc // 2), lambda lb: (0, lb, 0, 0, 0))]
    args = [z, dk, bs, cp, a8]
    if h0 is not None:
        in_specs.append(pl.BlockSpec((2, None, B, sc), lambda lb: (0, lb, 0, 0)))
        args.append(h0)
    body = functools.partial(_s5_body, B=B, nC=nC, has_h0=h0 is not None)
    return pl.pallas_call(
        body,
        grid=(nlb,),
        in_specs=in_specs,
        out_specs=[pl.BlockSpec((T, LANES), lambda lb: (0, lb)),
                   pl.BlockSpec((2, None, B, sc), lambda lb: (0, lb, 0, 0))],
        out_shape=[jax.ShapeDtypeStruct((T, MIX_W), F32),
                   jax.ShapeDtypeStruct((2, nlb, B, sc), F32)],
        scratch_shapes=[pltpu.VMEM((T, LANES), F32), pltpu.VMEM((B * nC, kc), BF16),
                        pltpu.VMEM((kc, kc), BF16), pltpu.VMEM((kc, sc), BF16), pltpu.VMEM((kc, sc), BF16),
                        pltpu.VMEM((sc // LANES, B * (nC + S5_PITCH_PAD), LANES), F32),
                        pltpu.VMEM((B * nC, sc), BF16), pltpu.VMEM((B * nC, kc), F32)],
        name="s5",
        compiler_params=_params("arbitrary"),
    )(*args)


def _s5_operators(a_re, a_im, log_dt, b_re, b_im, c_re, c_im):
    TC = S5_CHUNK
    nlb = MIX_W // LANES
    ng = S5_LBLK_GROUPS
    lr = jnp.minimum(a_re, -1e-4)
    li = a_im
    dt = jnp.exp(log_dt)[..., None]
    k = jnp.arange(TC + 1, dtype=F32)[:, None, None, None]
    mag = jnp.exp(k * (lr * dt)[None])
    pr = mag * jnp.cos(k * (li * dt)[None])
    pi = mag * jnp.sin(k * (li * dt)[None])
    ar, ai = pr[1], pi[1]
    den = lr * lr + li * li
    sr = ((ar - 1.0) * lr + ai * li) / den
    si = (ai * lr - (ar - 1.0) * li) / den
    bbr = sr[..., None] * b_re[None] - si[..., None] * b_im[None]
    bbi = sr[..., None] * b_im[None] + si[..., None] * b_re[None]
    car = c_re[None] * pr[:, :, :, None, :] - c_im[None] * pi[:, :, :, None, :]
    cai = c_re[None] * pi[:, :, :, None, :] + c_im[None] * pr[:, :, :, None, :]
    eye = jnp.eye(ng, dtype=F32)

    kern = jnp.sum(car[:TC, :, :, :, None, :] * jnp.swapaxes(bbr, -1, -2)[None, :, :, None]
                   - cai[:TC, :, :, :, None, :] * jnp.swapaxes(bbi, -1, -2)[None, :, :, None], axis=-1)
    dk = jnp.einsum('kdbgoc,gG->dbkgcGo', kern.reshape(TC, 2, nlb, ng, SSM_GROUP, SSM_GROUP), eye)
    dk = dk.reshape(2, nlb, TC, LANES, LANES).astype(BF16)

    def lanes2(x):
        return jnp.concatenate([x, x], -1)

    def powers(fwd, bwd):
        x = lanes2(jnp.stack([fwd, bwd], 0)).reshape(2, TC, nlb, ng, 1, LANES)
        return jnp.swapaxes(x, 1, 2)

    def per_group(x):
        return lanes2(x).reshape(2, nlb, 1, ng, SSM_GROUP, LANES)

    er = powers(jnp.flip(pr[:TC, 0], 0), pr[:TC, 1])
    ei = powers(jnp.flip(pi[:TC, 0], 0), pi[:TC, 1])
    btr = per_group(jnp.swapaxes(bbr, -1, -2))
    bti = per_group(jnp.swapaxes(bbi, -1, -2))
    bs = jnp.stack([er * btr - ei * bti, er * bti + ei * btr], 4).astype(BF16)

    fr = powers(pr[1:, 0], jnp.flip(pr[1:, 1], 0))
    fi = powers(pi[1:, 0], jnp.flip(pi[1:, 1], 0))
    ctr = per_group(c_re)
    cti = per_group(c_im)
    cp = jnp.stack([ctr * fr - cti * fi, -(ctr * fi + cti * fr)], 4).astype(BF16)

    sw = ng * SSM_STATE
    a8 = jnp.stack([pr[TC].reshape(2, nlb, 1, sw), pi[TC].reshape(2, nlb, 1, sw)], 2)
    return dk, bs, cp, a8


def _head_masks(shape):
    lane = lax.broadcasted_iota(jnp.int32, shape, 1)
    return lane < NA_HEAD_DIM


def _cattn_body(q_ref, k_ref, v_ref, o_ref):
    first = _head_masks(q_ref.shape[1:])
    for bb in range(q_ref.shape[0]):
        q = q_ref[bb]
        k = k_ref[bb]
        v = v_ref[bb]
        outs = []
        for e in range(2):
            qe = jnp.where(first if e == 0 else jnp.logical_not(first), q, jnp.zeros_like(q))
            s = _dot_nt(qe, k) * (NA_HEAD_DIM ** -0.5)
            m = jnp.max(s, -1, keepdims=True)
            p = jnp.exp(s - m)
            l = jnp.sum(p, -1, keepdims=True)
            outs.append(_dot(p.astype(BF16), v) / l)
        o_ref[bb] = jnp.where(first, outs[0], outs[1]).astype(BF16)


def _context_attention(z, *, B, L):
    nblk = MIX_W // LANES
    nb = CTX_ATTN_BATCH

    def sec(s):
        return pl.BlockSpec((nb, L, LANES), lambda b, hp: (b, 0, s * nblk + hp))

    return pl.pallas_call(
        _cattn_body,
        grid=(B // nb, nblk),
        in_specs=[sec(5), sec(6), sec(7)],
        out_specs=pl.BlockSpec((nb, L, LANES), lambda b, hp: (b, 0, hp)),
        out_shape=jax.ShapeDtypeStruct((B, L, MIX_W), BF16),
        name="ctx_attention",
        compiler_params=_params("arbitrary", "arbitrary"),
    )(z, z, z)


def _na_chunks(rows):
    half = NA_KR // 2
    plan, kinds = [], []
    for r0 in range(0, rows, NA_CHUNK_ROWS):
        rs = [min(max(r - half, 0), rows - NA_KR) for r in range(r0, r0 + NA_CHUNK_ROWS)]
        ws = min(rs[0], rows - NA_WIN_ROWS)
        assert rs[-1] + NA_KR <= ws + NA_WIN_ROWS
        kind = tuple((r0 + n - ws, rs[n] - ws) for n in range(NA_CHUNK_ROWS))
        if kind not in kinds:
            kinds.append(kind)
        plan.append((ws, kinds.index(kind)))
    return plan, kinds


def _na_body(q_ref, k_ref, v_ref, kc_ref, vc_ref, tb_ref, o_ref, bias_scr, *, rows):
    scale = NA_HEAD_DIM ** -0.5
    nq = NA_CHUNK_ROWS * GRID_W
    plan, kinds = _na_chunks(rows)
    n_off = 2 * NA_KR - 1

    @pl.when(pl.program_id(1) == 0)
    def _():
        for t, kind in enumerate(kinds):
            for e in range(2):
                for n, (r_rel, rs_rel) in enumerate(kind):
                    for kj in range(NA_WIN_ROWS):
                        off = kj - r_rel + NA_KR - 1 if rs_rel <= kj < rs_rel + NA_KR else n_off
                        lo = (kj % 2) * GRID_W
                        bias_scr[t, e * nq + n * GRID_W:e * nq + (n + 1) * GRID_W, kj * GRID_W:(kj + 1) * GRID_W] = \
                            tb_ref[e, off, :, lo:lo + GRID_W]

    kctx = kc_ref[...].astype(BF16)
    vctx = vc_ref[...].astype(BF16)
    first = _head_masks((nq, LANES))
    for c, (ws, kind) in enumerate(plan):
        qc = q_ref[c * nq:(c + 1) * nq, :]
        qs = jnp.concatenate([jnp.where(first, qc, jnp.zeros_like(qc)),
                              jnp.where(first, jnp.zeros_like(qc), qc)], 0)
        kw = k_ref[ws * GRID_W:(ws + NA_WIN_ROWS) * GRID_W, :]
        vw = v_ref[ws * GRID_W:(ws + NA_WIN_ROWS) * GRID_W, :]
        s_loc = _dot_nt(qs, kw) * scale + bias_scr[kind]
        s_ctx = _dot_nt(qs, kctx) * scale
        m = jnp.maximum(jnp.max(s_loc, -1, keepdims=True), jnp.max(s_ctx, -1, keepdims=True))
        p_loc = jnp.exp(s_loc - m)
        p_ctx = jnp.exp(s_ctx - m)
        l = jnp.sum(p_loc, -1, keepdims=True) + jnp.sum(p_ctx, -1, keepdims=True)
        o = (_dot(p_loc.astype(BF16), vw) + _dot(p_ctx.astype(BF16), vctx)) / l
        o_ref[c * nq:(c + 1) * nq, :] = jnp.where(first, o[:nq], o[nq:]).astype(BF16)


def _neighbourhood_attention(z, cache_k, cache_v, blocks, *, B, L, layer):
    nblk = MIX_W // LANES
    rows = L // GRID_W
    Lc = cache_k.shape[2]
    _, kinds = _na_chunks(rows)

    def sec(s):
        return pl.BlockSpec((None, L, LANES), lambda hp, b: (b, 0, s * nblk + hp))

    ctx = pl.BlockSpec((None, None, Lc, LANES), lambda hp, b: (b, layer, 0, hp))
    return pl.pallas_call(
        functools.partial(_na_body, rows=rows),
        grid=(nblk, B),
        in_specs=[sec(5), sec(6), sec(7), ctx, ctx,
                  pl.BlockSpec((None, 2, 2 * NA_KR, GRID_W, 2 * GRID_W), lambda hp, b: (hp, 0, 0, 0, 0))],
        out_specs=pl.BlockSpec((None, L, LANES), lambda hp, b: (b, 0, hp)),
        out_shape=jax.ShapeDtypeStruct((B, L, MIX_W), BF16),
        scratch_shapes=[pltpu.VMEM((len(kinds), 2 * NA_CHUNK_ROWS * GRID_W, NA_WIN_ROWS * GRID_W), F32)],
        name="nbr_attention",
        compiler_params=_params("arbitrary", "arbitrary"),
    )(z, z, z, cache_k, cache_v, blocks)


def _na_bias_blocks(rpb):
    nr, nc = 2 * NA_KR - 1, 2 * NA_KW - 1
    qc = np.arange(GRID_W)
    kc = np.arange(GRID_W)
    ws = np.clip(qc - NA_KW // 2, 0, GRID_W - NA_KW)
    col_ok = (kc[None, :] >= ws[:, None]) & (kc[None, :] < ws[:, None] + NA_KW)
    coff = np.clip(kc[None, :] - qc[:, None] + NA_KW - 1, 0, nc - 1)
    sel_c = ((coff[None] == np.arange(nc)[:, None, None]) & col_ok[None]).astype(np.float32)
    H = rpb.shape[0]
    t = jnp.einsum('hrc,cqk->hrqk', rpb.astype(F32), sel_c, precision=lax.Precision.HIGHEST)
    t = jnp.where(col_ok[None, None], t, NEG_INF)
    t = jnp.concatenate([t, jnp.full((H, 1, GRID_W, GRID_W), NEG_INF, F32)], 1)
    return jnp.concatenate([t, t], -1).reshape(H // 2, 2, nr + 1, GRID_W, 2 * GRID_W)


def _merge_body(x_ref, p_ref, r_ref, u_ref, y_ref, n_ref, ga_ref, gb_ref, gc_ref,
                d_ref, wglu_ref, wbr_ref, wo_ref, lg_ref, lb_ref, o_ref,
                wglu_s, wbr_s, wo_s, *, L, row0, rstride):
    i = pl.program_id(0)
    tm = x_ref.shape[0]

    @pl.when(i == 0)
    def _():
        wglu_s[...] = wglu_ref[...].astype(BF16)
        wbr_s[...] = wbr_ref[...].astype(BF16)
        wo_s[...] = wo_ref[...].astype(BF16)

    row = row0 + rstride * ((i * tm) // L)
    g1 = _mod_row(p_ref, row, 2)

    y = d_ref[...] * u_ref[...].astype(F32) + y_ref[...]
    y = jax.nn.gelu(y)
    s_out = y * jax.nn.sigmoid(_dot(y.astype(BF16), wglu_s[...]))

    def gate(ref):
        return jax.nn.sigmoid(ref[...].astype(F32))

    merged = (gate(ga_ref) * _dot(r_ref[...], wbr_s[0])
              + gate(gb_ref) * _dot(s_out.astype(BF16), wbr_s[1])
              + gate(gc_ref) * _dot(n_ref[...], wbr_s[2]))
    m = _dot(merged.astype(BF16), wo_s[...])
    o_ref[...] = _layer_norm(DEEPNORM_ALPHA * x_ref[...] + g1 * m, lg_ref[...], lb_ref[...])


def _merge(x, p, z, r_out, y, n_out, ssm_d, w_glu, w_branch, w_o, ln_g, ln_b, *, layer, L, row0, rstride):
    T = x.shape[0]
    tm = MERGE_TILE
    gate0 = 8 * MIX_W // D_MODEL

    def tok(w):
        return pl.BlockSpec((tm, w), lambda i: (i, 0))

    def full(shape):
        return pl.BlockSpec((None,) + shape, lambda i: (layer,) + (0,) * len(shape))

    body = functools.partial(_merge_body, L=L, row0=row0, rstride=rstride)
    return pl.pallas_call(
        body,
        grid=(T // tm,),
        in_specs=[tok(D_MODEL), full((N_PAD_ROWS, 6 * D_MODEL)), tok(MIX_W),
                  pl.BlockSpec((tm, MIX_W), lambda i: (i, SU_SECTION)), tok(MIX_W), tok(MIX_W),
                  pl.BlockSpec((tm, D_MODEL), lambda i: (i, gate0)),
                  pl.BlockSpec((tm, D_MODEL), lambda i: (i, gate0 + 1)),
                  pl.BlockSpec((tm, D_MODEL), lambda i: (i, gate0 + 2)),
                  full((1, MIX_W)), full((MIX_W, MIX_W)), full((3, MIX_W, D_MODEL)),
                  full((D_MODEL, D_MODEL)), full((1, D_MODEL)), full((1, D_MODEL))],
        out_specs=tok(D_MODEL),
        out_shape=jax.ShapeDtypeStruct((T, D_MODEL), F32),
        scratch_shapes=[pltpu.VMEM((MIX_W, MIX_W), BF16), pltpu.VMEM((3, MIX_W, D_MODEL), BF16),
                        pltpu.VMEM((D_MODEL, D_MODEL), BF16)],
        name="merge",
        compiler_params=_params("arbitrary"),
    )(x, p, r_out, z, y, n_out, z, z, z, ssm_d.reshape(DEPTH, 1, MIX_W), w_glu, w_branch, w_o,
      ln_g.reshape(DEPTH, 1, D_MODEL), ln_b.reshape(DEPTH, 1, D_MODEL))


def _ffn_body(x_ref, p_ref, wa_ref, wb_ref, cwa_ref, cwb_ref, cba_ref, cbb_ref, wd_ref, lg_ref, lb_ref,
              o_ref, h_scr, acc_scr, *, L, row0, rstride):
    i = pl.program_id(0)
    j = pl.program_id(1)
    tm = x_ref.shape[0]
    nb = tm // L

    @pl.when(j == 0)
    def _():
        for s in range(nb):
            row = row0 + rstride * (i * nb + s)
            sh = _mod_row(p_ref, row, 3)
            sc = _mod_row(p_ref, row, 4)
            h_scr[s * L:(s + 1) * L, :] = (x_ref[s * L:(s + 1) * L, :] * (1.0 + sc) + sh).astype(BF16)
        acc_scr[...] = jnp.zeros_like(acc_scr)

    t = lax.broadcasted_iota(jnp.int32, (tm, 1), 0) % L
    has_prev = t != 0
    has_next = t != L - 1

    def conv(w_ref, cw_ref, cb_ref):
        zc = _dot(h_scr[...], w_ref[...].astype(BF16))
        zp = jnp.where(has_prev, pltpu.roll(zc, 1, 0), 0.0)
        zn = jnp.where(has_next, pltpu.roll(zc, tm - 1, 0), 0.0)
        return zp * cw_ref[0:1, :] + zc * cw_ref[1:2, :] + zn * cw_ref[2:3, :] + cb_ref[...]

    a = conv(wa_ref, cwa_ref, cba_ref)
    b = conv(wb_ref, cwb_ref, cbb_ref)
    acc_scr[...] += _dot((jax.nn.gelu(a) * b).astype(BF16), wd_ref[...].astype(BF16))

    @pl.when(j == pl.num_programs(1) - 1)
    def _():
        for s in range(nb):
            row = row0 + rstride * (i * nb + s)
            g2 = _mod_row(p_ref, row, 5)
            sl = slice(s * L, (s + 1) * L)
            o_ref[sl, :] = _layer_norm(DEEPNORM_ALPHA * x_ref[sl, :] + g2 * acc_scr[sl, :],
                                       lg_ref[...], lb_ref[...])


def _conv_ffn(x, p, w_up, conv_w, conv_b, w_down, ln_g, ln_b, *, layer, L, row0, rstride):
    T = x.shape[0]
    tm = TOKEN_TILE
    nff = D_FF // FF_TILE
    body = functools.partial(_ffn_body, L=L, row0=row0, rstride=rstride)
    conv_b = conv_b.reshape(DEPTH, 1, 2 * D_FF)
    return pl.pallas_call(
        body,
        grid=(T // tm, nff),
        in_specs=[pl.BlockSpec((tm, D_MODEL), lambda i, j: (i, 0)),
                  pl.BlockSpec((None, N_PAD_ROWS, 6 * D_MODEL), lambda i, j: (layer, 0, 0)),
                  pl.BlockSpec((None, D_MODEL, FF_TILE), lambda i, j: (layer, 0, j)),
                  pl.BlockSpec((None, D_MODEL, FF_TILE), lambda i, j: (layer, 0, nff + j)),
                  pl.BlockSpec((None, 3, FF_TILE), lambda i, j: (layer, 0, j)),
                  pl.BlockSpec((None, 3, FF_TILE), lambda i, j: (layer, 0, nff + j)),
                  pl.BlockSpec((None, 1, FF_TILE), lambda i, j: (layer, 0, j)),
                  pl.BlockSpec((None, 1, FF_TILE), lambda i, j: (layer, 0, nff + j)),
                  pl.BlockSpec((None, FF_TILE, D_MODEL), lambda i, j: (layer, j, 0)),
                  pl.BlockSpec((None, 1, D_MODEL), lambda i, j: (layer, 0, 0)),
                  pl.BlockSpec((None, 1, D_MODEL), lambda i, j: (layer, 0, 0))],
        out_specs=pl.BlockSpec((tm, D_MODEL), lambda i, j: (i, 0)),
        out_shape=jax.ShapeDtypeStruct((T, D_MODEL), F32),
        scratch_shapes=[pltpu.VMEM((tm, D_MODEL), BF16), pltpu.VMEM((tm, D_MODEL), F32)],
        name="conv_ffn",
        compiler_params=_params("arbitrary", "arbitrary"),
    )(x, p, w_up, w_up, conv_w, conv_w, conv_b, conv_b, w_down, ln_g.reshape(DEPTH, 1, D_MODEL),
      ln_b.reshape(DEPTH, 1, D_MODEL))


def _s5_states_in(state_ssm, layer):
    B = state_ssm.shape[0]
    nlb = MIX_W // LANES
    h = state_ssm[:, layer].reshape(B, 2, nlb, S5_LBLK_GROUPS, SSM_STATE, 2)
    return jnp.transpose(h, (1, 2, 0, 5, 3, 4)).reshape(2, nlb, B, 2 * S5_LBLK_GROUPS * SSM_STATE)


def _s5_states_out(fin):
    nlb, B = fin.shape[1], fin.shape[2]
    h = fin.reshape(2, nlb, B, 2, S5_LBLK_GROUPS, SSM_STATE)
    return jnp.transpose(h, (2, 0, 1, 4, 5, 3)).reshape(B, 2, SSM_GROUPS, SSM_STATE, 2)


def _layer(x, p, lw, *, B, L, row0, rstride, latent, layer, extra):
    T = B * L
    z, *kv = _inproj(x, p, lw['w_in'], layer=layer, L=L, row0=row0, rstride=rstride, want_kv=not latent)
    z3 = z.reshape(B, L, IN_COLS)
    log_gamma = jax.nn.log_sigmoid(lw['ret_decay'].astype(F32))
    if latent:
        r_out = _retention(z3, log_gamma, B=B, L=L, rope_tabs=extra['rope'], s0=extra['state_ret'],
                           layer=layer, want_state=False)[0]
        n_out = _neighbourhood_attention(z3, extra['cache_k'], extra['cache_v'], extra['bias'][layer],
                                         B=B, L=L, layer=layer)
        y, _ = _s5(z, lw['s5'], _s5_states_in(extra['state_ssm'], layer), B=B, L=L)
        states = None
    else:
        r_out, ret_state = _retention(z3, log_gamma, B=B, L=L, want_state=True)
        n_out = _context_attention(z3, B=B, L=L)
        y, fin = _s5(z, lw['s5'], None, B=B, L=L)
        nk = kv[0].reshape(B, L, NA_HEADS, NA_HEAD_DIM)
        nv = kv[1].reshape(B, L, NA_HEADS, NA_HEAD_DIM)
        states = (ret_state, _s5_states_out(fin), nk, nv)
    x = _merge(x, p, z, r_out.reshape(T, MIX_W), y, n_out.reshape(T, MIX_W),
               lw['ssm_d'], lw['ssm_w_glu'], lw['w_branch'], lw['w_o'], lw['ln1_g'], lw['ln1_b'],
               layer=layer, L=L, row0=row0, rstride=rstride)
    x = _conv_ffn(x, p, lw['w_up'], lw['conv_w'], lw['conv_b'], lw['w_down'], lw['ln2_g'], lw['ln2_b'],
                  layer=layer, L=L, row0=row0, rstride=rstride)
    return x, states


def kernel(x_prompt, x_sample, state_ret, state_ssm, cache_na_k, cache_na_v, c, c_ctx, w_ada, b_ada, w_in, ret_decay, ssm_a_re, ssm_a_im, ssm_log_dt, ssm_b_re, ssm_b_im, ssm_c_re, ssm_c_im, ssm_d, ssm_w_glu, na_rpb, w_branch, w_o, ln1_g, ln1_b, w_up, conv_w, conv_b, w_down, ln2_g, ln2_b):
    B, L, _ = x_prompt.shape
    Bd, Ld, _ = x_sample.shape
    Lc = cache_na_k.shape[2]

    cond = jnp.concatenate([c_ctx[None, :], c, jnp.zeros((N_PAD_ROWS - 1 - Bd, D_MODEL), F32)], 0)
    p_all = _ada(cond, w_ada, b_ada)

    extra = dict(rope=_rope_tables(Ld), state_ret=state_ret, state_ssm=state_ssm,
                 cache_k=cache_na_k.reshape(Bd, DEPTH, Lc, MIX_W),
                 cache_v=cache_na_v.reshape(Bd, DEPTH, Lc, MIX_W),
                 bias=[_na_bias_blocks(na_rpb[l]) for l in range(DEPTH)])

    xp = x_prompt.reshape(B * L, D_MODEL)
    xs = x_sample.reshape(Bd * Ld, D_MODEL)
    ret_states, ssm_states, na_ks, na_vs = [], [], [], []
    for l in range(DEPTH):
        lw = dict(w_in=w_in, ret_decay=ret_decay[l], ssm_d=ssm_d, ssm_w_glu=ssm_w_glu,
                  w_branch=w_branch, w_o=w_o, ln1_g=ln1_g, ln1_b=ln1_b, w_up=w_up,
                  conv_w=conv_w, conv_b=conv_b, w_down=w_down, ln2_g=ln2_g, ln2_b=ln2_b,
                  s5=_s5_operators(ssm_a_re[l], ssm_a_im[l], ssm_log_dt[l], ssm_b_re[l], ssm_b_im[l],
                                   ssm_c_re[l], ssm_c_im[l]))
        xp, (s_ret, s_ssm, nk, nv) = _layer(xp, p_all, lw, B=B, L=L, row0=0, rstride=0,
                                            latent=False, layer=l, extra=None)
        ret_states.append(s_ret)
        ssm_states.append(s_ssm)
        na_ks.append(nk)
        na_vs.append(nv)
        xs, _ = _layer(xs, p_all, lw, B=Bd, L=Ld, row0=1, rstride=1, latent=True, layer=l, extra=extra)
    return (xp.reshape(B, L, D_MODEL), xs.reshape(Bd, Ld, D_MODEL),
            jnp.stack(ret_states, 1), jnp.stack(ssm_states, 1), jnp.stack(na_ks, 1), jnp.stack(na_vs, 1))
```

```python
import functools

import jax
import jax.numpy as jnp
import numpy as np
from jax import lax
from jax.experimental import pallas as pl
from jax.experimental.pallas import tpu as pltpu

F32 = jnp.float32
BF16 = jnp.bfloat16

D_MODEL = 1024
DEPTH = 2
GRID_W = 64
MIX_W = D_MODEL // 2
N_RET_HEADS = 4
RET_DK = MIX_W // N_RET_HEADS
SSM_GROUP = 16
SSM_GROUPS = MIX_W // SSM_GROUP
SSM_STATE = 64
NA_HEADS = 8
NA_HEAD_DIM = MIX_W // NA_HEADS
NA_KR = 8
NA_KW = 16
D_FF = ((8 * D_MODEL // 3 + 127) // 128) * 128
ROPE_BASE = 10000.0
LN_EPS = 1e-5
NEG_INF = -1e30
DEEPNORM_ALPHA = (2 * DEPTH) ** 0.25
IN_COLS = 8 * MIX_W + 3 * D_MODEL

VMEM_LIMIT_BYTES = 56 * 1024 * 1024
LANES = 128

TOKEN_TILE = 1024
MERGE_TILE = 256
COL_TILE = 1024
SU_SECTION = 4
NK_SECTION = 6
FF_TILE = 256
RET_CHUNK = 256
RET_ROWS = 1024
S5_CHUNK = 8
S5_PITCH_PAD = 8
S5_LBLK_GROUPS = LANES // SSM_GROUP
N_PAD_ROWS = 8
CTX_ATTN_BATCH = 4
NA_CHUNK_ROWS = 4
NA_WIN_ROWS = 12


def _params(*sem):
    return pltpu.CompilerParams(dimension_semantics=sem, vmem_limit_bytes=VMEM_LIMIT_BYTES)


def _dot(a, b):
    return jnp.dot(a, b, preferred_element_type=F32)


def _dot_nt(a, b):
    return lax.dot_general(a, b, (((1,), (1,)), ((), ())), preferred_element_type=F32)


def _layer_norm(x, g, b):
    mu = jnp.mean(x, -1, keepdims=True)
    xc = x - mu
    var = jnp.mean(xc * xc, -1, keepdims=True)
    return xc * lax.rsqrt(var + LN_EPS) * g + b


def _ada_body(c_ref, w_ref, b_ref, o_ref):
    c = c_ref[...]
    s = c * jax.nn.sigmoid(c)
    o_ref[...] = _dot(s.astype(BF16), w_ref[...].astype(BF16)) + b_ref[...]


def _ada(cond, w_ada, b_ada):
    tn = 1024
    return pl.pallas_call(
        _ada_body,
        grid=(DEPTH, 6 * D_MODEL // tn),
        in_specs=[pl.BlockSpec((N_PAD_ROWS, D_MODEL), lambda l, j: (0, 0)),
                  pl.BlockSpec((None, D_MODEL, tn), lambda l, j: (l, 0, j)),
                  pl.BlockSpec((None, 1, tn), lambda l, j: (l, 0, j))],
        out_specs=pl.BlockSpec((None, N_PAD_ROWS, tn), lambda l, j: (l, 0, j)),
        out_shape=jax.ShapeDtypeStruct((DEPTH, N_PAD_ROWS, 6 * D_MODEL), F32),
        name="ada",
        compiler_params=_params("arbitrary", "arbitrary"),
    )(cond, w_ada, b_ada.reshape(DEPTH, 1, 6 * D_MODEL))


def _mod_row(p_ref, row, k):
    return p_ref[pl.ds(row, 1), k * D_MODEL:(k + 1) * D_MODEL]


def _kv_tile(n):
    col = (NK_SECTION + n) * MIX_W
    return col // COL_TILE, col % COL_TILE


def _inproj_body(x_ref, p_ref, w_ref, z_ref, *rest, L, row0, rstride):
    h_scr, w_scr = rest[-2:]
    kv_refs = rest[:-2]
    j = pl.program_id(0)
    i = pl.program_id(1)
    nb = x_ref.shape[0] // L

    @pl.when(j == 0)
    def _():
        for s in range(nb):
            row = row0 + rstride * (i * nb + s)
            sh = _mod_row(p_ref, row, 0)
            sc = _mod_row(p_ref, row, 1)
            h_scr[i, s * L:(s + 1) * L, :] = (x_ref[s * L:(s + 1) * L, :] * (1.0 + sc) + sh).astype(BF16)

    @pl.when(i == 0)
    def _():
        w_scr[...] = w_ref[...].astype(BF16)

    acc = _dot(h_scr[i], w_scr[...])
    z_ref[...] = acc.astype(BF16)

    for n, ref in enumerate(kv_refs):
        tile, off = _kv_tile(n)

        @pl.when(j == tile)
        def _(ref=ref, off=off):
            ref[...] = acc[:, off:off + MIX_W]


def _inproj(x, p, w_in, *, layer, L, row0, rstride, want_kv):
    T = x.shape[0]
    tm = TOKEN_TILE
    n_i = T // tm
    body = functools.partial(_inproj_body, L=L, row0=row0, rstride=rstride)
    n_kv = 2 if want_kv else 0

    def only_at(tile):
        return lambda j, i: jnp.where(j < tile, 0, jnp.where(j > tile, n_i - 1, i))

    kv_i = [only_at(_kv_tile(n)[0]) for n in range(n_kv)]
    return pl.pallas_call(
        body,
        grid=(IN_COLS // COL_TILE, n_i),
        in_specs=[pl.BlockSpec((tm, D_MODEL), lambda j, i: (jnp.where(j == 0, i, n_i - 1), 0)),
                  pl.BlockSpec((None, N_PAD_ROWS, 6 * D_MODEL), lambda j, i: (layer, 0, 0)),
                  pl.BlockSpec((None, D_MODEL, COL_TILE), lambda j, i: (layer, 0, j))],
        out_specs=[pl.BlockSpec((tm, COL_TILE), lambda j, i: (i, j))]
        + [pl.BlockSpec((tm, MIX_W), lambda j, i, f=f: (f(j, i), 0)) for f in kv_i],
        out_shape=[jax.ShapeDtypeStruct((T, IN_COLS), BF16)] + [jax.ShapeDtypeStruct((T, MIX_W), F32)] * n_kv,
        scratch_shapes=[pltpu.VMEM((n_i, tm, D_MODEL), BF16), pltpu.VMEM((D_MODEL, COL_TILE), BF16)],
        name="inproj",
        compiler_params=_params("arbitrary", "arbitrary"),
    )(x, p, w_in)


def _rope(x, cos, s_up, s_dn):
    return x * cos + pltpu.roll(x, 96, 1) * s_up + pltpu.roll(x, 32, 1) * s_dn


def _ret_body(*refs, n, rope, has_s0, want_state):
    refs = list(refs)
    lg_ref, q_ref, k_ref, v_ref, g_ref = refs[:5]
    refs = refs[5:]
    if rope:
        cos_ref, sup_ref, sdn_ref = refs[:3]
        refs = refs[3:]
    if has_s0:
        s0_ref = refs[0]
        refs = refs[1:]
    o_ref = refs[0]
    refs = refs[1:]
    if want_state:
        st_ref = refs[0]
        refs = refs[1:]
    q_scr, k_scr, sb_scr, decay_scr = refs

    C = RET_CHUNK
    h = pl.program_id(0)
    lf = lg_ref[0, h]
    lb = lg_ref[1, h]

    @pl.when(pl.program_id(1) == 0)
    def _():
        ti = lax.broadcasted_iota(jnp.int32, (C, C), 0)
        si = lax.broadcasted_iota(jnp.int32, (C, C), 1)
        dlt = (ti - si).astype(F32)
        decay_scr[...] = (jnp.where(dlt >= 0, jnp.exp(lf * jnp.maximum(dlt, 0.0)), 0.0)
                          + jnp.where(dlt <= 0, jnp.exp(lb * jnp.maximum(-dlt, 0.0)), 0.0))

    tcol = lax.broadcasted_iota(jnp.int32, (C, 1), 0).astype(F32)
    qd_f = jnp.exp(lf * (tcol + 1.0))
    qd_b = jnp.exp(lb * (C - tcol))
    kd_f = jnp.exp(lf * (C - 1.0 - tcol))
    kd_b = jnp.exp(lb * tcol)
    cd_f = jnp.exp(lf * jnp.full((1, RET_DK), float(C), F32))
    cd_b = jnp.exp(lb * jnp.full((1, RET_DK), float(C), F32))

    def kv_outer(kc, vc, kd):
        return _dot((kc * kd).T.astype(BF16), vc)

    for bb in range(q_ref.shape[0]):
        q = q_ref[bb].astype(F32)
        k = k_ref[bb].astype(F32)
        if rope:
            q = _rope(q, cos_ref[...], sup_ref[...], sdn_ref[...])
            k = _rope(k, cos_ref[...], sup_ref[...], sdn_ref[...])
        q_scr[bb] = q
        k_scr[bb] = k * (RET_DK ** -0.5)

        s_b = s0_ref[bb, 1] if has_s0 else jnp.zeros((RET_DK, RET_DK), F32)
        for i in reversed(range(n)):
            sb_scr[bb, i] = s_b
            if i > 0 or want_state:
                s_b = s_b * cd_b + kv_outer(k_scr[bb, i * C:(i + 1) * C, :], v_ref[bb, i * C:(i + 1) * C, :], kd_b)

        s_f = s0_ref[bb, 0] if has_s0 else jnp.zeros((RET_DK, RET_DK), F32)
        for i in range(n):
            sl = slice(i * C, (i + 1) * C)
            qc = q_scr[bb, sl, :]
            kc = k_scr[bb, sl, :]
            vc = v_ref[bb, sl, :]
            att = _dot_nt(qc.astype(BF16), kc.astype(BF16)) * decay_scr[...]
            o = _dot(att.astype(BF16), vc)
            o = o + _dot((qc * qd_f).astype(BF16), s_f.astype(BF16))
            o = o + _dot((qc * qd_b).astype(BF16), sb_scr[bb, i].astype(BF16))
            mu = jnp.mean(o, -1, keepdims=True)
            oc = o - mu
            var = jnp.mean(oc * oc, -1, keepdims=True)
            gc = g_ref[bb, sl, :].astype(F32)
            o_ref[bb, sl, :] = (oc * lax.rsqrt(var + LN_EPS) * (gc * jax.nn.sigmoid(gc))).astype(BF16)
            if i < n - 1 or want_state:
                s_f = s_f * cd_f + kv_outer(kc, vc, kd_f)

        if want_state:
            st_ref[bb, 0] = s_f
            st_ref[bb, 1] = s_b


def _retention(z, log_gamma, *, B, L, rope_tabs=None, s0=None, layer=0, want_state):
    n = L // RET_CHUNK
    H = N_RET_HEADS
    nblk = MIX_W // RET_DK

    nbb = max(1, RET_ROWS // L)
    assert B % nbb == 0

    def sec(s):
        return pl.BlockSpec((nbb, L, RET_DK), lambda h, b: (b, 0, s * nblk + h))

    in_specs = [pl.BlockSpec(memory_space=pltpu.SMEM), sec(0), sec(1), sec(2), sec(3)]
    args = [log_gamma, z, z, z, z]
    if rope_tabs is not None:
        in_specs += [pl.BlockSpec((L, RET_DK), lambda h, b: (0, 0))] * 3
        args += list(rope_tabs)
    if s0 is not None:
        in_specs.append(pl.BlockSpec((nbb, None, 2, None, RET_DK, RET_DK), lambda h, b: (b, layer, 0, h, 0, 0)))
        args.append(s0)
    out_specs = [pl.BlockSpec((nbb, L, RET_DK), lambda h, b: (b, 0, h))]
    out_shape = [jax.ShapeDtypeStruct((B, L, MIX_W), BF16)]
    if want_state:
        out_specs.append(pl.BlockSpec((nbb, 2, None, RET_DK, RET_DK), lambda h, b: (b, 0, h, 0, 0)))
        out_shape.append(jax.ShapeDtypeStruct((B, 2, H, RET_DK, RET_DK), F32))
    body = functools.partial(_ret_body, n=n, rope=rope_tabs is not None, has_s0=s0 is not None,
                             want_state=want_state)
    return pl.pallas_call(
        body,
        grid=(H, B // nbb),
        in_specs=in_specs,
        out_specs=out_specs,
        out_shape=out_shape,
        scratch_shapes=[pltpu.VMEM((nbb, L, RET_DK), F32), pltpu.VMEM((nbb, L, RET_DK), F32),
                        pltpu.VMEM((nbb, n, RET_DK, RET_DK), F32), pltpu.VMEM((RET_CHUNK, RET_CHUNK), F32)],
        name="retention",
        compiler_params=_params("arbitrary", "arbitrary"),
    )(*args)


def _rope_tables(L):
    pos = jnp.arange(L)
    row = (pos // GRID_W).astype(F32)
    col = (pos % GRID_W).astype(F32)
    quarter = RET_DK // 4
    inv_freq = ROPE_BASE ** (-jnp.arange(quarter, dtype=F32) / quarter)
    ang_r = row[:, None] * inv_freq[None, :]
    ang_c = col[:, None] * inv_freq[None, :]
    zero = jnp.zeros_like(ang_r)
    cos = jnp.concatenate([jnp.cos(ang_r), jnp.cos(ang_r), jnp.cos(ang_c), jnp.cos(ang_c)], -1)
    s_up = jnp.concatenate([-jnp.sin(ang_r), zero, -jnp.sin(ang_c), zero], -1)
    s_dn = jnp.concatenate([zero, jnp.sin(ang_r), zero, jnp.sin(ang_c)], -1)
    return cos, s_up, s_dn


def _s5_body(*refs, B, nC, has_h0):
    refs = list(refs)
    u_ref, c0_ref, bs_ref, cp_ref, a8_ref = refs[:5]
    refs = refs[5:]
    if has_h0:
        h0_ref = refs[0]
        refs = refs[1:]
    y_ref, fin_ref, u_scr, a_scr, m_scr, bs_scr, cp_scr, c0_scr, s_scr, x_scr, y_scr = refs

    TC = S5_CHUNK
    R = B * nC
    P = nC + S5_PITCH_PAD
    nsl = s_scr.shape[0]
    half = nsl // 2
    ng = S5_LBLK_GROUPS
    sw = ng * SSM_STATE

    u_scr[...] = u_ref[...].astype(F32)
    for s in range(TC):
        a_scr[:, s * LANES:(s + 1) * LANES] = u_scr[pl.ds(s, R, stride=TC), :].astype(BF16)

    bs_scr[...] = jnp.zeros_like(bs_scr)
    cp_scr[...] = jnp.zeros_like(cp_scr)
    c0_scr[...] = jnp.zeros_like(c0_scr)

    for d in range(2):
        for g in range(ng):
            for part in range(2):
                cols = slice(part * sw + g * SSM_STATE, part * sw + (g + 1) * SSM_STATE)
                lo = (g % 2) * SSM_STATE
                c0_scr[g * SSM_GROUP:(g + 1) * SSM_GROUP, cols] = c0_ref[d, g, part, :, lo:lo + SSM_STATE]
                for s in range(TC):
                    rows = slice(s * LANES + g * SSM_GROUP, s * LANES + (g + 1) * SSM_GROUP)
                    bs_scr[rows, cols] = bs_ref[d, s, g, part, :, lo:lo + SSM_STATE]
                    cp_scr[rows, cols] = cp_ref[d, s, g, part, :, lo:lo + SSM_STATE]

        lag = _dot_nt(bs_scr[...], c0_scr[...]).astype(BF16)
        for s in range(TC):
            for t in range(TC):
                k = (t - s) if d == 0 else (s - t)
                src = (TC - 1 - k) if d == 0 else k
                blk = lag[src * LANES:(src + 1) * LANES, :] if k >= 0 else jnp.zeros((LANES, LANES), BF16)
                m_scr[s * LANES:(s + 1) * LANES, t * LANES:(t + 1) * LANES] = blk

        a = a_scr[...]
        yd = _dot(a, m_scr[...])
        if d == 0:
            y_scr[...] = yd
        else:
            y_scr[...] += yd

        sm = _dot(a, bs_scr[...])
        for b in range(B):
            for sl in range(nsl):
                s_scr[sl, b * P:b * P + nC, :] = sm[b * nC:(b + 1) * nC, sl * LANES:(sl + 1) * LANES]

        a_r = [jnp.broadcast_to(a8_ref[d, 0, :, q * LANES:(q + 1) * LANES], (B, LANES)) for q in range(half)]
        a_i = [jnp.broadcast_to(a8_ref[d, 1, :, q * LANES:(q + 1) * LANES], (B, LANES)) for q in range(half)]
        if has_h0:
            init = tuple(h0_ref[d, :, sl * LANES:(sl + 1) * LANES] for sl in range(nsl))
        else:
            init = tuple(jnp.zeros((B, LANES), F32) for _ in range(nsl))

        def step(j, carry, d=d, a_r=a_r, a_i=a_i):
            c = j if d == 0 else nC - 1 - j
            rows = pl.ds(c, B, stride=P)
            new_r, new_i = [], []
            for q in range(half):
                xr, xi = carry[q], carry[half + q]
                sr = s_scr[q, rows, :]
                si = s_scr[half + q, rows, :]
                s_scr[q, rows, :] = xr
                s_scr[half + q, rows, :] = xi
                new_r.append(a_r[q] * xr - a_i[q] * xi + sr)
                new_i.append(a_r[q] * xi + a_i[q] * xr + si)
            return tuple(new_r + new_i)

        fin = lax.fori_loop(0, nC, step, init)
        for sl in range(nsl):
            fin_ref[d, :, sl * LANES:(sl + 1) * LANES] = fin[sl]

        for b in range(B):
            for sl in range(nsl):
                x_scr[b * nC:(b + 1) * nC, sl * LANES:(sl + 1) * LANES] = \
                    s_scr[sl, b * P:b * P + nC, :].astype(BF16)
        y_scr[...] += _dot_nt(x_scr[...], cp_scr[...])

    for t in range(TC):
        y_ref[pl.ds(t, R, stride=TC), :] = y_scr[:, t * LANES:(t + 1) * LANES]


def _s5(z, ops, h0, *, B, L):
    c0, bs, cp, a8 = ops
    T = B * L
    nC = L // S5_CHUNK
    nlb = MIX_W // LANES
    sc = 2 * S5_LBLK_GROUPS * SSM_STATE
    kc = S5_CHUNK * LANES
    su0 = SU_SECTION * MIX_W // LANES

    blocks = pl.BlockSpec((2, None, S5_CHUNK, S5_LBLK_GROUPS, 2, SSM_GROUP, LANES),
                          lambda lb: (0, lb, 0, 0, 0, 0, 0))
    in_specs = [pl.BlockSpec((T, LANES), lambda lb: (0, su0 + lb)),
                pl.BlockSpec((2, None, S5_LBLK_GROUPS, 2, SSM_GROUP, LANES), lambda lb: (0, lb, 0, 0, 0, 0)),
                blocks, blocks,
                pl.BlockSpec((2, None, 2, 1, sc // 2), lambda lb: (0, lb, 0, 0, 0))]
    args = [z, c0, bs, cp, a8]
    if h0 is not None:
        in_specs.append(pl.BlockSpec((2, None, B, sc), lambda lb: (0, lb, 0, 0)))
        args.append(h0)
    body = functools.partial(_s5_body, B=B, nC=nC, has_h0=h0 is not None)
    return pl.pallas_call(
        body,
        grid=(nlb,),
        in_specs=in_specs,
        out_specs=[pl.BlockSpec((T, LANES), lambda lb: (0, lb)),
                   pl.BlockSpec((2, None, B, sc), lambda lb: (0, lb, 0, 0))],
        out_shape=[jax.ShapeDtypeStruct((T, MIX_W), F32),
                   jax.ShapeDtypeStruct((2, nlb, B, sc), F32)],
        scratch_shapes=[pltpu.VMEM((T, LANES), F32), pltpu.VMEM((B * nC, kc), BF16),
                        pltpu.VMEM((kc, kc), BF16), pltpu.VMEM((kc, sc), BF16), pltpu.VMEM((kc, sc), BF16),
                        pltpu.VMEM((LANES, sc), BF16),
                        pltpu.VMEM((sc // LANES, B * (nC + S5_PITCH_PAD), LANES), F32),
                        pltpu.VMEM((B * nC, sc), BF16), pltpu.VMEM((B * nC, kc), F32)],
        name="s5",
        compiler_params=_params("arbitrary"),
    )(*args)


def _s5_operators(a_re, a_im, log_dt, b_re, b_im, c_re, c_im):
    TC = S5_CHUNK
    nlb = MIX_W // LANES
    ng = S5_LBLK_GROUPS
    lr = jnp.minimum(a_re, -1e-4)
    li = a_im
    dt = jnp.exp(log_dt)[..., None]
    k = jnp.arange(TC + 1, dtype=F32)[:, None, None, None]
    mag = jnp.exp(k * (lr * dt)[None])
    pr = mag * jnp.cos(k * (li * dt)[None])
    pi = mag * jnp.sin(k * (li * dt)[None])
    ar, ai = pr[1], pi[1]
    den = lr * lr + li * li
    sr = ((ar - 1.0) * lr + ai * li) / den
    si = (ai * lr - (ar - 1.0) * li) / den
    bbr = sr[..., None] * b_re[None] - si[..., None] * b_im[None]
    bbi = sr[..., None] * b_im[None] + si[..., None] * b_re[None]

    def lanes2(x):
        return jnp.concatenate([x, x], -1)

    def powers(fwd, bwd):
        x = lanes2(jnp.stack([fwd, bwd], 0)).reshape(2, TC, nlb, ng, 1, LANES)
        return jnp.swapaxes(x, 1, 2)

    def per_group(x):
        return lanes2(x).reshape(2, nlb, 1, ng, SSM_GROUP, LANES)

    er = powers(jnp.flip(pr[:TC, 0], 0), pr[:TC, 1])
    ei = powers(jnp.flip(pi[:TC, 0], 0), pi[:TC, 1])
    btr = per_group(jnp.swapaxes(bbr, -1, -2))
    bti = per_group(jnp.swapaxes(bbi, -1, -2))
    bs = jnp.stack([er * btr - ei * bti, er * bti + ei * btr], 4).astype(BF16)

    fr = powers(pr[1:, 0], jnp.flip(pr[1:, 1], 0))
    fi = powers(pi[1:, 0], jnp.flip(pi[1:, 1], 0))
    ctr = per_group(c_re)
    cti = per_group(c_im)
    cp = jnp.stack([ctr * fr - cti * fi, -(ctr * fi + cti * fr)], 4).astype(BF16)
    c0 = jnp.stack([ctr, -cti], 4)[:, :, 0].astype(BF16)

    sw = ng * SSM_STATE
    a8 = jnp.stack([pr[TC].reshape(2, nlb, 1, sw), pi[TC].reshape(2, nlb, 1, sw)], 2)
    return c0, bs, cp, a8


def _head_masks(shape):
    lane = lax.broadcasted_iota(jnp.int32, shape, 1)
    return lane < NA_HEAD_DIM


def _cattn_body(q_ref, k_ref, v_ref, o_ref):
    first = _head_masks(q_ref.shape[1:])
    for bb in range(q_ref.shape[0]):
        q = q_ref[bb]
        k = k_ref[bb]
        v = v_ref[bb]
        outs = []
        for e in range(2):
            qe = jnp.where(first if e == 0 else jnp.logical_not(first), q, jnp.zeros_like(q))
            s = _dot_nt(qe, k) * (NA_HEAD_DIM ** -0.5)
            m = jnp.max(s, -1, keepdims=True)
            p = jnp.exp(s - m)
            l = jnp.sum(p, -1, keepdims=True)
            outs.append(_dot(p.astype(BF16), v) / l)
        o_ref[bb] = jnp.where(first, outs[0], outs[1]).astype(BF16)


def _context_attention(z, *, B, L):
    nblk = MIX_W // LANES
    nb = CTX_ATTN_BATCH

    def sec(s):
        return pl.BlockSpec((nb, L, LANES), lambda b, hp: (b, 0, s * nblk + hp))

    return pl.pallas_call(
        _cattn_body,
        grid=(B // nb, nblk),
        in_specs=[sec(5), sec(6), sec(7)],
        out_specs=pl.BlockSpec((nb, L, LANES), lambda b, hp: (b, 0, hp)),
        out_shape=jax.ShapeDtypeStruct((B, L, MIX_W), BF16),
        name="ctx_attention",
        compiler_params=_params("arbitrary", "arbitrary"),
    )(z, z, z)


def _na_chunks(rows):
    half = NA_KR // 2
    plan, kinds = [], []
    for r0 in range(0, rows, NA_CHUNK_ROWS):
        rs = [min(max(r - half, 0), rows - NA_KR) for r in range(r0, r0 + NA_CHUNK_ROWS)]
        ws = min(rs[0], rows - NA_WIN_ROWS)
        assert rs[-1] + NA_KR <= ws + NA_WIN_ROWS
        kind = tuple((r0 + n - ws, rs[n] - ws) for n in range(NA_CHUNK_ROWS))
        if kind not in kinds:
            kinds.append(kind)
        plan.append((ws, kinds.index(kind)))
    return plan, kinds


def _na_body(q_ref, k_ref, v_ref, kc_ref, vc_ref, tb_ref, o_ref, bias_scr, *, rows):
    scale = NA_HEAD_DIM ** -0.5
    nq = NA_CHUNK_ROWS * GRID_W
    plan, kinds = _na_chunks(rows)
    n_off = 2 * NA_KR - 1

    @pl.when(pl.program_id(1) == 0)
    def _():
        for t, kind in enumerate(kinds):
            for e in range(2):
                for n, (r_rel, rs_rel) in enumerate(kind):
                    for kj in range(NA_WIN_ROWS):
                        off = kj - r_rel + NA_KR - 1 if rs_rel <= kj < rs_rel + NA_KR else n_off
                        lo = (kj % 2) * GRID_W
                        bias_scr[t, e * nq + n * GRID_W:e * nq + (n + 1) * GRID_W, kj * GRID_W:(kj + 1) * GRID_W] = \
                            tb_ref[e, off, :, lo:lo + GRID_W]

    kctx = kc_ref[...].astype(BF16)
    vctx = vc_ref[...].astype(BF16)
    first = _head_masks((nq, LANES))
    for c, (ws, kind) in enumerate(plan):
        qc = q_ref[c * nq:(c + 1) * nq, :]
        qs = jnp.concatenate([jnp.where(first, qc, jnp.zeros_like(qc)),
                              jnp.where(first, jnp.zeros_like(qc), qc)], 0)
        kw = k_ref[ws * GRID_W:(ws + NA_WIN_ROWS) * GRID_W, :]
        vw = v_ref[ws * GRID_W:(ws + NA_WIN_ROWS) * GRID_W, :]
        s_loc = _dot_nt(qs, kw) * scale + bias_scr[kind]
        s_ctx = _dot_nt(qs, kctx) * scale
        m = jnp.maximum(jnp.max(s_loc, -1, keepdims=True), jnp.max(s_ctx, -1, keepdims=True))
        p_loc = jnp.exp(s_loc - m)
        p_ctx = jnp.exp(s_ctx - m)
        l = jnp.sum(p_loc, -1, keepdims=True) + jnp.sum(p_ctx, -1, keepdims=True)
        o = (_dot(p_loc.astype(BF16), vw) + _dot(p_ctx.astype(BF16), vctx)) / l
        o_ref[c * nq:(c + 1) * nq, :] = jnp.where(first, o[:nq], o[nq:]).astype(BF16)


def _neighbourhood_attention(z, cache_k, cache_v, blocks, *, B, L, layer):
    nblk = MIX_W // LANES
    rows = L // GRID_W
    Lc = cache_k.shape[2]
    _, kinds = _na_chunks(rows)

    def sec(s):
        return pl.BlockSpec((None, L, LANES), lambda hp, b: (b, 0, s * nblk + hp))

    ctx = pl.BlockSpec((None, None, Lc, LANES), lambda hp, b: (b, layer, 0, hp))
    return pl.pallas_call(
        functools.partial(_na_body, rows=rows),
        grid=(nblk, B),
        in_specs=[sec(5), sec(6), sec(7), ctx, ctx,
                  pl.BlockSpec((None, 2, 2 * NA_KR, GRID_W, 2 * GRID_W), lambda hp, b: (hp, 0, 0, 0, 0))],
        out_specs=pl.BlockSpec((None, L, LANES), lambda hp, b: (b, 0, hp)),
        out_shape=jax.ShapeDtypeStruct((B, L, MIX_W), BF16),
        scratch_shapes=[pltpu.VMEM((len(kinds), 2 * NA_CHUNK_ROWS * GRID_W, NA_WIN_ROWS * GRID_W), F32)],
        name="nbr_attention",
        compiler_params=_params("arbitrary", "arbitrary"),
    )(z, z, z, cache_k, cache_v, blocks)


def _na_bias_blocks(rpb):
    nr, nc = 2 * NA_KR - 1, 2 * NA_KW - 1
    qc = np.arange(GRID_W)
    kc = np.arange(GRID_W)
    ws = np.clip(qc - NA_KW // 2, 0, GRID_W - NA_KW)
    col_ok = (kc[None, :] >= ws[:, None]) & (kc[None, :] < ws[:, None] + NA_KW)
    coff = np.clip(kc[None, :] - qc[:, None] + NA_KW - 1, 0, nc - 1)
    sel_c = ((coff[None] == np.arange(nc)[:, None, None]) & col_ok[None]).astype(np.float32)
    sel_c = np.concatenate([sel_c, sel_c], -1)
    ok = np.concatenate([col_ok, col_ok], -1)[None] & (np.arange(nr + 1) < nr)[:, None, None]
    H = rpb.shape[0]
    rows = jnp.pad(rpb.astype(F32), ((0, 0), (0, 1), (0, 0)))
    t = jnp.einsum('hrc,cqk->hrqk', rows, sel_c, precision=lax.Precision.HIGHEST)
    return jnp.where(ok[None], t, NEG_INF).reshape(H // 2, 2, nr + 1, GRID_W, 2 * GRID_W)


def _merge_body(x_ref, p_ref, r_ref, u_ref, y_ref, n_ref, ga_ref, gb_ref, gc_ref,
                d_ref, wglu_ref, wbr_ref, wo_ref, lg_ref, lb_ref, o_ref,
                wglu_s, wbr_s, wo_s, *, L, row0, rstride):
    i = pl.program_id(0)
    tm = x_ref.shape[0]

    @pl.when(i == 0)
    def _():
        wglu_s[...] = wglu_ref[...].astype(BF16)
        wbr_s[...] = wbr_ref[...].astype(BF16)
        wo_s[...] = wo_ref[...].astype(BF16)

    row = row0 + rstride * ((i * tm) // L)
    g1 = _mod_row(p_ref, row, 2)

    y = d_ref[...] * u_ref[...].astype(F32) + y_ref[...]
    y = jax.nn.gelu(y)
    s_out = y * jax.nn.sigmoid(_dot(y.astype(BF16), wglu_s[...]))

    def gate(ref):
        return jax.nn.sigmoid(ref[...].astype(F32))

    merged = (gate(ga_ref) * _dot(r_ref[...], wbr_s[0])
              + gate(gb_ref) * _dot(s_out.astype(BF16), wbr_s[1])
              + gate(gc_ref) * _dot(n_ref[...], wbr_s[2]))
    m = _dot(merged.astype(BF16), wo_s[...])
    o_ref[...] = _layer_norm(DEEPNORM_ALPHA * x_ref[...] + g1 * m, lg_ref[...], lb_ref[...])


def _merge(x, p, z, r_out, y, n_out, ssm_d, w_glu, w_branch, w_o, ln_g, ln_b, *, layer, L, row0, rstride):
    T = x.shape[0]
    tm = MERGE_TILE
    gate0 = 8 * MIX_W // D_MODEL

    def tok(w):
        return pl.BlockSpec((tm, w), lambda i: (i, 0))

    def full(shape):
        return pl.BlockSpec((None,) + shape, lambda i: (layer,) + (0,) * len(shape))

    body = functools.partial(_merge_body, L=L, row0=row0, rstride=rstride)
    return pl.pallas_call(
        body,
        grid=(T // tm,),
        in_specs=[tok(D_MODEL), full((N_PAD_ROWS, 6 * D_MODEL)), tok(MIX_W),
                  pl.BlockSpec((tm, MIX_W), lambda i: (i, SU_SECTION)), tok(MIX_W), tok(MIX_W),
                  pl.BlockSpec((tm, D_MODEL), lambda i: (i, gate0)),
                  pl.BlockSpec((tm, D_MODEL), lambda i: (i, gate0 + 1)),
                  pl.BlockSpec((tm, D_MODEL), lambda i: (i, gate0 + 2)),
                  full((1, MIX_W)), full((MIX_W, MIX_W)), full((3, MIX_W, D_MODEL)),
                  full((D_MODEL, D_MODEL)), full((1, D_MODEL)), full((1, D_MODEL))],
        out_specs=tok(D_MODEL),
        out_shape=jax.ShapeDtypeStruct((T, D_MODEL), F32),
        scratch_shapes=[pltpu.VMEM((MIX_W, MIX_W), BF16), pltpu.VMEM((3, MIX_W, D_MODEL), BF16),
                        pltpu.VMEM((D_MODEL, D_MODEL), BF16)],
        name="merge",
        compiler_params=_params("arbitrary"),
    )(x, p, r_out, z, y, n_out, z, z, z, ssm_d.reshape(DEPTH, 1, MIX_W), w_glu, w_branch, w_o,
      ln_g.reshape(DEPTH, 1, D_MODEL), ln_b.reshape(DEPTH, 1, D_MODEL))


def _ffn_body(x_ref, p_ref, wa_ref, wb_ref, cwa_ref, cwb_ref, cba_ref, cbb_ref, wd_ref, lg_ref, lb_ref,
              o_ref, h_scr, acc_scr, *, L, row0, rstride):
    i = pl.program_id(0)
    j = pl.program_id(1)
    tm = x_ref.shape[0]
    nb = tm // L

    @pl.when(j == 0)
    def _():
        for s in range(nb):
            row = row0 + rstride * (i * nb + s)
            sh = _mod_row(p_ref, row, 3)
            sc = _mod_row(p_ref, row, 4)
            h_scr[s * L:(s + 1) * L, :] = (x_ref[s * L:(s + 1) * L, :] * (1.0 + sc) + sh).astype(BF16)
        acc_scr[...] = jnp.zeros_like(acc_scr)

    t = lax.broadcasted_iota(jnp.int32, (tm, 1), 0) % L
    has_prev = t != 0
    has_next = t != L - 1

    def conv(w_ref, cw_ref, cb_ref):
        zc = _dot(h_scr[...], w_ref[...].astype(BF16))
        zp = jnp.where(has_prev, pltpu.roll(zc, 1, 0), 0.0)
        zn = jnp.where(has_next, pltpu.roll(zc, tm - 1, 0), 0.0)
        return zp * cw_ref[0:1, :] + zc * cw_ref[1:2, :] + zn * cw_ref[2:3, :] + cb_ref[...]

    a = conv(wa_ref, cwa_ref, cba_ref)
    b = conv(wb_ref, cwb_ref, cbb_ref)
    acc_scr[...] += _dot((jax.nn.gelu(a) * b).astype(BF16), wd_ref[...].astype(BF16))

    @pl.when(j == pl.num_programs(1) - 1)
    def _():
        for s in range(nb):
            row = row0 + rstride * (i * nb + s)
            g2 = _mod_row(p_ref, row, 5)
            sl = slice(s * L, (s + 1) * L)
            o_ref[sl, :] = _layer_norm(DEEPNORM_ALPHA * x_ref[sl, :] + g2 * acc_scr[sl, :],
                                       lg_ref[...], lb_ref[...])


def _conv_ffn(x, p, w_up, conv_w, conv_b, w_down, ln_g, ln_b, *, layer, L, row0, rstride):
    T = x.shape[0]
    tm = TOKEN_TILE
    nff = D_FF // FF_TILE
    body = functools.partial(_ffn_body, L=L, row0=row0, rstride=rstride)
    conv_b = conv_b.reshape(DEPTH, 1, 2 * D_FF)
    return pl.pallas_call(
        body,
        grid=(T // tm, nff),
        in_specs=[pl.BlockSpec((tm, D_MODEL), lambda i, j: (i, 0)),
                  pl.BlockSpec((None, N_PAD_ROWS, 6 * D_MODEL), lambda i, j: (layer, 0, 0)),
                  pl.BlockSpec((None, D_MODEL, FF_TILE), lambda i, j: (layer, 0, j)),
                  pl.BlockSpec((None, D_MODEL, FF_TILE), lambda i, j: (layer, 0, nff + j)),
                  pl.BlockSpec((None, 3, FF_TILE), lambda i, j: (layer, 0, j)),
                  pl.BlockSpec((None, 3, FF_TILE), lambda i, j: (layer, 0, nff + j)),
                  pl.BlockSpec((None, 1, FF_TILE), lambda i, j: (layer, 0, j)),
                  pl.BlockSpec((None, 1, FF_TILE), lambda i, j: (layer, 0, nff + j)),
                  pl.BlockSpec((None, FF_TILE, D_MODEL), lambda i, j: (layer, j, 0)),
                  pl.BlockSpec((None, 1, D_MODEL), lambda i, j: (layer, 0, 0)),
                  pl.BlockSpec((None, 1, D_MODEL), lambda i, j: (layer, 0, 0))],
        out_specs=pl.BlockSpec((tm, D_MODEL), lambda i, j: (i, 0)),
        out_shape=jax.ShapeDtypeStruct((T, D_MODEL), F32),
        scratch_shapes=[pltpu.VMEM((tm, D_MODEL), BF16), pltpu.VMEM((tm, D_MODEL), F32)],
        name="conv_ffn",
        compiler_params=_params("arbitrary", "arbitrary"),
    )(x, p, w_up, w_up, conv_w, conv_w, conv_b, conv_b, w_down, ln_g.reshape(DEPTH, 1, D_MODEL),
      ln_b.reshape(DEPTH, 1, D_MODEL))


def _s5_states_in(state_ssm, layer):
    B = state_ssm.shape[0]
    nlb = MIX_W // LANES
    h = state_ssm[:, layer].reshape(B, 2, nlb, S5_LBLK_GROUPS, SSM_STATE, 2)
    return jnp.transpose(h, (1, 2, 0, 5, 3, 4)).reshape(2, nlb, B, 2 * S5_LBLK_GROUPS * SSM_STATE)


def _s5_states_out(fin):
    nlb, B = fin.shape[1], fin.shape[2]
    h = fin.reshape(2, nlb, B, 2, S5_LBLK_GROUPS, SSM_STATE)
    return jnp.transpose(h, (2, 0, 1, 4, 5, 3)).reshape(B, 2, SSM_GROUPS, SSM_STATE, 2)


def _layer(x, p, lw, *, B, L, row0, rstride, latent, layer, extra):
    T = B * L
    z, *kv = _inproj(x, p, lw['w_in'], layer=layer, L=L, row0=row0, rstride=rstride, want_kv=not latent)
    z3 = z.reshape(B, L, IN_COLS)
    log_gamma = jax.nn.log_sigmoid(lw['ret_decay'].astype(F32))
    if latent:
        r_out = _retention(z3, log_gamma, B=B, L=L, rope_tabs=extra['rope'], s0=extra['state_ret'],
                           layer=layer, want_state=False)[0]
        n_out = _neighbourhood_attention(z3, extra['cache_k'], extra['cache_v'], extra['bias'][layer],
                                         B=B, L=L, layer=layer)
        y, _ = _s5(z, lw['s5'], _s5_states_in(extra['state_ssm'], layer), B=B, L=L)
        states = None
    else:
        r_out, ret_state = _retention(z3, log_gamma, B=B, L=L, want_state=True)
        n_out = _context_attention(z3, B=B, L=L)
        y, fin = _s5(z, lw['s5'], None, B=B, L=L)
        states = (ret_state, _s5_states_out(fin), kv[0].reshape(B, L, MIX_W), kv[1].reshape(B, L, MIX_W))
    x = _merge(x, p, z, r_out.reshape(T, MIX_W), y, n_out.reshape(T, MIX_W),
               lw['ssm_d'], lw['ssm_w_glu'], lw['w_branch'], lw['w_o'], lw['ln1_g'], lw['ln1_b'],
               layer=layer, L=L, row0=row0, rstride=rstride)
    x = _conv_ffn(x, p, lw['w_up'], lw['conv_w'], lw['conv_b'], lw['w_down'], lw['ln2_g'], lw['ln2_b'],
                  layer=layer, L=L, row0=row0, rstride=rstride)
    return x, states


def kernel(x_prompt, x_sample, state_ret, state_ssm, cache_na_k, cache_na_v, c, c_ctx, w_ada, b_ada, w_in, ret_decay, ssm_a_re, ssm_a_im, ssm_log_dt, ssm_b_re, ssm_b_im, ssm_c_re, ssm_c_im, ssm_d, ssm_w_glu, na_rpb, w_branch, w_o, ln1_g, ln1_b, w_up, conv_w, conv_b, w_down, ln2_g, ln2_b):
    B, L, _ = x_prompt.shape
    Bd, Ld, _ = x_sample.shape
    Lc = cache_na_k.shape[2]

    cond = jnp.concatenate([c_ctx[None, :], c, jnp.zeros((N_PAD_ROWS - 1 - Bd, D_MODEL), F32)], 0)
    p_all = _ada(cond, w_ada, b_ada)

    extra = dict(rope=_rope_tables(Ld), state_ret=state_ret, state_ssm=state_ssm,
                 cache_k=cache_na_k.reshape(Bd, DEPTH, Lc, MIX_W),
                 cache_v=cache_na_v.reshape(Bd, DEPTH, Lc, MIX_W),
                 bias=[_na_bias_blocks(na_rpb[l]) for l in range(DEPTH)])

    xp = x_prompt.reshape(B * L, D_MODEL)
    xs = x_sample.reshape(Bd * Ld, D_MODEL)
    ret_states, ssm_states, na_ks, na_vs = [], [], [], []
    for l in range(DEPTH):
        lw = dict(w_in=w_in, ret_decay=ret_decay[l], ssm_d=ssm_d, ssm_w_glu=ssm_w_glu,
                  w_branch=w_branch, w_o=w_o, ln1_g=ln1_g, ln1_b=ln1_b, w_up=w_up,
                  conv_w=conv_w, conv_b=conv_b, w_down=w_down, ln2_g=ln2_g, ln2_b=ln2_b,
                  s5=_s5_operators(ssm_a_re[l], ssm_a_im[l], ssm_log_dt[l], ssm_b_re[l], ssm_b_im[l],
                                   ssm_c_re[l], ssm_c_im[l]))
        xp, (s_ret, s_ssm, nk, nv) = _layer(xp, p_all, lw, B=B, L=L, row0=0, rstride=0,
                                            latent=False, layer=l, extra=None)
        ret_states.append(s_ret)
        ssm_states.append(s_ssm)
        na_ks.append(nk)
        na_vs.append(nv)
        xs, _ = _layer(xs, p_all, lw, B=Bd, L=Ld, row0=1, rstride=1, latent=True, layer=l, extra=extra)
    def heads(xs_):
        return jnp.stack(xs_, 1).reshape(B, DEPTH, L, NA_HEADS, NA_HEAD_DIM)

    return (xp.reshape(B, L, D_MODEL), xs.reshape(Bd, Ld, D_MODEL),
            jnp.stack(ret_states, 1), jnp.stack(ssm_states, 1), heads(na_ks), heads(na_vs))
```

```python
import functools

import jax
import jax.numpy as jnp
import numpy as np
from jax import lax
from jax.experimental import pallas as pl
from jax.experimental.pallas import tpu as pltpu

F32 = jnp.float32
BF16 = jnp.bfloat16

D_MODEL = 1024
DEPTH = 2
GRID_W = 64
MIX_W = D_MODEL // 2
N_RET_HEADS = 4
RET_DK = MIX_W // N_RET_HEADS
SSM_GROUP = 16
SSM_GROUPS = MIX_W // SSM_GROUP
SSM_STATE = 64
NA_HEADS = 8
NA_HEAD_DIM = MIX_W // NA_HEADS
NA_KR = 8
NA_KW = 16
D_FF = ((8 * D_MODEL // 3 + 127) // 128) * 128
ROPE_BASE = 10000.0
LN_EPS = 1e-5
NEG_INF = -1e30
DEEPNORM_ALPHA = (2 * DEPTH) ** 0.25
IN_COLS = 8 * MIX_W + 3 * D_MODEL

VMEM_LIMIT_BYTES = 56 * 1024 * 1024
LANES = 128

TOKEN_TILE = 1024
MERGE_TILE = 256
COL_TILE = 1024
SU_SECTION = 4
NK_SECTION = 6
FF_TILE = 256
RET_CHUNK = 256
RET_ROWS = 1024
S5_CHUNK = 8
S5_PITCH_PAD = 8
S5_LBLK_GROUPS = LANES // SSM_GROUP
N_PAD_ROWS = 8
CTX_ATTN_BATCH = 4
NA_CHUNK_ROWS = 4
NA_WIN_ROWS = 12


def _params(*sem):
    return pltpu.CompilerParams(dimension_semantics=sem, vmem_limit_bytes=VMEM_LIMIT_BYTES)


def _dot(a, b):
    return jnp.dot(a, b, preferred_element_type=F32)


def _dot_nt(a, b):
    return lax.dot_general(a, b, (((1,), (1,)), ((), ())), preferred_element_type=F32)


def _layer_norm(x, g, b):
    mu = jnp.mean(x, -1, keepdims=True)
    xc = x - mu
    var = jnp.mean(xc * xc, -1, keepdims=True)
    return xc * lax.rsqrt(var + LN_EPS) * g + b


def _ada_body(c_ref, w_ref, b_ref, o_ref):
    c = c_ref[...]
    s = c * jax.nn.sigmoid(c)
    o_ref[...] = _dot(s.astype(BF16), w_ref[...].astype(BF16)) + b_ref[...]


def _ada(cond, w_ada, b_ada):
    tn = 1024
    return pl.pallas_call(
        _ada_body,
        grid=(DEPTH, 6 * D_MODEL // tn),
        in_specs=[pl.BlockSpec((N_PAD_ROWS, D_MODEL), lambda l, j: (0, 0)),
                  pl.BlockSpec((None, D_MODEL, tn), lambda l, j: (l, 0, j)),
                  pl.BlockSpec((None, 1, tn), lambda l, j: (l, 0, j))],
        out_specs=pl.BlockSpec((None, N_PAD_ROWS, tn), lambda l, j: (l, 0, j)),
        out_shape=jax.ShapeDtypeStruct((DEPTH, N_PAD_ROWS, 6 * D_MODEL), F32),
        name="ada",
        compiler_params=_params("arbitrary", "arbitrary"),
    )(cond, w_ada, b_ada.reshape(DEPTH, 1, 6 * D_MODEL))


def _mod_row(p_ref, row, k):
    return p_ref[pl.ds(row, 1), k * D_MODEL:(k + 1) * D_MODEL]


def _kv_tile(n):
    col = (NK_SECTION + n) * MIX_W
    return col // COL_TILE, col % COL_TILE


def _inproj_body(x_ref, p_ref, w_ref, z_ref, *rest, L, row0, rstride):
    h_scr, w_scr = rest[-2:]
    kv_refs = rest[:-2]
    j = pl.program_id(0)
    i = pl.program_id(1)
    nb = x_ref.shape[0] // L

    @pl.when(j == 0)
    def _():
        for s in range(nb):
            row = row0 + rstride * (i * nb + s)
            sh = _mod_row(p_ref, row, 0)
            sc = _mod_row(p_ref, row, 1)
            h_scr[i, s * L:(s + 1) * L, :] = (x_ref[s * L:(s + 1) * L, :] * (1.0 + sc) + sh).astype(BF16)

    @pl.when(i == 0)
    def _():
        w_scr[...] = w_ref[...].astype(BF16)

    acc = _dot(h_scr[i], w_scr[...])
    z_ref[...] = acc.astype(BF16)

    for n, ref in enumerate(kv_refs):
        tile, off = _kv_tile(n)

        @pl.when(j == tile)
        def _(ref=ref, off=off):
            ref[...] = acc[:, off:off + MIX_W]


def _inproj(x, p, w_in, *, layer, L, row0, rstride, want_kv):
    T = x.shape[0]
    tm = TOKEN_TILE
    n_i = T // tm
    body = functools.partial(_inproj_body, L=L, row0=row0, rstride=rstride)
    n_kv = 2 if want_kv else 0

    def only_at(tile):
        return lambda j, i: jnp.where(j < tile, 0, jnp.where(j > tile, n_i - 1, i))

    kv_i = [only_at(_kv_tile(n)[0]) for n in range(n_kv)]
    return pl.pallas_call(
        body,
        grid=(IN_COLS // COL_TILE, n_i),
        in_specs=[pl.BlockSpec((tm, D_MODEL), lambda j, i: (jnp.where(j == 0, i, n_i - 1), 0)),
                  pl.BlockSpec((None, N_PAD_ROWS, 6 * D_MODEL), lambda j, i: (layer, 0, 0)),
                  pl.BlockSpec((None, D_MODEL, COL_TILE), lambda j, i: (layer, 0, j))],
        out_specs=[pl.BlockSpec((tm, COL_TILE), lambda j, i: (i, j))]
        + [pl.BlockSpec((tm, MIX_W), lambda j, i, f=f: (f(j, i), 0)) for f in kv_i],
        out_shape=[jax.ShapeDtypeStruct((T, IN_COLS), BF16)] + [jax.ShapeDtypeStruct((T, MIX_W), F32)] * n_kv,
        scratch_shapes=[pltpu.VMEM((n_i, tm, D_MODEL), BF16), pltpu.VMEM((D_MODEL, COL_TILE), BF16)],
        name="inproj",
        compiler_params=_params("arbitrary", "arbitrary"),
    )(x, p, w_in)


def _rope(x, cos, s_up, s_dn):
    return x * cos + pltpu.roll(x, 96, 1) * s_up + pltpu.roll(x, 32, 1) * s_dn


def _ret_body(*refs, n, rope, has_s0, want_state):
    refs = list(refs)
    lg_ref, q_ref, k_ref, v_ref, g_ref = refs[:5]
    refs = refs[5:]
    if rope:
        cos_ref, sup_ref, sdn_ref = refs[:3]
        refs = refs[3:]
    if has_s0:
        s0_ref = refs[0]
        refs = refs[1:]
    o_ref = refs[0]
    refs = refs[1:]
    if want_state:
        st_ref = refs[0]
        refs = refs[1:]
    q_scr, k_scr, sb_scr, decay_scr = refs

    C = RET_CHUNK
    h = pl.program_id(0)
    lf = lg_ref[0, h]
    lb = lg_ref[1, h]

    @pl.when(pl.program_id(1) == 0)
    def _():
        ti = lax.broadcasted_iota(jnp.int32, (C, C), 0)
        si = lax.broadcasted_iota(jnp.int32, (C, C), 1)
        dlt = (ti - si).astype(F32)
        decay_scr[...] = (jnp.where(dlt >= 0, jnp.exp(lf * jnp.maximum(dlt, 0.0)), 0.0)
                          + jnp.where(dlt <= 0, jnp.exp(lb * jnp.maximum(-dlt, 0.0)), 0.0))

    tcol = lax.broadcasted_iota(jnp.int32, (C, 1), 0).astype(F32)
    qd_f = jnp.exp(lf * (tcol + 1.0))
    qd_b = jnp.exp(lb * (C - tcol))
    kd_f = jnp.exp(lf * (C - 1.0 - tcol))
    kd_b = jnp.exp(lb * tcol)
    cd_f = jnp.exp(lf * jnp.full((1, RET_DK), float(C), F32))
    cd_b = jnp.exp(lb * jnp.full((1, RET_DK), float(C), F32))

    def kv_outer(kc, vc, kd):
        return _dot((kc * kd).T.astype(BF16), vc)

    for bb in range(q_ref.shape[0]):
        q = q_ref[bb].astype(F32)
        k = k_ref[bb].astype(F32)
        if rope:
            q = _rope(q, cos_ref[...], sup_ref[...], sdn_ref[...])
            k = _rope(k, cos_ref[...], sup_ref[...], sdn_ref[...])
        q_scr[bb] = q
        k_scr[bb] = k * (RET_DK ** -0.5)

        s_b = s0_ref[bb, 1] if has_s0 else jnp.zeros((RET_DK, RET_DK), F32)
        for i in reversed(range(n)):
            sb_scr[bb, i] = s_b
            if i > 0 or want_state:
                s_b = s_b * cd_b + kv_outer(k_scr[bb, i * C:(i + 1) * C, :], v_ref[bb, i * C:(i + 1) * C, :], kd_b)

        s_f = s0_ref[bb, 0] if has_s0 else jnp.zeros((RET_DK, RET_DK), F32)
        for i in range(n):
            sl = slice(i * C, (i + 1) * C)
            qc = q_scr[bb, sl, :]
            kc = k_scr[bb, sl, :]
            vc = v_ref[bb, sl, :]
            att = _dot_nt(qc.astype(BF16), kc.astype(BF16)) * decay_scr[...]
            o = _dot(att.astype(BF16), vc)
            o = o + _dot((qc * qd_f).astype(BF16), s_f.astype(BF16))
            o = o + _dot((qc * qd_b).astype(BF16), sb_scr[bb, i].astype(BF16))
            mu = jnp.mean(o, -1, keepdims=True)
            oc = o - mu
            var = jnp.mean(oc * oc, -1, keepdims=True)
            gc = g_ref[bb, sl, :].astype(F32)
            o_ref[bb, sl, :] = (oc * lax.rsqrt(var + LN_EPS) * (gc * jax.nn.sigmoid(gc))).astype(BF16)
            if i < n - 1 or want_state:
                s_f = s_f * cd_f + kv_outer(kc, vc, kd_f)

        if want_state:
            st_ref[bb, 0] = s_f
            st_ref[bb, 1] = s_b


def _retention(z, log_gamma, *, B, L, rope_tabs=None, s0=None, layer=0, want_state):
    n = L // RET_CHUNK
    H = N_RET_HEADS
    nblk = MIX_W // RET_DK

    nbb = max(1, RET_ROWS // L)
    assert B % nbb == 0

    def sec(s):
        return pl.BlockSpec((nbb, L, RET_DK), lambda h, b: (b, 0, s * nblk + h))

    in_specs = [pl.BlockSpec(memory_space=pltpu.SMEM), sec(0), sec(1), sec(2), sec(3)]
    args = [log_gamma, z, z, z, z]
    if rope_tabs is not None:
        in_specs += [pl.BlockSpec((L, RET_DK), lambda h, b: (0, 0))] * 3
        args += list(rope_tabs)
    if s0 is not None:
        in_specs.append(pl.BlockSpec((nbb, None, 2, None, RET_DK, RET_DK), lambda h, b: (b, layer, 0, h, 0, 0)))
        args.append(s0)
    out_specs = [pl.BlockSpec((nbb, L, RET_DK), lambda h, b: (b, 0, h))]
    out_shape = [jax.ShapeDtypeStruct((B, L, MIX_W), BF16)]
    if want_state:
        out_specs.append(pl.BlockSpec((nbb, 2, None, RET_DK, RET_DK), lambda h, b: (b, 0, h, 0, 0)))
        out_shape.append(jax.ShapeDtypeStruct((B, 2, H, RET_DK, RET_DK), F32))
    body = functools.partial(_ret_body, n=n, rope=rope_tabs is not None, has_s0=s0 is not None,
                             want_state=want_state)
    return pl.pallas_call(
        body,
        grid=(H, B // nbb),
        in_specs=in_specs,
        out_specs=out_specs,
        out_shape=out_shape,
        scratch_shapes=[pltpu.VMEM((nbb, L, RET_DK), F32), pltpu.VMEM((nbb, L, RET_DK), F32),
                        pltpu.VMEM((nbb, n, RET_DK, RET_DK), F32), pltpu.VMEM((RET_CHUNK, RET_CHUNK), F32)],
        name="retention",
        compiler_params=_params("arbitrary", "arbitrary"),
    )(*args)


def _rope_tables(L):
    pos = jnp.arange(L)
    row = (pos // GRID_W).astype(F32)
    col = (pos % GRID_W).astype(F32)
    quarter = RET_DK // 4
    inv_freq = ROPE_BASE ** (-jnp.arange(quarter, dtype=F32) / quarter)
    ang_r = row[:, None] * inv_freq[None, :]
    ang_c = col[:, None] * inv_freq[None, :]
    zero = jnp.zeros_like(ang_r)
    cos = jnp.concatenate([jnp.cos(ang_r), jnp.cos(ang_r), jnp.cos(ang_c), jnp.cos(ang_c)], -1)
    s_up = jnp.concatenate([-jnp.sin(ang_r), zero, -jnp.sin(ang_c), zero], -1)
    s_dn = jnp.concatenate([zero, jnp.sin(ang_r), zero, jnp.sin(ang_c)], -1)
    return cos, s_up, s_dn


def _s5_body(*refs, B, nC, has_h0):
    refs = list(refs)
    u_ref, c0_ref, bs_ref, cp_ref, a8_ref = refs[:5]
    refs = refs[5:]
    if has_h0:
        h0_ref = refs[0]
        refs = refs[1:]
    y_ref, fin_ref, u_scr, a_scr, m_scr, bs_scr, cp_scr, c0_scr, s_scr, x_scr, y_scr = refs

    TC = S5_CHUNK
    R = B * nC
    P = nC + S5_PITCH_PAD
    nsl = s_scr.shape[0]
    half = nsl // 2
    ng = S5_LBLK_GROUPS
    sw = ng * SSM_STATE

    u_scr[...] = u_ref[...].astype(F32)
    for s in range(TC):
        a_scr[:, s * LANES:(s + 1) * LANES] = u_scr[pl.ds(s, R, stride=TC), :].astype(BF16)

    bs_scr[...] = jnp.zeros_like(bs_scr)
    cp_scr[...] = jnp.zeros_like(cp_scr)
    c0_scr[...] = jnp.zeros_like(c0_scr)

    for d in range(2):
        for g in range(ng):
            for part in range(2):
                cols = slice(part * sw + g * SSM_STATE, part * sw + (g + 1) * SSM_STATE)
                lo = (g % 2) * SSM_STATE
                c0_scr[g * SSM_GROUP:(g + 1) * SSM_GROUP, cols] = c0_ref[d, g, part, :, lo:lo + SSM_STATE]
                for s in range(TC):
                    rows = slice(s * LANES + g * SSM_GROUP, s * LANES + (g + 1) * SSM_GROUP)
                    bs_scr[rows, cols] = bs_ref[d, s, g, part, :, lo:lo + SSM_STATE]
                    cp_scr[rows, cols] = cp_ref[d, s, g, part, :, lo:lo + SSM_STATE]

        lag = _dot_nt(bs_scr[...], c0_scr[...]).astype(BF16)
        for s in range(TC):
            for t in range(TC):
                k = (t - s) if d == 0 else (s - t)
                src = (TC - 1 - k) if d == 0 else k
                blk = lag[src * LANES:(src + 1) * LANES, :] if k >= 0 else jnp.zeros((LANES, LANES), BF16)
                m_scr[s * LANES:(s + 1) * LANES, t * LANES:(t + 1) * LANES] = blk

        a = a_scr[...]
        yd = _dot(a, m_scr[...])
        if d == 0:
            y_scr[...] = yd
        else:
            y_scr[...] += yd

        sm = _dot(a, bs_scr[...])
        for b in range(B):
            for sl in range(nsl):
                s_scr[sl, b * P:b * P + nC, :] = sm[b * nC:(b + 1) * nC, sl * LANES:(sl + 1) * LANES]

        a_r = [jnp.broadcast_to(a8_ref[d, 0, :, q * LANES:(q + 1) * LANES], (B, LANES)) for q in range(half)]
        a_i = [jnp.broadcast_to(a8_ref[d, 1, :, q * LANES:(q + 1) * LANES], (B, LANES)) for q in range(half)]
        if has_h0:
            init = tuple(h0_ref[d, :, sl * LANES:(sl + 1) * LANES] for sl in range(nsl))
        else:
            init = tuple(jnp.zeros((B, LANES), F32) for _ in range(nsl))

        def step(j, carry, d=d, a_r=a_r, a_i=a_i):
            c = j if d == 0 else nC - 1 - j
            rows = pl.ds(c, B, stride=P)
            new_r, new_i = [], []
            for q in range(half):
                xr, xi = carry[q], carry[half + q]
                sr = s_scr[q, rows, :]
                si = s_scr[half + q, rows, :]
                s_scr[q, rows, :] = xr
                s_scr[half + q, rows, :] = xi
                new_r.append(a_r[q] * xr - a_i[q] * xi + sr)
                new_i.append(a_r[q] * xi + a_i[q] * xr + si)
            return tuple(new_r + new_i)

        fin = lax.fori_loop(0, nC, step, init)
        for sl in range(nsl):
            fin_ref[d, :, sl * LANES:(sl + 1) * LANES] = fin[sl]

        for b in range(B):
            for sl in range(nsl):
                x_scr[b * nC:(b + 1) * nC, sl * LANES:(sl + 1) * LANES] = \
                    s_scr[sl, b * P:b * P + nC, :].astype(BF16)
        y_scr[...] += _dot_nt(x_scr[...], cp_scr[...])

    for t in range(TC):
        y_ref[pl.ds(t, R, stride=TC), :] = y_scr[:, t * LANES:(t + 1) * LANES]


def _s5(z, ops, h0, *, B, L):
    c0, bs, cp, a8 = ops
    T = B * L
    nC = L // S5_CHUNK
    nlb = MIX_W // LANES
    sc = 2 * S5_LBLK_GROUPS * SSM_STATE
    kc = S5_CHUNK * LANES
    su0 = SU_SECTION * MIX_W // LANES

    blocks = pl.BlockSpec((2, None, S5_CHUNK, S5_LBLK_GROUPS, 2, SSM_GROUP, LANES),
                          lambda lb: (0, lb, 0, 0, 0, 0, 0))
    in_specs = [pl.BlockSpec((T, LANES), lambda lb: (0, su0 + lb)),
                pl.BlockSpec((2, None, S5_LBLK_GROUPS, 2, SSM_GROUP, LANES), lambda lb: (0, lb, 0, 0, 0, 0)),
                blocks, blocks,
                pl.BlockSpec((2, None, 2, 1, sc // 2), lambda lb: (0, lb, 0, 0, 0))]
    args = [z, c0, bs, cp, a8]
    if h0 is not None:
        in_specs.append(pl.BlockSpec((2, None, B, sc), lambda lb: (0, lb, 0, 0)))
        args.append(h0)
    body = functools.partial(_s5_body, B=B, nC=nC, has_h0=h0 is not None)
    return pl.pallas_call(
        body,
        grid=(nlb,),
        in_specs=in_specs,
        out_specs=[pl.BlockSpec((T, LANES), lambda lb: (0, lb)),
                   pl.BlockSpec((2, None, B, sc), lambda lb: (0, lb, 0, 0))],
        out_shape=[jax.ShapeDtypeStruct((T, MIX_W), F32),
                   jax.ShapeDtypeStruct((2, nlb, B, sc), F32)],
        scratch_shapes=[pltpu.VMEM((T, LANES), F32), pltpu.VMEM((B * nC, kc), BF16),
                        pltpu.VMEM((kc, kc), BF16), pltpu.VMEM((kc, sc), BF16), pltpu.VMEM((kc, sc), BF16),
                        pltpu.VMEM((LANES, sc), BF16),
                        pltpu.VMEM((sc // LANES, B * (nC + S5_PITCH_PAD), LANES), F32),
                        pltpu.VMEM((B * nC, sc), BF16), pltpu.VMEM((B * nC, kc), F32)],
        name="s5",
        compiler_params=_params("arbitrary"),
    )(*args)


def _s5_operators(a_re, a_im, log_dt, b_re, b_im, c_re, c_im):
    TC = S5_CHUNK
    nlb = MIX_W // LANES
    ng = S5_LBLK_GROUPS
    lr = jnp.minimum(a_re, -1e-4)
    li = a_im
    dt = jnp.exp(log_dt)[..., None]
    k = jnp.arange(TC + 1, dtype=F32)[:, None, None, None]
    mag = jnp.exp(k * (lr * dt)[None])
    pr = mag * jnp.cos(k * (li * dt)[None])
    pi = mag * jnp.sin(k * (li * dt)[None])
    ar, ai = pr[1], pi[1]
    den = lr * lr + li * li
    sr = ((ar - 1.0) * lr + ai * li) / den
    si = (ai * lr - (ar - 1.0) * li) / den
    bbr = sr[..., None] * b_re[None] - si[..., None] * b_im[None]
    bbi = sr[..., None] * b_im[None] + si[..., None] * b_re[None]

    def lanes2(x):
        return jnp.concatenate([x, x], -1)

    def powers(fwd, bwd):
        x = lanes2(jnp.stack([fwd, bwd], 0)).reshape(2, TC, nlb, ng, 1, LANES)
        return jnp.swapaxes(x, 1, 2)

    def per_group(x):
        return lanes2(x).reshape(2, nlb, 1, ng, SSM_GROUP, LANES)

    er = powers(jnp.flip(pr[:TC, 0], 0), pr[:TC, 1])
    ei = powers(jnp.flip(pi[:TC, 0], 0), pi[:TC, 1])
    btr = per_group(jnp.swapaxes(bbr, -1, -2))
    bti = per_group(jnp.swapaxes(bbi, -1, -2))
    bs = jnp.stack([er * btr - ei * bti, er * bti + ei * btr], 4).astype(BF16)

    fr = powers(pr[1:, 0], jnp.flip(pr[1:, 1], 0))
    fi = powers(pi[1:, 0], jnp.flip(pi[1:, 1], 0))
    ctr = per_group(c_re)
    cti = per_group(c_im)
    cp = jnp.stack([ctr * fr - cti * fi, -(ctr * fi + cti * fr)], 4).astype(BF16)
    c0 = jnp.stack([ctr, -cti], 4)[:, :, 0].astype(BF16)

    sw = ng * SSM_STATE
    a8 = jnp.stack([pr[TC].reshape(2, nlb, 1, sw), pi[TC].reshape(2, nlb, 1, sw)], 2)
    return c0, bs, cp, a8


def _head_masks(shape):
    lane = lax.broadcasted_iota(jnp.int32, shape, 1)
    return lane < NA_HEAD_DIM


def _cattn_body(q_ref, k_ref, v_ref, o_ref):
    first = _head_masks(q_ref.shape[1:])
    for bb in range(q_ref.shape[0]):
        q = q_ref[bb]
        k = k_ref[bb]
        v = v_ref[bb]
        outs = []
        for e in range(2):
            qe = jnp.where(first if e == 0 else jnp.logical_not(first), q, jnp.zeros_like(q))
            s = _dot_nt(qe, k) * (NA_HEAD_DIM ** -0.5)
            m = jnp.max(s, -1, keepdims=True)
            p = jnp.exp(s - m)
            l = jnp.sum(p, -1, keepdims=True)
            outs.append(_dot(p.astype(BF16), v) / l)
        o_ref[bb] = jnp.where(first, outs[0], outs[1]).astype(BF16)


def _context_attention(z, *, B, L):
    nblk = MIX_W // LANES
    nb = CTX_ATTN_BATCH

    def sec(s):
        return pl.BlockSpec((nb, L, LANES), lambda b, hp: (b, 0, s * nblk + hp))

    return pl.pallas_call(
        _cattn_body,
        grid=(B // nb, nblk),
        in_specs=[sec(5), sec(6), sec(7)],
        out_specs=pl.BlockSpec((nb, L, LANES), lambda b, hp: (b, 0, hp)),
        out_shape=jax.ShapeDtypeStruct((B, L, MIX_W), BF16),
        name="ctx_attention",
        compiler_params=_params("arbitrary", "arbitrary"),
    )(z, z, z)


def _na_chunks(rows):
    half = NA_KR // 2
    plan, kinds = [], []
    for r0 in range(0, rows, NA_CHUNK_ROWS):
        rs = [min(max(r - half, 0), rows - NA_KR) for r in range(r0, r0 + NA_CHUNK_ROWS)]
        ws = min(rs[0], rows - NA_WIN_ROWS)
        assert rs[-1] + NA_KR <= ws + NA_WIN_ROWS
        kind = tuple((r0 + n - ws, rs[n] - ws) for n in range(NA_CHUNK_ROWS))
        if kind not in kinds:
            kinds.append(kind)
        plan.append((ws, kinds.index(kind)))
    return plan, kinds


def _na_body(q_ref, k_ref, v_ref, kc_ref, vc_ref, tb_ref, o_ref, bias_scr, *, rows):
    scale = NA_HEAD_DIM ** -0.5
    nq = NA_CHUNK_ROWS * GRID_W
    plan, kinds = _na_chunks(rows)
    n_off = 2 * NA_KR - 1

    @pl.when(pl.program_id(1) == 0)
    def _():
        for t, kind in enumerate(kinds):
            for e in range(2):
                for n, (r_rel, rs_rel) in enumerate(kind):
                    for kj in range(NA_WIN_ROWS):
                        off = kj - r_rel + NA_KR - 1 if rs_rel <= kj < rs_rel + NA_KR else n_off
                        lo = (kj % 2) * GRID_W
                        bias_scr[t, e * nq + n * GRID_W:e * nq + (n + 1) * GRID_W, kj * GRID_W:(kj + 1) * GRID_W] = \
                            tb_ref[e, off, :, lo:lo + GRID_W]

    kctx = kc_ref[...].astype(BF16)
    vctx = vc_ref[...].astype(BF16)
    first = _head_masks((nq, LANES))
    for c, (ws, kind) in enumerate(plan):
        qc = q_ref[c * nq:(c + 1) * nq, :]
        qs = jnp.concatenate([jnp.where(first, qc, jnp.zeros_like(qc)),
                              jnp.where(first, jnp.zeros_like(qc), qc)], 0)
        kw = k_ref[ws * GRID_W:(ws + NA_WIN_ROWS) * GRID_W, :]
        vw = v_ref[ws * GRID_W:(ws + NA_WIN_ROWS) * GRID_W, :]
        s_loc = _dot_nt(qs, kw) * scale + bias_scr[kind]
        s_ctx = _dot_nt(qs, kctx) * scale
        m = jnp.maximum(jnp.max(s_loc, -1, keepdims=True), jnp.max(s_ctx, -1, keepdims=True))
        p_loc = jnp.exp(s_loc - m)
        p_ctx = jnp.exp(s_ctx - m)
        l = jnp.sum(p_loc, -1, keepdims=True) + jnp.sum(p_ctx, -1, keepdims=True)
        o = (_dot(p_loc.astype(BF16), vw) + _dot(p_ctx.astype(BF16), vctx)) / l
        o_ref[c * nq:(c + 1) * nq, :] = jnp.where(first, o[:nq], o[nq:]).astype(BF16)


def _neighbourhood_attention(z, cache_k, cache_v, blocks, *, B, L, layer):
    nblk = MIX_W // LANES
    rows = L // GRID_W
    Lc = cache_k.shape[2]
    _, kinds = _na_chunks(rows)

    def sec(s):
        return pl.BlockSpec((None, L, LANES), lambda hp, b: (b, 0, s * nblk + hp))

    ctx = pl.BlockSpec((None, None, Lc, LANES), lambda hp, b: (b, layer, 0, hp))
    return pl.pallas_call(
        functools.partial(_na_body, rows=rows),
        grid=(nblk, B),
        in_specs=[sec(5), sec(6), sec(7), ctx, ctx,
                  pl.BlockSpec((None, 2, 2 * NA_KR, GRID_W, 2 * GRID_W), lambda hp, b: (hp, 0, 0, 0, 0))],
        out_specs=pl.BlockSpec((None, L, LANES), lambda hp, b: (b, 0, hp)),
        out_shape=jax.ShapeDtypeStruct((B, L, MIX_W), BF16),
        scratch_shapes=[pltpu.VMEM((len(kinds), 2 * NA_CHUNK_ROWS * GRID_W, NA_WIN_ROWS * GRID_W), F32)],
        name="nbr_attention",
        compiler_params=_params("arbitrary", "arbitrary"),
    )(z, z, z, cache_k, cache_v, blocks)


def _na_bias_blocks(rpb):
    nr, nc = 2 * NA_KR - 1, 2 * NA_KW - 1
    qc = np.arange(GRID_W)
    kc = np.arange(GRID_W)
    ws = np.clip(qc - NA_KW // 2, 0, GRID_W - NA_KW)
    col_ok = (kc[None, :] >= ws[:, None]) & (kc[None, :] < ws[:, None] + NA_KW)
    coff = np.clip(kc[None, :] - qc[:, None] + NA_KW - 1, 0, nc - 1)
    sel_c = ((coff[None] == np.arange(nc)[:, None, None]) & col_ok[None]).astype(np.float32)
    sel_c = np.concatenate([sel_c, sel_c], -1)
    ok = np.concatenate([col_ok, col_ok], -1)[None] & (np.arange(nr + 1) < nr)[:, None, None]
    H = rpb.shape[0]
    rows = jnp.pad(rpb.astype(F32), ((0, 0), (0, 1), (0, 0)))
    t = jnp.einsum('hrc,cqk->hrqk', rows, sel_c, precision=lax.Precision.HIGHEST)
    return jnp.where(ok[None], t, NEG_INF).reshape(H // 2, 2, nr + 1, GRID_W, 2 * GRID_W)


def _merge_body(x_ref, p_ref, r_ref, u_ref, y_ref, n_ref, ga_ref, gb_ref, gc_ref,
                d_ref, wglu_ref, wbr_ref, wo_ref, lg_ref, lb_ref, o_ref,
                wglu_s, wbr_s, wo_s, *, L, row0, rstride):
    i = pl.program_id(0)
    tm = x_ref.shape[0]

    @pl.when(i == 0)
    def _():
        wglu_s[...] = wglu_ref[...].astype(BF16)
        wbr_s[...] = wbr_ref[...].astype(BF16)
        wo_s[...] = wo_ref[...].astype(BF16)

    row = row0 + rstride * ((i * tm) // L)
    g1 = _mod_row(p_ref, row, 2)

    y = d_ref[...] * u_ref[...].astype(F32) + y_ref[...]
    y = jax.nn.gelu(y)
    s_out = y * jax.nn.sigmoid(_dot(y.astype(BF16), wglu_s[...]))

    def gate(ref):
        return jax.nn.sigmoid(ref[...].astype(F32))

    merged = (gate(ga_ref) * _dot(r_ref[...], wbr_s[0])
              + gate(gb_ref) * _dot(s_out.astype(BF16), wbr_s[1])
              + gate(gc_ref) * _dot(n_ref[...], wbr_s[2]))
    m = _dot(merged.astype(BF16), wo_s[...])
    o_ref[...] = _layer_norm(DEEPNORM_ALPHA * x_ref[...] + g1 * m, lg_ref[...], lb_ref[...])


def _merge(x, p, z, r_out, y, n_out, ssm_d, w_glu, w_branch, w_o, ln_g, ln_b, *, layer, L, row0, rstride):
    T = x.shape[0]
    tm = MERGE_TILE
    gate0 = 8 * MIX_W // D_MODEL

    def tok(w):
        return pl.BlockSpec((tm, w), lambda i: (i, 0))

    def full(shape):
        return pl.BlockSpec((None,) + shape, lambda i: (layer,) + (0,) * len(shape))

    body = functools.partial(_merge_body, L=L, row0=row0, rstride=rstride)
    return pl.pallas_call(
        body,
        grid=(T // tm,),
        in_specs=[tok(D_MODEL), full((N_PAD_ROWS, 6 * D_MODEL)), tok(MIX_W),
                  pl.BlockSpec((tm, MIX_W), lambda i: (i, SU_SECTION)), tok(MIX_W), tok(MIX_W),
                  pl.BlockSpec((tm, D_MODEL), lambda i: (i, gate0)),
                  pl.BlockSpec((tm, D_MODEL), lambda i: (i, gate0 + 1)),
                  pl.BlockSpec((tm, D_MODEL), lambda i: (i, gate0 + 2)),
                  full((1, MIX_W)), full((MIX_W, MIX_W)), full((3, MIX_W, D_MODEL)),
                  full((D_MODEL, D_MODEL)), full((1, D_MODEL)), full((1, D_MODEL))],
        out_specs=tok(D_MODEL),
        out_shape=jax.ShapeDtypeStruct((T, D_MODEL), F32),
        scratch_shapes=[pltpu.VMEM((MIX_W, MIX_W), BF16), pltpu.VMEM((3, MIX_W, D_MODEL), BF16),
                        pltpu.VMEM((D_MODEL, D_MODEL), BF16)],
        name="merge",
        compiler_params=_params("arbitrary"),
    )(x, p, r_out, z, y, n_out, z, z, z, ssm_d.reshape(DEPTH, 1, MIX_W), w_glu, w_branch, w_o,
      ln_g.reshape(DEPTH, 1, D_MODEL), ln_b.reshape(DEPTH, 1, D_MODEL))


def _ffn_body(x_ref, p_ref, wa_ref, wb_ref, cwa_ref, cwb_ref, cba_ref, cbb_ref, wd_ref, lg_ref, lb_ref,
              o_ref, h_scr, acc_scr, mp_scr, mn_scr, *, L, row0, rstride):
    i = pl.program_id(0)
    j = pl.program_id(1)
    tm = x_ref.shape[0]
    nb = tm // L

    @pl.when(j == 0)
    def _():
        for s in range(nb):
            row = row0 + rstride * (i * nb + s)
            sh = _mod_row(p_ref, row, 3)
            sc = _mod_row(p_ref, row, 4)
            h_scr[s * L:(s + 1) * L, :] = (x_ref[s * L:(s + 1) * L, :] * (1.0 + sc) + sh).astype(BF16)
        acc_scr[...] = jnp.zeros_like(acc_scr)
        t = lax.broadcasted_iota(jnp.int32, (tm, FF_TILE), 0) % L
        mp_scr[...] = (t != 0).astype(BF16)
        mn_scr[...] = (t != L - 1).astype(BF16)

    def conv(w_ref, cw_ref, cb_ref):
        zc = _dot(h_scr[...], w_ref[...].astype(BF16))
        zp = pltpu.roll(zc, 1, 0).astype(BF16) * mp_scr[...]
        zn = pltpu.roll(zc, tm - 1, 0).astype(BF16) * mn_scr[...]
        cw = cw_ref[...].astype(BF16)
        return zp * cw[0:1, :] + zc.astype(BF16) * cw[1:2, :] + zn * cw[2:3, :] + cb_ref[...].astype(BF16)

    a = conv(wa_ref, cwa_ref, cba_ref)
    b = conv(wb_ref, cwb_ref, cbb_ref)
    acc_scr[...] += _dot(jax.nn.gelu(a) * b, wd_ref[...].astype(BF16))

    @pl.when(j == pl.num_programs(1) - 1)
    def _():
        for s in range(nb):
            row = row0 + rstride * (i * nb + s)
            g2 = _mod_row(p_ref, row, 5)
            sl = slice(s * L, (s + 1) * L)
            o_ref[sl, :] = _layer_norm(DEEPNORM_ALPHA * x_ref[sl, :] + g2 * acc_scr[sl, :],
                                       lg_ref[...], lb_ref[...])


def _conv_ffn(x, p, w_up, conv_w, conv_b, w_down, ln_g, ln_b, *, layer, L, row0, rstride):
    T = x.shape[0]
    tm = TOKEN_TILE
    nff = D_FF // FF_TILE
    body = functools.partial(_ffn_body, L=L, row0=row0, rstride=rstride)
    conv_b = conv_b.reshape(DEPTH, 1, 2 * D_FF)
    return pl.pallas_call(
        body,
        grid=(T // tm, nff),
        in_specs=[pl.BlockSpec((tm, D_MODEL), lambda i, j: (i, 0)),
                  pl.BlockSpec((None, N_PAD_ROWS, 6 * D_MODEL), lambda i, j: (layer, 0, 0)),
                  pl.BlockSpec((None, D_MODEL, FF_TILE), lambda i, j: (layer, 0, j)),
                  pl.BlockSpec((None, D_MODEL, FF_TILE), lambda i, j: (layer, 0, nff + j)),
                  pl.BlockSpec((None, 3, FF_TILE), lambda i, j: (layer, 0, j)),
                  pl.BlockSpec((None, 3, FF_TILE), lambda i, j: (layer, 0, nff + j)),
                  pl.BlockSpec((None, 1, FF_TILE), lambda i, j: (layer, 0, j)),
                  pl.BlockSpec((None, 1, FF_TILE), lambda i, j: (layer, 0, nff + j)),
                  pl.BlockSpec((None, FF_TILE, D_MODEL), lambda i, j: (layer, j, 0)),
                  pl.BlockSpec((None, 1, D_MODEL), lambda i, j: (layer, 0, 0)),
                  pl.BlockSpec((None, 1, D_MODEL), lambda i, j: (layer, 0, 0))],
        out_specs=pl.BlockSpec((tm, D_MODEL), lambda i, j: (i, 0)),
        out_shape=jax.ShapeDtypeStruct((T, D_MODEL), F32),
        scratch_shapes=[pltpu.VMEM((tm, D_MODEL), BF16), pltpu.VMEM((tm, D_MODEL), F32),
                        pltpu.VMEM((tm, FF_TILE), BF16), pltpu.VMEM((tm, FF_TILE), BF16)],
        name="conv_ffn",
        compiler_params=_params("arbitrary", "arbitrary"),
    )(x, p, w_up, w_up, conv_w, conv_w, conv_b, conv_b, w_down, ln_g.reshape(DEPTH, 1, D_MODEL),
      ln_b.reshape(DEPTH, 1, D_MODEL))


def _s5_states_in(state_ssm, layer):
    B = state_ssm.shape[0]
    nlb = MIX_W // LANES
    h = state_ssm[:, layer].reshape(B, 2, nlb, S5_LBLK_GROUPS, SSM_STATE, 2)
    return jnp.transpose(h, (1, 2, 0, 5, 3, 4)).reshape(2, nlb, B, 2 * S5_LBLK_GROUPS * SSM_STATE)


def _s5_states_out(fin):
    nlb, B = fin.shape[1], fin.shape[2]
    h = fin.reshape(2, nlb, B, 2, S5_LBLK_GROUPS, SSM_STATE)
    return jnp.transpose(h, (2, 0, 1, 4, 5, 3)).reshape(B, 2, SSM_GROUPS, SSM_STATE, 2)


def _layer(x, p, lw, *, B, L, row0, rstride, latent, layer, extra):
    T = B * L
    z, *kv = _inproj(x, p, lw['w_in'], layer=layer, L=L, row0=row0, rstride=rstride, want_kv=not latent)
    z3 = z.reshape(B, L, IN_COLS)
    log_gamma = jax.nn.log_sigmoid(lw['ret_decay'].astype(F32))
    if latent:
        r_out = _retention(z3, log_gamma, B=B, L=L, rope_tabs=extra['rope'], s0=extra['state_ret'],
                           layer=layer, want_state=False)[0]
        n_out = _neighbourhood_attention(z3, extra['cache_k'], extra['cache_v'], extra['bias'][layer],
                                         B=B, L=L, layer=layer)
        y, _ = _s5(z, lw['s5'], _s5_states_in(extra['state_ssm'], layer), B=B, L=L)
        states = None
    else:
        r_out, ret_state = _retention(z3, log_gamma, B=B, L=L, want_state=True)
        n_out = _context_attention(z3, B=B, L=L)
        y, fin = _s5(z, lw['s5'], None, B=B, L=L)
        states = (ret_state, _s5_states_out(fin), kv[0].reshape(B, L, MIX_W), kv[1].reshape(B, L, MIX_W))
    x = _merge(x, p, z, r_out.reshape(T, MIX_W), y, n_out.reshape(T, MIX_W),
               lw['ssm_d'], lw['ssm_w_glu'], lw['w_branch'], lw['w_o'], lw['ln1_g'], lw['ln1_b'],
               layer=layer, L=L, row0=row0, rstride=rstride)
    x = _conv_ffn(x, p, lw['w_up'], lw['conv_w'], lw['conv_b'], lw['w_down'], lw['ln2_g'], lw['ln2_b'],
                  layer=layer, L=L, row0=row0, rstride=rstride)
    return x, states


def kernel(x_prompt, x_sample, state_ret, state_ssm, cache_na_k, cache_na_v, c, c_ctx, w_ada, b_ada, w_in, ret_decay, ssm_a_re, ssm_a_im, ssm_log_dt, ssm_b_re, ssm_b_im, ssm_c_re, ssm_c_im, ssm_d, ssm_w_glu, na_rpb, w_branch, w_o, ln1_g, ln1_b, w_up, conv_w, conv_b, w_down, ln2_g, ln2_b):
    B, L, _ = x_prompt.shape
    Bd, Ld, _ = x_sample.shape
    Lc = cache_na_k.shape[2]

    cond = jnp.concatenate([c_ctx[None, :], c, jnp.zeros((N_PAD_ROWS - 1 - Bd, D_MODEL), F32)], 0)
    p_all = _ada(cond, w_ada, b_ada)

    extra = dict(rope=_rope_tables(Ld), state_ret=state_ret, state_ssm=state_ssm,
                 cache_k=cache_na_k.reshape(Bd, DEPTH, Lc, MIX_W),
                 cache_v=cache_na_v.reshape(Bd, DEPTH, Lc, MIX_W),
                 bias=[_na_bias_blocks(na_rpb[l]) for l in range(DEPTH)])

    xp = x_prompt.reshape(B * L, D_MODEL)
    xs = x_sample.reshape(Bd * Ld, D_MODEL)
    ret_states, ssm_states, na_ks, na_vs = [], [], [], []
    for l in range(DEPTH):
        lw = dict(w_in=w_in, ret_decay=ret_decay[l], ssm_d=ssm_d, ssm_w_glu=ssm_w_glu,
                  w_branch=w_branch, w_o=w_o, ln1_g=ln1_g, ln1_b=ln1_b, w_up=w_up,
                  conv_w=conv_w, conv_b=conv_b, w_down=w_down, ln2_g=ln2_g, ln2_b=ln2_b,
                  s5=_s5_operators(ssm_a_re[l], ssm_a_im[l], ssm_log_dt[l], ssm_b_re[l], ssm_b_im[l],
                                   ssm_c_re[l], ssm_c_im[l]))
        xp, (s_ret, s_ssm, nk, nv) = _layer(xp, p_all, lw, B=B, L=L, row0=0, rstride=0,
                                            latent=False, layer=l, extra=None)
        ret_states.append(s_ret)
        ssm_states.append(s_ssm)
        na_ks.append(nk)
        na_vs.append(nv)
        xs, _ = _layer(xs, p_all, lw, B=Bd, L=Ld, row0=1, rstride=1, latent=True, layer=l, extra=extra)
    def heads(xs_):
        return jnp.stack(xs_, 1).reshape(B, DEPTH, L, NA_HEADS, NA_HEAD_DIM)

    return (xp.reshape(B, L, D_MODEL), xs.reshape(Bd, Ld, D_MODEL),
            jnp.stack(ret_states, 1), jnp.stack(ssm_states, 1), heads(na_ks), heads(na_vs))
```

```python
import functools

import jax
import jax.numpy as jnp
import numpy as np
from jax import lax
from jax.experimental import pallas as pl
from jax.experimental.pallas import tpu as pltpu

F32 = jnp.float32
BF16 = jnp.bfloat16

D_MODEL = 1024
DEPTH = 2
GRID_W = 64
MIX_W = D_MODEL // 2
N_RET_HEADS = 4
RET_DK = MIX_W // N_RET_HEADS
SSM_GROUP = 16
SSM_GROUPS = MIX_W // SSM_GROUP
SSM_STATE = 64
NA_HEADS = 8
NA_HEAD_DIM = MIX_W // NA_HEADS
NA_KR = 8
NA_KW = 16
D_FF = ((8 * D_MODEL // 3 + 127) // 128) * 128
ROPE_BASE = 10000.0
LN_EPS = 1e-5
NEG_INF = -1e30
DEEPNORM_ALPHA = (2 * DEPTH) ** 0.25
IN_COLS = 8 * MIX_W + 3 * D_MODEL

VMEM_LIMIT_BYTES = 56 * 1024 * 1024
LANES = 128

TOKEN_TILE = 1024
MERGE_TILE = 512
COL_TILE = 1024
SU_SECTION = 4
NK_SECTION = 6
FF_TILE = 256
RET_CHUNK = 256
RET_ROWS = 1024
S5_CHUNK = 8
S5_PITCH_PAD = 8
S5_LBLK_GROUPS = LANES // SSM_GROUP
N_PAD_ROWS = 8
CTX_ATTN_BATCH = 4
NA_CHUNK_ROWS = 4
NA_WIN_ROWS = 12


def _params(*sem):
    return pltpu.CompilerParams(dimension_semantics=sem, vmem_limit_bytes=VMEM_LIMIT_BYTES)


def _dot(a, b):
    return jnp.dot(a, b, preferred_element_type=F32)


def _dot_nt(a, b):
    return lax.dot_general(a, b, (((1,), (1,)), ((), ())), preferred_element_type=F32)


def _layer_norm(x, g, b):
    mu = jnp.mean(x, -1, keepdims=True)
    xc = x - mu
    var = jnp.mean(xc * xc, -1, keepdims=True)
    return xc * lax.rsqrt(var + LN_EPS) * g + b


def _ada_body(c_ref, w_ref, b_ref, o_ref):
    c = c_ref[...]
    s = c * jax.nn.sigmoid(c)
    o_ref[...] = _dot(s.astype(BF16), w_ref[...].astype(BF16)) + b_ref[...]


def _ada(cond, w_ada, b_ada):
    tn = 1024
    return pl.pallas_call(
        _ada_body,
        grid=(DEPTH, 6 * D_MODEL // tn),
        in_specs=[pl.BlockSpec((N_PAD_ROWS, D_MODEL), lambda l, j: (0, 0)),
                  pl.BlockSpec((None, D_MODEL, tn), lambda l, j: (l, 0, j)),
                  pl.BlockSpec((None, 1, tn), lambda l, j: (l, 0, j))],
        out_specs=pl.BlockSpec((None, N_PAD_ROWS, tn), lambda l, j: (l, 0, j)),
        out_shape=jax.ShapeDtypeStruct((DEPTH, N_PAD_ROWS, 6 * D_MODEL), F32),
        name="ada",
        compiler_params=_params("arbitrary", "arbitrary"),
    )(cond, w_ada, b_ada.reshape(DEPTH, 1, 6 * D_MODEL))


def _mod_row(p_ref, row, k):
    return p_ref[pl.ds(row, 1), k * D_MODEL:(k + 1) * D_MODEL]


def _kv_tile(n):
    col = (NK_SECTION + n) * MIX_W
    return col // COL_TILE, col % COL_TILE


def _inproj_body(x_ref, p_ref, w_ref, z_ref, *rest, L, row0, rstride):
    h_scr, w_scr = rest[-2:]
    kv_refs = rest[:-2]
    j = pl.program_id(0)
    i = pl.program_id(1)
    nb = x_ref.shape[0] // L

    @pl.when(j == 0)
    def _():
        for s in range(nb):
            row = row0 + rstride * (i * nb + s)
            sh = _mod_row(p_ref, row, 0)
            sc = _mod_row(p_ref, row, 1)
            h_scr[i, s * L:(s + 1) * L, :] = (x_ref[s * L:(s + 1) * L, :] * (1.0 + sc) + sh).astype(BF16)

    @pl.when(i == 0)
    def _():
        w_scr[...] = w_ref[...].astype(BF16)

    acc = _dot(h_scr[i], w_scr[...])
    z_ref[...] = acc.astype(BF16)

    for n, ref in enumerate(kv_refs):
        tile, off = _kv_tile(n)

        @pl.when(j == tile)
        def _(ref=ref, off=off):
            ref[...] = acc[:, off:off + MIX_W]


def _inproj(x, p, w_in, *, layer, L, row0, rstride, want_kv):
    T = x.shape[0]
    tm = TOKEN_TILE
    n_i = T // tm
    body = functools.partial(_inproj_body, L=L, row0=row0, rstride=rstride)
    n_kv = 2 if want_kv else 0

    def only_at(tile):
        return lambda j, i: jnp.where(j < tile, 0, jnp.where(j > tile, n_i - 1, i))

    kv_i = [only_at(_kv_tile(n)[0]) for n in range(n_kv)]
    return pl.pallas_call(
        body,
        grid=(IN_COLS // COL_TILE, n_i),
        in_specs=[pl.BlockSpec((tm, D_MODEL), lambda j, i: (jnp.where(j == 0, i, n_i - 1), 0)),
                  pl.BlockSpec((None, N_PAD_ROWS, 6 * D_MODEL), lambda j, i: (layer, 0, 0)),
                  pl.BlockSpec((None, D_MODEL, COL_TILE), lambda j, i: (layer, 0, j))],
        out_specs=[pl.BlockSpec((tm, COL_TILE), lambda j, i: (i, j))]
        + [pl.BlockSpec((tm, MIX_W), lambda j, i, f=f: (f(j, i), 0)) for f in kv_i],
        out_shape=[jax.ShapeDtypeStruct((T, IN_COLS), BF16)] + [jax.ShapeDtypeStruct((T, MIX_W), F32)] * n_kv,
        scratch_shapes=[pltpu.VMEM((n_i, tm, D_MODEL), BF16), pltpu.VMEM((D_MODEL, COL_TILE), BF16)],
        name="inproj",
        compiler_params=_params("arbitrary", "arbitrary"),
    )(x, p, w_in)


def _rope(x, cos, s_up, s_dn):
    return x * cos + pltpu.roll(x, 96, 1) * s_up + pltpu.roll(x, 32, 1) * s_dn


def _ret_body(*refs, n, rope, has_s0, want_state):
    refs = list(refs)
    lg_ref, q_ref, k_ref, v_ref, g_ref = refs[:5]
    refs = refs[5:]
    if rope:
        cos_ref, sup_ref, sdn_ref = refs[:3]
        refs = refs[3:]
    if has_s0:
        s0_ref = refs[0]
        refs = refs[1:]
    o_ref = refs[0]
    refs = refs[1:]
    if want_state:
        st_ref = refs[0]
        refs = refs[1:]
    q_scr, k_scr, sb_scr, decay_scr = refs

    C = RET_CHUNK
    h = pl.program_id(0)
    lf = lg_ref[0, h]
    lb = lg_ref[1, h]

    @pl.when(pl.program_id(1) == 0)
    def _():
        ti = lax.broadcasted_iota(jnp.int32, (C, C), 0)
        si = lax.broadcasted_iota(jnp.int32, (C, C), 1)
        dlt = (ti - si).astype(F32)
        decay_scr[...] = (jnp.where(dlt >= 0, jnp.exp(lf * jnp.maximum(dlt, 0.0)), 0.0)
                          + jnp.where(dlt <= 0, jnp.exp(lb * jnp.maximum(-dlt, 0.0)), 0.0))

    tcol = lax.broadcasted_iota(jnp.int32, (C, 1), 0).astype(F32)
    qd_f = jnp.exp(lf * (tcol + 1.0))
    qd_b = jnp.exp(lb * (C - tcol))
    kd_f = jnp.exp(lf * (C - 1.0 - tcol))
    kd_b = jnp.exp(lb * tcol)
    cd_f = jnp.exp(lf * jnp.full((1, RET_DK), float(C), F32))
    cd_b = jnp.exp(lb * jnp.full((1, RET_DK), float(C), F32))

    def kv_outer(kc, vc, kd):
        return _dot((kc * kd).T.astype(BF16), vc)

    for bb in range(q_ref.shape[0]):
        q = q_ref[bb].astype(F32)
        k = k_ref[bb].astype(F32)
        if rope:
            q = _rope(q, cos_ref[...], sup_ref[...], sdn_ref[...])
            k = _rope(k, cos_ref[...], sup_ref[...], sdn_ref[...])
        q_scr[bb] = q
        k_scr[bb] = k * (RET_DK ** -0.5)

        s_b = s0_ref[bb, 1] if has_s0 else jnp.zeros((RET_DK, RET_DK), F32)
        for i in reversed(range(n)):
            sb_scr[bb, i] = s_b
            if i > 0 or want_state:
                s_b = s_b * cd_b + kv_outer(k_scr[bb, i * C:(i + 1) * C, :], v_ref[bb, i * C:(i + 1) * C, :], kd_b)

        s_f = s0_ref[bb, 0] if has_s0 else jnp.zeros((RET_DK, RET_DK), F32)
        for i in range(n):
            sl = slice(i * C, (i + 1) * C)
            qc = q_scr[bb, sl, :]
            kc = k_scr[bb, sl, :]
            vc = v_ref[bb, sl, :]
            att = _dot_nt(qc.astype(BF16), kc.astype(BF16)) * decay_scr[...]
            o = _dot(att.astype(BF16), vc)
            o = o + _dot((qc * qd_f).astype(BF16), s_f.astype(BF16))
            o = o + _dot((qc * qd_b).astype(BF16), sb_scr[bb, i].astype(BF16))
            mu = jnp.mean(o, -1, keepdims=True)
            oc = o - mu
            var = jnp.mean(oc * oc, -1, keepdims=True)
            gc = g_ref[bb, sl, :].astype(F32)
            o_ref[bb, sl, :] = (oc * lax.rsqrt(var + LN_EPS) * (gc * jax.nn.sigmoid(gc))).astype(BF16)
            if i < n - 1 or want_state:
                s_f = s_f * cd_f + kv_outer(kc, vc, kd_f)

        if want_state:
            st_ref[bb, 0] = s_f
            st_ref[bb, 1] = s_b


def _retention(z, log_gamma, *, B, L, rope_tabs=None, s0=None, layer=0, want_state):
    n = L // RET_CHUNK
    H = N_RET_HEADS
    nblk = MIX_W // RET_DK

    nbb = max(1, RET_ROWS // L)
    assert B % nbb == 0

    def sec(s):
        return pl.BlockSpec((nbb, L, RET_DK), lambda h, b: (b, 0, s * nblk + h))

    in_specs = [pl.BlockSpec(memory_space=pltpu.SMEM), sec(0), sec(1), sec(2), sec(3)]
    args = [log_gamma, z, z, z, z]
    if rope_tabs is not None:
        in_specs += [pl.BlockSpec((L, RET_DK), lambda h, b: (0, 0))] * 3
        args += list(rope_tabs)
    if s0 is not None:
        in_specs.append(pl.BlockSpec((nbb, None, 2, None, RET_DK, RET_DK), lambda h, b: (b, layer, 0, h, 0, 0)))
        args.append(s0)
    out_specs = [pl.BlockSpec((nbb, L, RET_DK), lambda h, b: (b, 0, h))]
    out_shape = [jax.ShapeDtypeStruct((B, L, MIX_W), BF16)]
    if want_state:
        out_specs.append(pl.BlockSpec((nbb, 2, None, RET_DK, RET_DK), lambda h, b: (b, 0, h, 0, 0)))
        out_shape.append(jax.ShapeDtypeStruct((B, 2, H, RET_DK, RET_DK), F32))
    body = functools.partial(_ret_body, n=n, rope=rope_tabs is not None, has_s0=s0 is not None,
                             want_state=want_state)
    return pl.pallas_call(
        body,
        grid=(H, B // nbb),
        in_specs=in_specs,
        out_specs=out_specs,
        out_shape=out_shape,
        scratch_shapes=[pltpu.VMEM((nbb, L, RET_DK), F32), pltpu.VMEM((nbb, L, RET_DK), F32),
                        pltpu.VMEM((nbb, n, RET_DK, RET_DK), F32), pltpu.VMEM((RET_CHUNK, RET_CHUNK), F32)],
        name="retention",
        compiler_params=_params("arbitrary", "arbitrary"),
    )(*args)


def _rope_tables(L):
    pos = jnp.arange(L)
    row = (pos // GRID_W).astype(F32)
    col = (pos % GRID_W).astype(F32)
    quarter = RET_DK // 4
    inv_freq = ROPE_BASE ** (-jnp.arange(quarter, dtype=F32) / quarter)
    ang_r = row[:, None] * inv_freq[None, :]
    ang_c = col[:, None] * inv_freq[None, :]
    zero = jnp.zeros_like(ang_r)
    cos = jnp.concatenate([jnp.cos(ang_r), jnp.cos(ang_r), jnp.cos(ang_c), jnp.cos(ang_c)], -1)
    s_up = jnp.concatenate([-jnp.sin(ang_r), zero, -jnp.sin(ang_c), zero], -1)
    s_dn = jnp.concatenate([zero, jnp.sin(ang_r), zero, jnp.sin(ang_c)], -1)
    return cos, s_up, s_dn


def _s5_body(*refs, B, nC, has_h0):
    refs = list(refs)
    u_ref, c0_ref, bs_ref, cp_ref, a8_ref = refs[:5]
    refs = refs[5:]
    if has_h0:
        h0_ref = refs[0]
        refs = refs[1:]
    y_ref, fin_ref, u_scr, a_scr, m_scr, bs_scr, cp_scr, c0_scr, s_scr, x_scr, y_scr = refs

    TC = S5_CHUNK
    R = B * nC
    P = nC + S5_PITCH_PAD
    nsl = s_scr.shape[0]
    half = nsl // 2
    ng = S5_LBLK_GROUPS
    sw = ng * SSM_STATE

    u_scr[...] = u_ref[...].astype(F32)
    for s in range(TC):
        a_scr[:, s * LANES:(s + 1) * LANES] = u_scr[pl.ds(s, R, stride=TC), :].astype(BF16)

    bs_scr[...] = jnp.zeros_like(bs_scr)
    cp_scr[...] = jnp.zeros_like(cp_scr)
    c0_scr[...] = jnp.zeros_like(c0_scr)

    for d in range(2):
        for g in range(ng):
            for part in range(2):
                cols = slice(part * sw + g * SSM_STATE, part * sw + (g + 1) * SSM_STATE)
                lo = (g % 2) * SSM_STATE
                c0_scr[g * SSM_GROUP:(g + 1) * SSM_GROUP, cols] = c0_ref[d, g, part, :, lo:lo + SSM_STATE]
                for s in range(TC):
                    rows = slice(s * LANES + g * SSM_GROUP, s * LANES + (g + 1) * SSM_GROUP)
                    bs_scr[rows, cols] = bs_ref[d, s, g, part, :, lo:lo + SSM_STATE]
                    cp_scr[rows, cols] = cp_ref[d, s, g, part, :, lo:lo + SSM_STATE]

        lag = _dot_nt(bs_scr[...], c0_scr[...]).astype(BF16)
        for s in range(TC):
            for t in range(TC):
                k = (t - s) if d == 0 else (s - t)
                src = (TC - 1 - k) if d == 0 else k
                blk = lag[src * LANES:(src + 1) * LANES, :] if k >= 0 else jnp.zeros((LANES, LANES), BF16)
                m_scr[s * LANES:(s + 1) * LANES, t * LANES:(t + 1) * LANES] = blk

        a = a_scr[...]
        yd = _dot(a, m_scr[...])
        if d == 0:
            y_scr[...] = yd
        else:
            y_scr[...] += yd

        sm = _dot(a, bs_scr[...])
        for b in range(B):
            for sl in range(nsl):
                s_scr[sl, b * P:b * P + nC, :] = sm[b * nC:(b + 1) * nC, sl * LANES:(sl + 1) * LANES]

        a_r = [jnp.broadcast_to(a8_ref[d, 0, :, q * LANES:(q + 1) * LANES], (B, LANES)) for q in range(half)]
        a_i = [jnp.broadcast_to(a8_ref[d, 1, :, q * LANES:(q + 1) * LANES], (B, LANES)) for q in range(half)]
        if has_h0:
            init = tuple(h0_ref[d, :, sl * LANES:(sl + 1) * LANES] for sl in range(nsl))
        else:
            init = tuple(jnp.zeros((B, LANES), F32) for _ in range(nsl))

        def step(j, carry, d=d, a_r=a_r, a_i=a_i):
            c = j if d == 0 else nC - 1 - j
            rows = pl.ds(c, B, stride=P)
            new_r, new_i = [], []
            for q in range(half):
                xr, xi = carry[q], carry[half + q]
                sr = s_scr[q, rows, :]
                si = s_scr[half + q, rows, :]
                s_scr[q, rows, :] = xr
                s_scr[half + q, rows, :] = xi
                new_r.append(a_r[q] * xr - a_i[q] * xi + sr)
                new_i.append(a_r[q] * xi + a_i[q] * xr + si)
            return tuple(new_r + new_i)

        fin = lax.fori_loop(0, nC, step, init)
        for sl in range(nsl):
            fin_ref[d, :, sl * LANES:(sl + 1) * LANES] = fin[sl]

        for b in range(B):
            for sl in range(nsl):
                x_scr[b * nC:(b + 1) * nC, sl * LANES:(sl + 1) * LANES] = \
                    s_scr[sl, b * P:b * P + nC, :].astype(BF16)
        y_scr[...] += _dot_nt(x_scr[...], cp_scr[...])

    for t in range(TC):
        y_ref[pl.ds(t, R, stride=TC), :] = y_scr[:, t * LANES:(t + 1) * LANES]


def _s5(z, ops, h0, *, B, L):
    c0, bs, cp, a8 = ops
    T = B * L
    nC = L // S5_CHUNK
    nlb = MIX_W // LANES
    sc = 2 * S5_LBLK_GROUPS * SSM_STATE
    kc = S5_CHUNK * LANES
    su0 = SU_SECTION * MIX_W // LANES

    blocks = pl.BlockSpec((2, None, S5_CHUNK, S5_LBLK_GROUPS, 2, SSM_GROUP, LANES),
                          lambda lb: (0, lb, 0, 0, 0, 0, 0))
    in_specs = [pl.BlockSpec((T, LANES), lambda lb: (0, su0 + lb)),
                pl.BlockSpec((2, None, S5_LBLK_GROUPS, 2, SSM_GROUP, LANES), lambda lb: (0, lb, 0, 0, 0, 0)),
                blocks, blocks,
                pl.BlockSpec((2, None, 2, 1, sc // 2), lambda lb: (0, lb, 0, 0, 0))]
    args = [z, c0, bs, cp, a8]
    if h0 is not None:
        in_specs.append(pl.BlockSpec((2, None, B, sc), lambda lb: (0, lb, 0, 0)))
        args.append(h0)
    body = functools.partial(_s5_body, B=B, nC=nC, has_h0=h0 is not None)
    return pl.pallas_call(
        body,
        grid=(nlb,),
        in_specs=in_specs,
        out_specs=[pl.BlockSpec((T, LANES), lambda lb: (0, lb)),
                   pl.BlockSpec((2, None, B, sc), lambda lb: (0, lb, 0, 0))],
        out_shape=[jax.ShapeDtypeStruct((T, MIX_W), F32),
                   jax.ShapeDtypeStruct((2, nlb, B, sc), F32)],
        scratch_shapes=[pltpu.VMEM((T, LANES), F32), pltpu.VMEM((B * nC, kc), BF16),
                        pltpu.VMEM((kc, kc), BF16), pltpu.VMEM((kc, sc), BF16), pltpu.VMEM((kc, sc), BF16),
                        pltpu.VMEM((LANES, sc), BF16),
                        pltpu.VMEM((sc // LANES, B * (nC + S5_PITCH_PAD), LANES), F32),
                        pltpu.VMEM((B * nC, sc), BF16), pltpu.VMEM((B * nC, kc), F32)],
        name="s5",
        compiler_params=_params("arbitrary"),
    )(*args)


def _s5_operators(a_re, a_im, log_dt, b_re, b_im, c_re, c_im):
    TC = S5_CHUNK
    nlb = MIX_W // LANES
    ng = S5_LBLK_GROUPS
    lr = jnp.minimum(a_re, -1e-4)
    li = a_im
    dt = jnp.exp(log_dt)[..., None]
    k = jnp.arange(TC + 1, dtype=F32)[:, None, None, None]
    mag = jnp.exp(k * (lr * dt)[None])
    pr = mag * jnp.cos(k * (li * dt)[None])
    pi = mag * jnp.sin(k * (li * dt)[None])
    ar, ai = pr[1], pi[1]
    den = lr * lr + li * li
    sr = ((ar - 1.0) * lr + ai * li) / den
    si = (ai * lr - (ar - 1.0) * li) / den
    bbr = sr[..., None] * b_re[None] - si[..., None] * b_im[None]
    bbi = sr[..., None] * b_im[None] + si[..., None] * b_re[None]

    def lanes2(x):
        return jnp.concatenate([x, x], -1)

    def powers(fwd, bwd):
        x = lanes2(jnp.stack([fwd, bwd], 0)).reshape(2, TC, nlb, ng, 1, LANES)
        return jnp.swapaxes(x, 1, 2)

    def per_group(x):
        return lanes2(x).reshape(2, nlb, 1, ng, SSM_GROUP, LANES)

    er = powers(jnp.flip(pr[:TC, 0], 0), pr[:TC, 1])
    ei = powers(jnp.flip(pi[:TC, 0], 0), pi[:TC, 1])
    btr = per_group(jnp.swapaxes(bbr, -1, -2))
    bti = per_group(jnp.swapaxes(bbi, -1, -2))
    bs = jnp.stack([er * btr - ei * bti, er * bti + ei * btr], 4).astype(BF16)

    fr = powers(pr[1:, 0], jnp.flip(pr[1:, 1], 0))
    fi = powers(pi[1:, 0], jnp.flip(pi[1:, 1], 0))
    ctr = per_group(c_re)
    cti = per_group(c_im)
    cp = jnp.stack([ctr * fr - cti * fi, -(ctr * fi + cti * fr)], 4).astype(BF16)
    c0 = jnp.stack([ctr, -cti], 4)[:, :, 0].astype(BF16)

    sw = ng * SSM_STATE
    a8 = jnp.stack([pr[TC].reshape(2, nlb, 1, sw), pi[TC].reshape(2, nlb, 1, sw)], 2)
    return c0, bs, cp, a8


def _head_masks(shape):
    lane = lax.broadcasted_iota(jnp.int32, shape, 1)
    return lane < NA_HEAD_DIM


def _cattn_body(q_ref, k_ref, v_ref, o_ref):
    first = _head_masks(q_ref.shape[1:])
    for bb in range(q_ref.shape[0]):
        q = q_ref[bb]
        k = k_ref[bb]
        v = v_ref[bb]
        outs = []
        for e in range(2):
            qe = jnp.where(first if e == 0 else jnp.logical_not(first), q, jnp.zeros_like(q))
            s = _dot_nt(qe, k) * (NA_HEAD_DIM ** -0.5)
            m = jnp.max(s, -1, keepdims=True)
            p = jnp.exp(s - m)
            l = jnp.sum(p, -1, keepdims=True)
            outs.append(_dot(p.astype(BF16), v) / l)
        o_ref[bb] = jnp.where(first, outs[0], outs[1]).astype(BF16)


def _context_attention(z, *, B, L):
    nblk = MIX_W // LANES
    nb = CTX_ATTN_BATCH

    def sec(s):
        return pl.BlockSpec((nb, L, LANES), lambda b, hp: (b, 0, s * nblk + hp))

    return pl.pallas_call(
        _cattn_body,
        grid=(B // nb, nblk),
        in_specs=[sec(5), sec(6), sec(7)],
        out_specs=pl.BlockSpec((nb, L, LANES), lambda b, hp: (b, 0, hp)),
        out_shape=jax.ShapeDtypeStruct((B, L, MIX_W), BF16),
        name="ctx_attention",
        compiler_params=_params("arbitrary", "arbitrary"),
    )(z, z, z)


def _na_chunks(rows):
    half = NA_KR // 2
    plan, kinds = [], []
    for r0 in range(0, rows, NA_CHUNK_ROWS):
        rs = [min(max(r - half, 0), rows - NA_KR) for r in range(r0, r0 + NA_CHUNK_ROWS)]
        ws = min(rs[0], rows - NA_WIN_ROWS)
        assert rs[-1] + NA_KR <= ws + NA_WIN_ROWS
        kind = tuple((r0 + n - ws, rs[n] - ws) for n in range(NA_CHUNK_ROWS))
        if kind not in kinds:
            kinds.append(kind)
        plan.append((ws, kinds.index(kind)))
    return plan, kinds


def _na_body(q_ref, k_ref, v_ref, kc_ref, vc_ref, tb_ref, o_ref, bias_scr, *, rows):
    scale = NA_HEAD_DIM ** -0.5
    nq = NA_CHUNK_ROWS * GRID_W
    plan, kinds = _na_chunks(rows)
    n_off = 2 * NA_KR - 1

    @pl.when(pl.program_id(1) == 0)
    def _():
        for t, kind in enumerate(kinds):
            for e in range(2):
                for n, (r_rel, rs_rel) in enumerate(kind):
                    for kj in range(NA_WIN_ROWS):
                        off = kj - r_rel + NA_KR - 1 if rs_rel <= kj < rs_rel + NA_KR else n_off
                        lo = (kj % 2) * GRID_W
                        bias_scr[t, e * nq + n * GRID_W:e * nq + (n + 1) * GRID_W, kj * GRID_W:(kj + 1) * GRID_W] = \
                            tb_ref[e, off, :, lo:lo + GRID_W]

    kctx = kc_ref[...].astype(BF16)
    vctx = vc_ref[...].astype(BF16)
    first = _head_masks((nq, LANES))
    for c, (ws, kind) in enumerate(plan):
        qc = q_ref[c * nq:(c + 1) * nq, :]
        qs = jnp.concatenate([jnp.where(first, qc, jnp.zeros_like(qc)),
                              jnp.where(first, jnp.zeros_like(qc), qc)], 0)
        kw = k_ref[ws * GRID_W:(ws + NA_WIN_ROWS) * GRID_W, :]
        vw = v_ref[ws * GRID_W:(ws + NA_WIN_ROWS) * GRID_W, :]
        s_loc = _dot_nt(qs, kw) * scale + bias_scr[kind]
        s_ctx = _dot_nt(qs, kctx) * scale
        m = jnp.maximum(jnp.max(s_loc, -1, keepdims=True), jnp.max(s_ctx, -1, keepdims=True))
        p_loc = jnp.exp(s_loc - m)
        p_ctx = jnp.exp(s_ctx - m)
        l = jnp.sum(p_loc, -1, keepdims=True) + jnp.sum(p_ctx, -1, keepdims=True)
        o = (_dot(p_loc.astype(BF16), vw) + _dot(p_ctx.astype(BF16), vctx)) / l
        o_ref[c * nq:(c + 1) * nq, :] = jnp.where(first, o[:nq], o[nq:]).astype(BF16)


def _neighbourhood_attention(z, cache_k, cache_v, blocks, *, B, L, layer):
    nblk = MIX_W // LANES
    rows = L // GRID_W
    Lc = cache_k.shape[2]
    _, kinds = _na_chunks(rows)

    def sec(s):
        return pl.BlockSpec((None, L, LANES), lambda hp, b: (b, 0, s * nblk + hp))

    ctx = pl.BlockSpec((None, None, Lc, LANES), lambda hp, b: (b, layer, 0, hp))
    return pl.pallas_call(
        functools.partial(_na_body, rows=rows),
        grid=(nblk, B),
        in_specs=[sec(5), sec(6), sec(7), ctx, ctx,
                  pl.BlockSpec((None, 2, 2 * NA_KR, GRID_W, 2 * GRID_W), lambda hp, b: (hp, 0, 0, 0, 0))],
        out_specs=pl.BlockSpec((None, L, LANES), lambda hp, b: (b, 0, hp)),
        out_shape=jax.ShapeDtypeStruct((B, L, MIX_W), BF16),
        scratch_shapes=[pltpu.VMEM((len(kinds), 2 * NA_CHUNK_ROWS * GRID_W, NA_WIN_ROWS * GRID_W), F32)],
        name="nbr_attention",
        compiler_params=_params("arbitrary", "arbitrary"),
    )(z, z, z, cache_k, cache_v, blocks)


def _na_bias_blocks(rpb):
    nr, nc = 2 * NA_KR - 1, 2 * NA_KW - 1
    qc = np.arange(GRID_W)
    kc = np.arange(GRID_W)
    ws = np.clip(qc - NA_KW // 2, 0, GRID_W - NA_KW)
    col_ok = (kc[None, :] >= ws[:, None]) & (kc[None, :] < ws[:, None] + NA_KW)
    coff = np.clip(kc[None, :] - qc[:, None] + NA_KW - 1, 0, nc - 1)
    sel_c = ((coff[None] == np.arange(nc)[:, None, None]) & col_ok[None]).astype(np.float32)
    sel_c = np.concatenate([sel_c, sel_c], -1)
    ok = np.concatenate([col_ok, col_ok], -1)[None] & (np.arange(nr + 1) < nr)[:, None, None]
    H = rpb.shape[0]
    rows = jnp.pad(rpb.astype(F32), ((0, 0), (0, 1), (0, 0)))
    t = jnp.einsum('hrc,cqk->hrqk', rows, sel_c, precision=lax.Precision.HIGHEST)
    return jnp.where(ok[None], t, NEG_INF).reshape(H // 2, 2, nr + 1, GRID_W, 2 * GRID_W)


def _merge_body(x_ref, p_ref, r_ref, u_ref, y_ref, n_ref, ga_ref, gb_ref, gc_ref,
                d_ref, wglu_ref, wbr_ref, wo_ref, lg_ref, lb_ref, o_ref,
                wglu_s, wbr_s, wo_s, *, L, row0, rstride):
    i = pl.program_id(0)
    tm = x_ref.shape[0]

    @pl.when(i == 0)
    def _():
        wglu_s[...] = wglu_ref[...].astype(BF16)
        wbr_s[...] = wbr_ref[...].astype(BF16)
        wo_s[...] = wo_ref[...].astype(BF16)

    row = row0 + rstride * ((i * tm) // L)
    g1 = _mod_row(p_ref, row, 2)

    y = d_ref[...] * u_ref[...].astype(F32) + y_ref[...]
    y = jax.nn.gelu(y)
    s_out = y * jax.nn.sigmoid(_dot(y.astype(BF16), wglu_s[...]))

    def gate(ref):
        return jax.nn.sigmoid(ref[...].astype(F32))

    merged = (gate(ga_ref) * _dot(r_ref[...], wbr_s[0])
              + gate(gb_ref) * _dot(s_out.astype(BF16), wbr_s[1])
              + gate(gc_ref) * _dot(n_ref[...], wbr_s[2]))
    m = _dot(merged.astype(BF16), wo_s[...])
    o_ref[...] = _layer_norm(DEEPNORM_ALPHA * x_ref[...] + g1 * m, lg_ref[...], lb_ref[...])


def _merge(x, p, z, r_out, y, n_out, ssm_d, w_glu, w_branch, w_o, ln_g, ln_b, *, layer, L, row0, rstride):
    T = x.shape[0]
    tm = MERGE_TILE
    assert L % tm == 0 or rstride == 0
    gate0 = 8 * MIX_W // D_MODEL

    def tok(w):
        return pl.BlockSpec((tm, w), lambda i: (i, 0))

    def full(shape):
        return pl.BlockSpec((None,) + shape, lambda i: (layer,) + (0,) * len(shape))

    body = functools.partial(_merge_body, L=L, row0=row0, rstride=rstride)
    return pl.pallas_call(
        body,
        grid=(T // tm,),
        in_specs=[tok(D_MODEL), full((N_PAD_ROWS, 6 * D_MODEL)), tok(MIX_W),
                  pl.BlockSpec((tm, MIX_W), lambda i: (i, SU_SECTION)), tok(MIX_W), tok(MIX_W),
                  pl.BlockSpec((tm, D_MODEL), lambda i: (i, gate0)),
                  pl.BlockSpec((tm, D_MODEL), lambda i: (i, gate0 + 1)),
                  pl.BlockSpec((tm, D_MODEL), lambda i: (i, gate0 + 2)),
                  full((1, MIX_W)), full((MIX_W, MIX_W)), full((3, MIX_W, D_MODEL)),
                  full((D_MODEL, D_MODEL)), full((1, D_MODEL)), full((1, D_MODEL))],
        out_specs=tok(D_MODEL),
        out_shape=jax.ShapeDtypeStruct((T, D_MODEL), F32),
        scratch_shapes=[pltpu.VMEM((MIX_W, MIX_W), BF16), pltpu.VMEM((3, MIX_W, D_MODEL), BF16),
                        pltpu.VMEM((D_MODEL, D_MODEL), BF16)],
        name="merge",
        compiler_params=_params("arbitrary"),
    )(x, p, r_out, z, y, n_out, z, z, z, ssm_d.reshape(DEPTH, 1, MIX_W), w_glu, w_branch, w_o,
      ln_g.reshape(DEPTH, 1, D_MODEL), ln_b.reshape(DEPTH, 1, D_MODEL))


def _ffn_body(x_ref, p_ref, wa_ref, wb_ref, cwa_ref, cwb_ref, cba_ref, cbb_ref, wd_ref, lg_ref, lb_ref,
              o_ref, h_scr, acc_scr, mp_scr, mn_scr, *, L, row0, rstride):
    i = pl.program_id(0)
    j = pl.program_id(1)
    tm = x_ref.shape[0]
    nb = tm // L

    @pl.when(j == 0)
    def _():
        for s in range(nb):
            row = row0 + rstride * (i * nb + s)
            sh = _mod_row(p_ref, row, 3)
            sc = _mod_row(p_ref, row, 4)
            h_scr[s * L:(s + 1) * L, :] = (x_ref[s * L:(s + 1) * L, :] * (1.0 + sc) + sh).astype(BF16)
        acc_scr[...] = jnp.zeros_like(acc_scr)
        t = lax.broadcasted_iota(jnp.int32, (tm, FF_TILE), 0) % L
        mp_scr[...] = (t != 0).astype(BF16)
        mn_scr[...] = (t != L - 1).astype(BF16)

    def conv(w_ref, cw_ref, cb_ref):
        zc = _dot(h_scr[...], w_ref[...])
        zp = pltpu.roll(zc, 1, 0).astype(BF16) * mp_scr[...]
        zn = pltpu.roll(zc, tm - 1, 0).astype(BF16) * mn_scr[...]
        cw = cw_ref[...].astype(BF16)
        return zp * cw[0:1, :] + zc.astype(BF16) * cw[1:2, :] + zn * cw[2:3, :] + cb_ref[...].astype(BF16)

    a = conv(wa_ref, cwa_ref, cba_ref)
    b = conv(wb_ref, cwb_ref, cbb_ref)
    acc_scr[...] += _dot(jax.nn.gelu(a) * b, wd_ref[...])

    @pl.when(j == pl.num_programs(1) - 1)
    def _():
        for s in range(nb):
            row = row0 + rstride * (i * nb + s)
            g2 = _mod_row(p_ref, row, 5)
            sl = slice(s * L, (s + 1) * L)
            o_ref[sl, :] = _layer_norm(DEEPNORM_ALPHA * x_ref[sl, :] + g2 * acc_scr[sl, :],
                                       lg_ref[...], lb_ref[...])


def _ffn_weights(w_up, w_down):
    wu = w_up.reshape(DEPTH, D_MODEL, 2 * D_FF // FF_TILE, FF_TILE)
    return jnp.transpose(wu, (0, 2, 1, 3)).astype(BF16), w_down.astype(BF16)


def _conv_ffn(x, p, w_up, conv_w, conv_b, w_down, ln_g, ln_b, *, layer, L, row0, rstride):
    T = x.shape[0]
    tm = TOKEN_TILE
    nff = D_FF // FF_TILE
    body = functools.partial(_ffn_body, L=L, row0=row0, rstride=rstride)
    conv_b = conv_b.reshape(DEPTH, 1, 2 * D_FF)
    return pl.pallas_call(
        body,
        grid=(T // tm, nff),
        in_specs=[pl.BlockSpec((tm, D_MODEL), lambda i, j: (i, 0)),
                  pl.BlockSpec((None, N_PAD_ROWS, 6 * D_MODEL), lambda i, j: (layer, 0, 0)),
                  pl.BlockSpec((None, None, D_MODEL, FF_TILE), lambda i, j: (layer, j, 0, 0)),
                  pl.BlockSpec((None, None, D_MODEL, FF_TILE), lambda i, j: (layer, nff + j, 0, 0)),
                  pl.BlockSpec((None, 3, FF_TILE), lambda i, j: (layer, 0, j)),
                  pl.BlockSpec((None, 3, FF_TILE), lambda i, j: (layer, 0, nff + j)),
                  pl.BlockSpec((None, 1, FF_TILE), lambda i, j: (layer, 0, j)),
                  pl.BlockSpec((None, 1, FF_TILE), lambda i, j: (layer, 0, nff + j)),
                  pl.BlockSpec((None, FF_TILE, D_MODEL), lambda i, j: (layer, j, 0)),
                  pl.BlockSpec((None, 1, D_MODEL), lambda i, j: (layer, 0, 0)),
                  pl.BlockSpec((None, 1, D_MODEL), lambda i, j: (layer, 0, 0))],
        out_specs=pl.BlockSpec((tm, D_MODEL), lambda i, j: (i, 0)),
        out_shape=jax.ShapeDtypeStruct((T, D_MODEL), F32),
        scratch_shapes=[pltpu.VMEM((tm, D_MODEL), BF16), pltpu.VMEM((tm, D_MODEL), F32),
                        pltpu.VMEM((tm, FF_TILE), BF16), pltpu.VMEM((tm, FF_TILE), BF16)],
        name="conv_ffn",
        compiler_params=_params("arbitrary", "arbitrary"),
    )(x, p, w_up, w_up, conv_w, conv_w, conv_b, conv_b, w_down, ln_g.reshape(DEPTH, 1, D_MODEL),
      ln_b.reshape(DEPTH, 1, D_MODEL))


def _s5_states_in(state_ssm, layer):
    B = state_ssm.shape[0]
    nlb = MIX_W // LANES
    h = state_ssm[:, layer].reshape(B, 2, nlb, S5_LBLK_GROUPS, SSM_STATE, 2)
    return jnp.transpose(h, (1, 2, 0, 5, 3, 4)).reshape(2, nlb, B, 2 * S5_LBLK_GROUPS * SSM_STATE)


def _s5_states_out(fin):
    nlb, B = fin.shape[1], fin.shape[2]
    h = fin.reshape(2, nlb, B, 2, S5_LBLK_GROUPS, SSM_STATE)
    return jnp.transpose(h, (2, 0, 1, 4, 5, 3)).reshape(B, 2, SSM_GROUPS, SSM_STATE, 2)


def _layer(x, p, lw, *, B, L, row0, rstride, latent, layer, extra):
    T = B * L
    z, *kv = _inproj(x, p, lw['w_in'], layer=layer, L=L, row0=row0, rstride=rstride, want_kv=not latent)
    z3 = z.reshape(B, L, IN_COLS)
    log_gamma = jax.nn.log_sigmoid(lw['ret_decay'].astype(F32))
    if latent:
        r_out = _retention(z3, log_gamma, B=B, L=L, rope_tabs=extra['rope'], s0=extra['state_ret'],
                           layer=layer, want_state=False)[0]
        n_out = _neighbourhood_attention(z3, extra['cache_k'], extra['cache_v'], extra['bias'][layer],
                                         B=B, L=L, layer=layer)
        y, _ = _s5(z, lw['s5'], _s5_states_in(extra['state_ssm'], layer), B=B, L=L)
        states = None
    else:
        r_out, ret_state = _retention(z3, log_gamma, B=B, L=L, want_state=True)
        n_out = _context_attention(z3, B=B, L=L)
        y, fin = _s5(z, lw['s5'], None, B=B, L=L)
        states = (ret_state, _s5_states_out(fin), kv[0].reshape(B, L, MIX_W), kv[1].reshape(B, L, MIX_W))
    x = _merge(x, p, z, r_out.reshape(T, MIX_W), y, n_out.reshape(T, MIX_W),
               lw['ssm_d'], lw['ssm_w_glu'], lw['w_branch'], lw['w_o'], lw['ln1_g'], lw['ln1_b'],
               layer=layer, L=L, row0=row0, rstride=rstride)
    x = _conv_ffn(x, p, lw['w_up'], lw['conv_w'], lw['conv_b'], lw['w_down'], lw['ln2_g'], lw['ln2_b'],
                  layer=layer, L=L, row0=row0, rstride=rstride)
    return x, states


def kernel(x_prompt, x_sample, state_ret, state_ssm, cache_na_k, cache_na_v, c, c_ctx, w_ada, b_ada, w_in, ret_decay, ssm_a_re, ssm_a_im, ssm_log_dt, ssm_b_re, ssm_b_im, ssm_c_re, ssm_c_im, ssm_d, ssm_w_glu, na_rpb, w_branch, w_o, ln1_g, ln1_b, w_up, conv_w, conv_b, w_down, ln2_g, ln2_b):
    B, L, _ = x_prompt.shape
    Bd, Ld, _ = x_sample.shape
    Lc = cache_na_k.shape[2]

    cond = jnp.concatenate([c_ctx[None, :], c, jnp.zeros((N_PAD_ROWS - 1 - Bd, D_MODEL), F32)], 0)
    p_all = _ada(cond, w_ada, b_ada)

    extra = dict(rope=_rope_tables(Ld), state_ret=state_ret, state_ssm=state_ssm,
                 cache_k=cache_na_k.reshape(Bd, DEPTH, Lc, MIX_W),
                 cache_v=cache_na_v.reshape(Bd, DEPTH, Lc, MIX_W),
                 bias=[_na_bias_blocks(na_rpb[l]) for l in range(DEPTH)])

    w_up, w_down = _ffn_weights(w_up, w_down)
    xp = x_prompt.reshape(B * L, D_MODEL)
    xs = x_sample.reshape(Bd * Ld, D_MODEL)
    ret_states, ssm_states, na_ks, na_vs = [], [], [], []
    for l in range(DEPTH):
        lw = dict(w_in=w_in, ret_decay=ret_decay[l], ssm_d=ssm_d, ssm_w_glu=ssm_w_glu,
                  w_branch=w_branch, w_o=w_o, ln1_g=ln1_g, ln1_b=ln1_b, w_up=w_up,
                  conv_w=conv_w, conv_b=conv_b, w_down=w_down, ln2_g=ln2_g, ln2_b=ln2_b,
                  s5=_s5_operators(ssm_a_re[l], ssm_a_im[l], ssm_log_dt[l], ssm_b_re[l], ssm_b_im[l],
                                   ssm_c_re[l], ssm_c_im[l]))
        xp, (s_ret, s_ssm, nk, nv) = _layer(xp, p_all, lw, B=B, L=L, row0=0, rstride=0,
                                            latent=False, layer=l, extra=None)
        ret_states.append(s_ret)
        ssm_states.append(s_ssm)
        na_ks.append(nk)
        na_vs.append(nv)
        xs, _ = _layer(xs, p_all, lw, B=Bd, L=Ld, row0=1, rstride=1, latent=True, layer=l, extra=extra)
    def heads(xs_):
        return jnp.stack(xs_, 1).reshape(B, DEPTH, L, NA_HEADS, NA_HEAD_DIM)

    return (xp.reshape(B, L, D_MODEL), xs.reshape(Bd, Ld, D_MODEL),
            jnp.stack(ret_states, 1), jnp.stack(ssm_states, 1), heads(na_ks), heads(na_vs))
```

```python
import functools

import jax
import jax.numpy as jnp
import numpy as np
from jax import lax
from jax.experimental import pallas as pl
from jax.experimental.pallas import tpu as pltpu

F32 = jnp.float32
BF16 = jnp.bfloat16

D_MODEL = 1024
DEPTH = 2
GRID_W = 64
MIX_W = D_MODEL // 2
N_RET_HEADS = 4
RET_DK = MIX_W // N_RET_HEADS
SSM_GROUP = 16
SSM_GROUPS = MIX_W // SSM_GROUP
SSM_STATE = 64
NA_HEADS = 8
NA_HEAD_DIM = MIX_W // NA_HEADS
NA_KR = 8
NA_KW = 16
D_FF = ((8 * D_MODEL // 3 + 127) // 128) * 128
ROPE_BASE = 10000.0
LN_EPS = 1e-5
NEG_INF = -1e30
DEEPNORM_ALPHA = (2 * DEPTH) ** 0.25
IN_COLS = 8 * MIX_W + 3 * D_MODEL

VMEM_LIMIT_BYTES = 56 * 1024 * 1024
LANES = 128

TOKEN_TILE = 1024
MERGE_TILE = 512
COL_TILE = 1024
SU_SECTION = 4
NK_SECTION = 6
FF_TILE = 256
RET_CHUNK = 256
RET_ROWS = 1024
S5_CHUNK = 8
S5_PITCH_PAD = 8
S5_LBLK_GROUPS = LANES // SSM_GROUP
N_PAD_ROWS = 8
CTX_ATTN_BATCH = 4
NA_CHUNK_ROWS = 4
NA_WIN_ROWS = 12


def _params(*sem):
    return pltpu.CompilerParams(dimension_semantics=sem, vmem_limit_bytes=VMEM_LIMIT_BYTES)


def _dot(a, b):
    return jnp.dot(a, b, preferred_element_type=F32)


def _dot_nt(a, b):
    return lax.dot_general(a, b, (((1,), (1,)), ((), ())), preferred_element_type=F32)


def _layer_norm(x, g, b):
    mu = jnp.mean(x, -1, keepdims=True)
    xc = x - mu
    var = jnp.mean(xc * xc, -1, keepdims=True)
    return xc * lax.rsqrt(var + LN_EPS) * g + b


def _ada_body(c_ref, w_ref, b_ref, o_ref):
    c = c_ref[...]
    s = c * jax.nn.sigmoid(c)
    o_ref[...] = _dot(s.astype(BF16), w_ref[...].astype(BF16)) + b_ref[...]


def _ada(cond, w_ada, b_ada):
    tn = 1024
    return pl.pallas_call(
        _ada_body,
        grid=(DEPTH, 6 * D_MODEL // tn),
        in_specs=[pl.BlockSpec((N_PAD_ROWS, D_MODEL), lambda l, j: (0, 0)),
                  pl.BlockSpec((None, D_MODEL, tn), lambda l, j: (l, 0, j)),
                  pl.BlockSpec((None, 1, tn), lambda l, j: (l, 0, j))],
        out_specs=pl.BlockSpec((None, N_PAD_ROWS, tn), lambda l, j: (l, 0, j)),
        out_shape=jax.ShapeDtypeStruct((DEPTH, N_PAD_ROWS, 6 * D_MODEL), F32),
        name="ada",
        compiler_params=_params("arbitrary", "arbitrary"),
    )(cond, w_ada, b_ada.reshape(DEPTH, 1, 6 * D_MODEL))


def _mod_row(p_ref, row, k):
    return p_ref[pl.ds(row, 1), k * D_MODEL:(k + 1) * D_MODEL]


def _kv_tile(n):
    col = (NK_SECTION + n) * MIX_W
    return col // COL_TILE, col % COL_TILE


def _inproj_body(x_ref, p_ref, w_ref, *rest, L, row0, rstride, n_kv):
    z_ref = rest[n_kv]
    kv_refs = rest[n_kv + 1:-2]
    h_scr, w_scr = rest[-2:]
    j = pl.program_id(0)
    i = pl.program_id(1)
    nb = x_ref.shape[0] // L

    @pl.when(j == 0)
    def _():
        for s in range(nb):
            row = row0 + rstride * (i * nb + s)
            sh = _mod_row(p_ref, row, 0)
            sc = _mod_row(p_ref, row, 1)
            h_scr[i, s * L:(s + 1) * L, :] = (x_ref[s * L:(s + 1) * L, :] * (1.0 + sc) + sh).astype(BF16)

    @pl.when(i == 0)
    def _():
        w_scr[...] = w_ref[...].astype(BF16)

    acc = _dot(h_scr[i], w_scr[...])
    z_ref[...] = acc.astype(BF16)

    for n, ref in enumerate(kv_refs):
        tile, off = _kv_tile(n)

        @pl.when(j == tile)
        def _(ref=ref, off=off):
            ref[...] = acc[:, off:off + MIX_W].reshape(ref.shape)


def _inproj(x, p, w_in, *, layer, L, row0, rstride, kv_bufs=()):
    T = x.shape[0]
    tm = TOKEN_TILE
    n_i = T // tm
    nb = tm // L
    n_kv = len(kv_bufs)
    body = functools.partial(_inproj_body, L=L, row0=row0, rstride=rstride, n_kv=n_kv)

    def only_at(tile):
        return lambda j, i: jnp.where(j < tile, 0, jnp.where(j > tile, n_i - 1, i))

    kv_i = [only_at(_kv_tile(n)[0]) for n in range(n_kv)]
    return pl.pallas_call(
        body,
        grid=(IN_COLS // COL_TILE, n_i),
        in_specs=[pl.BlockSpec((tm, D_MODEL), lambda j, i: (jnp.where(j == 0, i, n_i - 1), 0)),
                  pl.BlockSpec((None, N_PAD_ROWS, 6 * D_MODEL), lambda j, i: (layer, 0, 0)),
                  pl.BlockSpec((None, D_MODEL, COL_TILE), lambda j, i: (layer, 0, j))]
        + [pl.BlockSpec(memory_space=pl.ANY)] * n_kv,
        out_specs=[pl.BlockSpec((tm, COL_TILE), lambda j, i: (i, j))]
        + [pl.BlockSpec((nb, None, L, MIX_W), lambda j, i, f=f: (f(j, i), layer, 0, 0)) for f in kv_i],
        out_shape=[jax.ShapeDtypeStruct((T, IN_COLS), BF16)]
        + [jax.ShapeDtypeStruct(b.shape, b.dtype) for b in kv_bufs],
        input_output_aliases={3 + n: 1 + n for n in range(n_kv)},
        scratch_shapes=[pltpu.VMEM((n_i, tm, D_MODEL), BF16), pltpu.VMEM((D_MODEL, COL_TILE), BF16)],
        name="inproj",
        compiler_params=_params("arbitrary", "arbitrary"),
    )(x, p, w_in, *kv_bufs)


def _rope(x, cos, s_up, s_dn):
    return x * cos + pltpu.roll(x, 96, 1) * s_up + pltpu.roll(x, 32, 1) * s_dn


def _ret_body(*refs, n, rope, has_s0, want_state):
    refs = list(refs)
    lg_ref, q_ref, k_ref, v_ref, g_ref = refs[:5]
    refs = refs[5:]
    if rope:
        cos_ref, sup_ref, sdn_ref = refs[:3]
        refs = refs[3:]
    if has_s0:
        s0_ref = refs[0]
        refs = refs[1:]
    if want_state:
        refs = refs[1:]
    o_ref = refs[0]
    refs = refs[1:]
    if want_state:
        st_ref = refs[0]
        refs = refs[1:]
    q_scr, k_scr, sb_scr, decay_scr = refs

    C = RET_CHUNK
    h = pl.program_id(0)
    lf = lg_ref[0, h]
    lb = lg_ref[1, h]

    @pl.when(pl.program_id(1) == 0)
    def _():
        ti = lax.broadcasted_iota(jnp.int32, (C, C), 0)
        si = lax.broadcasted_iota(jnp.int32, (C, C), 1)
        dlt = (ti - si).astype(F32)
        decay_scr[...] = (jnp.where(dlt >= 0, jnp.exp(lf * jnp.maximum(dlt, 0.0)), 0.0)
                          + jnp.where(dlt <= 0, jnp.exp(lb * jnp.maximum(-dlt, 0.0)), 0.0))

    tcol = lax.broadcasted_iota(jnp.int32, (C, 1), 0).astype(F32)
    qd_f = jnp.exp(lf * (tcol + 1.0))
    qd_b = jnp.exp(lb * (C - tcol))
    kd_f = jnp.exp(lf * (C - 1.0 - tcol))
    kd_b = jnp.exp(lb * tcol)
    cd_f = jnp.exp(lf * jnp.full((1, RET_DK), float(C), F32))
    cd_b = jnp.exp(lb * jnp.full((1, RET_DK), float(C), F32))

    def kv_outer(kc, vc, kd):
        return _dot((kc * kd).T.astype(BF16), vc)

    for bb in range(q_ref.shape[0]):
        q = q_ref[bb].astype(F32)
        k = k_ref[bb].astype(F32)
        if rope:
            q = _rope(q, cos_ref[...], sup_ref[...], sdn_ref[...])
            k = _rope(k, cos_ref[...], sup_ref[...], sdn_ref[...])
        q_scr[bb] = q
        k_scr[bb] = k * (RET_DK ** -0.5)

        s_b = s0_ref[bb, 1] if has_s0 else jnp.zeros((RET_DK, RET_DK), F32)
        for i in reversed(range(n)):
            sb_scr[bb, i] = s_b
            if i > 0 or want_state:
                s_b = s_b * cd_b + kv_outer(k_scr[bb, i * C:(i + 1) * C, :], v_ref[bb, i * C:(i + 1) * C, :], kd_b)

        s_f = s0_ref[bb, 0] if has_s0 else jnp.zeros((RET_DK, RET_DK), F32)
        for i in range(n):
            sl = slice(i * C, (i + 1) * C)
            qc = q_scr[bb, sl, :]
            kc = k_scr[bb, sl, :]
            vc = v_ref[bb, sl, :]
            att = _dot_nt(qc.astype(BF16), kc.astype(BF16)) * decay_scr[...]
            o = _dot(att.astype(BF16), vc)
            o = o + _dot((qc * qd_f).astype(BF16), s_f.astype(BF16))
            o = o + _dot((qc * qd_b).astype(BF16), sb_scr[bb, i].astype(BF16))
            mu = jnp.mean(o, -1, keepdims=True)
            oc = o - mu
            var = jnp.mean(oc * oc, -1, keepdims=True)
            gc = g_ref[bb, sl, :].astype(F32)
            o_ref[bb, sl, :] = (oc * lax.rsqrt(var + LN_EPS) * (gc * jax.nn.sigmoid(gc))).astype(BF16)
            if i < n - 1 or want_state:
                s_f = s_f * cd_f + kv_outer(kc, vc, kd_f)

        if want_state:
            st_ref[bb, 0] = s_f
            st_ref[bb, 1] = s_b


def _retention(z, log_gamma, *, B, L, rope_tabs=None, s0=None, layer=0, state_buf=None):
    want_state = state_buf is not None
    n = L // RET_CHUNK
    H = N_RET_HEADS
    nblk = MIX_W // RET_DK

    nbb = max(1, RET_ROWS // L)
    assert B % nbb == 0

    def sec(s):
        return pl.BlockSpec((nbb, L, RET_DK), lambda h, b: (b, 0, s * nblk + h))

    in_specs = [pl.BlockSpec(memory_space=pltpu.SMEM), sec(0), sec(1), sec(2), sec(3)]
    args = [log_gamma, z, z, z, z]
    if rope_tabs is not None:
        in_specs += [pl.BlockSpec((L, RET_DK), lambda h, b: (0, 0))] * 3
        args += list(rope_tabs)
    if s0 is not None:
        in_specs.append(pl.BlockSpec((nbb, None, 2, None, RET_DK, RET_DK), lambda h, b: (b, layer, 0, h, 0, 0)))
        args.append(s0)
    out_specs = [pl.BlockSpec((nbb, L, RET_DK), lambda h, b: (b, 0, h))]
    out_shape = [jax.ShapeDtypeStruct((B, L, MIX_W), BF16)]
    aliases = {}
    if want_state:
        aliases = {len(args): 1}
        in_specs.append(pl.BlockSpec(memory_space=pl.ANY))
        args.append(state_buf)
        out_specs.append(pl.BlockSpec((nbb, None, 2, None, RET_DK, RET_DK), lambda h, b: (b, layer, 0, h, 0, 0)))
        out_shape.append(jax.ShapeDtypeStruct(state_buf.shape, state_buf.dtype))
    body = functools.partial(_ret_body, n=n, rope=rope_tabs is not None, has_s0=s0 is not None,
                             want_state=want_state)
    return pl.pallas_call(
        body,
        grid=(H, B // nbb),
        in_specs=in_specs,
        out_specs=out_specs,
        out_shape=out_shape,
        input_output_aliases=aliases,
        scratch_shapes=[pltpu.VMEM((nbb, L, RET_DK), F32), pltpu.VMEM((nbb, L, RET_DK), F32),
                        pltpu.VMEM((nbb, n, RET_DK, RET_DK), F32), pltpu.VMEM((RET_CHUNK, RET_CHUNK), F32)],
        name="retention",
        compiler_params=_params("arbitrary", "arbitrary"),
    )(*args)


def _rope_tables(L):
    pos = jnp.arange(L)
    row = (pos // GRID_W).astype(F32)
    col = (pos % GRID_W).astype(F32)
    quarter = RET_DK // 4
    inv_freq = ROPE_BASE ** (-jnp.arange(quarter, dtype=F32) / quarter)
    ang_r = row[:, None] * inv_freq[None, :]
    ang_c = col[:, None] * inv_freq[None, :]
    zero = jnp.zeros_like(ang_r)
    cos = jnp.concatenate([jnp.cos(ang_r), jnp.cos(ang_r), jnp.cos(ang_c), jnp.cos(ang_c)], -1)
    s_up = jnp.concatenate([-jnp.sin(ang_r), zero, -jnp.sin(ang_c), zero], -1)
    s_dn = jnp.concatenate([zero, jnp.sin(ang_r), zero, jnp.sin(ang_c)], -1)
    return cos, s_up, s_dn


def _s5_body(*refs, B, nC, has_h0):
    refs = list(refs)
    u_ref, c0_ref, bs_ref, cp_ref, a8_ref = refs[:5]
    refs = refs[5:]
    if has_h0:
        h0_ref = refs[0]
        refs = refs[1:]
    y_ref, fin_ref, u_scr, a_scr, m_scr, bs_scr, cp_scr, c0_scr, s_scr, x_scr, y_scr = refs

    TC = S5_CHUNK
    R = B * nC
    P = nC + S5_PITCH_PAD
    nsl = s_scr.shape[0]
    half = nsl // 2
    ng = S5_LBLK_GROUPS
    sw = ng * SSM_STATE

    u_scr[...] = u_ref[...].astype(F32)
    for s in range(TC):
        a_scr[:, s * LANES:(s + 1) * LANES] = u_scr[pl.ds(s, R, stride=TC), :].astype(BF16)

    bs_scr[...] = jnp.zeros_like(bs_scr)
    cp_scr[...] = jnp.zeros_like(cp_scr)
    c0_scr[...] = jnp.zeros_like(c0_scr)

    for d in range(2):
        for g in range(ng):
            for part in range(2):
                cols = slice(part * sw + g * SSM_STATE, part * sw + (g + 1) * SSM_STATE)
                lo = (g % 2) * SSM_STATE
                c0_scr[g * SSM_GROUP:(g + 1) * SSM_GROUP, cols] = c0_ref[d, g, part, :, lo:lo + SSM_STATE]
                for s in range(TC):
                    rows = slice(s * LANES + g * SSM_GROUP, s * LANES + (g + 1) * SSM_GROUP)
                    bs_scr[rows, cols] = bs_ref[d, s, g, part, :, lo:lo + SSM_STATE]
                    cp_scr[rows, cols] = cp_ref[d, s, g, part, :, lo:lo + SSM_STATE]

        lag = _dot_nt(bs_scr[...], c0_scr[...]).astype(BF16)
        for s in range(TC):
            for t in range(TC):
                k = (t - s) if d == 0 else (s - t)
                src = (TC - 1 - k) if d == 0 else k
                blk = lag[src * LANES:(src + 1) * LANES, :] if k >= 0 else jnp.zeros((LANES, LANES), BF16)
                m_scr[s * LANES:(s + 1) * LANES, t * LANES:(t + 1) * LANES] = blk

        a = a_scr[...]
        yd = _dot(a, m_scr[...])
        if d == 0:
            y_scr[...] = yd
        else:
            y_scr[...] += yd

        sm = _dot(a, bs_scr[...])
        for b in range(B):
            for sl in range(nsl):
                s_scr[sl, b * P:b * P + nC, :] = sm[b * nC:(b + 1) * nC, sl * LANES:(sl + 1) * LANES]

        a_r = [jnp.broadcast_to(a8_ref[d, 0, :, q * LANES:(q + 1) * LANES], (B, LANES)) for q in range(half)]
        a_i = [jnp.broadcast_to(a8_ref[d, 1, :, q * LANES:(q + 1) * LANES], (B, LANES)) for q in range(half)]
        if has_h0:
            init = tuple(h0_ref[d, :, sl * LANES:(sl + 1) * LANES] for sl in range(nsl))
        else:
            init = tuple(jnp.zeros((B, LANES), F32) for _ in range(nsl))

        def step(j, carry, d=d, a_r=a_r, a_i=a_i):
            c = j if d == 0 else nC - 1 - j
            rows = pl.ds(c, B, stride=P)
            new_r, new_i = [], []
            for q in range(half):
                xr, xi = carry[q], carry[half + q]
                sr = s_scr[q, rows, :]
                si = s_scr[half + q, rows, :]
                s_scr[q, rows, :] = xr
                s_scr[half + q, rows, :] = xi
                new_r.append(a_r[q] * xr - a_i[q] * xi + sr)
                new_i.append(a_r[q] * xi + a_i[q] * xr + si)
            return tuple(new_r + new_i)

        fin = lax.fori_loop(0, nC, step, init)
        for sl in range(nsl):
            fin_ref[d, :, sl * LANES:(sl + 1) * LANES] = fin[sl]

        for b in range(B):
            for sl in range(nsl):
                x_scr[b * nC:(b + 1) * nC, sl * LANES:(sl + 1) * LANES] = \
                    s_scr[sl, b * P:b * P + nC, :].astype(BF16)
        y_scr[...] += _dot_nt(x_scr[...], cp_scr[...])

    for t in range(TC):
        y_ref[pl.ds(t, R, stride=TC), :] = y_scr[:, t * LANES:(t + 1) * LANES]


def _s5(z, ops, h0, *, B, L):
    c0, bs, cp, a8 = ops
    T = B * L
    nC = L // S5_CHUNK
    nlb = MIX_W // LANES
    sc = 2 * S5_LBLK_GROUPS * SSM_STATE
    kc = S5_CHUNK * LANES
    su0 = SU_SECTION * MIX_W // LANES

    blocks = pl.BlockSpec((2, None, S5_CHUNK, S5_LBLK_GROUPS, 2, SSM_GROUP, LANES),
                          lambda lb: (0, lb, 0, 0, 0, 0, 0))
    in_specs = [pl.BlockSpec((T, LANES), lambda lb: (0, su0 + lb)),
                pl.BlockSpec((2, None, S5_LBLK_GROUPS, 2, SSM_GROUP, LANES), lambda lb: (0, lb, 0, 0, 0, 0)),
                blocks, blocks,
                pl.BlockSpec((2, None, 2, 1, sc // 2), lambda lb: (0, lb, 0, 0, 0))]
    args = [z, c0, bs, cp, a8]
    if h0 is not None:
        in_specs.append(pl.BlockSpec((2, None, B, sc), lambda lb: (0, lb, 0, 0)))
        args.append(h0)
    body = functools.partial(_s5_body, B=B, nC=nC, has_h0=h0 is not None)
    return pl.pallas_call(
        body,
        grid=(nlb,),
        in_specs=in_specs,
        out_specs=[pl.BlockSpec((T, LANES), lambda lb: (0, lb)),
                   pl.BlockSpec((2, None, B, sc), lambda lb: (0, lb, 0, 0))],
        out_shape=[jax.ShapeDtypeStruct((T, MIX_W), F32),
                   jax.ShapeDtypeStruct((2, nlb, B, sc), F32)],
        scratch_shapes=[pltpu.VMEM((T, LANES), F32), pltpu.VMEM((B * nC, kc), BF16),
                        pltpu.VMEM((kc, kc), BF16), pltpu.VMEM((kc, sc), BF16), pltpu.VMEM((kc, sc), BF16),
                        pltpu.VMEM((LANES, sc), BF16),
                        pltpu.VMEM((sc // LANES, B * (nC + S5_PITCH_PAD), LANES), F32),
                        pltpu.VMEM((B * nC, sc), BF16), pltpu.VMEM((B * nC, kc), F32)],
        name="s5",
        compiler_params=_params("arbitrary"),
    )(*args)


def _s5_operators(a_re, a_im, log_dt, b_re, b_im, c_re, c_im):
    TC = S5_CHUNK
    nlb = MIX_W // LANES
    ng = S5_LBLK_GROUPS
    lr = jnp.minimum(a_re, -1e-4)
    li = a_im
    dt = jnp.exp(log_dt)[..., None]
    k = jnp.arange(TC + 1, dtype=F32)[:, None, None, None]
    mag = jnp.exp(k * (lr * dt)[None])
    pr = mag * jnp.cos(k * (li * dt)[None])
    pi = mag * jnp.sin(k * (li * dt)[None])
    ar, ai = pr[1], pi[1]
    den = lr * lr + li * li
    sr = ((ar - 1.0) * lr + ai * li) / den
    si = (ai * lr - (ar - 1.0) * li) / den
    bbr = sr[..., None] * b_re[None] - si[..., None] * b_im[None]
    bbi = sr[..., None] * b_im[None] + si[..., None] * b_re[None]

    def lanes2(x):
        return jnp.concatenate([x, x], -1)

    def powers(fwd, bwd):
        x = lanes2(jnp.stack([fwd, bwd], 0)).reshape(2, TC, nlb, ng, 1, LANES)
        return jnp.swapaxes(x, 1, 2)

    def per_group(x):
        return lanes2(x).reshape(2, nlb, 1, ng, SSM_GROUP, LANES)

    er = powers(jnp.flip(pr[:TC, 0], 0), pr[:TC, 1])
    ei = powers(jnp.flip(pi[:TC, 0], 0), pi[:TC, 1])
    btr = per_group(jnp.swapaxes(bbr, -1, -2))
    bti = per_group(jnp.swapaxes(bbi, -1, -2))
    bs = jnp.stack([er * btr - ei * bti, er * bti + ei * btr], 4).astype(BF16)

    fr = powers(pr[1:, 0], jnp.flip(pr[1:, 1], 0))
    fi = powers(pi[1:, 0], jnp.flip(pi[1:, 1], 0))
    ctr = per_group(c_re)
    cti = per_group(c_im)
    cp = jnp.stack([ctr * fr - cti * fi, -(ctr * fi + cti * fr)], 4).astype(BF16)
    c0 = jnp.stack([ctr, -cti], 4)[:, :, 0].astype(BF16)

    sw = ng * SSM_STATE
    a8 = jnp.stack([pr[TC].reshape(2, nlb, 1, sw), pi[TC].reshape(2, nlb, 1, sw)], 2)
    return c0, bs, cp, a8


def _head_masks(shape):
    lane = lax.broadcasted_iota(jnp.int32, shape, 1)
    return lane < NA_HEAD_DIM


def _cattn_body(q_ref, k_ref, v_ref, o_ref):
    first = _head_masks(q_ref.shape[1:])
    for bb in range(q_ref.shape[0]):
        q = q_ref[bb]
        k = k_ref[bb]
        v = v_ref[bb]
        outs = []
        for e in range(2):
            qe = jnp.where(first if e == 0 else jnp.logical_not(first), q, jnp.zeros_like(q))
            s = _dot_nt(qe, k) * (NA_HEAD_DIM ** -0.5)
            m = jnp.max(s, -1, keepdims=True)
            p = jnp.exp(s - m)
            l = jnp.sum(p, -1, keepdims=True)
            outs.append(_dot(p.astype(BF16), v) / l)
        o_ref[bb] = jnp.where(first, outs[0], outs[1]).astype(BF16)


def _context_attention(z, *, B, L):
    nblk = MIX_W // LANES
    nb = CTX_ATTN_BATCH

    def sec(s):
        return pl.BlockSpec((nb, L, LANES), lambda b, hp: (b, 0, s * nblk + hp))

    return pl.pallas_call(
        _cattn_body,
        grid=(B // nb, nblk),
        in_specs=[sec(5), sec(6), sec(7)],
        out_specs=pl.BlockSpec((nb, L, LANES), lambda b, hp: (b, 0, hp)),
        out_shape=jax.ShapeDtypeStruct((B, L, MIX_W), BF16),
        name="ctx_attention",
        compiler_params=_params("arbitrary", "arbitrary"),
    )(z, z, z)


def _na_chunks(rows):
    half = NA_KR // 2
    plan, kinds = [], []
    for r0 in range(0, rows, NA_CHUNK_ROWS):
        rs = [min(max(r - half, 0), rows - NA_KR) for r in range(r0, r0 + NA_CHUNK_ROWS)]
        ws = min(rs[0], rows - NA_WIN_ROWS)
        assert rs[-1] + NA_KR <= ws + NA_WIN_ROWS
        kind = tuple((r0 + n - ws, rs[n] - ws) for n in range(NA_CHUNK_ROWS))
        if kind not in kinds:
            kinds.append(kind)
        plan.append((ws, kinds.index(kind)))
    return plan, kinds


def _na_body(q_ref, k_ref, v_ref, kc_ref, vc_ref, tb_ref, o_ref, bias_scr, *, rows):
    scale = NA_HEAD_DIM ** -0.5
    nq = NA_CHUNK_ROWS * GRID_W
    plan, kinds = _na_chunks(rows)
    n_off = 2 * NA_KR - 1

    @pl.when(pl.program_id(1) == 0)
    def _():
        for t, kind in enumerate(kinds):
            for e in range(2):
                for n, (r_rel, rs_rel) in enumerate(kind):
                    for kj in range(NA_WIN_ROWS):
                        off = kj - r_rel + NA_KR - 1 if rs_rel <= kj < rs_rel + NA_KR else n_off
                        lo = (kj % 2) * GRID_W
                        bias_scr[t, e * nq + n * GRID_W:e * nq + (n + 1) * GRID_W, kj * GRID_W:(kj + 1) * GRID_W] = \
                            tb_ref[e, off, :, lo:lo + GRID_W]

    kctx = kc_ref[...].astype(BF16)
    vctx = vc_ref[...].astype(BF16)
    first = _head_masks((nq, LANES))
    for c, (ws, kind) in enumerate(plan):
        qc = q_ref[c * nq:(c + 1) * nq, :]
        qs = jnp.concatenate([jnp.where(first, qc, jnp.zeros_like(qc)),
                              jnp.where(first, jnp.zeros_like(qc), qc)], 0)
        kw = k_ref[ws * GRID_W:(ws + NA_WIN_ROWS) * GRID_W, :]
        vw = v_ref[ws * GRID_W:(ws + NA_WIN_ROWS) * GRID_W, :]
        s_loc = _dot_nt(qs, kw) * scale + bias_scr[kind]
        s_ctx = _dot_nt(qs, kctx) * scale
        m = jnp.maximum(jnp.max(s_loc, -1, keepdims=True), jnp.max(s_ctx, -1, keepdims=True))
        p_loc = jnp.exp(s_loc - m)
        p_ctx = jnp.exp(s_ctx - m)
        l = jnp.sum(p_loc, -1, keepdims=True) + jnp.sum(p_ctx, -1, keepdims=True)
        o = (_dot(p_loc.astype(BF16), vw) + _dot(p_ctx.astype(BF16), vctx)) / l
        o_ref[c * nq:(c + 1) * nq, :] = jnp.where(first, o[:nq], o[nq:]).astype(BF16)


def _neighbourhood_attention(z, cache_k, cache_v, blocks, *, B, L, layer):
    nblk = MIX_W // LANES
    rows = L // GRID_W
    Lc = cache_k.shape[2]
    _, kinds = _na_chunks(rows)

    def sec(s):
        return pl.BlockSpec((None, L, LANES), lambda hp, b: (b, 0, s * nblk + hp))

    ctx = pl.BlockSpec((None, None, Lc, LANES), lambda hp, b: (b, layer, 0, hp))
    return pl.pallas_call(
        functools.partial(_na_body, rows=rows),
        grid=(nblk, B),
        in_specs=[sec(5), sec(6), sec(7), ctx, ctx,
                  pl.BlockSpec((None, 2, 2 * NA_KR, GRID_W, 2 * GRID_W), lambda hp, b: (hp, 0, 0, 0, 0))],
        out_specs=pl.BlockSpec((None, L, LANES), lambda hp, b: (b, 0, hp)),
        out_shape=jax.ShapeDtypeStruct((B, L, MIX_W), BF16),
        scratch_shapes=[pltpu.VMEM((len(kinds), 2 * NA_CHUNK_ROWS * GRID_W, NA_WIN_ROWS * GRID_W), F32)],
        name="nbr_attention",
        compiler_params=_params("arbitrary", "arbitrary"),
    )(z, z, z, cache_k, cache_v, blocks)


def _na_bias_blocks(rpb):
    nr, nc = 2 * NA_KR - 1, 2 * NA_KW - 1
    qc = np.arange(GRID_W)
    kc = np.arange(GRID_W)
    ws = np.clip(qc - NA_KW // 2, 0, GRID_W - NA_KW)
    col_ok = (kc[None, :] >= ws[:, None]) & (kc[None, :] < ws[:, None] + NA_KW)
    coff = np.clip(kc[None, :] - qc[:, None] + NA_KW - 1, 0, nc - 1)
    sel_c = ((coff[None] == np.arange(nc)[:, None, None]) & col_ok[None]).astype(np.float32)
    sel_c = np.concatenate([sel_c, sel_c], -1)
    ok = np.concatenate([col_ok, col_ok], -1)[None] & (np.arange(nr + 1) < nr)[:, None, None]
    H = rpb.shape[0]
    rows = jnp.pad(rpb.astype(F32), ((0, 0), (0, 1), (0, 0)))
    t = jnp.einsum('hrc,cqk->hrqk', rows, sel_c, precision=lax.Precision.HIGHEST)
    return jnp.where(ok[None], t, NEG_INF).reshape(H // 2, 2, nr + 1, GRID_W, 2 * GRID_W)


def _merge_body(x_ref, p_ref, r_ref, u_ref, y_ref, n_ref, ga_ref, gb_ref, gc_ref,
                d_ref, wglu_ref, wbr_ref, wo_ref, lg_ref, lb_ref, o_ref,
                wglu_s, wbr_s, wo_s, *, L, row0, rstride):
    i = pl.program_id(0)
    tm = x_ref.shape[0]

    @pl.when(i == 0)
    def _():
        wglu_s[...] = wglu_ref[...].astype(BF16)
        wbr_s[...] = wbr_ref[...].astype(BF16)
        wo_s[...] = wo_ref[...].astype(BF16)

    row = row0 + rstride * ((i * tm) // L)
    g1 = _mod_row(p_ref, row, 2)

    y = d_ref[...] * u_ref[...].astype(F32) + y_ref[...]
    y = jax.nn.gelu(y)
    s_out = y * jax.nn.sigmoid(_dot(y.astype(BF16), wglu_s[...]))

    def gate(ref):
        return jax.nn.sigmoid(ref[...].astype(F32))

    merged = (gate(ga_ref) * _dot(r_ref[...], wbr_s[0])
              + gate(gb_ref) * _dot(s_out.astype(BF16), wbr_s[1])
              + gate(gc_ref) * _dot(n_ref[...], wbr_s[2]))
    m = _dot(merged.astype(BF16), wo_s[...])
    o_ref[...] = _layer_norm(DEEPNORM_ALPHA * x_ref[...] + g1 * m, lg_ref[...], lb_ref[...])


def _merge(x, p, z, r_out, y, n_out, ssm_d, w_glu, w_branch, w_o, ln_g, ln_b, *, layer, L, row0, rstride):
    T = x.shape[0]
    tm = MERGE_TILE
    assert L % tm == 0 or rstride == 0
    gate0 = 8 * MIX_W // D_MODEL

    def tok(w):
        return pl.BlockSpec((tm, w), lambda i: (i, 0))

    def full(shape):
        return pl.BlockSpec((None,) + shape, lambda i: (layer,) + (0,) * len(shape))

    body = functools.partial(_merge_body, L=L, row0=row0, rstride=rstride)
    return pl.pallas_call(
        body,
        grid=(T // tm,),
        in_specs=[tok(D_MODEL), full((N_PAD_ROWS, 6 * D_MODEL)), tok(MIX_W),
                  pl.BlockSpec((tm, MIX_W), lambda i: (i, SU_SECTION)), tok(MIX_W), tok(MIX_W),
                  pl.BlockSpec((tm, D_MODEL), lambda i: (i, gate0)),
                  pl.BlockSpec((tm, D_MODEL), lambda i: (i, gate0 + 1)),
                  pl.BlockSpec((tm, D_MODEL), lambda i: (i, gate0 + 2)),
                  full((1, MIX_W)), full((MIX_W, MIX_W)), full((3, MIX_W, D_MODEL)),
                  full((D_MODEL, D_MODEL)), full((1, D_MODEL)), full((1, D_MODEL))],
        out_specs=tok(D_MODEL),
        out_shape=jax.ShapeDtypeStruct((T, D_MODEL), F32),
        scratch_shapes=[pltpu.VMEM((MIX_W, MIX_W), BF16), pltpu.VMEM((3, MIX_W, D_MODEL), BF16),
                        pltpu.VMEM((D_MODEL, D_MODEL), BF16)],
        name="merge",
        compiler_params=_params("arbitrary"),
    )(x, p, r_out, z, y, n_out, z, z, z, ssm_d.reshape(DEPTH, 1, MIX_W), w_glu, w_branch, w_o,
      ln_g.reshape(DEPTH, 1, D_MODEL), ln_b.reshape(DEPTH, 1, D_MODEL))


def _ffn_body(x_ref, p_ref, wa_ref, wb_ref, cwa_ref, cwb_ref, cba_ref, cbb_ref, wd_ref, lg_ref, lb_ref,
              o_ref, h_scr, acc_scr, mp_scr, mn_scr, *, L, row0, rstride):
    i = pl.program_id(0)
    j = pl.program_id(1)
    tm = x_ref.shape[0]
    nb = tm // L

    @pl.when(j == 0)
    def _():
        for s in range(nb):
            row = row0 + rstride * (i * nb + s)
            sh = _mod_row(p_ref, row, 3)
            sc = _mod_row(p_ref, row, 4)
            h_scr[s * L:(s + 1) * L, :] = (x_ref[s * L:(s + 1) * L, :] * (1.0 + sc) + sh).astype(BF16)
        acc_scr[...] = jnp.zeros_like(acc_scr)
        t = lax.broadcasted_iota(jnp.int32, (tm, FF_TILE), 0) % L
        mp_scr[...] = (t != 0).astype(BF16)
        mn_scr[...] = (t != L - 1).astype(BF16)

    def conv(w_ref, cw_ref, cb_ref):
        zc = _dot(h_scr[...], w_ref[...].astype(BF16))
        zp = pltpu.roll(zc, 1, 0).astype(BF16) * mp_scr[...]
        zn = pltpu.roll(zc, tm - 1, 0).astype(BF16) * mn_scr[...]
        cw = cw_ref[...].astype(BF16)
        return zp * cw[0:1, :] + zc.astype(BF16) * cw[1:2, :] + zn * cw[2:3, :] + cb_ref[...].astype(BF16)

    a = conv(wa_ref, cwa_ref, cba_ref)
    b = conv(wb_ref, cwb_ref, cbb_ref)
    acc_scr[...] += _dot(jax.nn.gelu(a) * b, wd_ref[...].astype(BF16))

    @pl.when(j == pl.num_programs(1) - 1)
    def _():
        for s in range(nb):
            row = row0 + rstride * (i * nb + s)
            g2 = _mod_row(p_ref, row, 5)
            sl = slice(s * L, (s + 1) * L)
            o_ref[sl, :] = _layer_norm(DEEPNORM_ALPHA * x_ref[sl, :] + g2 * acc_scr[sl, :],
                                       lg_ref[...], lb_ref[...])


def _conv_ffn(x, p, w_up, conv_w, conv_b, w_down, ln_g, ln_b, *, layer, L, row0, rstride):
    T = x.shape[0]
    tm = TOKEN_TILE
    nff = D_FF // FF_TILE
    body = functools.partial(_ffn_body, L=L, row0=row0, rstride=rstride)
    conv_b = conv_b.reshape(DEPTH, 1, 2 * D_FF)
    return pl.pallas_call(
        body,
        grid=(T // tm, nff),
        in_specs=[pl.BlockSpec((tm, D_MODEL), lambda i, j: (i, 0)),
                  pl.BlockSpec((None, N_PAD_ROWS, 6 * D_MODEL), lambda i, j: (layer, 0, 0)),
                  pl.BlockSpec((None, D_MODEL, FF_TILE), lambda i, j: (layer, 0, j)),
                  pl.BlockSpec((None, D_MODEL, FF_TILE), lambda i, j: (layer, 0, nff + j)),
                  pl.BlockSpec((None, 3, FF_TILE), lambda i, j: (layer, 0, j)),
                  pl.BlockSpec((None, 3, FF_TILE), lambda i, j: (layer, 0, nff + j)),
                  pl.BlockSpec((None, 1, FF_TILE), lambda i, j: (layer, 0, j)),
                  pl.BlockSpec((None, 1, FF_TILE), lambda i, j: (layer, 0, nff + j)),
                  pl.BlockSpec((None, FF_TILE, D_MODEL), lambda i, j: (layer, j, 0)),
                  pl.BlockSpec((None, 1, D_MODEL), lambda i, j: (layer, 0, 0)),
                  pl.BlockSpec((None, 1, D_MODEL), lambda i, j: (layer, 0, 0))],
        out_specs=pl.BlockSpec((tm, D_MODEL), lambda i, j: (i, 0)),
        out_shape=jax.ShapeDtypeStruct((T, D_MODEL), F32),
        scratch_shapes=[pltpu.VMEM((tm, D_MODEL), BF16), pltpu.VMEM((tm, D_MODEL), F32),
                        pltpu.VMEM((tm, FF_TILE), BF16), pltpu.VMEM((tm, FF_TILE), BF16)],
        name="conv_ffn",
        compiler_params=_params("arbitrary", "arbitrary"),
    )(x, p, w_up, w_up, conv_w, conv_w, conv_b, conv_b, w_down, ln_g.reshape(DEPTH, 1, D_MODEL),
      ln_b.reshape(DEPTH, 1, D_MODEL))


def _s5_states_in(state_ssm, layer):
    B = state_ssm.shape[0]
    nlb = MIX_W // LANES
    h = state_ssm[:, layer].reshape(B, 2, nlb, S5_LBLK_GROUPS, SSM_STATE, 2)
    return jnp.transpose(h, (1, 2, 0, 5, 3, 4)).reshape(2, nlb, B, 2 * S5_LBLK_GROUPS * SSM_STATE)


def _s5_states_out(fin):
    nlb, B = fin.shape[1], fin.shape[2]
    h = fin.reshape(2, nlb, B, 2, S5_LBLK_GROUPS, SSM_STATE)
    return jnp.transpose(h, (2, 0, 1, 4, 5, 3)).reshape(B, 2, SSM_GROUPS, SSM_STATE, 2)


def _layer(x, p, lw, *, B, L, row0, rstride, latent, layer, extra):
    T = B * L
    log_gamma = jax.nn.log_sigmoid(lw['ret_decay'].astype(F32))
    if latent:
        z, = _inproj(x, p, lw['w_in'], layer=layer, L=L, row0=row0, rstride=rstride)
        z3 = z.reshape(B, L, IN_COLS)
        r_out = _retention(z3, log_gamma, B=B, L=L, rope_tabs=extra['rope'], s0=extra['state_ret'],
                           layer=layer)[0]
        n_out = _neighbourhood_attention(z3, extra['cache_k'], extra['cache_v'], extra['bias'][layer],
                                         B=B, L=L, layer=layer)
        y, _ = _s5(z, lw['s5'], _s5_states_in(extra['state_ssm'], layer), B=B, L=L)
        states = None
    else:
        ret_buf, k_buf, v_buf = extra
        z, k_buf, v_buf = _inproj(x, p, lw['w_in'], layer=layer, L=L, row0=row0, rstride=rstride,
                                  kv_bufs=(k_buf, v_buf))
        z3 = z.reshape(B, L, IN_COLS)
        r_out, ret_buf = _retention(z3, log_gamma, B=B, L=L, layer=layer, state_buf=ret_buf)
        n_out = _context_attention(z3, B=B, L=L)
        y, fin = _s5(z, lw['s5'], None, B=B, L=L)
        states = ((ret_buf, k_buf, v_buf), _s5_states_out(fin))
    x = _merge(x, p, z, r_out.reshape(T, MIX_W), y, n_out.reshape(T, MIX_W),
               lw['ssm_d'], lw['ssm_w_glu'], lw['w_branch'], lw['w_o'], lw['ln1_g'], lw['ln1_b'],
               layer=layer, L=L, row0=row0, rstride=rstride)
    x = _conv_ffn(x, p, lw['w_up'], lw['conv_w'], lw['conv_b'], lw['w_down'], lw['ln2_g'], lw['ln2_b'],
                  layer=layer, L=L, row0=row0, rstride=rstride)
    return x, states


def kernel(x_prompt, x_sample, state_ret, state_ssm, cache_na_k, cache_na_v, c, c_ctx, w_ada, b_ada, w_in, ret_decay, ssm_a_re, ssm_a_im, ssm_log_dt, ssm_b_re, ssm_b_im, ssm_c_re, ssm_c_im, ssm_d, ssm_w_glu, na_rpb, w_branch, w_o, ln1_g, ln1_b, w_up, conv_w, conv_b, w_down, ln2_g, ln2_b):
    B, L, _ = x_prompt.shape
    Bd, Ld, _ = x_sample.shape
    Lc = cache_na_k.shape[2]

    cond = jnp.concatenate([c_ctx[None, :], c, jnp.zeros((N_PAD_ROWS - 1 - Bd, D_MODEL), F32)], 0)
    p_all = _ada(cond, w_ada, b_ada)

    extra = dict(rope=_rope_tables(Ld), state_ret=state_ret, state_ssm=state_ssm,
                 cache_k=cache_na_k.reshape(Bd, DEPTH, Lc, MIX_W),
                 cache_v=cache_na_v.reshape(Bd, DEPTH, Lc, MIX_W),
                 bias=[_na_bias_blocks(na_rpb[l]) for l in range(DEPTH)])

    xp = x_prompt.reshape(B * L, D_MODEL)
    xs = x_sample.reshape(Bd * Ld, D_MODEL)
    bufs = (jnp.zeros((B, DEPTH, 2, N_RET_HEADS, RET_DK, RET_DK), F32),
            jnp.zeros((B, DEPTH, L, MIX_W), F32), jnp.zeros((B, DEPTH, L, MIX_W), F32))
    ssm_states = []
    for l in range(DEPTH):
        lw = dict(w_in=w_in, ret_decay=ret_decay[l], ssm_d=ssm_d, ssm_w_glu=ssm_w_glu,
                  w_branch=w_branch, w_o=w_o, ln1_g=ln1_g, ln1_b=ln1_b, w_up=w_up,
                  conv_w=conv_w, conv_b=conv_b, w_down=w_down, ln2_g=ln2_g, ln2_b=ln2_b,
                  s5=_s5_operators(ssm_a_re[l], ssm_a_im[l], ssm_log_dt[l], ssm_b_re[l], ssm_b_im[l],
                                   ssm_c_re[l], ssm_c_im[l]))
        xp, (bufs, s_ssm) = _layer(xp, p_all, lw, B=B, L=L, row0=0, rstride=0, latent=False, layer=l, extra=bufs)
        ssm_states.append(s_ssm)
        xs, _ = _layer(xs, p_all, lw, B=Bd, L=Ld, row0=1, rstride=1, latent=True, layer=l, extra=extra)
    ret_buf, k_buf, v_buf = bufs
    heads = (B, DEPTH, L, NA_HEADS, NA_HEAD_DIM)
    return (xp.reshape(B, L, D_MODEL), xs.reshape(Bd, Ld, D_MODEL),
            ret_buf, jnp.stack(ssm_states, 1), k_buf.reshape(heads), v_buf.reshape(heads))
```

```python
import functools

import jax
import jax.numpy as jnp
import numpy as np
from jax import lax
from jax.experimental import pallas as pl
from jax.experimental.pallas import tpu as pltpu

F32 = jnp.float32
BF16 = jnp.bfloat16

D_MODEL = 1024
DEPTH = 2
GRID_W = 64
MIX_W = D_MODEL // 2
N_RET_HEADS = 4
RET_DK = MIX_W // N_RET_HEADS
SSM_GROUP = 16
SSM_GROUPS = MIX_W // SSM_GROUP
SSM_STATE = 64
NA_HEADS = 8
NA_HEAD_DIM = MIX_W // NA_HEADS
NA_KR = 8
NA_KW = 16
D_FF = ((8 * D_MODEL // 3 + 127) // 128) * 128
ROPE_BASE = 10000.0
LN_EPS = 1e-5
NEG_INF = -1e30
DEEPNORM_ALPHA = (2 * DEPTH) ** 0.25
IN_COLS = 8 * MIX_W + 3 * D_MODEL

VMEM_LIMIT_BYTES = 56 * 1024 * 1024
LANES = 128

TOKEN_TILE = 1024
MERGE_TILE = 512
COL_TILE = 1024
SU_SECTION = 4
NK_SECTION = 6
FF_TILE = 256
RET_CHUNK = 256
RET_ROWS = 1024
S5_CHUNK = 8
S5_PITCH_PAD = 8
S5_LBLK_GROUPS = LANES // SSM_GROUP
N_PAD_ROWS = 8
CTX_ATTN_BATCH = 4
NA_CHUNK_ROWS = 4
NA_WIN_ROWS = 12


def _params(*sem):
    return pltpu.CompilerParams(dimension_semantics=sem, vmem_limit_bytes=VMEM_LIMIT_BYTES)


def _dot(a, b):
    return jnp.dot(a, b, preferred_element_type=F32)


def _dot_nt(a, b):
    return lax.dot_general(a, b, (((1,), (1,)), ((), ())), preferred_element_type=F32)


def _layer_norm(x, g, b):
    mu = jnp.mean(x, -1, keepdims=True)
    xc = x - mu
    var = jnp.mean(xc * xc, -1, keepdims=True)
    return xc * lax.rsqrt(var + LN_EPS) * g + b


def _ada_body(c_ref, w_ref, b_ref, o_ref):
    c = c_ref[...]
    s = c * jax.nn.sigmoid(c)
    o_ref[...] = _dot(s.astype(BF16), w_ref[...].astype(BF16)) + b_ref[...]


def _ada(cond, w_ada, b_ada):
    tn = 1024
    return pl.pallas_call(
        _ada_body,
        grid=(DEPTH, 6 * D_MODEL // tn),
        in_specs=[pl.BlockSpec((N_PAD_ROWS, D_MODEL), lambda l, j: (0, 0)),
                  pl.BlockSpec((None, D_MODEL, tn), lambda l, j: (l, 0, j)),
                  pl.BlockSpec((None, 1, tn), lambda l, j: (l, 0, j))],
        out_specs=pl.BlockSpec((None, N_PAD_ROWS, tn), lambda l, j: (l, 0, j)),
        out_shape=jax.ShapeDtypeStruct((DEPTH, N_PAD_ROWS, 6 * D_MODEL), F32),
        name="ada",
        compiler_params=_params("arbitrary", "arbitrary"),
    )(cond, w_ada, b_ada.reshape(DEPTH, 1, 6 * D_MODEL))


def _mod_row(p_ref, row, k):
    return p_ref[pl.ds(row, 1), k * D_MODEL:(k + 1) * D_MODEL]


def _kv_tile(n):
    col = (NK_SECTION + n) * MIX_W
    return col // COL_TILE, col % COL_TILE


def _inproj_body(x_ref, p_ref, w_ref, *rest, L, row0, rstride, n_kv):
    z_ref = rest[n_kv]
    kv_refs = rest[n_kv + 1:-2]
    h_scr, w_scr = rest[-2:]
    j = pl.program_id(0)
    i = pl.program_id(1)
    nb = x_ref.shape[0] // L

    @pl.when(j == 0)
    def _():
        for s in range(nb):
            row = row0 + rstride * (i * nb + s)
            sh = _mod_row(p_ref, row, 0)
            sc = _mod_row(p_ref, row, 1)
            h_scr[i, s * L:(s + 1) * L, :] = (x_ref[s * L:(s + 1) * L, :] * (1.0 + sc) + sh).astype(BF16)

    @pl.when(i == 0)
    def _():
        w_scr[...] = w_ref[...].astype(BF16)

    acc = _dot(h_scr[i], w_scr[...])
    z_ref[...] = acc.astype(BF16)

    for n, ref in enumerate(kv_refs):
        tile, off = _kv_tile(n)

        @pl.when(j == tile)
        def _(ref=ref, off=off):
            ref[...] = acc[:, off:off + MIX_W].reshape(ref.shape)


def _inproj(x, p, w_in, *, layer, L, row0, rstride, kv_bufs=()):
    T = x.shape[0]
    tm = TOKEN_TILE
    n_i = T // tm
    nb = tm // L
    n_kv = len(kv_bufs)
    body = functools.partial(_inproj_body, L=L, row0=row0, rstride=rstride, n_kv=n_kv)

    def only_at(tile):
        return lambda j, i: jnp.where(j < tile, 0, jnp.where(j > tile, n_i - 1, i))

    kv_i = [only_at(_kv_tile(n)[0]) for n in range(n_kv)]
    return pl.pallas_call(
        body,
        grid=(IN_COLS // COL_TILE, n_i),
        in_specs=[pl.BlockSpec((tm, D_MODEL), lambda j, i: (jnp.where(j == 0, i, n_i - 1), 0)),
                  pl.BlockSpec((None, N_PAD_ROWS, 6 * D_MODEL), lambda j, i: (layer, 0, 0)),
                  pl.BlockSpec((None, D_MODEL, COL_TILE), lambda j, i: (layer, 0, j))]
        + [pl.BlockSpec(memory_space=pl.ANY)] * n_kv,
        out_specs=[pl.BlockSpec((tm, COL_TILE), lambda j, i: (i, j))]
        + [pl.BlockSpec((nb, None, L, MIX_W), lambda j, i, f=f: (f(j, i), layer, 0, 0)) for f in kv_i],
        out_shape=[jax.ShapeDtypeStruct((T, IN_COLS), BF16)]
        + [jax.ShapeDtypeStruct(b.shape, b.dtype) for b in kv_bufs],
        input_output_aliases={3 + n: 1 + n for n in range(n_kv)},
        scratch_shapes=[pltpu.VMEM((n_i, tm, D_MODEL), BF16), pltpu.VMEM((D_MODEL, COL_TILE), BF16)],
        name="inproj",
        compiler_params=_params("arbitrary", "arbitrary"),
    )(x, p, w_in, *kv_bufs)


def _rope(x, cos, s_up, s_dn):
    return x * cos + pltpu.roll(x, 96, 1) * s_up + pltpu.roll(x, 32, 1) * s_dn


def _ret_body(*refs, n, rope, has_s0, want_state, layer):
    refs = list(refs)
    lg_ref, q_ref, k_ref, v_ref, g_ref = refs[:5]
    refs = refs[5:]
    if rope:
        cos_ref, sup_ref, sdn_ref = refs[:3]
        refs = refs[3:]
    if has_s0:
        s0_ref = refs[0]
        refs = refs[1:]
    if want_state:
        refs = refs[1:]
    o_ref = refs[0]
    refs = refs[1:]
    if want_state:
        st_ref = refs[0]
        refs = refs[1:]
    q_scr, k_scr, sb_scr, decay_scr = refs

    C = RET_CHUNK
    h = pl.program_id(0)
    lf = lg_ref[layer, 0, h]
    lb = lg_ref[layer, 1, h]

    @pl.when(pl.program_id(1) == 0)
    def _():
        ti = lax.broadcasted_iota(jnp.int32, (C, C), 0)
        si = lax.broadcasted_iota(jnp.int32, (C, C), 1)
        dlt = (ti - si).astype(F32)
        decay_scr[...] = (jnp.where(dlt >= 0, jnp.exp(lf * jnp.maximum(dlt, 0.0)), 0.0)
                          + jnp.where(dlt <= 0, jnp.exp(lb * jnp.maximum(-dlt, 0.0)), 0.0))

    tcol = lax.broadcasted_iota(jnp.int32, (C, 1), 0).astype(F32)
    qd_f = jnp.exp(lf * (tcol + 1.0))
    qd_b = jnp.exp(lb * (C - tcol))
    kd_f = jnp.exp(lf * (C - 1.0 - tcol))
    kd_b = jnp.exp(lb * tcol)
    cd_f = jnp.exp(lf * jnp.full((1, RET_DK), float(C), F32))
    cd_b = jnp.exp(lb * jnp.full((1, RET_DK), float(C), F32))

    def kv_outer(kc, vc, kd):
        return _dot((kc * kd).T.astype(BF16), vc)

    for bb in range(q_ref.shape[0]):
        q = q_ref[bb].astype(F32)
        k = k_ref[bb].astype(F32)
        if rope:
            q = _rope(q, cos_ref[...], sup_ref[...], sdn_ref[...])
            k = _rope(k, cos_ref[...], sup_ref[...], sdn_ref[...])
        q_scr[bb] = q
        k_scr[bb] = k * (RET_DK ** -0.5)

        s_b = s0_ref[bb, 1] if has_s0 else jnp.zeros((RET_DK, RET_DK), F32)
        for i in reversed(range(n)):
            sb_scr[bb, i] = s_b
            if i > 0 or want_state:
                s_b = s_b * cd_b + kv_outer(k_scr[bb, i * C:(i + 1) * C, :], v_ref[bb, i * C:(i + 1) * C, :], kd_b)

        s_f = s0_ref[bb, 0] if has_s0 else jnp.zeros((RET_DK, RET_DK), F32)
        for i in range(n):
            sl = slice(i * C, (i + 1) * C)
            qc = q_scr[bb, sl, :]
            kc = k_scr[bb, sl, :]
            vc = v_ref[bb, sl, :]
            att = _dot_nt(qc.astype(BF16), kc.astype(BF16)) * decay_scr[...]
            o = _dot(att.astype(BF16), vc)
            o = o + _dot((qc * qd_f).astype(BF16), s_f.astype(BF16))
            o = o + _dot((qc * qd_b).astype(BF16), sb_scr[bb, i].astype(BF16))
            mu = jnp.mean(o, -1, keepdims=True)
            oc = o - mu
            var = jnp.mean(oc * oc, -1, keepdims=True)
            gc = g_ref[bb, sl, :].astype(F32)
            o_ref[bb, sl, :] = (oc * lax.rsqrt(var + LN_EPS) * (gc * jax.nn.sigmoid(gc))).astype(BF16)
            if i < n - 1 or want_state:
                s_f = s_f * cd_f + kv_outer(kc, vc, kd_f)

        if want_state:
            st_ref[bb, 0] = s_f
            st_ref[bb, 1] = s_b


def _retention(z, log_gamma, *, B, L, rope_tabs=None, s0=None, layer=0, state_buf=None):
    want_state = state_buf is not None
    n = L // RET_CHUNK
    H = N_RET_HEADS
    nblk = MIX_W // RET_DK

    nbb = max(1, RET_ROWS // L)
    assert B % nbb == 0

    def sec(s):
        return pl.BlockSpec((nbb, L, RET_DK), lambda h, b: (b, 0, s * nblk + h))

    in_specs = [pl.BlockSpec(memory_space=pltpu.SMEM), sec(0), sec(1), sec(2), sec(3)]
    args = [log_gamma, z, z, z, z]
    if rope_tabs is not None:
        in_specs += [pl.BlockSpec((L, RET_DK), lambda h, b: (0, 0))] * 3
        args += list(rope_tabs)
    if s0 is not None:
        in_specs.append(pl.BlockSpec((nbb, None, 2, None, RET_DK, RET_DK), lambda h, b: (b, layer, 0, h, 0, 0)))
        args.append(s0)
    out_specs = [pl.BlockSpec((nbb, L, RET_DK), lambda h, b: (b, 0, h))]
    out_shape = [jax.ShapeDtypeStruct((B, L, MIX_W), BF16)]
    aliases = {}
    if want_state:
        aliases = {len(args): 1}
        in_specs.append(pl.BlockSpec(memory_space=pl.ANY))
        args.append(state_buf)
        out_specs.append(pl.BlockSpec((nbb, None, 2, None, RET_DK, RET_DK), lambda h, b: (b, layer, 0, h, 0, 0)))
        out_shape.append(jax.ShapeDtypeStruct(state_buf.shape, state_buf.dtype))
    body = functools.partial(_ret_body, n=n, rope=rope_tabs is not None, has_s0=s0 is not None,
                             want_state=want_state, layer=layer)
    return pl.pallas_call(
        body,
        grid=(H, B // nbb),
        in_specs=in_specs,
        out_specs=out_specs,
        out_shape=out_shape,
        input_output_aliases=aliases,
        scratch_shapes=[pltpu.VMEM((nbb, L, RET_DK), F32), pltpu.VMEM((nbb, L, RET_DK), F32),
                        pltpu.VMEM((nbb, n, RET_DK, RET_DK), F32), pltpu.VMEM((RET_CHUNK, RET_CHUNK), F32)],
        name="retention",
        compiler_params=_params("arbitrary", "arbitrary"),
    )(*args)


def _rope_tables(L):
    pos = jnp.arange(L)
    row = (pos // GRID_W).astype(F32)
    col = (pos % GRID_W).astype(F32)
    quarter = RET_DK // 4
    inv_freq = ROPE_BASE ** (-jnp.arange(quarter, dtype=F32) / quarter)
    ang_r = row[:, None] * inv_freq[None, :]
    ang_c = col[:, None] * inv_freq[None, :]
    zero = jnp.zeros_like(ang_r)
    cos = jnp.concatenate([jnp.cos(ang_r), jnp.cos(ang_r), jnp.cos(ang_c), jnp.cos(ang_c)], -1)
    s_up = jnp.concatenate([-jnp.sin(ang_r), zero, -jnp.sin(ang_c), zero], -1)
    s_dn = jnp.concatenate([zero, jnp.sin(ang_r), zero, jnp.sin(ang_c)], -1)
    return cos, s_up, s_dn


def _s5_body(*refs, B, nC, has_h0):
    refs = list(refs)
    u_ref, c0_ref, bs_ref, cp_ref, a8_ref = refs[:5]
    refs = refs[5:]
    if has_h0:
        h0_ref = refs[0]
        refs = refs[1:]
    y_ref, fin_ref, u_scr, a_scr, m_scr, bs_scr, cp_scr, c0_scr, s_scr, x_scr, y_scr = refs

    TC = S5_CHUNK
    R = B * nC
    P = nC + S5_PITCH_PAD
    nsl = s_scr.shape[0]
    half = nsl // 2
    ng = S5_LBLK_GROUPS
    sw = ng * SSM_STATE

    u_scr[...] = u_ref[...].astype(F32)
    for s in range(TC):
        a_scr[:, s * LANES:(s + 1) * LANES] = u_scr[pl.ds(s, R, stride=TC), :].astype(BF16)

    bs_scr[...] = jnp.zeros_like(bs_scr)
    cp_scr[...] = jnp.zeros_like(cp_scr)
    c0_scr[...] = jnp.zeros_like(c0_scr)

    for d in range(2):
        for g in range(ng):
            for part in range(2):
                cols = slice(part * sw + g * SSM_STATE, part * sw + (g + 1) * SSM_STATE)
                lo = (g % 2) * SSM_STATE
                c0_scr[g * SSM_GROUP:(g + 1) * SSM_GROUP, cols] = c0_ref[d, g, part, :, lo:lo + SSM_STATE]
                for s in range(TC):
                    rows = slice(s * LANES + g * SSM_GROUP, s * LANES + (g + 1) * SSM_GROUP)
                    bs_scr[rows, cols] = bs_ref[d, s, g, part, :, lo:lo + SSM_STATE]
                    cp_scr[rows, cols] = cp_ref[d, s, g, part, :, lo:lo + SSM_STATE]

        lag = _dot_nt(bs_scr[...], c0_scr[...]).astype(BF16)
        for s in range(TC):
            for t in range(TC):
                k = (t - s) if d == 0 else (s - t)
                src = (TC - 1 - k) if d == 0 else k
                blk = lag[src * LANES:(src + 1) * LANES, :] if k >= 0 else jnp.zeros((LANES, LANES), BF16)
                m_scr[s * LANES:(s + 1) * LANES, t * LANES:(t + 1) * LANES] = blk

        a = a_scr[...]
        yd = _dot(a, m_scr[...])
        if d == 0:
            y_scr[...] = yd
        else:
            y_scr[...] += yd

        sm = _dot(a, bs_scr[...])
        for b in range(B):
            for sl in range(nsl):
                s_scr[sl, b * P:b * P + nC, :] = sm[b * nC:(b + 1) * nC, sl * LANES:(sl + 1) * LANES]

        a_r = [jnp.broadcast_to(a8_ref[d, 0, :, q * LANES:(q + 1) * LANES], (B, LANES)) for q in range(half)]
        a_i = [jnp.broadcast_to(a8_ref[d, 1, :, q * LANES:(q + 1) * LANES], (B, LANES)) for q in range(half)]
        if has_h0:
            init = tuple(h0_ref[d, :, sl * LANES:(sl + 1) * LANES] for sl in range(nsl))
        else:
            init = tuple(jnp.zeros((B, LANES), F32) for _ in range(nsl))

        def step(j, carry, d=d, a_r=a_r, a_i=a_i):
            c = j if d == 0 else nC - 1 - j
            rows = pl.ds(c, B, stride=P)
            new_r, new_i = [], []
            for q in range(half):
                xr, xi = carry[q], carry[half + q]
                sr = s_scr[q, rows, :]
                si = s_scr[half + q, rows, :]
                s_scr[q, rows, :] = xr
                s_scr[half + q, rows, :] = xi
                new_r.append(a_r[q] * xr - a_i[q] * xi + sr)
                new_i.append(a_r[q] * xi + a_i[q] * xr + si)
            return tuple(new_r + new_i)

        fin = lax.fori_loop(0, nC, step, init)
        for sl in range(nsl):
            fin_ref[d, :, sl * LANES:(sl + 1) * LANES] = fin[sl]

        for b in range(B):
            for sl in range(nsl):
                x_scr[b * nC:(b + 1) * nC, sl * LANES:(sl + 1) * LANES] = \
                    s_scr[sl, b * P:b * P + nC, :].astype(BF16)
        y_scr[...] += _dot_nt(x_scr[...], cp_scr[...])

    for t in range(TC):
        y_ref[pl.ds(t, R, stride=TC), :] = y_scr[:, t * LANES:(t + 1) * LANES]


def _s5(z, ops, h0, *, B, L, layer):
    c0, bs, cp, a8 = ops
    T = B * L
    nC = L // S5_CHUNK
    nlb = MIX_W // LANES
    sc = 2 * S5_LBLK_GROUPS * SSM_STATE
    kc = S5_CHUNK * LANES
    su0 = SU_SECTION * MIX_W // LANES

    blocks = pl.BlockSpec((None, 2, None, S5_CHUNK, S5_LBLK_GROUPS, 2, SSM_GROUP, LANES),
                          lambda lb: (layer, 0, lb, 0, 0, 0, 0, 0))
    in_specs = [pl.BlockSpec((T, LANES), lambda lb: (0, su0 + lb)),
                pl.BlockSpec((None, 2, None, S5_LBLK_GROUPS, 2, SSM_GROUP, LANES),
                             lambda lb: (layer, 0, lb, 0, 0, 0, 0)),
                blocks, blocks,
                pl.BlockSpec((None, 2, None, 2, 1, sc // 2), lambda lb: (layer, 0, lb, 0, 0, 0))]
    args = [z, c0, bs, cp, a8]
    if h0 is not None:
        in_specs.append(pl.BlockSpec((2, None, B, sc), lambda lb: (0, lb, 0, 0)))
        args.append(h0)
    body = functools.partial(_s5_body, B=B, nC=nC, has_h0=h0 is not None)
    return pl.pallas_call(
        body,
        grid=(nlb,),
        in_specs=in_specs,
        out_specs=[pl.BlockSpec((T, LANES), lambda lb: (0, lb)),
                   pl.BlockSpec((2, None, B, sc), lambda lb: (0, lb, 0, 0))],
        out_shape=[jax.ShapeDtypeStruct((T, MIX_W), F32),
                   jax.ShapeDtypeStruct((2, nlb, B, sc), F32)],
        scratch_shapes=[pltpu.VMEM((T, LANES), F32), pltpu.VMEM((B * nC, kc), BF16),
                        pltpu.VMEM((kc, kc), BF16), pltpu.VMEM((kc, sc), BF16), pltpu.VMEM((kc, sc), BF16),
                        pltpu.VMEM((LANES, sc), BF16),
                        pltpu.VMEM((sc // LANES, B * (nC + S5_PITCH_PAD), LANES), F32),
                        pltpu.VMEM((B * nC, sc), BF16), pltpu.VMEM((B * nC, kc), F32)],
        name="s5",
        compiler_params=_params("arbitrary"),
    )(*args)


def _s5_operators(a_re, a_im, log_dt, b_re, b_im, c_re, c_im):
    TC = S5_CHUNK
    nlb = MIX_W // LANES
    ng = S5_LBLK_GROUPS
    lr = jnp.minimum(a_re, -1e-4)
    li = a_im
    dt = jnp.exp(log_dt)[..., None]
    k = jnp.arange(TC + 1, dtype=F32)[:, None, None, None]
    mag = jnp.exp(k * (lr * dt)[None])
    pr = mag * jnp.cos(k * (li * dt)[None])
    pi = mag * jnp.sin(k * (li * dt)[None])
    ar, ai = pr[1], pi[1]
    den = lr * lr + li * li
    sr = ((ar - 1.0) * lr + ai * li) / den
    si = (ai * lr - (ar - 1.0) * li) / den
    bbr = sr[..., None] * b_re[None] - si[..., None] * b_im[None]
    bbi = sr[..., None] * b_im[None] + si[..., None] * b_re[None]

    def lanes2(x):
        return jnp.concatenate([x, x], -1)

    def powers(fwd, bwd):
        x = lanes2(jnp.stack([fwd, bwd], 0)).reshape(2, TC, nlb, ng, 1, LANES)
        return jnp.swapaxes(x, 1, 2)

    def per_group(x):
        return lanes2(x).reshape(2, nlb, 1, ng, SSM_GROUP, LANES)

    er = powers(jnp.flip(pr[:TC, 0], 0), pr[:TC, 1])
    ei = powers(jnp.flip(pi[:TC, 0], 0), pi[:TC, 1])
    btr = per_group(jnp.swapaxes(bbr, -1, -2))
    bti = per_group(jnp.swapaxes(bbi, -1, -2))
    bs = jnp.stack([er * btr - ei * bti, er * bti + ei * btr], 4).astype(BF16)

    fr = powers(pr[1:, 0], jnp.flip(pr[1:, 1], 0))
    fi = powers(pi[1:, 0], jnp.flip(pi[1:, 1], 0))
    ctr = per_group(c_re)
    cti = per_group(c_im)
    cp = jnp.stack([ctr * fr - cti * fi, -(ctr * fi + cti * fr)], 4).astype(BF16)
    c0 = jnp.stack([ctr, -cti], 4)[:, :, 0].astype(BF16)

    sw = ng * SSM_STATE
    a8 = jnp.stack([pr[TC].reshape(2, nlb, 1, sw), pi[TC].reshape(2, nlb, 1, sw)], 2)
    return c0, bs, cp, a8


def _head_masks(shape):
    lane = lax.broadcasted_iota(jnp.int32, shape, 1)
    return lane < NA_HEAD_DIM


def _cattn_body(q_ref, k_ref, v_ref, o_ref):
    first = _head_masks(q_ref.shape[1:])
    for bb in range(q_ref.shape[0]):
        q = q_ref[bb]
        k = k_ref[bb]
        v = v_ref[bb]
        outs = []
        for e in range(2):
            qe = jnp.where(first if e == 0 else jnp.logical_not(first), q, jnp.zeros_like(q))
            s = _dot_nt(qe, k) * (NA_HEAD_DIM ** -0.5)
            m = jnp.max(s, -1, keepdims=True)
            p = jnp.exp(s - m)
            l = jnp.sum(p, -1, keepdims=True)
            outs.append(_dot(p.astype(BF16), v) / l)
        o_ref[bb] = jnp.where(first, outs[0], outs[1]).astype(BF16)


def _context_attention(z, *, B, L):
    nblk = MIX_W // LANES
    nb = CTX_ATTN_BATCH

    def sec(s):
        return pl.BlockSpec((nb, L, LANES), lambda b, hp: (b, 0, s * nblk + hp))

    return pl.pallas_call(
        _cattn_body,
        grid=(B // nb, nblk),
        in_specs=[sec(5), sec(6), sec(7)],
        out_specs=pl.BlockSpec((nb, L, LANES), lambda b, hp: (b, 0, hp)),
        out_shape=jax.ShapeDtypeStruct((B, L, MIX_W), BF16),
        name="ctx_attention",
        compiler_params=_params("arbitrary", "arbitrary"),
    )(z, z, z)


def _na_chunks(rows):
    half = NA_KR // 2
    plan, kinds = [], []
    for r0 in range(0, rows, NA_CHUNK_ROWS):
        rs = [min(max(r - half, 0), rows - NA_KR) for r in range(r0, r0 + NA_CHUNK_ROWS)]
        ws = min(rs[0], rows - NA_WIN_ROWS)
        assert rs[-1] + NA_KR <= ws + NA_WIN_ROWS
        kind = tuple((r0 + n - ws, rs[n] - ws) for n in range(NA_CHUNK_ROWS))
        if kind not in kinds:
            kinds.append(kind)
        plan.append((ws, kinds.index(kind)))
    return plan, kinds


def _na_body(q_ref, k_ref, v_ref, kc_ref, vc_ref, tb_ref, o_ref, bias_scr, *, rows):
    scale = NA_HEAD_DIM ** -0.5
    nq = NA_CHUNK_ROWS * GRID_W
    plan, kinds = _na_chunks(rows)
    n_off = 2 * NA_KR - 1

    @pl.when(pl.program_id(1) == 0)
    def _():
        for t, kind in enumerate(kinds):
            for e in range(2):
                for n, (r_rel, rs_rel) in enumerate(kind):
                    for kj in range(NA_WIN_ROWS):
                        off = kj - r_rel + NA_KR - 1 if rs_rel <= kj < rs_rel + NA_KR else n_off
                        lo = (kj % 2) * GRID_W
                        bias_scr[t, e * nq + n * GRID_W:e * nq + (n + 1) * GRID_W, kj * GRID_W:(kj + 1) * GRID_W] = \
                            tb_ref[e, off, :, lo:lo + GRID_W]

    kctx = kc_ref[...].astype(BF16)
    vctx = vc_ref[...].astype(BF16)
    first = _head_masks((nq, LANES))
    for c, (ws, kind) in enumerate(plan):
        qc = q_ref[c * nq:(c + 1) * nq, :]
        qs = jnp.concatenate([jnp.where(first, qc, jnp.zeros_like(qc)),
                              jnp.where(first, jnp.zeros_like(qc), qc)], 0)
        kw = k_ref[ws * GRID_W:(ws + NA_WIN_ROWS) * GRID_W, :]
        vw = v_ref[ws * GRID_W:(ws + NA_WIN_ROWS) * GRID_W, :]
        s_loc = _dot_nt(qs, kw) * scale + bias_scr[kind]
        s_ctx = _dot_nt(qs, kctx) * scale
        m = jnp.maximum(jnp.max(s_loc, -1, keepdims=True), jnp.max(s_ctx, -1, keepdims=True))
        p_loc = jnp.exp(s_loc - m)
        p_ctx = jnp.exp(s_ctx - m)
        l = jnp.sum(p_loc, -1, keepdims=True) + jnp.sum(p_ctx, -1, keepdims=True)
        o = (_dot(p_loc.astype(BF16), vw) + _dot(p_ctx.astype(BF16), vctx)) / l
        o_ref[c * nq:(c + 1) * nq, :] = jnp.where(first, o[:nq], o[nq:]).astype(BF16)


def _neighbourhood_attention(z, cache_k, cache_v, blocks, *, B, L, layer):
    nblk = MIX_W // LANES
    rows = L // GRID_W
    Lc = cache_k.shape[2]
    _, kinds = _na_chunks(rows)

    def sec(s):
        return pl.BlockSpec((None, L, LANES), lambda hp, b: (b, 0, s * nblk + hp))

    ctx = pl.BlockSpec((None, None, Lc, LANES), lambda hp, b: (b, layer, 0, hp))
    return pl.pallas_call(
        functools.partial(_na_body, rows=rows),
        grid=(nblk, B),
        in_specs=[sec(5), sec(6), sec(7), ctx, ctx,
                  pl.BlockSpec((None, None, 2, 2 * NA_KR, GRID_W, 2 * GRID_W),
                               lambda hp, b: (layer, hp, 0, 0, 0, 0))],
        out_specs=pl.BlockSpec((None, L, LANES), lambda hp, b: (b, 0, hp)),
        out_shape=jax.ShapeDtypeStruct((B, L, MIX_W), BF16),
        scratch_shapes=[pltpu.VMEM((len(kinds), 2 * NA_CHUNK_ROWS * GRID_W, NA_WIN_ROWS * GRID_W), F32)],
        name="nbr_attention",
        compiler_params=_params("arbitrary", "arbitrary"),
    )(z, z, z, cache_k, cache_v, blocks)


def _na_bias_blocks(rpb):
    nr, nc = 2 * NA_KR - 1, 2 * NA_KW - 1
    qc = np.arange(GRID_W)
    kc = np.arange(GRID_W)
    ws = np.clip(qc - NA_KW // 2, 0, GRID_W - NA_KW)
    col_ok = (kc[None, :] >= ws[:, None]) & (kc[None, :] < ws[:, None] + NA_KW)
    coff = np.clip(kc[None, :] - qc[:, None] + NA_KW - 1, 0, nc - 1)
    sel_c = ((coff[None] == np.arange(nc)[:, None, None]) & col_ok[None]).astype(np.float32)
    sel_c = np.concatenate([sel_c, sel_c], -1)
    ok = np.concatenate([col_ok, col_ok], -1)[None] & (np.arange(nr + 1) < nr)[:, None, None]
    H = rpb.shape[0]
    rows = jnp.pad(rpb.astype(F32), ((0, 0), (0, 1), (0, 0)))
    t = jnp.einsum('hrc,cqk->hrqk', rows, sel_c, precision=lax.Precision.HIGHEST)
    return jnp.where(ok[None], t, NEG_INF).reshape(H // 2, 2, nr + 1, GRID_W, 2 * GRID_W)


def _merge_body(x_ref, p_ref, r_ref, u_ref, y_ref, n_ref, ga_ref, gb_ref, gc_ref,
                d_ref, wglu_ref, wbr_ref, wo_ref, lg_ref, lb_ref, o_ref,
                wglu_s, wbr_s, wo_s, *, L, row0, rstride):
    i = pl.program_id(0)
    tm = x_ref.shape[0]

    @pl.when(i == 0)
    def _():
        wglu_s[...] = wglu_ref[...].astype(BF16)
        wbr_s[...] = wbr_ref[...].astype(BF16)
        wo_s[...] = wo_ref[...].astype(BF16)

    row = row0 + rstride * ((i * tm) // L)
    g1 = _mod_row(p_ref, row, 2)

    y = d_ref[...] * u_ref[...].astype(F32) + y_ref[...]
    y = jax.nn.gelu(y)
    s_out = y * jax.nn.sigmoid(_dot(y.astype(BF16), wglu_s[...]))

    def gate(ref):
        return jax.nn.sigmoid(ref[...].astype(F32))

    merged = (gate(ga_ref) * _dot(r_ref[...], wbr_s[0])
              + gate(gb_ref) * _dot(s_out.astype(BF16), wbr_s[1])
              + gate(gc_ref) * _dot(n_ref[...], wbr_s[2]))
    m = _dot(merged.astype(BF16), wo_s[...])
    o_ref[...] = _layer_norm(DEEPNORM_ALPHA * x_ref[...] + g1 * m, lg_ref[...], lb_ref[...])


def _merge(x, p, z, r_out, y, n_out, ssm_d, w_glu, w_branch, w_o, ln_g, ln_b, *, layer, L, row0, rstride):
    T = x.shape[0]
    tm = MERGE_TILE
    assert L % tm == 0 or rstride == 0
    gate0 = 8 * MIX_W // D_MODEL

    def tok(w):
        return pl.BlockSpec((tm, w), lambda i: (i, 0))

    def full(shape):
        return pl.BlockSpec((None,) + shape, lambda i: (layer,) + (0,) * len(shape))

    body = functools.partial(_merge_body, L=L, row0=row0, rstride=rstride)
    return pl.pallas_call(
        body,
        grid=(T // tm,),
        in_specs=[tok(D_MODEL), full((N_PAD_ROWS, 6 * D_MODEL)), tok(MIX_W),
                  pl.BlockSpec((tm, MIX_W), lambda i: (i, SU_SECTION)), tok(MIX_W), tok(MIX_W),
                  pl.BlockSpec((tm, D_MODEL), lambda i: (i, gate0)),
                  pl.BlockSpec((tm, D_MODEL), lambda i: (i, gate0 + 1)),
                  pl.BlockSpec((tm, D_MODEL), lambda i: (i, gate0 + 2)),
                  full((1, MIX_W)), full((MIX_W, MIX_W)), full((3, MIX_W, D_MODEL)),
                  full((D_MODEL, D_MODEL)), full((1, D_MODEL)), full((1, D_MODEL))],
        out_specs=tok(D_MODEL),
        out_shape=jax.ShapeDtypeStruct((T, D_MODEL), F32),
        scratch_shapes=[pltpu.VMEM((MIX_W, MIX_W), BF16), pltpu.VMEM((3, MIX_W, D_MODEL), BF16),
                        pltpu.VMEM((D_MODEL, D_MODEL), BF16)],
        name="merge",
        compiler_params=_params("arbitrary"),
    )(x, p, r_out, z, y, n_out, z, z, z, ssm_d.reshape(DEPTH, 1, MIX_W), w_glu, w_branch, w_o,
      ln_g.reshape(DEPTH, 1, D_MODEL), ln_b.reshape(DEPTH, 1, D_MODEL))


def _ffn_body(x_ref, p_ref, wa_ref, wb_ref, cwa_ref, cwb_ref, cba_ref, cbb_ref, wd_ref, lg_ref, lb_ref,
              o_ref, h_scr, acc_scr, mp_scr, mn_scr, *, L, row0, rstride):
    i = pl.program_id(0)
    j = pl.program_id(1)
    tm = x_ref.shape[0]
    nb = tm // L

    @pl.when(j == 0)
    def _():
        for s in range(nb):
            row = row0 + rstride * (i * nb + s)
            sh = _mod_row(p_ref, row, 3)
            sc = _mod_row(p_ref, row, 4)
            h_scr[s * L:(s + 1) * L, :] = (x_ref[s * L:(s + 1) * L, :] * (1.0 + sc) + sh).astype(BF16)
        acc_scr[...] = jnp.zeros_like(acc_scr)
        t = lax.broadcasted_iota(jnp.int32, (tm, FF_TILE), 0) % L
        mp_scr[...] = (t != 0).astype(BF16)
        mn_scr[...] = (t != L - 1).astype(BF16)

    def conv(w_ref, cw_ref, cb_ref):
        zc = _dot(h_scr[...], w_ref[...].astype(BF16))
        zp = pltpu.roll(zc, 1, 0).astype(BF16) * mp_scr[...]
        zn = pltpu.roll(zc, tm - 1, 0).astype(BF16) * mn_scr[...]
        cw = cw_ref[...].astype(BF16)
        return zp * cw[0:1, :] + zc.astype(BF16) * cw[1:2, :] + zn * cw[2:3, :] + cb_ref[...].astype(BF16)

    a = conv(wa_ref, cwa_ref, cba_ref)
    b = conv(wb_ref, cwb_ref, cbb_ref)
    acc_scr[...] += _dot(jax.nn.gelu(a) * b, wd_ref[...].astype(BF16))

    @pl.when(j == pl.num_programs(1) - 1)
    def _():
        for s in range(nb):
            row = row0 + rstride * (i * nb + s)
            g2 = _mod_row(p_ref, row, 5)
            sl = slice(s * L, (s + 1) * L)
            o_ref[sl, :] = _layer_norm(DEEPNORM_ALPHA * x_ref[sl, :] + g2 * acc_scr[sl, :],
                                       lg_ref[...], lb_ref[...])


def _conv_ffn(x, p, w_up, conv_w, conv_b, w_down, ln_g, ln_b, *, layer, L, row0, rstride):
    T = x.shape[0]
    tm = TOKEN_TILE
    nff = D_FF // FF_TILE
    body = functools.partial(_ffn_body, L=L, row0=row0, rstride=rstride)
    conv_b = conv_b.reshape(DEPTH, 1, 2 * D_FF)
    return pl.pallas_call(
        body,
        grid=(T // tm, nff),
        in_specs=[pl.BlockSpec((tm, D_MODEL), lambda i, j: (i, 0)),
                  pl.BlockSpec((None, N_PAD_ROWS, 6 * D_MODEL), lambda i, j: (layer, 0, 0)),
                  pl.BlockSpec((None, D_MODEL, FF_TILE), lambda i, j: (layer, 0, j)),
                  pl.BlockSpec((None, D_MODEL, FF_TILE), lambda i, j: (layer, 0, nff + j)),
                  pl.BlockSpec((None, 3, FF_TILE), lambda i, j: (layer, 0, j)),
                  pl.BlockSpec((None, 3, FF_TILE), lambda i, j: (layer, 0, nff + j)),
                  pl.BlockSpec((None, 1, FF_TILE), lambda i, j: (layer, 0, j)),
                  pl.BlockSpec((None, 1, FF_TILE), lambda i, j: (layer, 0, nff + j)),
                  pl.BlockSpec((None, FF_TILE, D_MODEL), lambda i, j: (layer, j, 0)),
                  pl.BlockSpec((None, 1, D_MODEL), lambda i, j: (layer, 0, 0)),
                  pl.BlockSpec((None, 1, D_MODEL), lambda i, j: (layer, 0, 0))],
        out_specs=pl.BlockSpec((tm, D_MODEL), lambda i, j: (i, 0)),
        out_shape=jax.ShapeDtypeStruct((T, D_MODEL), F32),
        scratch_shapes=[pltpu.VMEM((tm, D_MODEL), BF16), pltpu.VMEM((tm, D_MODEL), F32),
                        pltpu.VMEM((tm, FF_TILE), BF16), pltpu.VMEM((tm, FF_TILE), BF16)],
        name="conv_ffn",
        compiler_params=_params("arbitrary", "arbitrary"),
    )(x, p, w_up, w_up, conv_w, conv_w, conv_b, conv_b, w_down, ln_g.reshape(DEPTH, 1, D_MODEL),
      ln_b.reshape(DEPTH, 1, D_MODEL))


def _s5_states_in(state_ssm, layer):
    B = state_ssm.shape[0]
    nlb = MIX_W // LANES
    h = state_ssm[:, layer].reshape(B, 2, nlb, S5_LBLK_GROUPS, SSM_STATE, 2)
    return jnp.transpose(h, (1, 2, 0, 5, 3, 4)).reshape(2, nlb, B, 2 * S5_LBLK_GROUPS * SSM_STATE)


def _s5_states_out(fin):
    nlb, B = fin.shape[1], fin.shape[2]
    h = fin.reshape(2, nlb, B, 2, S5_LBLK_GROUPS, SSM_STATE)
    return jnp.transpose(h, (2, 0, 1, 4, 5, 3)).reshape(B, 2, SSM_GROUPS, SSM_STATE, 2)


def _layer(x, p, lw, *, B, L, row0, rstride, latent, layer, extra):
    T = B * L
    log_gamma = lw['log_gamma']
    if latent:
        z, = _inproj(x, p, lw['w_in'], layer=layer, L=L, row0=row0, rstride=rstride)
        z3 = z.reshape(B, L, IN_COLS)
        r_out = _retention(z3, log_gamma, B=B, L=L, rope_tabs=extra['rope'], s0=extra['state_ret'],
                           layer=layer)[0]
        n_out = _neighbourhood_attention(z3, extra['cache_k'], extra['cache_v'], extra['bias'],
                                         B=B, L=L, layer=layer)
        y, _ = _s5(z, lw['s5'], _s5_states_in(extra['state_ssm'], layer), B=B, L=L, layer=layer)
        states = None
    else:
        ret_buf, k_buf, v_buf = extra
        z, k_buf, v_buf = _inproj(x, p, lw['w_in'], layer=layer, L=L, row0=row0, rstride=rstride,
                                  kv_bufs=(k_buf, v_buf))
        z3 = z.reshape(B, L, IN_COLS)
        r_out, ret_buf = _retention(z3, log_gamma, B=B, L=L, layer=layer, state_buf=ret_buf)
        n_out = _context_attention(z3, B=B, L=L)
        y, fin = _s5(z, lw['s5'], None, B=B, L=L, layer=layer)
        states = ((ret_buf, k_buf, v_buf), _s5_states_out(fin))
    x = _merge(x, p, z, r_out.reshape(T, MIX_W), y, n_out.reshape(T, MIX_W),
               lw['ssm_d'], lw['ssm_w_glu'], lw['w_branch'], lw['w_o'], lw['ln1_g'], lw['ln1_b'],
               layer=layer, L=L, row0=row0, rstride=rstride)
    x = _conv_ffn(x, p, lw['w_up'], lw['conv_w'], lw['conv_b'], lw['w_down'], lw['ln2_g'], lw['ln2_b'],
                  layer=layer, L=L, row0=row0, rstride=rstride)
    return x, states


def kernel(x_prompt, x_sample, state_ret, state_ssm, cache_na_k, cache_na_v, c, c_ctx, w_ada, b_ada, w_in, ret_decay, ssm_a_re, ssm_a_im, ssm_log_dt, ssm_b_re, ssm_b_im, ssm_c_re, ssm_c_im, ssm_d, ssm_w_glu, na_rpb, w_branch, w_o, ln1_g, ln1_b, w_up, conv_w, conv_b, w_down, ln2_g, ln2_b):
    B, L, _ = x_prompt.shape
    Bd, Ld, _ = x_sample.shape
    Lc = cache_na_k.shape[2]

    cond = jnp.concatenate([c_ctx[None, :], c, jnp.zeros((N_PAD_ROWS - 1 - Bd, D_MODEL), F32)], 0)
    p_all = _ada(cond, w_ada, b_ada)

    extra = dict(rope=_rope_tables(Ld), state_ret=state_ret, state_ssm=state_ssm,
                 cache_k=cache_na_k.reshape(Bd, DEPTH, Lc, MIX_W),
                 cache_v=cache_na_v.reshape(Bd, DEPTH, Lc, MIX_W),
                 bias=jax.vmap(_na_bias_blocks)(na_rpb))

    xp = x_prompt.reshape(B * L, D_MODEL)
    xs = x_sample.reshape(Bd * Ld, D_MODEL)
    bufs = (jnp.zeros((B, DEPTH, 2, N_RET_HEADS, RET_DK, RET_DK), F32),
            jnp.zeros((B, DEPTH, L, MIX_W), F32), jnp.zeros((B, DEPTH, L, MIX_W), F32))
    ssm_states = []
    lw = dict(w_in=w_in, log_gamma=jax.nn.log_sigmoid(ret_decay.astype(F32)), ssm_d=ssm_d, ssm_w_glu=ssm_w_glu,
              w_branch=w_branch, w_o=w_o, ln1_g=ln1_g, ln1_b=ln1_b, w_up=w_up,
              conv_w=conv_w, conv_b=conv_b, w_down=w_down, ln2_g=ln2_g, ln2_b=ln2_b,
              s5=jax.vmap(_s5_operators)(ssm_a_re, ssm_a_im, ssm_log_dt, ssm_b_re, ssm_b_im, ssm_c_re, ssm_c_im))
    for l in range(DEPTH):
        xp, (bufs, s_ssm) = _layer(xp, p_all, lw, B=B, L=L, row0=0, rstride=0, latent=False, layer=l, extra=bufs)
        ssm_states.append(s_ssm)
        xs, _ = _layer(xs, p_all, lw, B=Bd, L=Ld, row0=1, rstride=1, latent=True, layer=l, extra=extra)
    ret_buf, k_buf, v_buf = bufs
    heads = (B, DEPTH, L, NA_HEADS, NA_HEAD_DIM)
    return (xp.reshape(B, L, D_MODEL), xs.reshape(Bd, Ld, D_MODEL),
            ret_buf, jnp.stack(ssm_states, 1), k_buf.reshape(heads), v_buf.reshape(heads))
```

```python
import functools

import jax
import jax.numpy as jnp
import numpy as np
from jax import lax
from jax.experimental import pallas as pl
from jax.experimental.pallas import tpu as pltpu

F32 = jnp.float32
BF16 = jnp.bfloat16

D_MODEL = 1024
DEPTH = 2
GRID_W = 64
MIX_W = D_MODEL // 2
N_RET_HEADS = 4
RET_DK = MIX_W // N_RET_HEADS
SSM_GROUP = 16
SSM_GROUPS = MIX_W // SSM_GROUP
SSM_STATE = 64
NA_HEADS = 8
NA_HEAD_DIM = MIX_W // NA_HEADS
NA_KR = 8
NA_KW = 16
D_FF = ((8 * D_MODEL // 3 + 127) // 128) * 128
ROPE_BASE = 10000.0
LN_EPS = 1e-5
NEG_INF = -1e30
DEEPNORM_ALPHA = (2 * DEPTH) ** 0.25
IN_COLS = 8 * MIX_W + 3 * D_MODEL

VMEM_LIMIT_BYTES = 56 * 1024 * 1024
LANES = 128

TOKEN_TILE = 1024
MERGE_TILE = 512
COL_TILE = 1024
SU_SECTION = 4
NK_SECTION = 6
FF_TILE = 256
RET_CHUNK = 256
RET_ROWS = 1024
S5_CHUNK = 8
S5_PITCH_PAD = 8
S5_LBLK_GROUPS = LANES // SSM_GROUP
N_PAD_ROWS = 8
CTX_ATTN_BATCH = 4
NA_CHUNK_ROWS = 2
NA_WIN_ROWS = 10


def _params(*sem):
    return pltpu.CompilerParams(dimension_semantics=sem, vmem_limit_bytes=VMEM_LIMIT_BYTES)


def _dot(a, b):
    return jnp.dot(a, b, preferred_element_type=F32)


def _dot_nt(a, b):
    return lax.dot_general(a, b, (((1,), (1,)), ((), ())), preferred_element_type=F32)


def _layer_norm(x, g, b):
    mu = jnp.mean(x, -1, keepdims=True)
    xc = x - mu
    var = jnp.mean(xc * xc, -1, keepdims=True)
    return xc * lax.rsqrt(var + LN_EPS) * g + b


def _ada_body(c_ref, w_ref, b_ref, o_ref):
    c = c_ref[...]
    s = c * jax.nn.sigmoid(c)
    o_ref[...] = _dot(s.astype(BF16), w_ref[...].astype(BF16)) + b_ref[...]


def _ada(cond, w_ada, b_ada):
    tn = 1024
    return pl.pallas_call(
        _ada_body,
        grid=(DEPTH, 6 * D_MODEL // tn),
        in_specs=[pl.BlockSpec((N_PAD_ROWS, D_MODEL), lambda l, j: (0, 0)),
                  pl.BlockSpec((None, D_MODEL, tn), lambda l, j: (l, 0, j)),
                  pl.BlockSpec((None, 1, tn), lambda l, j: (l, 0, j))],
        out_specs=pl.BlockSpec((None, N_PAD_ROWS, tn), lambda l, j: (l, 0, j)),
        out_shape=jax.ShapeDtypeStruct((DEPTH, N_PAD_ROWS, 6 * D_MODEL), F32),
        name="ada",
        compiler_params=_params("arbitrary", "arbitrary"),
    )(cond, w_ada, b_ada.reshape(DEPTH, 1, 6 * D_MODEL))


def _mod_row(p_ref, row, k):
    return p_ref[pl.ds(row, 1), k * D_MODEL:(k + 1) * D_MODEL]


def _kv_tile(n):
    col = (NK_SECTION + n) * MIX_W
    return col // COL_TILE, col % COL_TILE


def _inproj_body(x_ref, p_ref, w_ref, *rest, L, row0, rstride, n_kv):
    z_ref = rest[n_kv]
    kv_refs = rest[n_kv + 1:-2]
    h_scr, w_scr = rest[-2:]
    j = pl.program_id(0)
    i = pl.program_id(1)
    nb = x_ref.shape[0] // L

    @pl.when(j == 0)
    def _():
        for s in range(nb):
            row = row0 + rstride * (i * nb + s)
            sh = _mod_row(p_ref, row, 0)
            sc = _mod_row(p_ref, row, 1)
            h_scr[i, s * L:(s + 1) * L, :] = (x_ref[s * L:(s + 1) * L, :] * (1.0 + sc) + sh).astype(BF16)

    @pl.when(i == 0)
    def _():
        w_scr[...] = w_ref[...].astype(BF16)

    acc = _dot(h_scr[i], w_scr[...])
    z_ref[...] = acc.astype(BF16)

    for n, ref in enumerate(kv_refs):
        tile, off = _kv_tile(n)

        @pl.when(j == tile)
        def _(ref=ref, off=off):
            ref[...] = acc[:, off:off + MIX_W].reshape(ref.shape)


def _inproj(x, p, w_in, *, layer, L, row0, rstride, kv_bufs=()):
    T = x.shape[0]
    tm = TOKEN_TILE
    n_i = T // tm
    nb = tm // L
    n_kv = len(kv_bufs)
    body = functools.partial(_inproj_body, L=L, row0=row0, rstride=rstride, n_kv=n_kv)

    def only_at(tile):
        return lambda j, i: jnp.where(j < tile, 0, jnp.where(j > tile, n_i - 1, i))

    kv_i = [only_at(_kv_tile(n)[0]) for n in range(n_kv)]
    return pl.pallas_call(
        body,
        grid=(IN_COLS // COL_TILE, n_i),
        in_specs=[pl.BlockSpec((tm, D_MODEL), lambda j, i: (jnp.where(j == 0, i, n_i - 1), 0)),
                  pl.BlockSpec((None, N_PAD_ROWS, 6 * D_MODEL), lambda j, i: (layer, 0, 0)),
                  pl.BlockSpec((None, D_MODEL, COL_TILE), lambda j, i: (layer, 0, j))]
        + [pl.BlockSpec(memory_space=pl.ANY)] * n_kv,
        out_specs=[pl.BlockSpec((tm, COL_TILE), lambda j, i: (i, j))]
        + [pl.BlockSpec((nb, None, L, MIX_W), lambda j, i, f=f: (f(j, i), layer, 0, 0)) for f in kv_i],
        out_shape=[jax.ShapeDtypeStruct((T, IN_COLS), BF16)]
        + [jax.ShapeDtypeStruct(b.shape, b.dtype) for b in kv_bufs],
        input_output_aliases={3 + n: 1 + n for n in range(n_kv)},
        scratch_shapes=[pltpu.VMEM((n_i, tm, D_MODEL), BF16), pltpu.VMEM((D_MODEL, COL_TILE), BF16)],
        name="inproj",
        compiler_params=_params("arbitrary", "arbitrary"),
    )(x, p, w_in, *kv_bufs)


def _rope(x, cos, s_up, s_dn):
    return x * cos + pltpu.roll(x, 96, 1) * s_up + pltpu.roll(x, 32, 1) * s_dn


def _ret_body(*refs, n, rope, has_s0, want_state, layer):
    refs = list(refs)
    lg_ref, q_ref, k_ref, v_ref, g_ref = refs[:5]
    refs = refs[5:]
    if rope:
        cos_ref, sup_ref, sdn_ref = refs[:3]
        refs = refs[3:]
    if has_s0:
        s0_ref = refs[0]
        refs = refs[1:]
    if want_state:
        refs = refs[1:]
    o_ref = refs[0]
    refs = refs[1:]
    if want_state:
        st_ref = refs[0]
        refs = refs[1:]
    q_scr, k_scr, sb_scr, decay_scr = refs

    C = RET_CHUNK
    h = pl.program_id(0)
    lf = lg_ref[layer, 0, h]
    lb = lg_ref[layer, 1, h]

    @pl.when(pl.program_id(1) == 0)
    def _():
        ti = lax.broadcasted_iota(jnp.int32, (C, C), 0)
        si = lax.broadcasted_iota(jnp.int32, (C, C), 1)
        dlt = (ti - si).astype(F32)
        decay_scr[...] = (jnp.where(dlt >= 0, jnp.exp(lf * jnp.maximum(dlt, 0.0)), 0.0)
                          + jnp.where(dlt <= 0, jnp.exp(lb * jnp.maximum(-dlt, 0.0)), 0.0))

    tcol = lax.broadcasted_iota(jnp.int32, (C, 1), 0).astype(F32)
    qd_f = jnp.exp(lf * (tcol + 1.0))
    qd_b = jnp.exp(lb * (C - tcol))
    kd_f = jnp.exp(lf * (C - 1.0 - tcol))
    kd_b = jnp.exp(lb * tcol)
    cd_f = jnp.exp(lf * jnp.full((1, RET_DK), float(C), F32))
    cd_b = jnp.exp(lb * jnp.full((1, RET_DK), float(C), F32))

    def kv_outer(kc, vc, kd):
        return _dot((kc * kd).T.astype(BF16), vc)

    for bb in range(q_ref.shape[0]):
        q = q_ref[bb].astype(F32)
        k = k_ref[bb].astype(F32)
        if rope:
            q = _rope(q, cos_ref[...], sup_ref[...], sdn_ref[...])
            k = _rope(k, cos_ref[...], sup_ref[...], sdn_ref[...])
        q_scr[bb] = q
        k_scr[bb] = k * (RET_DK ** -0.5)

        s_b = s0_ref[bb, 1] if has_s0 else jnp.zeros((RET_DK, RET_DK), F32)
        for i in reversed(range(n)):
            sb_scr[bb, i] = s_b
            if i > 0 or want_state:
                s_b = s_b * cd_b + kv_outer(k_scr[bb, i * C:(i + 1) * C, :], v_ref[bb, i * C:(i + 1) * C, :], kd_b)

        s_f = s0_ref[bb, 0] if has_s0 else jnp.zeros((RET_DK, RET_DK), F32)
        for i in range(n):
            sl = slice(i * C, (i + 1) * C)
            qc = q_scr[bb, sl, :]
            kc = k_scr[bb, sl, :]
            vc = v_ref[bb, sl, :]
            att = _dot_nt(qc.astype(BF16), kc.astype(BF16)) * decay_scr[...]
            o = _dot(att.astype(BF16), vc)
            o = o + _dot((qc * qd_f).astype(BF16), s_f.astype(BF16))
            o = o + _dot((qc * qd_b).astype(BF16), sb_scr[bb, i].astype(BF16))
            mu = jnp.mean(o, -1, keepdims=True)
            oc = o - mu
            var = jnp.mean(oc * oc, -1, keepdims=True)
            gc = g_ref[bb, sl, :].astype(F32)
            o_ref[bb, sl, :] = (oc * lax.rsqrt(var + LN_EPS) * (gc * jax.nn.sigmoid(gc))).astype(BF16)
            if i < n - 1 or want_state:
                s_f = s_f * cd_f + kv_outer(kc, vc, kd_f)

        if want_state:
            st_ref[bb, 0] = s_f
            st_ref[bb, 1] = s_b


def _retention(z, log_gamma, *, B, L, rope_tabs=None, s0=None, layer=0, state_buf=None):
    want_state = state_buf is not None
    n = L // RET_CHUNK
    H = N_RET_HEADS
    nblk = MIX_W // RET_DK

    nbb = max(1, RET_ROWS // L)
    assert B % nbb == 0

    def sec(s):
        return pl.BlockSpec((nbb, L, RET_DK), lambda h, b: (b, 0, s * nblk + h))

    in_specs = [pl.BlockSpec(memory_space=pltpu.SMEM), sec(0), sec(1), sec(2), sec(3)]
    args = [log_gamma, z, z, z, z]
    if rope_tabs is not None:
        in_specs += [pl.BlockSpec((L, RET_DK), lambda h, b: (0, 0))] * 3
        args += list(rope_tabs)
    if s0 is not None:
        in_specs.append(pl.BlockSpec((nbb, None, 2, None, RET_DK, RET_DK), lambda h, b: (b, layer, 0, h, 0, 0)))
        args.append(s0)
    out_specs = [pl.BlockSpec((nbb, L, RET_DK), lambda h, b: (b, 0, h))]
    out_shape = [jax.ShapeDtypeStruct((B, L, MIX_W), BF16)]
    aliases = {}
    if want_state:
        aliases = {len(args): 1}
        in_specs.append(pl.BlockSpec(memory_space=pl.ANY))
        args.append(state_buf)
        out_specs.append(pl.BlockSpec((nbb, None, 2, None, RET_DK, RET_DK), lambda h, b: (b, layer, 0, h, 0, 0)))
        out_shape.append(jax.ShapeDtypeStruct(state_buf.shape, state_buf.dtype))
    body = functools.partial(_ret_body, n=n, rope=rope_tabs is not None, has_s0=s0 is not None,
                             want_state=want_state, layer=layer)
    return pl.pallas_call(
        body,
        grid=(H, B // nbb),
        in_specs=in_specs,
        out_specs=out_specs,
        out_shape=out_shape,
        input_output_aliases=aliases,
        scratch_shapes=[pltpu.VMEM((nbb, L, RET_DK), F32), pltpu.VMEM((nbb, L, RET_DK), F32),
                        pltpu.VMEM((nbb, n, RET_DK, RET_DK), F32), pltpu.VMEM((RET_CHUNK, RET_CHUNK), F32)],
        name="retention",
        compiler_params=_params("arbitrary", "arbitrary"),
    )(*args)


def _rope_tables(L):
    pos = jnp.arange(L)
    row = (pos // GRID_W).astype(F32)
    col = (pos % GRID_W).astype(F32)
    quarter = RET_DK // 4
    inv_freq = ROPE_BASE ** (-jnp.arange(quarter, dtype=F32) / quarter)
    ang_r = row[:, None] * inv_freq[None, :]
    ang_c = col[:, None] * inv_freq[None, :]
    zero = jnp.zeros_like(ang_r)
    cos = jnp.concatenate([jnp.cos(ang_r), jnp.cos(ang_r), jnp.cos(ang_c), jnp.cos(ang_c)], -1)
    s_up = jnp.concatenate([-jnp.sin(ang_r), zero, -jnp.sin(ang_c), zero], -1)
    s_dn = jnp.concatenate([zero, jnp.sin(ang_r), zero, jnp.sin(ang_c)], -1)
    return cos, s_up, s_dn


def _s5_body(*refs, B, nC, has_h0):
    refs = list(refs)
    u_ref, c0_ref, bs_ref, cp_ref, a8_ref = refs[:5]
    refs = refs[5:]
    if has_h0:
        h0_ref = refs[0]
        refs = refs[1:]
    y_ref, fin_ref, u_scr, a_scr, m_scr, bs_scr, cp_scr, c0_scr, s_scr, x_scr, y_scr = refs

    TC = S5_CHUNK
    R = B * nC
    P = nC + S5_PITCH_PAD
    nsl = s_scr.shape[0]
    half = nsl // 2
    ng = S5_LBLK_GROUPS
    sw = ng * SSM_STATE

    u_scr[...] = u_ref[...].astype(F32)
    for s in range(TC):
        a_scr[:, s * LANES:(s + 1) * LANES] = u_scr[pl.ds(s, R, stride=TC), :].astype(BF16)

    bs_scr[...] = jnp.zeros_like(bs_scr)
    cp_scr[...] = jnp.zeros_like(cp_scr)
    c0_scr[...] = jnp.zeros_like(c0_scr)

    for d in range(2):
        for g in range(ng):
            for part in range(2):
                cols = slice(part * sw + g * SSM_STATE, part * sw + (g + 1) * SSM_STATE)
                lo = (g % 2) * SSM_STATE
                c0_scr[g * SSM_GROUP:(g + 1) * SSM_GROUP, cols] = c0_ref[d, g, part, :, lo:lo + SSM_STATE]
                for s in range(TC):
                    rows = slice(s * LANES + g * SSM_GROUP, s * LANES + (g + 1) * SSM_GROUP)
                    bs_scr[rows, cols] = bs_ref[d, s, g, part, :, lo:lo + SSM_STATE]
                    cp_scr[rows, cols] = cp_ref[d, s, g, part, :, lo:lo + SSM_STATE]

        lag = _dot_nt(bs_scr[...], c0_scr[...]).astype(BF16)
        for s in range(TC):
            for t in range(TC):
                k = (t - s) if d == 0 else (s - t)
                src = (TC - 1 - k) if d == 0 else k
                blk = lag[src * LANES:(src + 1) * LANES, :] if k >= 0 else jnp.zeros((LANES, LANES), BF16)
                m_scr[s * LANES:(s + 1) * LANES, t * LANES:(t + 1) * LANES] = blk

        a = a_scr[...]
        yd = _dot(a, m_scr[...])
        if d == 0:
            y_scr[...] = yd
        else:
            y_scr[...] += yd

        sm = _dot(a, bs_scr[...])
        for b in range(B):
            for sl in range(nsl):
                s_scr[sl, b * P:b * P + nC, :] = sm[b * nC:(b + 1) * nC, sl * LANES:(sl + 1) * LANES]

        a_r = [jnp.broadcast_to(a8_ref[d, 0, :, q * LANES:(q + 1) * LANES], (B, LANES)) for q in range(half)]
        a_i = [jnp.broadcast_to(a8_ref[d, 1, :, q * LANES:(q + 1) * LANES], (B, LANES)) for q in range(half)]
        if has_h0:
            init = tuple(h0_ref[d, :, sl * LANES:(sl + 1) * LANES] for sl in range(nsl))
        else:
            init = tuple(jnp.zeros((B, LANES), F32) for _ in range(nsl))

        def step(j, carry, d=d, a_r=a_r, a_i=a_i):
            c = j if d == 0 else nC - 1 - j
            rows = pl.ds(c, B, stride=P)
            new_r, new_i = [], []
            for q in range(half):
                xr, xi = carry[q], carry[half + q]
                sr = s_scr[q, rows, :]
                si = s_scr[half + q, rows, :]
                s_scr[q, rows, :] = xr
                s_scr[half + q, rows, :] = xi
                new_r.append(a_r[q] * xr - a_i[q] * xi + sr)
                new_i.append(a_r[q] * xi + a_i[q] * xr + si)
            return tuple(new_r + new_i)

        fin = lax.fori_loop(0, nC, step, init)
        for sl in range(nsl):
            fin_ref[d, :, sl * LANES:(sl + 1) * LANES] = fin[sl]

        for b in range(B):
            for sl in range(nsl):
                x_scr[b * nC:(b + 1) * nC, sl * LANES:(sl + 1) * LANES] = \
                    s_scr[sl, b * P:b * P + nC, :].astype(BF16)
        y_scr[...] += _dot_nt(x_scr[...], cp_scr[...])

    for t in range(TC):
        y_ref[pl.ds(t, R, stride=TC), :] = y_scr[:, t * LANES:(t + 1) * LANES]


def _s5(z, ops, h0, *, B, L, layer):
    c0, bs, cp, a8 = ops
    T = B * L
    nC = L // S5_CHUNK
    nlb = MIX_W // LANES
    sc = 2 * S5_LBLK_GROUPS * SSM_STATE
    kc = S5_CHUNK * LANES
    su0 = SU_SECTION * MIX_W // LANES

    blocks = pl.BlockSpec((None, 2, None, S5_CHUNK, S5_LBLK_GROUPS, 2, SSM_GROUP, LANES),
                          lambda lb: (layer, 0, lb, 0, 0, 0, 0, 0))
    in_specs = [pl.BlockSpec((T, LANES), lambda lb: (0, su0 + lb)),
                pl.BlockSpec((None, 2, None, S5_LBLK_GROUPS, 2, SSM_GROUP, LANES),
                             lambda lb: (layer, 0, lb, 0, 0, 0, 0)),
                blocks, blocks,
                pl.BlockSpec((None, 2, None, 2, 1, sc // 2), lambda lb: (layer, 0, lb, 0, 0, 0))]
    args = [z, c0, bs, cp, a8]
    if h0 is not None:
        in_specs.append(pl.BlockSpec((2, None, B, sc), lambda lb: (0, lb, 0, 0)))
        args.append(h0)
    body = functools.partial(_s5_body, B=B, nC=nC, has_h0=h0 is not None)
    return pl.pallas_call(
        body,
        grid=(nlb,),
        in_specs=in_specs,
        out_specs=[pl.BlockSpec((T, LANES), lambda lb: (0, lb)),
                   pl.BlockSpec((2, None, B, sc), lambda lb: (0, lb, 0, 0))],
        out_shape=[jax.ShapeDtypeStruct((T, MIX_W), F32),
                   jax.ShapeDtypeStruct((2, nlb, B, sc), F32)],
        scratch_shapes=[pltpu.VMEM((T, LANES), F32), pltpu.VMEM((B * nC, kc), BF16),
                        pltpu.VMEM((kc, kc), BF16), pltpu.VMEM((kc, sc), BF16), pltpu.VMEM((kc, sc), BF16),
                        pltpu.VMEM((LANES, sc), BF16),
                        pltpu.VMEM((sc // LANES, B * (nC + S5_PITCH_PAD), LANES), F32),
                        pltpu.VMEM((B * nC, sc), BF16), pltpu.VMEM((B * nC, kc), F32)],
        name="s5",
        compiler_params=_params("arbitrary"),
    )(*args)


def _s5_operators(a_re, a_im, log_dt, b_re, b_im, c_re, c_im):
    TC = S5_CHUNK
    nlb = MIX_W // LANES
    ng = S5_LBLK_GROUPS
    lr = jnp.minimum(a_re, -1e-4)
    li = a_im
    dt = jnp.exp(log_dt)[..., None]
    k = jnp.arange(TC + 1, dtype=F32)[:, None, None, None]
    mag = jnp.exp(k * (lr * dt)[None])
    pr = mag * jnp.cos(k * (li * dt)[None])
    pi = mag * jnp.sin(k * (li * dt)[None])
    ar, ai = pr[1], pi[1]
    den = lr * lr + li * li
    sr = ((ar - 1.0) * lr + ai * li) / den
    si = (ai * lr - (ar - 1.0) * li) / den
    bbr = sr[..., None] * b_re[None] - si[..., None] * b_im[None]
    bbi = sr[..., None] * b_im[None] + si[..., None] * b_re[None]

    def lanes2(x):
        return jnp.concatenate([x, x], -1)

    def powers(fwd, bwd):
        x = lanes2(jnp.stack([fwd, bwd], 0)).reshape(2, TC, nlb, ng, 1, LANES)
        return jnp.swapaxes(x, 1, 2)

    def per_group(x):
        return lanes2(x).reshape(2, nlb, 1, ng, SSM_GROUP, LANES)

    er = powers(jnp.flip(pr[:TC, 0], 0), pr[:TC, 1])
    ei = powers(jnp.flip(pi[:TC, 0], 0), pi[:TC, 1])
    btr = per_group(jnp.swapaxes(bbr, -1, -2))
    bti = per_group(jnp.swapaxes(bbi, -1, -2))
    bs = jnp.stack([er * btr - ei * bti, er * bti + ei * btr], 4).astype(BF16)

    fr = powers(pr[1:, 0], jnp.flip(pr[1:, 1], 0))
    fi = powers(pi[1:, 0], jnp.flip(pi[1:, 1], 0))
    ctr = per_group(c_re)
    cti = per_group(c_im)
    cp = jnp.stack([ctr * fr - cti * fi, -(ctr * fi + cti * fr)], 4).astype(BF16)
    c0 = jnp.stack([ctr, -cti], 4)[:, :, 0].astype(BF16)

    sw = ng * SSM_STATE
    a8 = jnp.stack([pr[TC].reshape(2, nlb, 1, sw), pi[TC].reshape(2, nlb, 1, sw)], 2)
    return c0, bs, cp, a8


def _head_masks(shape):
    lane = lax.broadcasted_iota(jnp.int32, shape, 1)
    return lane < NA_HEAD_DIM


def _cattn_body(q_ref, k_ref, v_ref, o_ref):
    first = _head_masks(q_ref.shape[1:])
    nq = q_ref.shape[1]
    for bb in range(q_ref.shape[0]):
        q = q_ref[bb]
        qs = jnp.concatenate([jnp.where(first, q, jnp.zeros_like(q)), jnp.where(first, jnp.zeros_like(q), q)], 0)
        s = _dot_nt(qs, k_ref[bb]) * (NA_HEAD_DIM ** -0.5)
        m = jnp.max(s, -1, keepdims=True)
        p = jnp.exp(s - m)
        l = jnp.sum(p, -1, keepdims=True)
        o = _dot(p.astype(BF16), v_ref[bb]) / l
        o_ref[bb] = jnp.where(first, o[:nq], o[nq:]).astype(BF16)


def _context_attention(z, *, B, L):
    nblk = MIX_W // LANES
    nb = CTX_ATTN_BATCH

    def sec(s):
        return pl.BlockSpec((nb, L, LANES), lambda b, hp: (b, 0, s * nblk + hp))

    return pl.pallas_call(
        _cattn_body,
        grid=(B // nb, nblk),
        in_specs=[sec(5), sec(6), sec(7)],
        out_specs=pl.BlockSpec((nb, L, LANES), lambda b, hp: (b, 0, hp)),
        out_shape=jax.ShapeDtypeStruct((B, L, MIX_W), BF16),
        name="ctx_attention",
        compiler_params=_params("arbitrary", "arbitrary"),
    )(z, z, z)


def _na_chunks(rows):
    half = NA_KR // 2
    plan, kinds = [], []
    for r0 in range(0, rows, NA_CHUNK_ROWS):
        rs = [min(max(r - half, 0), rows - NA_KR) for r in range(r0, r0 + NA_CHUNK_ROWS)]
        ws = min(rs[0], rows - NA_WIN_ROWS)
        assert rs[-1] + NA_KR <= ws + NA_WIN_ROWS
        kind = tuple((r0 + n - ws, rs[n] - ws) for n in range(NA_CHUNK_ROWS))
        if kind not in kinds:
            kinds.append(kind)
        plan.append((ws, kinds.index(kind)))
    return plan, kinds


def _na_body(q_ref, k_ref, v_ref, kc_ref, vc_ref, tb_ref, o_ref, bias_scr, *, rows):
    scale = NA_HEAD_DIM ** -0.5
    nq = NA_CHUNK_ROWS * GRID_W
    plan, kinds = _na_chunks(rows)
    n_off = 2 * NA_KR - 1

    @pl.when(pl.program_id(1) == 0)
    def _():
        for t, kind in enumerate(kinds):
            for e in range(2):
                for n, (r_rel, rs_rel) in enumerate(kind):
                    for kj in range(NA_WIN_ROWS):
                        off = kj - r_rel + NA_KR - 1 if rs_rel <= kj < rs_rel + NA_KR else n_off
                        lo = (kj % 2) * GRID_W
                        bias_scr[t, e * nq + n * GRID_W:e * nq + (n + 1) * GRID_W, kj * GRID_W:(kj + 1) * GRID_W] = \
                            tb_ref[e, off, :, lo:lo + GRID_W]

    kctx = kc_ref[...].astype(BF16)
    vctx = vc_ref[...].astype(BF16)
    first = _head_masks((nq, LANES))
    for c, (ws, kind) in enumerate(plan):
        qc = q_ref[c * nq:(c + 1) * nq, :]
        qs = jnp.concatenate([jnp.where(first, qc, jnp.zeros_like(qc)),
                              jnp.where(first, jnp.zeros_like(qc), qc)], 0)
        kw = k_ref[ws * GRID_W:(ws + NA_WIN_ROWS) * GRID_W, :]
        vw = v_ref[ws * GRID_W:(ws + NA_WIN_ROWS) * GRID_W, :]
        s_loc = _dot_nt(qs, kw) * scale + bias_scr[kind]
        s_ctx = _dot_nt(qs, kctx) * scale
        m = jnp.maximum(jnp.max(s_loc, -1, keepdims=True), jnp.max(s_ctx, -1, keepdims=True))
        p_loc = jnp.exp(s_loc - m)
        p_ctx = jnp.exp(s_ctx - m)
        l = jnp.sum(p_loc, -1, keepdims=True) + jnp.sum(p_ctx, -1, keepdims=True)
        o = (_dot(p_loc.astype(BF16), vw) + _dot(p_ctx.astype(BF16), vctx)) / l
        o_ref[c * nq:(c + 1) * nq, :] = jnp.where(first, o[:nq], o[nq:]).astype(BF16)


def _neighbourhood_attention(z, cache_k, cache_v, blocks, *, B, L, layer):
    nblk = MIX_W // LANES
    rows = L // GRID_W
    Lc = cache_k.shape[2]
    _, kinds = _na_chunks(rows)

    def sec(s):
        return pl.BlockSpec((None, L, LANES), lambda hp, b: (b, 0, s * nblk + hp))

    ctx = pl.BlockSpec((None, None, Lc, LANES), lambda hp, b: (b, layer, 0, hp))
    return pl.pallas_call(
        functools.partial(_na_body, rows=rows),
        grid=(nblk, B),
        in_specs=[sec(5), sec(6), sec(7), ctx, ctx,
                  pl.BlockSpec((None, None, 2, 2 * NA_KR, GRID_W, 2 * GRID_W),
                               lambda hp, b: (layer, hp, 0, 0, 0, 0))],
        out_specs=pl.BlockSpec((None, L, LANES), lambda hp, b: (b, 0, hp)),
        out_shape=jax.ShapeDtypeStruct((B, L, MIX_W), BF16),
        scratch_shapes=[pltpu.VMEM((len(kinds), 2 * NA_CHUNK_ROWS * GRID_W, NA_WIN_ROWS * GRID_W), F32)],
        name="nbr_attention",
        compiler_params=_params("arbitrary", "arbitrary"),
    )(z, z, z, cache_k, cache_v, blocks)


def _na_bias_blocks(rpb):
    nr, nc = 2 * NA_KR - 1, 2 * NA_KW - 1
    qc = np.arange(GRID_W)
    kc = np.arange(GRID_W)
    ws = np.clip(qc - NA_KW // 2, 0, GRID_W - NA_KW)
    col_ok = (kc[None, :] >= ws[:, None]) & (kc[None, :] < ws[:, None] + NA_KW)
    coff = np.clip(kc[None, :] - qc[:, None] + NA_KW - 1, 0, nc - 1)
    sel_c = ((coff[None] == np.arange(nc)[:, None, None]) & col_ok[None]).astype(np.float32)
    sel_c = np.concatenate([sel_c, sel_c], -1)
    ok = np.concatenate([col_ok, col_ok], -1)[None] & (np.arange(nr + 1) < nr)[:, None, None]
    H = rpb.shape[0]
    rows = jnp.pad(rpb.astype(F32), ((0, 0), (0, 1), (0, 0)))
    t = jnp.einsum('hrc,cqk->hrqk', rows, sel_c, precision=lax.Precision.HIGHEST)
    return jnp.where(ok[None], t, NEG_INF).reshape(H // 2, 2, nr + 1, GRID_W, 2 * GRID_W)


def _merge_body(x_ref, p_ref, r_ref, u_ref, y_ref, n_ref, ga_ref, gb_ref, gc_ref,
                d_ref, wglu_ref, wbr_ref, wo_ref, lg_ref, lb_ref, o_ref,
                wglu_s, wbr_s, wo_s, *, L, row0, rstride):
    i = pl.program_id(0)
    tm = x_ref.shape[0]

    @pl.when(i == 0)
    def _():
        wglu_s[...] = wglu_ref[...].astype(BF16)
        wbr_s[...] = wbr_ref[...].astype(BF16)
        wo_s[...] = wo_ref[...].astype(BF16)

    row = row0 + rstride * ((i * tm) // L)
    g1 = _mod_row(p_ref, row, 2)

    y = d_ref[...] * u_ref[...].astype(F32) + y_ref[...]
    y = jax.nn.gelu(y)
    s_out = y * jax.nn.sigmoid(_dot(y.astype(BF16), wglu_s[...]))

    def gate(ref):
        return jax.nn.sigmoid(ref[...].astype(F32))

    merged = (gate(ga_ref) * _dot(r_ref[...], wbr_s[0])
              + gate(gb_ref) * _dot(s_out.astype(BF16), wbr_s[1])
              + gate(gc_ref) * _dot(n_ref[...], wbr_s[2]))
    m = _dot(merged.astype(BF16), wo_s[...])
    o_ref[...] = _layer_norm(DEEPNORM_ALPHA * x_ref[...] + g1 * m, lg_ref[...], lb_ref[...])


def _merge(x, p, z, r_out, y, n_out, ssm_d, w_glu, w_branch, w_o, ln_g, ln_b, *, layer, L, row0, rstride):
    T = x.shape[0]
    tm = MERGE_TILE
    assert L % tm == 0 or rstride == 0
    gate0 = 8 * MIX_W // D_MODEL

    def tok(w):
        return pl.BlockSpec((tm, w), lambda i: (i, 0))

    def full(shape):
        return pl.BlockSpec((None,) + shape, lambda i: (layer,) + (0,) * len(shape))

    body = functools.partial(_merge_body, L=L, row0=row0, rstride=rstride)
    return pl.pallas_call(
        body,
        grid=(T // tm,),
        in_specs=[tok(D_MODEL), full((N_PAD_ROWS, 6 * D_MODEL)), tok(MIX_W),
                  pl.BlockSpec((tm, MIX_W), lambda i: (i, SU_SECTION)), tok(MIX_W), tok(MIX_W),
                  pl.BlockSpec((tm, D_MODEL), lambda i: (i, gate0)),
                  pl.BlockSpec((tm, D_MODEL), lambda i: (i, gate0 + 1)),
                  pl.BlockSpec((tm, D_MODEL), lambda i: (i, gate0 + 2)),
                  full((1, MIX_W)), full((MIX_W, MIX_W)), full((3, MIX_W, D_MODEL)),
                  full((D_MODEL, D_MODEL)), full((1, D_MODEL)), full((1, D_MODEL))],
        out_specs=tok(D_MODEL),
        out_shape=jax.ShapeDtypeStruct((T, D_MODEL), F32),
        scratch_shapes=[pltpu.VMEM((MIX_W, MIX_W), BF16), pltpu.VMEM((3, MIX_W, D_MODEL), BF16),
                        pltpu.VMEM((D_MODEL, D_MODEL), BF16)],
        name="merge",
        compiler_params=_params("arbitrary"),
    )(x, p, r_out, z, y, n_out, z, z, z, ssm_d.reshape(DEPTH, 1, MIX_W), w_glu, w_branch, w_o,
      ln_g.reshape(DEPTH, 1, D_MODEL), ln_b.reshape(DEPTH, 1, D_MODEL))


def _ffn_body(x_ref, p_ref, wa_ref, wb_ref, cwa_ref, cwb_ref, cba_ref, cbb_ref, wd_ref, lg_ref, lb_ref,
              o_ref, h_scr, acc_scr, mp_scr, mn_scr, *, L, row0, rstride):
    i = pl.program_id(0)
    j = pl.program_id(1)
    tm = x_ref.shape[0]
    nb = tm // L

    @pl.when(j == 0)
    def _():
        for s in range(nb):
            row = row0 + rstride * (i * nb + s)
            sh = _mod_row(p_ref, row, 3)
            sc = _mod_row(p_ref, row, 4)
            h_scr[s * L:(s + 1) * L, :] = (x_ref[s * L:(s + 1) * L, :] * (1.0 + sc) + sh).astype(BF16)
        acc_scr[...] = jnp.zeros_like(acc_scr)
        t = lax.broadcasted_iota(jnp.int32, (tm, FF_TILE), 0) % L
        mp_scr[...] = (t != 0).astype(BF16)
        mn_scr[...] = (t != L - 1).astype(BF16)

    def conv(w_ref, cw_ref, cb_ref):
        zc = _dot(h_scr[...], w_ref[...].astype(BF16))
        zp = pltpu.roll(zc, 1, 0).astype(BF16) * mp_scr[...]
        zn = pltpu.roll(zc, tm - 1, 0).astype(BF16) * mn_scr[...]
        cw = cw_ref[...].astype(BF16)
        return zp * cw[0:1, :] + zc.astype(BF16) * cw[1:2, :] + zn * cw[2:3, :] + cb_ref[...].astype(BF16)

    a = conv(wa_ref, cwa_ref, cba_ref)
    b = conv(wb_ref, cwb_ref, cbb_ref)
    acc_scr[...] += _dot(jax.nn.gelu(a) * b, wd_ref[...].astype(BF16))

    @pl.when(j == pl.num_programs(1) - 1)
    def _():
        for s in range(nb):
            row = row0 + rstride * (i * nb + s)
            g2 = _mod_row(p_ref, row, 5)
            sl = slice(s * L, (s + 1) * L)
            o_ref[sl, :] = _layer_norm(DEEPNORM_ALPHA * x_ref[sl, :] + g2 * acc_scr[sl, :],
                                       lg_ref[...], lb_ref[...])


def _conv_ffn(x, p, w_up, conv_w, conv_b, w_down, ln_g, ln_b, *, layer, L, row0, rstride):
    T = x.shape[0]
    tm = TOKEN_TILE
    nff = D_FF // FF_TILE
    body = functools.partial(_ffn_body, L=L, row0=row0, rstride=rstride)
    conv_b = conv_b.reshape(DEPTH, 1, 2 * D_FF)
    return pl.pallas_call(
        body,
        grid=(T // tm, nff),
        in_specs=[pl.BlockSpec((tm, D_MODEL), lambda i, j: (i, 0)),
                  pl.BlockSpec((None, N_PAD_ROWS, 6 * D_MODEL), lambda i, j: (layer, 0, 0)),
                  pl.BlockSpec((None, D_MODEL, FF_TILE), lambda i, j: (layer, 0, j)),
                  pl.BlockSpec((None, D_MODEL, FF_TILE), lambda i, j: (layer, 0, nff + j)),
                  pl.BlockSpec((None, 3, FF_TILE), lambda i, j: (layer, 0, j)),
                  pl.BlockSpec((None, 3, FF_TILE), lambda i, j: (layer, 0, nff + j)),
                  pl.BlockSpec((None, 1, FF_TILE), lambda i, j: (layer, 0, j)),
                  pl.BlockSpec((None, 1, FF_TILE), lambda i, j: (layer, 0, nff + j)),
                  pl.BlockSpec((None, FF_TILE, D_MODEL), lambda i, j: (layer, j, 0)),
                  pl.BlockSpec((None, 1, D_MODEL), lambda i, j: (layer, 0, 0)),
                  pl.BlockSpec((None, 1, D_MODEL), lambda i, j: (layer, 0, 0))],
        out_specs=pl.BlockSpec((tm, D_MODEL), lambda i, j: (i, 0)),
        out_shape=jax.ShapeDtypeStruct((T, D_MODEL), F32),
        scratch_shapes=[pltpu.VMEM((tm, D_MODEL), BF16), pltpu.VMEM((tm, D_MODEL), F32),
                        pltpu.VMEM((tm, FF_TILE), BF16), pltpu.VMEM((tm, FF_TILE), BF16)],
        name="conv_ffn",
        compiler_params=_params("arbitrary", "arbitrary"),
    )(x, p, w_up, w_up, conv_w, conv_w, conv_b, conv_b, w_down, ln_g.reshape(DEPTH, 1, D_MODEL),
      ln_b.reshape(DEPTH, 1, D_MODEL))


def _s5_states_in(state_ssm, layer):
    B = state_ssm.shape[0]
    nlb = MIX_W // LANES
    h = state_ssm[:, layer].reshape(B, 2, nlb, S5_LBLK_GROUPS, SSM_STATE, 2)
    return jnp.transpose(h, (1, 2, 0, 5, 3, 4)).reshape(2, nlb, B, 2 * S5_LBLK_GROUPS * SSM_STATE)


def _s5_states_out(fin):
    nlb, B = fin.shape[1], fin.shape[2]
    h = fin.reshape(2, nlb, B, 2, S5_LBLK_GROUPS, SSM_STATE)
    return jnp.transpose(h, (2, 0, 1, 4, 5, 3)).reshape(B, 2, SSM_GROUPS, SSM_STATE, 2)


def _layer(x, p, lw, *, B, L, row0, rstride, latent, layer, extra):
    T = B * L
    log_gamma = lw['log_gamma']
    if latent:
        z, = _inproj(x, p, lw['w_in'], layer=layer, L=L, row0=row0, rstride=rstride)
        z3 = z.reshape(B, L, IN_COLS)
        r_out = _retention(z3, log_gamma, B=B, L=L, rope_tabs=extra['rope'], s0=extra['state_ret'],
                           layer=layer)[0]
        n_out = _neighbourhood_attention(z3, extra['cache_k'], extra['cache_v'], extra['bias'],
                                         B=B, L=L, layer=layer)
        y, _ = _s5(z, lw['s5'], _s5_states_in(extra['state_ssm'], layer), B=B, L=L, layer=layer)
        states = None
    else:
        ret_buf, k_buf, v_buf = extra
        z, k_buf, v_buf = _inproj(x, p, lw['w_in'], layer=layer, L=L, row0=row0, rstride=rstride,
                                  kv_bufs=(k_buf, v_buf))
        z3 = z.reshape(B, L, IN_COLS)
        r_out, ret_buf = _retention(z3, log_gamma, B=B, L=L, layer=layer, state_buf=ret_buf)
        n_out = _context_attention(z3, B=B, L=L)
        y, fin = _s5(z, lw['s5'], None, B=B, L=L, layer=layer)
        states = ((ret_buf, k_buf, v_buf), _s5_states_out(fin))
    x = _merge(x, p, z, r_out.reshape(T, MIX_W), y, n_out.reshape(T, MIX_W),
               lw['ssm_d'], lw['ssm_w_glu'], lw['w_branch'], lw['w_o'], lw['ln1_g'], lw['ln1_b'],
               layer=layer, L=L, row0=row0, rstride=rstride)
    x = _conv_ffn(x, p, lw['w_up'], lw['conv_w'], lw['conv_b'], lw['w_down'], lw['ln2_g'], lw['ln2_b'],
                  layer=layer, L=L, row0=row0, rstride=rstride)
    return x, states


def kernel(x_prompt, x_sample, state_ret, state_ssm, cache_na_k, cache_na_v, c, c_ctx, w_ada, b_ada, w_in, ret_decay, ssm_a_re, ssm_a_im, ssm_log_dt, ssm_b_re, ssm_b_im, ssm_c_re, ssm_c_im, ssm_d, ssm_w_glu, na_rpb, w_branch, w_o, ln1_g, ln1_b, w_up, conv_w, conv_b, w_down, ln2_g, ln2_b):
    B, L, _ = x_prompt.shape
    Bd, Ld, _ = x_sample.shape
    Lc = cache_na_k.shape[2]

    cond = jnp.concatenate([c_ctx[None, :], c, jnp.zeros((N_PAD_ROWS - 1 - Bd, D_MODEL), F32)], 0)
    p_all = _ada(cond, w_ada, b_ada)

    extra = dict(rope=_rope_tables(Ld), state_ret=state_ret, state_ssm=state_ssm,
                 cache_k=cache_na_k.reshape(Bd, DEPTH, Lc, MIX_W),
                 cache_v=cache_na_v.reshape(Bd, DEPTH, Lc, MIX_W),
                 bias=jax.vmap(_na_bias_blocks)(na_rpb))

    xp = x_prompt.reshape(B * L, D_MODEL)
    xs = x_sample.reshape(Bd * Ld, D_MODEL)
    bufs = (jnp.zeros((B, DEPTH, 2, N_RET_HEADS, RET_DK, RET_DK), F32),
            jnp.zeros((B, DEPTH, L, MIX_W), F32), jnp.zeros((B, DEPTH, L, MIX_W), F32))
    ssm_states = []
    lw = dict(w_in=w_in, log_gamma=jax.nn.log_sigmoid(ret_decay.astype(F32)), ssm_d=ssm_d, ssm_w_glu=ssm_w_glu,
              w_branch=w_branch, w_o=w_o, ln1_g=ln1_g, ln1_b=ln1_b, w_up=w_up,
              conv_w=conv_w, conv_b=conv_b, w_down=w_down, ln2_g=ln2_g, ln2_b=ln2_b,
              s5=jax.vmap(_s5_operators)(ssm_a_re, ssm_a_im, ssm_log_dt, ssm_b_re, ssm_b_im, ssm_c_re, ssm_c_im))
    for l in range(DEPTH):
        xp, (bufs, s_ssm) = _layer(xp, p_all, lw, B=B, L=L, row0=0, rstride=0, latent=False, layer=l, extra=bufs)
        ssm_states.append(s_ssm)
        xs, _ = _layer(xs, p_all, lw, B=Bd, L=Ld, row0=1, rstride=1, latent=True, layer=l, extra=extra)
    ret_buf, k_buf, v_buf = bufs
    heads = (B, DEPTH, L, NA_HEADS, NA_HEAD_DIM)
    return (xp.reshape(B, L, D_MODEL), xs.reshape(Bd, Ld, D_MODEL),
            ret_buf, jnp.stack(ssm_states, 1), k_buf.reshape(heads), v_buf.reshape(heads))
```

```python
import functools

import jax
import jax.numpy as jnp
import numpy as np
from jax import lax
from jax.experimental import pallas as pl
from jax.experimental.pallas import tpu as pltpu

F32 = jnp.float32
BF16 = jnp.bfloat16

D_MODEL = 1024
DEPTH = 2
GRID_W = 64
MIX_W = D_MODEL // 2
N_RET_HEADS = 4
RET_DK = MIX_W // N_RET_HEADS
SSM_GROUP = 16
SSM_GROUPS = MIX_W // SSM_GROUP
SSM_STATE = 64
NA_HEADS = 8
NA_HEAD_DIM = MIX_W // NA_HEADS
NA_KR = 8
NA_KW = 16
D_FF = ((8 * D_MODEL // 3 + 127) // 128) * 128
ROPE_BASE = 10000.0
LN_EPS = 1e-5
NEG_INF = -1e30
DEEPNORM_ALPHA = (2 * DEPTH) ** 0.25
IN_COLS = 8 * MIX_W + 3 * D_MODEL

VMEM_LIMIT_BYTES = 56 * 1024 * 1024
LANES = 128

TOKEN_TILE = 1024
MERGE_TILE = 512
COL_TILE = 1024
SU_SECTION = 4
NK_SECTION = 6
FF_TILE = 256
RET_CHUNK = 256
RET_ROWS = 1024
S5_CHUNK = 8
S5_PITCH_PAD = 8
S5_LBLK_GROUPS = LANES // SSM_GROUP
N_PAD_ROWS = 8
CTX_ATTN_BATCH = 4
NA_CHUNK_ROWS = 4
NA_WIN_ROWS = 12


def _params(*sem):
    return pltpu.CompilerParams(dimension_semantics=sem, vmem_limit_bytes=VMEM_LIMIT_BYTES)


def _dot(a, b):
    return jnp.dot(a, b, preferred_element_type=F32)


def _dot_nt(a, b):
    return lax.dot_general(a, b, (((1,), (1,)), ((), ())), preferred_element_type=F32)


def _layer_norm(x, g, b):
    mu = jnp.mean(x, -1, keepdims=True)
    xc = x - mu
    var = jnp.mean(xc * xc, -1, keepdims=True)
    return xc * lax.rsqrt(var + LN_EPS) * g + b


def _ada_body(c_ref, w_ref, b_ref, o_ref):
    c = c_ref[...]
    s = c * jax.nn.sigmoid(c)
    o_ref[...] = _dot(s.astype(BF16), w_ref[...].astype(BF16)) + b_ref[...]


def _ada(cond, w_ada, b_ada):
    tn = 1024
    return pl.pallas_call(
        _ada_body,
        grid=(DEPTH, 6 * D_MODEL // tn),
        in_specs=[pl.BlockSpec((N_PAD_ROWS, D_MODEL), lambda l, j: (0, 0)),
                  pl.BlockSpec((None, D_MODEL, tn), lambda l, j: (l, 0, j)),
                  pl.BlockSpec((None, 1, tn), lambda l, j: (l, 0, j))],
        out_specs=pl.BlockSpec((None, N_PAD_ROWS, tn), lambda l, j: (l, 0, j)),
        out_shape=jax.ShapeDtypeStruct((DEPTH, N_PAD_ROWS, 6 * D_MODEL), F32),
        name="ada",
        compiler_params=_params("arbitrary", "arbitrary"),
    )(cond, w_ada, b_ada.reshape(DEPTH, 1, 6 * D_MODEL))


def _mod_row(p_ref, row, k):
    return p_ref[pl.ds(row, 1), k * D_MODEL:(k + 1) * D_MODEL]


def _kv_tile(n):
    col = (NK_SECTION + n) * MIX_W
    return col // COL_TILE, col % COL_TILE


def _inproj_body(x_ref, p_ref, w_ref, *rest, L, row0, rstride, n_kv):
    z_ref = rest[n_kv]
    kv_refs = rest[n_kv + 1:-2]
    h_scr, w_scr = rest[-2:]
    j = pl.program_id(0)
    i = pl.program_id(1)
    nb = x_ref.shape[0] // L

    @pl.when(j == 0)
    def _():
        for s in range(nb):
            row = row0 + rstride * (i * nb + s)
            sh = _mod_row(p_ref, row, 0)
            sc = _mod_row(p_ref, row, 1)
            h_scr[i, s * L:(s + 1) * L, :] = (x_ref[s * L:(s + 1) * L, :] * (1.0 + sc) + sh).astype(BF16)

    @pl.when(i == 0)
    def _():
        w_scr[...] = w_ref[...].astype(BF16)

    acc = _dot(h_scr[i], w_scr[...])
    z_ref[...] = acc.astype(BF16)

    for n, ref in enumerate(kv_refs):
        tile, off = _kv_tile(n)

        @pl.when(j == tile)
        def _(ref=ref, off=off):
            ref[...] = acc[:, off:off + MIX_W].reshape(ref.shape)


def _inproj(x, p, w_in, *, layer, L, row0, rstride, kv_bufs=()):
    T = x.shape[0]
    tm = TOKEN_TILE
    n_i = T // tm
    nb = tm // L
    n_kv = len(kv_bufs)
    body = functools.partial(_inproj_body, L=L, row0=row0, rstride=rstride, n_kv=n_kv)

    def only_at(tile):
        return lambda j, i: jnp.where(j < tile, 0, jnp.where(j > tile, n_i - 1, i))

    kv_i = [only_at(_kv_tile(n)[0]) for n in range(n_kv)]
    return pl.pallas_call(
        body,
        grid=(IN_COLS // COL_TILE, n_i),
        in_specs=[pl.BlockSpec((tm, D_MODEL), lambda j, i: (jnp.where(j == 0, i, n_i - 1), 0)),
                  pl.BlockSpec((None, N_PAD_ROWS, 6 * D_MODEL), lambda j, i: (layer, 0, 0)),
                  pl.BlockSpec((None, D_MODEL, COL_TILE), lambda j, i: (layer, 0, j))]
        + [pl.BlockSpec(memory_space=pl.ANY)] * n_kv,
        out_specs=[pl.BlockSpec((tm, COL_TILE), lambda j, i: (i, j))]
        + [pl.BlockSpec((nb, None, L, MIX_W), lambda j, i, f=f: (f(j, i), layer, 0, 0)) for f in kv_i],
        out_shape=[jax.ShapeDtypeStruct((T, IN_COLS), BF16)]
        + [jax.ShapeDtypeStruct(b.shape, b.dtype) for b in kv_bufs],
        input_output_aliases={3 + n: 1 + n for n in range(n_kv)},
        scratch_shapes=[pltpu.VMEM((n_i, tm, D_MODEL), BF16), pltpu.VMEM((D_MODEL, COL_TILE), BF16)],
        name="inproj",
        compiler_params=_params("arbitrary", "arbitrary"),
    )(x, p, w_in, *kv_bufs)


def _rope(x, cos, s_up, s_dn):
    return x * cos + pltpu.roll(x, 96, 1) * s_up + pltpu.roll(x, 32, 1) * s_dn


def _ret_body(*refs, n, rope, has_s0, want_state, layer):
    refs = list(refs)
    lg_ref, q_ref, k_ref, v_ref, g_ref = refs[:5]
    refs = refs[5:]
    if rope:
        cos_ref, sup_ref, sdn_ref = refs[:3]
        refs = refs[3:]
    if has_s0:
        s0_ref = refs[0]
        refs = refs[1:]
    if want_state:
        refs = refs[1:]
    o_ref = refs[0]
    refs = refs[1:]
    if want_state:
        st_ref = refs[0]
        refs = refs[1:]
    q_scr, k_scr, sb_scr, decay_scr = refs

    C = RET_CHUNK
    h = pl.program_id(0)
    lf = lg_ref[layer, 0, h]
    lb = lg_ref[layer, 1, h]

    @pl.when(pl.program_id(1) == 0)
    def _():
        ti = lax.broadcasted_iota(jnp.int32, (C, C), 0)
        si = lax.broadcasted_iota(jnp.int32, (C, C), 1)
        dlt = (ti - si).astype(F32)
        decay_scr[...] = (jnp.where(dlt >= 0, jnp.exp(lf * jnp.maximum(dlt, 0.0)), 0.0)
                          + jnp.where(dlt <= 0, jnp.exp(lb * jnp.maximum(-dlt, 0.0)), 0.0))

    tcol = lax.broadcasted_iota(jnp.int32, (C, 1), 0).astype(F32)
    qd_f = jnp.exp(lf * (tcol + 1.0))
    qd_b = jnp.exp(lb * (C - tcol))
    kd_f = jnp.exp(lf * (C - 1.0 - tcol))
    kd_b = jnp.exp(lb * tcol)
    cd_f = jnp.exp(lf * jnp.full((1, RET_DK), float(C), F32))
    cd_b = jnp.exp(lb * jnp.full((1, RET_DK), float(C), F32))

    def kv_outer(kc, vc, kd):
        return _dot((kc * kd).T.astype(BF16), vc)

    for bb in range(q_ref.shape[0]):
        q = q_ref[bb].astype(F32)
        k = k_ref[bb].astype(F32)
        if rope:
            q = _rope(q, cos_ref[...], sup_ref[...], sdn_ref[...])
            k = _rope(k, cos_ref[...], sup_ref[...], sdn_ref[...])
        q_scr[bb] = q
        k_scr[bb] = k * (RET_DK ** -0.5)

        s_b = s0_ref[bb, 1] if has_s0 else jnp.zeros((RET_DK, RET_DK), F32)
        for i in reversed(range(n)):
            sb_scr[bb, i] = s_b
            if i > 0 or want_state:
                s_b = s_b * cd_b + kv_outer(k_scr[bb, i * C:(i + 1) * C, :], v_ref[bb, i * C:(i + 1) * C, :], kd_b)

        s_f = s0_ref[bb, 0] if has_s0 else jnp.zeros((RET_DK, RET_DK), F32)
        for i in range(n):
            sl = slice(i * C, (i + 1) * C)
            qc = q_scr[bb, sl, :]
            kc = k_scr[bb, sl, :]
            vc = v_ref[bb, sl, :]
            att = _dot_nt(qc.astype(BF16), kc.astype(BF16)) * decay_scr[...]
            o = _dot(att.astype(BF16), vc)
            o = o + _dot((qc * qd_f).astype(BF16), s_f.astype(BF16))
            o = o + _dot((qc * qd_b).astype(BF16), sb_scr[bb, i].astype(BF16))
            mu = jnp.mean(o, -1, keepdims=True)
            oc = o - mu
            var = jnp.mean(oc * oc, -1, keepdims=True)
            gc = g_ref[bb, sl, :].astype(F32)
            o_ref[bb, sl, :] = (oc * lax.rsqrt(var + LN_EPS) * (gc * jax.nn.sigmoid(gc))).astype(BF16)
            if i < n - 1 or want_state:
                s_f = s_f * cd_f + kv_outer(kc, vc, kd_f)

        if want_state:
            st_ref[bb, 0] = s_f
            st_ref[bb, 1] = s_b


def _retention(z, log_gamma, *, B, L, rope_tabs=None, s0=None, layer=0, state_buf=None):
    want_state = state_buf is not None
    n = L // RET_CHUNK
    H = N_RET_HEADS
    nblk = MIX_W // RET_DK

    nbb = max(1, RET_ROWS // L)
    assert B % nbb == 0

    def sec(s):
        return pl.BlockSpec((nbb, L, RET_DK), lambda h, b: (b, 0, s * nblk + h))

    in_specs = [pl.BlockSpec(memory_space=pltpu.SMEM), sec(0), sec(1), sec(2), sec(3)]
    args = [log_gamma, z, z, z, z]
    if rope_tabs is not None:
        in_specs += [pl.BlockSpec((L, RET_DK), lambda h, b: (0, 0))] * 3
        args += list(rope_tabs)
    if s0 is not None:
        in_specs.append(pl.BlockSpec((nbb, None, 2, None, RET_DK, RET_DK), lambda h, b: (b, layer, 0, h, 0, 0)))
        args.append(s0)
    out_specs = [pl.BlockSpec((nbb, L, RET_DK), lambda h, b: (b, 0, h))]
    out_shape = [jax.ShapeDtypeStruct((B, L, MIX_W), BF16)]
    aliases = {}
    if want_state:
        aliases = {len(args): 1}
        in_specs.append(pl.BlockSpec(memory_space=pl.ANY))
        args.append(state_buf)
        out_specs.append(pl.BlockSpec((nbb, None, 2, None, RET_DK, RET_DK), lambda h, b: (b, layer, 0, h, 0, 0)))
        out_shape.append(jax.ShapeDtypeStruct(state_buf.shape, state_buf.dtype))
    body = functools.partial(_ret_body, n=n, rope=rope_tabs is not None, has_s0=s0 is not None,
                             want_state=want_state, layer=layer)
    return pl.pallas_call(
        body,
        grid=(H, B // nbb),
        in_specs=in_specs,
        out_specs=out_specs,
        out_shape=out_shape,
        input_output_aliases=aliases,
        scratch_shapes=[pltpu.VMEM((nbb, L, RET_DK), F32), pltpu.VMEM((nbb, L, RET_DK), F32),
                        pltpu.VMEM((nbb, n, RET_DK, RET_DK), F32), pltpu.VMEM((RET_CHUNK, RET_CHUNK), F32)],
        name="retention",
        compiler_params=_params("arbitrary", "arbitrary"),
    )(*args)


def _rope_tables(L):
    pos = jnp.arange(L)
    row = (pos // GRID_W).astype(F32)
    col = (pos % GRID_W).astype(F32)
    quarter = RET_DK // 4
    inv_freq = ROPE_BASE ** (-jnp.arange(quarter, dtype=F32) / quarter)
    ang_r = row[:, None] * inv_freq[None, :]
    ang_c = col[:, None] * inv_freq[None, :]
    zero = jnp.zeros_like(ang_r)
    cos = jnp.concatenate([jnp.cos(ang_r), jnp.cos(ang_r), jnp.cos(ang_c), jnp.cos(ang_c)], -1)
    s_up = jnp.concatenate([-jnp.sin(ang_r), zero, -jnp.sin(ang_c), zero], -1)
    s_dn = jnp.concatenate([zero, jnp.sin(ang_r), zero, jnp.sin(ang_c)], -1)
    return cos, s_up, s_dn


def _s5_body(*refs, B, nC, has_h0):
    refs = list(refs)
    u_ref, c0_ref, bs_ref, cp_ref, a8_ref = refs[:5]
    refs = refs[5:]
    if has_h0:
        h0_ref = refs[0]
        refs = refs[1:]
    y_ref, fin_ref, u_scr, a_scr, m_scr, bs_scr, cp_scr, c0_scr, s_scr, x_scr, y_scr = refs

    TC = S5_CHUNK
    R = B * nC
    P = nC + S5_PITCH_PAD
    nsl = s_scr.shape[0]
    half = nsl // 2
    ng = S5_LBLK_GROUPS
    sw = ng * SSM_STATE

    u_scr[...] = u_ref[...].astype(F32)
    for s in range(TC):
        a_scr[:, s * LANES:(s + 1) * LANES] = u_scr[pl.ds(s, R, stride=TC), :].astype(BF16)

    bs_scr[...] = jnp.zeros_like(bs_scr)
    cp_scr[...] = jnp.zeros_like(cp_scr)
    c0_scr[...] = jnp.zeros_like(c0_scr)

    for d in range(2):
        for g in range(ng):
            for part in range(2):
                cols = slice(part * sw + g * SSM_STATE, part * sw + (g + 1) * SSM_STATE)
                lo = (g % 2) * SSM_STATE
                c0_scr[g * SSM_GROUP:(g + 1) * SSM_GROUP, cols] = c0_ref[d, g, part, :, lo:lo + SSM_STATE]
                for s in range(TC):
                    rows = slice(s * LANES + g * SSM_GROUP, s * LANES + (g + 1) * SSM_GROUP)
                    bs_scr[rows, cols] = bs_ref[d, s, g, part, :, lo:lo + SSM_STATE]
                    cp_scr[rows, cols] = cp_ref[d, s, g, part, :, lo:lo + SSM_STATE]

        lag = _dot_nt(bs_scr[...], c0_scr[...]).astype(BF16)
        for s in range(TC):
            for t in range(TC):
                k = (t - s) if d == 0 else (s - t)
                src = (TC - 1 - k) if d == 0 else k
                blk = lag[src * LANES:(src + 1) * LANES, :] if k >= 0 else jnp.zeros((LANES, LANES), BF16)
                m_scr[s * LANES:(s + 1) * LANES, t * LANES:(t + 1) * LANES] = blk

        a = a_scr[...]
        yd = _dot(a, m_scr[...])
        if d == 0:
            y_scr[...] = yd
        else:
            y_scr[...] += yd

        sm = _dot(a, bs_scr[...])
        for b in range(B):
            for sl in range(nsl):
                s_scr[sl, b * P:b * P + nC, :] = sm[b * nC:(b + 1) * nC, sl * LANES:(sl + 1) * LANES]

        a_r = [jnp.broadcast_to(a8_ref[d, 0, :, q * LANES:(q + 1) * LANES], (B, LANES)) for q in range(half)]
        a_i = [jnp.broadcast_to(a8_ref[d, 1, :, q * LANES:(q + 1) * LANES], (B, LANES)) for q in range(half)]
        if has_h0:
            init = tuple(h0_ref[d, :, sl * LANES:(sl + 1) * LANES] for sl in range(nsl))
        else:
            init = tuple(jnp.zeros((B, LANES), F32) for _ in range(nsl))

        def step(j, carry, d=d, a_r=a_r, a_i=a_i):
            c = j if d == 0 else nC - 1 - j
            rows = pl.ds(c, B, stride=P)
            new_r, new_i = [], []
            for q in range(half):
                xr, xi = carry[q], carry[half + q]
                sr = s_scr[q, rows, :]
                si = s_scr[half + q, rows, :]
                s_scr[q, rows, :] = xr
                s_scr[half + q, rows, :] = xi
                new_r.append(a_r[q] * xr - a_i[q] * xi + sr)
                new_i.append(a_r[q] * xi + a_i[q] * xr + si)
            return tuple(new_r + new_i)

        fin = lax.fori_loop(0, nC, step, init)
        for sl in range(nsl):
            fin_ref[d, :, sl * LANES:(sl + 1) * LANES] = fin[sl]

        for b in range(B):
            for sl in range(nsl):
                x_scr[b * nC:(b + 1) * nC, sl * LANES:(sl + 1) * LANES] = \
                    s_scr[sl, b * P:b * P + nC, :].astype(BF16)
        y_scr[...] += _dot_nt(x_scr[...], cp_scr[...])

    for t in range(TC):
        y_ref[pl.ds(t, R, stride=TC), :] = y_scr[:, t * LANES:(t + 1) * LANES]


def _s5(z, ops, h0, *, B, L, layer):
    c0, bs, cp, a8 = ops
    T = B * L
    nC = L // S5_CHUNK
    nlb = MIX_W // LANES
    sc = 2 * S5_LBLK_GROUPS * SSM_STATE
    kc = S5_CHUNK * LANES
    su0 = SU_SECTION * MIX_W // LANES

    blocks = pl.BlockSpec((None, 2, None, S5_CHUNK, S5_LBLK_GROUPS, 2, SSM_GROUP, LANES),
                          lambda lb: (layer, 0, lb, 0, 0, 0, 0, 0))
    in_specs = [pl.BlockSpec((T, LANES), lambda lb: (0, su0 + lb)),
                pl.BlockSpec((None, 2, None, S5_LBLK_GROUPS, 2, SSM_GROUP, LANES),
                             lambda lb: (layer, 0, lb, 0, 0, 0, 0)),
                blocks, blocks,
                pl.BlockSpec((None, 2, None, 2, 1, sc // 2), lambda lb: (layer, 0, lb, 0, 0, 0))]
    args = [z, c0, bs, cp, a8]
    if h0 is not None:
        in_specs.append(pl.BlockSpec((2, None, B, sc), lambda lb: (0, lb, 0, 0)))
        args.append(h0)
    body = functools.partial(_s5_body, B=B, nC=nC, has_h0=h0 is not None)
    return pl.pallas_call(
        body,
        grid=(nlb,),
        in_specs=in_specs,
        out_specs=[pl.BlockSpec((T, LANES), lambda lb: (0, lb)),
                   pl.BlockSpec((2, None, B, sc), lambda lb: (0, lb, 0, 0))],
        out_shape=[jax.ShapeDtypeStruct((T, MIX_W), F32),
                   jax.ShapeDtypeStruct((2, nlb, B, sc), F32)],
        scratch_shapes=[pltpu.VMEM((T, LANES), F32), pltpu.VMEM((B * nC, kc), BF16),
                        pltpu.VMEM((kc, kc), BF16), pltpu.VMEM((kc, sc), BF16), pltpu.VMEM((kc, sc), BF16),
                        pltpu.VMEM((LANES, sc), BF16),
                        pltpu.VMEM((sc // LANES, B * (nC + S5_PITCH_PAD), LANES), F32),
                        pltpu.VMEM((B * nC, sc), BF16), pltpu.VMEM((B * nC, kc), F32)],
        name="s5",
        compiler_params=_params("arbitrary"),
    )(*args)


def _s5_operators(a_re, a_im, log_dt, b_re, b_im, c_re, c_im):
    TC = S5_CHUNK
    nlb = MIX_W // LANES
    ng = S5_LBLK_GROUPS
    lr = jnp.minimum(a_re, -1e-4)
    li = a_im
    dt = jnp.exp(log_dt)[..., None]
    k = jnp.arange(TC + 1, dtype=F32)[:, None, None, None]
    mag = jnp.exp(k * (lr * dt)[None])
    pr = mag * jnp.cos(k * (li * dt)[None])
    pi = mag * jnp.sin(k * (li * dt)[None])
    ar, ai = pr[1], pi[1]
    den = lr * lr + li * li
    sr = ((ar - 1.0) * lr + ai * li) / den
    si = (ai * lr - (ar - 1.0) * li) / den
    bbr = sr[..., None] * b_re[None] - si[..., None] * b_im[None]
    bbi = sr[..., None] * b_im[None] + si[..., None] * b_re[None]

    def lanes2(x):
        return jnp.concatenate([x, x], -1)

    def powers(fwd, bwd):
        x = lanes2(jnp.stack([fwd, bwd], 0)).reshape(2, TC, nlb, ng, 1, LANES)
        return jnp.swapaxes(x, 1, 2)

    def per_group(x):
        return lanes2(x).reshape(2, nlb, 1, ng, SSM_GROUP, LANES)

    er = powers(jnp.flip(pr[:TC, 0], 0), pr[:TC, 1])
    ei = powers(jnp.flip(pi[:TC, 0], 0), pi[:TC, 1])
    btr = per_group(jnp.swapaxes(bbr, -1, -2))
    bti = per_group(jnp.swapaxes(bbi, -1, -2))
    bs = jnp.stack([er * btr - ei * bti, er * bti + ei * btr], 4).astype(BF16)

    fr = powers(pr[1:, 0], jnp.flip(pr[1:, 1], 0))
    fi = powers(pi[1:, 0], jnp.flip(pi[1:, 1], 0))
    ctr = per_group(c_re)
    cti = per_group(c_im)
    cp = jnp.stack([ctr * fr - cti * fi, -(ctr * fi + cti * fr)], 4).astype(BF16)
    c0 = jnp.stack([ctr, -cti], 4)[:, :, 0].astype(BF16)

    sw = ng * SSM_STATE
    a8 = jnp.stack([pr[TC].reshape(2, nlb, 1, sw), pi[TC].reshape(2, nlb, 1, sw)], 2)
    return c0, bs, cp, a8


def _head_masks(shape):
    lane = lax.broadcasted_iota(jnp.int32, shape, 1)
    return lane < NA_HEAD_DIM


def _cattn_body(q_ref, k_ref, v_ref, o_ref):
    first = _head_masks(q_ref.shape[1:])
    nq = q_ref.shape[1]
    for bb in range(q_ref.shape[0]):
        q = q_ref[bb]
        qs = jnp.concatenate([jnp.where(first, q, jnp.zeros_like(q)), jnp.where(first, jnp.zeros_like(q), q)], 0)
        s = _dot_nt(qs, k_ref[bb]) * (NA_HEAD_DIM ** -0.5)
        m = jnp.max(s, -1, keepdims=True)
        p = jnp.exp(s - m)
        l = jnp.sum(p, -1, keepdims=True)
        o = _dot(p.astype(BF16), v_ref[bb]) / l
        o_ref[bb] = jnp.where(first, o[:nq], o[nq:]).astype(BF16)


def _context_attention(z, *, B, L):
    nblk = MIX_W // LANES
    nb = CTX_ATTN_BATCH

    def sec(s):
        return pl.BlockSpec((nb, L, LANES), lambda b, hp: (b, 0, s * nblk + hp))

    return pl.pallas_call(
        _cattn_body,
        grid=(B // nb, nblk),
        in_specs=[sec(5), sec(6), sec(7)],
        out_specs=pl.BlockSpec((nb, L, LANES), lambda b, hp: (b, 0, hp)),
        out_shape=jax.ShapeDtypeStruct((B, L, MIX_W), BF16),
        name="ctx_attention",
        compiler_params=_params("arbitrary", "arbitrary"),
    )(z, z, z)


def _na_chunks(rows):
    half = NA_KR // 2
    plan, kinds = [], []
    for r0 in range(0, rows, NA_CHUNK_ROWS):
        rs = [min(max(r - half, 0), rows - NA_KR) for r in range(r0, r0 + NA_CHUNK_ROWS)]
        ws = min(rs[0], rows - NA_WIN_ROWS)
        assert rs[-1] + NA_KR <= ws + NA_WIN_ROWS
        kind = tuple((r0 + n - ws, rs[n] - ws) for n in range(NA_CHUNK_ROWS))
        if kind not in kinds:
            kinds.append(kind)
        plan.append((ws, kinds.index(kind)))
    return plan, kinds


def _na_body(q_ref, k_ref, v_ref, kc_ref, vc_ref, tb_ref, o_ref, bias_scr, *, rows):
    scale = NA_HEAD_DIM ** -0.5
    nq = NA_CHUNK_ROWS * GRID_W
    plan, kinds = _na_chunks(rows)
    n_off = 2 * NA_KR - 1

    @pl.when(pl.program_id(1) == 0)
    def _():
        for t, kind in enumerate(kinds):
            for e in range(2):
                for n, (r_rel, rs_rel) in enumerate(kind):
                    for kj in range(NA_WIN_ROWS):
                        off = kj - r_rel + NA_KR - 1 if rs_rel <= kj < rs_rel + NA_KR else n_off
                        lo = (kj % 2) * GRID_W
                        bias_scr[t, e * nq + n * GRID_W:e * nq + (n + 1) * GRID_W, kj * GRID_W:(kj + 1) * GRID_W] = \
                            tb_ref[e, off, :, lo:lo + GRID_W]

    kctx = kc_ref[...].astype(BF16)
    vctx = vc_ref[...].astype(BF16)
    first = _head_masks((nq, LANES))
    for c, (ws, kind) in enumerate(plan):
        qc = q_ref[c * nq:(c + 1) * nq, :]
        qs = jnp.concatenate([jnp.where(first, qc, jnp.zeros_like(qc)),
                              jnp.where(first, jnp.zeros_like(qc), qc)], 0)
        kw = k_ref[ws * GRID_W:(ws + NA_WIN_ROWS) * GRID_W, :]
        vw = v_ref[ws * GRID_W:(ws + NA_WIN_ROWS) * GRID_W, :]
        s_loc = _dot_nt(qs, kw) * scale + bias_scr[kind]
        s_ctx = _dot_nt(qs, kctx) * scale
        m = jnp.maximum(jnp.max(s_loc, -1, keepdims=True), jnp.max(s_ctx, -1, keepdims=True))
        p_loc = jnp.exp(s_loc - m)
        p_ctx = jnp.exp(s_ctx - m)
        l = jnp.sum(p_loc, -1, keepdims=True) + jnp.sum(p_ctx, -1, keepdims=True)
        o = (_dot(p_loc.astype(BF16), vw) + _dot(p_ctx.astype(BF16), vctx)) / l
        o_ref[c * nq:(c + 1) * nq, :] = jnp.where(first, o[:nq], o[nq:]).astype(BF16)


def _neighbourhood_attention(z, cache_k, cache_v, blocks, *, B, L, layer):
    nblk = MIX_W // LANES
    rows = L // GRID_W
    Lc = cache_k.shape[2]
    _, kinds = _na_chunks(rows)

    def sec(s):
        return pl.BlockSpec((None, L, LANES), lambda hp, b: (b, 0, s * nblk + hp))

    ctx = pl.BlockSpec((None, None, Lc, LANES), lambda hp, b: (b, layer, 0, hp))
    return pl.pallas_call(
        functools.partial(_na_body, rows=rows),
        grid=(nblk, B),
        in_specs=[sec(5), sec(6), sec(7), ctx, ctx,
                  pl.BlockSpec((None, None, 2, 2 * NA_KR, GRID_W, 2 * GRID_W),
                               lambda hp, b: (layer, hp, 0, 0, 0, 0))],
        out_specs=pl.BlockSpec((None, L, LANES), lambda hp, b: (b, 0, hp)),
        out_shape=jax.ShapeDtypeStruct((B, L, MIX_W), BF16),
        scratch_shapes=[pltpu.VMEM((len(kinds), 2 * NA_CHUNK_ROWS * GRID_W, NA_WIN_ROWS * GRID_W), F32)],
        name="nbr_attention",
        compiler_params=_params("arbitrary", "arbitrary"),
    )(z, z, z, cache_k, cache_v, blocks)


def _na_bias_blocks(rpb):
    nr, nc = 2 * NA_KR - 1, 2 * NA_KW - 1
    qc = np.arange(GRID_W)
    kc = np.arange(GRID_W)
    ws = np.clip(qc - NA_KW // 2, 0, GRID_W - NA_KW)
    col_ok = (kc[None, :] >= ws[:, None]) & (kc[None, :] < ws[:, None] + NA_KW)
    coff = np.clip(kc[None, :] - qc[:, None] + NA_KW - 1, 0, nc - 1)
    sel_c = ((coff[None] == np.arange(nc)[:, None, None]) & col_ok[None]).astype(np.float32)
    sel_c = np.concatenate([sel_c, sel_c], -1)
    ok = np.concatenate([col_ok, col_ok], -1)[None] & (np.arange(nr + 1) < nr)[:, None, None]
    H = rpb.shape[0]
    rows = jnp.pad(rpb.astype(F32), ((0, 0), (0, 1), (0, 0)))
    t = jnp.einsum('hrc,cqk->hrqk', rows, sel_c, precision=lax.Precision.HIGHEST)
    return jnp.where(ok[None], t, NEG_INF).reshape(H // 2, 2, nr + 1, GRID_W, 2 * GRID_W)


def _merge_body(x_ref, p_ref, r_ref, u_ref, y_ref, n_ref, ga_ref, gb_ref, gc_ref,
                d_ref, wglu_ref, wbr_ref, wo_ref, lg_ref, lb_ref, o_ref,
                wglu_s, wbr_s, wo_s, *, L, row0, rstride):
    i = pl.program_id(0)
    tm = x_ref.shape[0]

    @pl.when(i == 0)
    def _():
        wglu_s[...] = wglu_ref[...].astype(BF16)
        wbr_s[...] = wbr_ref[...].astype(BF16)
        wo_s[...] = wo_ref[...].astype(BF16)

    row = row0 + rstride * ((i * tm) // L)
    g1 = _mod_row(p_ref, row, 2)

    y = d_ref[...] * u_ref[...].astype(F32) + y_ref[...]
    y = jax.nn.gelu(y)
    s_out = y * jax.nn.sigmoid(_dot(y.astype(BF16), wglu_s[...]))

    def gate(ref):
        return jax.nn.sigmoid(ref[...].astype(F32))

    merged = (gate(ga_ref) * _dot(r_ref[...], wbr_s[0])
              + gate(gb_ref) * _dot(s_out.astype(BF16), wbr_s[1])
              + gate(gc_ref) * _dot(n_ref[...], wbr_s[2]))
    m = _dot(merged.astype(BF16), wo_s[...])
    o_ref[...] = _layer_norm(DEEPNORM_ALPHA * x_ref[...] + g1 * m, lg_ref[...], lb_ref[...])


def _merge(x, p, z, r_out, y, n_out, ssm_d, w_glu, w_branch, w_o, ln_g, ln_b, *, layer, L, row0, rstride):
    T = x.shape[0]
    tm = MERGE_TILE
    assert L % tm == 0 or rstride == 0
    gate0 = 8 * MIX_W // D_MODEL

    def tok(w):
        return pl.BlockSpec((tm, w), lambda i: (i, 0))

    def full(shape):
        return pl.BlockSpec((None,) + shape, lambda i: (layer,) + (0,) * len(shape))

    body = functools.partial(_merge_body, L=L, row0=row0, rstride=rstride)
    return pl.pallas_call(
        body,
        grid=(T // tm,),
        in_specs=[tok(D_MODEL), full((N_PAD_ROWS, 6 * D_MODEL)), tok(MIX_W),
                  pl.BlockSpec((tm, MIX_W), lambda i: (i, SU_SECTION)), tok(MIX_W), tok(MIX_W),
                  pl.BlockSpec((tm, D_MODEL), lambda i: (i, gate0)),
                  pl.BlockSpec((tm, D_MODEL), lambda i: (i, gate0 + 1)),
                  pl.BlockSpec((tm, D_MODEL), lambda i: (i, gate0 + 2)),
                  full((1, MIX_W)), full((MIX_W, MIX_W)), full((3, MIX_W, D_MODEL)),
                  full((D_MODEL, D_MODEL)), full((1, D_MODEL)), full((1, D_MODEL))],
        out_specs=tok(D_MODEL),
        out_shape=jax.ShapeDtypeStruct((T, D_MODEL), F32),
        scratch_shapes=[pltpu.VMEM((MIX_W, MIX_W), BF16), pltpu.VMEM((3, MIX_W, D_MODEL), BF16),
                        pltpu.VMEM((D_MODEL, D_MODEL), BF16)],
        name="merge",
        compiler_params=_params("arbitrary"),
    )(x, p, r_out, z, y, n_out, z, z, z, ssm_d.reshape(DEPTH, 1, MIX_W), w_glu, w_branch, w_o,
      ln_g.reshape(DEPTH, 1, D_MODEL), ln_b.reshape(DEPTH, 1, D_MODEL))


def _ffn_body(x_ref, p_ref, wa_ref, wb_ref, cwa_ref, cwb_ref, cba_ref, cbb_ref, wd_ref, lg_ref, lb_ref,
              o_ref, h_scr, acc_scr, mp_scr, mn_scr, *, L, row0, rstride):
    i = pl.program_id(0)
    j = pl.program_id(1)
    tm = x_ref.shape[0]
    nb = tm // L

    @pl.when(j == 0)
    def _():
        for s in range(nb):
            row = row0 + rstride * (i * nb + s)
            sh = _mod_row(p_ref, row, 3)
            sc = _mod_row(p_ref, row, 4)
            h_scr[s * L:(s + 1) * L, :] = (x_ref[s * L:(s + 1) * L, :] * (1.0 + sc) + sh).astype(BF16)
        acc_scr[...] = jnp.zeros_like(acc_scr)
        t = lax.broadcasted_iota(jnp.int32, (tm, FF_TILE), 0) % L
        mp_scr[...] = (t != 0).astype(BF16)
        mn_scr[...] = (t != L - 1).astype(BF16)

    def conv(w_ref, cw_ref, cb_ref):
        zc = _dot(h_scr[...], w_ref[...].astype(BF16))
        zp = pltpu.roll(zc, 1, 0).astype(BF16) * mp_scr[...]
        zn = pltpu.roll(zc, tm - 1, 0).astype(BF16) * mn_scr[...]
        cw = cw_ref[...].astype(BF16)
        return zp * cw[0:1, :] + zc.astype(BF16) * cw[1:2, :] + zn * cw[2:3, :] + cb_ref[...].astype(BF16)

    a = conv(wa_ref, cwa_ref, cba_ref)
    b = conv(wb_ref, cwb_ref, cbb_ref)
    acc_scr[...] += _dot(jax.nn.gelu(a) * b, wd_ref[...].astype(BF16))

    @pl.when(j == pl.num_programs(1) - 1)
    def _():
        for s in range(nb):
            row = row0 + rstride * (i * nb + s)
            g2 = _mod_row(p_ref, row, 5)
            sl = slice(s * L, (s + 1) * L)
            o_ref[sl, :] = _layer_norm(DEEPNORM_ALPHA * x_ref[sl, :] + g2 * acc_scr[sl, :],
                                       lg_ref[...], lb_ref[...])


def _conv_ffn(x, p, w_up, conv_w, conv_b, w_down, ln_g, ln_b, *, layer, L, row0, rstride):
    T = x.shape[0]
    tm = TOKEN_TILE
    nff = D_FF // FF_TILE
    body = functools.partial(_ffn_body, L=L, row0=row0, rstride=rstride)
    conv_b = conv_b.reshape(DEPTH, 1, 2 * D_FF)
    return pl.pallas_call(
        body,
        grid=(T // tm, nff),
        in_specs=[pl.BlockSpec((tm, D_MODEL), lambda i, j: (i, 0)),
                  pl.BlockSpec((None, N_PAD_ROWS, 6 * D_MODEL), lambda i, j: (layer, 0, 0)),
                  pl.BlockSpec((None, D_MODEL, FF_TILE), lambda i, j: (layer, 0, j)),
                  pl.BlockSpec((None, D_MODEL, FF_TILE), lambda i, j: (layer, 0, nff + j)),
                  pl.BlockSpec((None, 3, FF_TILE), lambda i, j: (layer, 0, j)),
                  pl.BlockSpec((None, 3, FF_TILE), lambda i, j: (layer, 0, nff + j)),
                  pl.BlockSpec((None, 1, FF_TILE), lambda i, j: (layer, 0, j)),
                  pl.BlockSpec((None, 1, FF_TILE), lambda i, j: (layer, 0, nff + j)),
                  pl.BlockSpec((None, FF_TILE, D_MODEL), lambda i, j: (layer, j, 0)),
                  pl.BlockSpec((None, 1, D_MODEL), lambda i, j: (layer, 0, 0)),
                  pl.BlockSpec((None, 1, D_MODEL), lambda i, j: (layer, 0, 0))],
        out_specs=pl.BlockSpec((tm, D_MODEL), lambda i, j: (i, 0)),
        out_shape=jax.ShapeDtypeStruct((T, D_MODEL), F32),
        scratch_shapes=[pltpu.VMEM((tm, D_MODEL), BF16), pltpu.VMEM((tm, D_MODEL), F32),
                        pltpu.VMEM((tm, FF_TILE), BF16), pltpu.VMEM((tm, FF_TILE), BF16)],
        name="conv_ffn",
        compiler_params=_params("arbitrary", "arbitrary"),
    )(x, p, w_up, w_up, conv_w, conv_w, conv_b, conv_b, w_down, ln_g.reshape(DEPTH, 1, D_MODEL),
      ln_b.reshape(DEPTH, 1, D_MODEL))


def _s5_states_in(state_ssm, layer):
    B = state_ssm.shape[0]
    nlb = MIX_W // LANES
    h = state_ssm[:, layer].reshape(B, 2, nlb, S5_LBLK_GROUPS, SSM_STATE, 2)
    return jnp.transpose(h, (1, 2, 0, 5, 3, 4)).reshape(2, nlb, B, 2 * S5_LBLK_GROUPS * SSM_STATE)


def _s5_states_out(fin):
    nlb, B = fin.shape[1], fin.shape[2]
    h = fin.reshape(2, nlb, B, 2, S5_LBLK_GROUPS, SSM_STATE)
    return jnp.transpose(h, (2, 0, 1, 4, 5, 3)).reshape(B, 2, SSM_GROUPS, SSM_STATE, 2)


def _layer(x, p, lw, *, B, L, row0, rstride, latent, layer, extra):
    T = B * L
    log_gamma = lw['log_gamma']
    if latent:
        z, = _inproj(x, p, lw['w_in'], layer=layer, L=L, row0=row0, rstride=rstride)
        z3 = z.reshape(B, L, IN_COLS)
        r_out = _retention(z3, log_gamma, B=B, L=L, rope_tabs=extra['rope'], s0=extra['state_ret'],
                           layer=layer)[0]
        n_out = _neighbourhood_attention(z3, extra['cache_k'], extra['cache_v'], extra['bias'],
                                         B=B, L=L, layer=layer)
        y, _ = _s5(z, lw['s5'], _s5_states_in(extra['state_ssm'], layer), B=B, L=L, layer=layer)
        states = None
    else:
        ret_buf, k_buf, v_buf = extra
        z, k_buf, v_buf = _inproj(x, p, lw['w_in'], layer=layer, L=L, row0=row0, rstride=rstride,
                                  kv_bufs=(k_buf, v_buf))
        z3 = z.reshape(B, L, IN_COLS)
        r_out, ret_buf = _retention(z3, log_gamma, B=B, L=L, layer=layer, state_buf=ret_buf)
        n_out = _context_attention(z3, B=B, L=L)
        y, fin = _s5(z, lw['s5'], None, B=B, L=L, layer=layer)
        states = ((ret_buf, k_buf, v_buf), _s5_states_out(fin))
    x = _merge(x, p, z, r_out.reshape(T, MIX_W), y, n_out.reshape(T, MIX_W),
               lw['ssm_d'], lw['ssm_w_glu'], lw['w_branch'], lw['w_o'], lw['ln1_g'], lw['ln1_b'],
               layer=layer, L=L, row0=row0, rstride=rstride)
    x = _conv_ffn(x, p, lw['w_up'], lw['conv_w'], lw['conv_b'], lw['w_down'], lw['ln2_g'], lw['ln2_b'],
                  layer=layer, L=L, row0=row0, rstride=rstride)
    return x, states


def kernel(x_prompt, x_sample, state_ret, state_ssm, cache_na_k, cache_na_v, c, c_ctx, w_ada, b_ada, w_in, ret_decay, ssm_a_re, ssm_a_im, ssm_log_dt, ssm_b_re, ssm_b_im, ssm_c_re, ssm_c_im, ssm_d, ssm_w_glu, na_rpb, w_branch, w_o, ln1_g, ln1_b, w_up, conv_w, conv_b, w_down, ln2_g, ln2_b):
    B, L, _ = x_prompt.shape
    Bd, Ld, _ = x_sample.shape
    Lc = cache_na_k.shape[2]

    cond = jnp.concatenate([c_ctx[None, :], c, jnp.zeros((N_PAD_ROWS - 1 - Bd, D_MODEL), F32)], 0)
    p_all = _ada(cond, w_ada, b_ada)

    extra = dict(rope=_rope_tables(Ld), state_ret=state_ret, state_ssm=state_ssm,
                 cache_k=cache_na_k.reshape(Bd, DEPTH, Lc, MIX_W),
                 cache_v=cache_na_v.reshape(Bd, DEPTH, Lc, MIX_W),
                 bias=jax.vmap(_na_bias_blocks)(na_rpb))

    xp = x_prompt.reshape(B * L, D_MODEL)
    xs = x_sample.reshape(Bd * Ld, D_MODEL)
    bufs = (jnp.zeros((B, DEPTH, 2, N_RET_HEADS, RET_DK, RET_DK), F32),
            jnp.zeros((B, DEPTH, L, MIX_W), F32), jnp.zeros((B, DEPTH, L, MIX_W), F32))
    ssm_states = []
    lw = dict(w_in=w_in, log_gamma=jax.nn.log_sigmoid(ret_decay.astype(F32)), ssm_d=ssm_d, ssm_w_glu=ssm_w_glu,
              w_branch=w_branch, w_o=w_o, ln1_g=ln1_g, ln1_b=ln1_b, w_up=w_up,
              conv_w=conv_w, conv_b=conv_b, w_down=w_down, ln2_g=ln2_g, ln2_b=ln2_b,
              s5=jax.vmap(_s5_operators)(ssm_a_re, ssm_a_im, ssm_log_dt, ssm_b_re, ssm_b_im, ssm_c_re, ssm_c_im))
    for l in range(DEPTH):
        xp, (bufs, s_ssm) = _layer(xp, p_all, lw, B=B, L=L, row0=0, rstride=0, latent=False, layer=l, extra=bufs)
        ssm_states.append(s_ssm)
        xs, _ = _layer(xs, p_all, lw, B=Bd, L=Ld, row0=1, rstride=1, latent=True, layer=l, extra=extra)
    ret_buf, k_buf, v_buf = bufs
    heads = (B, DEPTH, L, NA_HEADS, NA_HEAD_DIM)
    return (xp.reshape(B, L, D_MODEL), xs.reshape(Bd, Ld, D_MODEL),
            ret_buf, jnp.stack(ssm_states, 1), k_buf.reshape(heads), v_buf.reshape(heads))
```

```python
import functools

import jax
import jax.numpy as jnp
import numpy as np
from jax import lax
from jax.experimental import pallas as pl
from jax.experimental.pallas import tpu as pltpu

F32 = jnp.float32
BF16 = jnp.bfloat16

D_MODEL = 1024
DEPTH = 2
GRID_W = 64
MIX_W = D_MODEL // 2
N_RET_HEADS = 4
RET_DK = MIX_W // N_RET_HEADS
SSM_GROUP = 16
SSM_GROUPS = MIX_W // SSM_GROUP
SSM_STATE = 64
NA_HEADS = 8
NA_HEAD_DIM = MIX_W // NA_HEADS
NA_KR = 8
NA_KW = 16
D_FF = ((8 * D_MODEL // 3 + 127) // 128) * 128
ROPE_BASE = 10000.0
LN_EPS = 1e-5
NEG_INF = -1e30
DEEPNORM_ALPHA = (2 * DEPTH) ** 0.25
IN_COLS = 8 * MIX_W + 3 * D_MODEL

VMEM_LIMIT_BYTES = 56 * 1024 * 1024
LANES = 128

TOKEN_TILE = 1024
MERGE_TILE = 512
COL_TILE = 1024
SU_SECTION = 4
NK_SECTION = 6
FF_TILE = 256
RET_CHUNK = 256
RET_ROWS = 1024
S5_CHUNK = 8
S5_PITCH_PAD = 8
S5_LBLK_GROUPS = LANES // SSM_GROUP
N_PAD_ROWS = 8
CTX_ATTN_BATCH = 4
NA_CHUNK_ROWS = 4
NA_WIN_ROWS = 12


def _params(*sem):
    return pltpu.CompilerParams(dimension_semantics=sem, vmem_limit_bytes=VMEM_LIMIT_BYTES)


def _dot(a, b):
    return jnp.dot(a, b, preferred_element_type=F32)


def _dot_nt(a, b):
    return lax.dot_general(a, b, (((1,), (1,)), ((), ())), preferred_element_type=F32)


def _layer_norm(x, g, b):
    mu = jnp.mean(x, -1, keepdims=True)
    xc = x - mu
    var = jnp.mean(xc * xc, -1, keepdims=True)
    return xc * lax.rsqrt(var + LN_EPS) * g + b


def _ada_body(c_ref, w_ref, b_ref, o_ref):
    c = c_ref[...]
    s = c * jax.nn.sigmoid(c)
    o_ref[...] = _dot(s.astype(BF16), w_ref[...].astype(BF16)) + b_ref[...]


def _ada(cond, w_ada, b_ada):
    tn = 1024
    return pl.pallas_call(
        _ada_body,
        grid=(DEPTH, 6 * D_MODEL // tn),
        in_specs=[pl.BlockSpec((N_PAD_ROWS, D_MODEL), lambda l, j: (0, 0)),
                  pl.BlockSpec((None, D_MODEL, tn), lambda l, j: (l, 0, j)),
                  pl.BlockSpec((None, 1, tn), lambda l, j: (l, 0, j))],
        out_specs=pl.BlockSpec((None, N_PAD_ROWS, tn), lambda l, j: (l, 0, j)),
        out_shape=jax.ShapeDtypeStruct((DEPTH, N_PAD_ROWS, 6 * D_MODEL), F32),
        name="ada",
        compiler_params=_params("arbitrary", "arbitrary"),
    )(cond, w_ada, b_ada.reshape(DEPTH, 1, 6 * D_MODEL))


def _mod_row(p_ref, row, k):
    return p_ref[pl.ds(row, 1), k * D_MODEL:(k + 1) * D_MODEL]


def _kv_tile(n):
    col = (NK_SECTION + n) * MIX_W
    return col // COL_TILE, col % COL_TILE


def _inproj_body(x_ref, p_ref, w_ref, *rest, L, row0, rstride, n_kv):
    z_ref = rest[n_kv]
    kv_refs = rest[n_kv + 1:-2]
    h_scr, w_scr = rest[-2:]
    j = pl.program_id(0)
    i = pl.program_id(1)
    nb = x_ref.shape[0] // L

    @pl.when(j == 0)
    def _():
        for s in range(nb):
            row = row0 + rstride * (i * nb + s)
            sh = _mod_row(p_ref, row, 0)
            sc = _mod_row(p_ref, row, 1)
            h_scr[i, s * L:(s + 1) * L, :] = (x_ref[s * L:(s + 1) * L, :] * (1.0 + sc) + sh).astype(BF16)

    @pl.when(i == 0)
    def _():
        w_scr[...] = w_ref[...].astype(BF16)

    acc = _dot(h_scr[i], w_scr[...])
    z_ref[...] = acc.astype(BF16)

    for n, ref in enumerate(kv_refs):
        tile, off = _kv_tile(n)

        @pl.when(j == tile)
        def _(ref=ref, off=off):
            ref[...] = acc[:, off:off + MIX_W].reshape(ref.shape)


def _inproj(x, p, w_in, *, layer, L, row0, rstride, kv_bufs=()):
    T = x.shape[0]
    tm = TOKEN_TILE
    n_i = T // tm
    nb = tm // L
    n_kv = len(kv_bufs)
    body = functools.partial(_inproj_body, L=L, row0=row0, rstride=rstride, n_kv=n_kv)

    def only_at(tile):
        return lambda j, i: jnp.where(j < tile, 0, jnp.where(j > tile, n_i - 1, i))

    kv_i = [only_at(_kv_tile(n)[0]) for n in range(n_kv)]
    return pl.pallas_call(
        body,
        grid=(IN_COLS // COL_TILE, n_i),
        in_specs=[pl.BlockSpec((tm, D_MODEL), lambda j, i: (jnp.where(j == 0, i, n_i - 1), 0)),
                  pl.BlockSpec((None, N_PAD_ROWS, 6 * D_MODEL), lambda j, i: (layer, 0, 0)),
                  pl.BlockSpec((None, D_MODEL, COL_TILE), lambda j, i: (layer, 0, j))]
        + [pl.BlockSpec(memory_space=pl.ANY)] * n_kv,
        out_specs=[pl.BlockSpec((tm, COL_TILE), lambda j, i: (i, j))]
        + [pl.BlockSpec((nb, None, L, MIX_W), lambda j, i, f=f: (f(j, i), layer, 0, 0)) for f in kv_i],
        out_shape=[jax.ShapeDtypeStruct((T, IN_COLS), BF16)]
        + [jax.ShapeDtypeStruct(b.shape, b.dtype) for b in kv_bufs],
        input_output_aliases={3 + n: 1 + n for n in range(n_kv)},
        scratch_shapes=[pltpu.VMEM((n_i, tm, D_MODEL), BF16), pltpu.VMEM((D_MODEL, COL_TILE), BF16)],
        name="inproj",
        compiler_params=_params("arbitrary", "arbitrary"),
    )(x, p, w_in, *kv_bufs)


def _rope(x, cos, s_up, s_dn):
    return x * cos + pltpu.roll(x, 96, 1) * s_up + pltpu.roll(x, 32, 1) * s_dn


def _ret_body(*refs, n, rope, has_s0, want_state, layer):
    refs = list(refs)
    lg_ref, q_ref, k_ref, v_ref, g_ref = refs[:5]
    refs = refs[5:]
    if rope:
        cos_ref, sup_ref, sdn_ref = refs[:3]
        refs = refs[3:]
    if has_s0:
        s0_ref = refs[0]
        refs = refs[1:]
    if want_state:
        refs = refs[1:]
    o_ref = refs[0]
    refs = refs[1:]
    if want_state:
        st_ref = refs[0]
        refs = refs[1:]
    q_scr, k_scr, sb_scr, decay_scr = refs

    C = RET_CHUNK
    h = pl.program_id(0)
    lf = lg_ref[layer, 0, h]
    lb = lg_ref[layer, 1, h]

    @pl.when(pl.program_id(1) == 0)
    def _():
        ti = lax.broadcasted_iota(jnp.int32, (C, C), 0)
        si = lax.broadcasted_iota(jnp.int32, (C, C), 1)
        dlt = (ti - si).astype(F32)
        decay_scr[...] = (jnp.where(dlt >= 0, jnp.exp(lf * jnp.maximum(dlt, 0.0)), 0.0)
                          + jnp.where(dlt <= 0, jnp.exp(lb * jnp.maximum(-dlt, 0.0)), 0.0))

    tcol = lax.broadcasted_iota(jnp.int32, (C, 1), 0).astype(F32)
    qd_f = jnp.exp(lf * (tcol + 1.0))
    qd_b = jnp.exp(lb * (C - tcol))
    kd_f = jnp.exp(lf * (C - 1.0 - tcol))
    kd_b = jnp.exp(lb * tcol)
    cd_f = jnp.exp(lf * jnp.full((1, RET_DK), float(C), F32))
    cd_b = jnp.exp(lb * jnp.full((1, RET_DK), float(C), F32))

    def kv_outer(kc, vc, kd):
        return lax.dot_general((kc * kd).astype(BF16), vc, (((0,), (0,)), ((), ())), preferred_element_type=F32)

    for bb in range(q_ref.shape[0]):
        q = q_ref[bb].astype(F32)
        k = k_ref[bb].astype(F32)
        if rope:
            q = _rope(q, cos_ref[...], sup_ref[...], sdn_ref[...])
            k = _rope(k, cos_ref[...], sup_ref[...], sdn_ref[...])
        q_scr[bb] = q
        k_scr[bb] = k * (RET_DK ** -0.5)

        s_b = s0_ref[bb, 1] if has_s0 else jnp.zeros((RET_DK, RET_DK), F32)
        for i in reversed(range(n)):
            sb_scr[bb, i] = s_b
            if i > 0 or want_state:
                s_b = s_b * cd_b + kv_outer(k_scr[bb, i * C:(i + 1) * C, :], v_ref[bb, i * C:(i + 1) * C, :], kd_b)

        s_f = s0_ref[bb, 0] if has_s0 else jnp.zeros((RET_DK, RET_DK), F32)
        for i in range(n):
            sl = slice(i * C, (i + 1) * C)
            qc = q_scr[bb, sl, :]
            kc = k_scr[bb, sl, :]
            vc = v_ref[bb, sl, :]
            att = _dot_nt(qc.astype(BF16), kc.astype(BF16)) * decay_scr[...]
            o = _dot(att.astype(BF16), vc)
            o = o + _dot((qc * qd_f).astype(BF16), s_f.astype(BF16))
            o = o + _dot((qc * qd_b).astype(BF16), sb_scr[bb, i].astype(BF16))
            mu = jnp.mean(o, -1, keepdims=True)
            oc = o - mu
            var = jnp.mean(oc * oc, -1, keepdims=True)
            gc = g_ref[bb, sl, :].astype(F32)
            o_ref[bb, sl, :] = (oc * lax.rsqrt(var + LN_EPS) * (gc * jax.nn.sigmoid(gc))).astype(BF16)
            if i < n - 1 or want_state:
                s_f = s_f * cd_f + kv_outer(kc, vc, kd_f)

        if want_state:
            st_ref[bb, 0] = s_f
            st_ref[bb, 1] = s_b


def _retention(z, log_gamma, *, B, L, rope_tabs=None, s0=None, layer=0, state_buf=None):
    want_state = state_buf is not None
    n = L // RET_CHUNK
    H = N_RET_HEADS
    nblk = MIX_W // RET_DK

    nbb = max(1, RET_ROWS // L)
    assert B % nbb == 0

    def sec(s):
        return pl.BlockSpec((nbb, L, RET_DK), lambda h, b: (b, 0, s * nblk + h))

    in_specs = [pl.BlockSpec(memory_space=pltpu.SMEM), sec(0), sec(1), sec(2), sec(3)]
    args = [log_gamma, z, z, z, z]
    if rope_tabs is not None:
        in_specs += [pl.BlockSpec((L, RET_DK), lambda h, b: (0, 0))] * 3
        args += list(rope_tabs)
    if s0 is not None:
        in_specs.append(pl.BlockSpec((nbb, None, 2, None, RET_DK, RET_DK), lambda h, b: (b, layer, 0, h, 0, 0)))
        args.append(s0)
    out_specs = [pl.BlockSpec((nbb, L, RET_DK), lambda h, b: (b, 0, h))]
    out_shape = [jax.ShapeDtypeStruct((B, L, MIX_W), BF16)]
    aliases = {}
    if want_state:
        aliases = {len(args): 1}
        in_specs.append(pl.BlockSpec(memory_space=pl.ANY))
        args.append(state_buf)
        out_specs.append(pl.BlockSpec((nbb, None, 2, None, RET_DK, RET_DK), lambda h, b: (b, layer, 0, h, 0, 0)))
        out_shape.append(jax.ShapeDtypeStruct(state_buf.shape, state_buf.dtype))
    body = functools.partial(_ret_body, n=n, rope=rope_tabs is not None, has_s0=s0 is not None,
                             want_state=want_state, layer=layer)
    return pl.pallas_call(
        body,
        grid=(H, B // nbb),
        in_specs=in_specs,
        out_specs=out_specs,
        out_shape=out_shape,
        input_output_aliases=aliases,
        scratch_shapes=[pltpu.VMEM((nbb, L, RET_DK), F32), pltpu.VMEM((nbb, L, RET_DK), F32),
                        pltpu.VMEM((nbb, n, RET_DK, RET_DK), F32), pltpu.VMEM((RET_CHUNK, RET_CHUNK), F32)],
        name="retention",
        compiler_params=_params("arbitrary", "arbitrary"),
    )(*args)


def _rope_tables(L):
    pos = jnp.arange(L)
    row = (pos // GRID_W).astype(F32)
    col = (pos % GRID_W).astype(F32)
    quarter = RET_DK // 4
    inv_freq = ROPE_BASE ** (-jnp.arange(quarter, dtype=F32) / quarter)
    ang_r = row[:, None] * inv_freq[None, :]
    ang_c = col[:, None] * inv_freq[None, :]
    zero = jnp.zeros_like(ang_r)
    cos = jnp.concatenate([jnp.cos(ang_r), jnp.cos(ang_r), jnp.cos(ang_c), jnp.cos(ang_c)], -1)
    s_up = jnp.concatenate([-jnp.sin(ang_r), zero, -jnp.sin(ang_c), zero], -1)
    s_dn = jnp.concatenate([zero, jnp.sin(ang_r), zero, jnp.sin(ang_c)], -1)
    return cos, s_up, s_dn


def _s5_body(*refs, B, nC, has_h0):
    refs = list(refs)
    u_ref, c0_ref, bs_ref, cp_ref, a8_ref = refs[:5]
    refs = refs[5:]
    if has_h0:
        h0_ref = refs[0]
        refs = refs[1:]
    y_ref, fin_ref, u_scr, a_scr, m_scr, bs_scr, cp_scr, c0_scr, s_scr, x_scr, y_scr = refs

    TC = S5_CHUNK
    R = B * nC
    P = nC + S5_PITCH_PAD
    nsl = s_scr.shape[0]
    half = nsl // 2
    ng = S5_LBLK_GROUPS
    sw = ng * SSM_STATE

    u_scr[...] = u_ref[...].astype(F32)
    for s in range(TC):
        a_scr[:, s * LANES:(s + 1) * LANES] = u_scr[pl.ds(s, R, stride=TC), :].astype(BF16)

    bs_scr[...] = jnp.zeros_like(bs_scr)
    cp_scr[...] = jnp.zeros_like(cp_scr)
    c0_scr[...] = jnp.zeros_like(c0_scr)

    for d in range(2):
        for g in range(ng):
            for part in range(2):
                cols = slice(part * sw + g * SSM_STATE, part * sw + (g + 1) * SSM_STATE)
                lo = (g % 2) * SSM_STATE
                c0_scr[g * SSM_GROUP:(g + 1) * SSM_GROUP, cols] = c0_ref[d, g, part, :, lo:lo + SSM_STATE]
                for s in range(TC):
                    rows = slice(s * LANES + g * SSM_GROUP, s * LANES + (g + 1) * SSM_GROUP)
                    bs_scr[rows, cols] = bs_ref[d, s, g, part, :, lo:lo + SSM_STATE]
                    cp_scr[rows, cols] = cp_ref[d, s, g, part, :, lo:lo + SSM_STATE]

        lag = _dot_nt(bs_scr[...], c0_scr[...]).astype(BF16)
        for s in range(TC):
            for t in range(TC):
                k = (t - s) if d == 0 else (s - t)
                src = (TC - 1 - k) if d == 0 else k
                blk = lag[src * LANES:(src + 1) * LANES, :] if k >= 0 else jnp.zeros((LANES, LANES), BF16)
                m_scr[s * LANES:(s + 1) * LANES, t * LANES:(t + 1) * LANES] = blk

        a = a_scr[...]
        yd = _dot(a, m_scr[...])
        if d == 0:
            y_scr[...] = yd
        else:
            y_scr[...] += yd

        sm = _dot(a, bs_scr[...])
        for b in range(B):
            for sl in range(nsl):
                s_scr[sl, b * P:b * P + nC, :] = sm[b * nC:(b + 1) * nC, sl * LANES:(sl + 1) * LANES]

        a_r = [jnp.broadcast_to(a8_ref[d, 0, :, q * LANES:(q + 1) * LANES], (B, LANES)) for q in range(half)]
        a_i = [jnp.broadcast_to(a8_ref[d, 1, :, q * LANES:(q + 1) * LANES], (B, LANES)) for q in range(half)]
        if has_h0:
            init = tuple(h0_ref[d, :, sl * LANES:(sl + 1) * LANES] for sl in range(nsl))
        else:
            init = tuple(jnp.zeros((B, LANES), F32) for _ in range(nsl))

        def step(j, carry, d=d, a_r=a_r, a_i=a_i):
            c = j if d == 0 else nC - 1 - j
            rows = pl.ds(c, B, stride=P)
            new_r, new_i = [], []
            for q in range(half):
                xr, xi = carry[q], carry[half + q]
                sr = s_scr[q, rows, :]
                si = s_scr[half + q, rows, :]
                s_scr[q, rows, :] = xr
                s_scr[half + q, rows, :] = xi
                new_r.append(a_r[q] * xr - a_i[q] * xi + sr)
                new_i.append(a_r[q] * xi + a_i[q] * xr + si)
            return tuple(new_r + new_i)

        fin = lax.fori_loop(0, nC, step, init)
        for sl in range(nsl):
            fin_ref[d, :, sl * LANES:(sl + 1) * LANES] = fin[sl]

        for b in range(B):
            for sl in range(nsl):
                x_scr[b * nC:(b + 1) * nC, sl * LANES:(sl + 1) * LANES] = \
                    s_scr[sl, b * P:b * P + nC, :].astype(BF16)
        y_scr[...] += _dot_nt(x_scr[...], cp_scr[...])

    for t in range(TC):
        y_ref[pl.ds(t, R, stride=TC), :] = y_scr[:, t * LANES:(t + 1) * LANES]


def _s5(z, ops, h0, *, B, L, layer):
    c0, bs, cp, a8 = ops
    T = B * L
    nC = L // S5_CHUNK
    nlb = MIX_W // LANES
    sc = 2 * S5_LBLK_GROUPS * SSM_STATE
    kc = S5_CHUNK * LANES
    su0 = SU_SECTION * MIX_W // LANES

    blocks = pl.BlockSpec((None, 2, None, S5_CHUNK, S5_LBLK_GROUPS, 2, SSM_GROUP, LANES),
                          lambda lb: (layer, 0, lb, 0, 0, 0, 0, 0))
    in_specs = [pl.BlockSpec((T, LANES), lambda lb: (0, su0 + lb)),
                pl.BlockSpec((None, 2, None, S5_LBLK_GROUPS, 2, SSM_GROUP, LANES),
                             lambda lb: (layer, 0, lb, 0, 0, 0, 0)),
                blocks, blocks,
                pl.BlockSpec((None, 2, None, 2, 1, sc // 2), lambda lb: (layer, 0, lb, 0, 0, 0))]
    args = [z, c0, bs, cp, a8]
    if h0 is not None:
        in_specs.append(pl.BlockSpec((2, None, B, sc), lambda lb: (0, lb, 0, 0)))
        args.append(h0)
    body = functools.partial(_s5_body, B=B, nC=nC, has_h0=h0 is not None)
    return pl.pallas_call(
        body,
        grid=(nlb,),
        in_specs=in_specs,
        out_specs=[pl.BlockSpec((T, LANES), lambda lb: (0, lb)),
                   pl.BlockSpec((2, None, B, sc), lambda lb: (0, lb, 0, 0))],
        out_shape=[jax.ShapeDtypeStruct((T, MIX_W), F32),
                   jax.ShapeDtypeStruct((2, nlb, B, sc), F32)],
        scratch_shapes=[pltpu.VMEM((T, LANES), F32), pltpu.VMEM((B * nC, kc), BF16),
                        pltpu.VMEM((kc, kc), BF16), pltpu.VMEM((kc, sc), BF16), pltpu.VMEM((kc, sc), BF16),
                        pltpu.VMEM((LANES, sc), BF16),
                        pltpu.VMEM((sc // LANES, B * (nC + S5_PITCH_PAD), LANES), F32),
                        pltpu.VMEM((B * nC, sc), BF16), pltpu.VMEM((B * nC, kc), F32)],
        name="s5",
        compiler_params=_params("arbitrary"),
    )(*args)


def _s5_operators(a_re, a_im, log_dt, b_re, b_im, c_re, c_im):
    TC = S5_CHUNK
    nlb = MIX_W // LANES
    ng = S5_LBLK_GROUPS
    lr = jnp.minimum(a_re, -1e-4)
    li = a_im
    dt = jnp.exp(log_dt)[..., None]
    k = jnp.arange(TC + 1, dtype=F32)[:, None, None, None]
    mag = jnp.exp(k * (lr * dt)[None])
    pr = mag * jnp.cos(k * (li * dt)[None])
    pi = mag * jnp.sin(k * (li * dt)[None])
    ar, ai = pr[1], pi[1]
    den = lr * lr + li * li
    sr = ((ar - 1.0) * lr + ai * li) / den
    si = (ai * lr - (ar - 1.0) * li) / den
    bbr = sr[..., None] * b_re[None] - si[..., None] * b_im[None]
    bbi = sr[..., None] * b_im[None] + si[..., None] * b_re[None]

    def lanes2(x):
        return jnp.concatenate([x, x], -1)

    def powers(fwd, bwd):
        x = lanes2(jnp.stack([fwd, bwd], 0)).reshape(2, TC, nlb, ng, 1, LANES)
        return jnp.swapaxes(x, 1, 2)

    def per_group(x):
        return lanes2(x).reshape(2, nlb, 1, ng, SSM_GROUP, LANES)

    er = powers(jnp.flip(pr[:TC, 0], 0), pr[:TC, 1])
    ei = powers(jnp.flip(pi[:TC, 0], 0), pi[:TC, 1])
    btr = per_group(jnp.swapaxes(bbr, -1, -2))
    bti = per_group(jnp.swapaxes(bbi, -1, -2))
    bs = jnp.stack([er * btr - ei * bti, er * bti + ei * btr], 4).astype(BF16)

    fr = powers(pr[1:, 0], jnp.flip(pr[1:, 1], 0))
    fi = powers(pi[1:, 0], jnp.flip(pi[1:, 1], 0))
    ctr = per_group(c_re)
    cti = per_group(c_im)
    cp = jnp.stack([ctr * fr - cti * fi, -(ctr * fi + cti * fr)], 4).astype(BF16)
    c0 = jnp.stack([ctr, -cti], 4)[:, :, 0].astype(BF16)

    sw = ng * SSM_STATE
    a8 = jnp.stack([pr[TC].reshape(2, nlb, 1, sw), pi[TC].reshape(2, nlb, 1, sw)], 2)
    return c0, bs, cp, a8


def _head_masks(shape):
    lane = lax.broadcasted_iota(jnp.int32, shape, 1)
    return lane < NA_HEAD_DIM


def _cattn_body(q_ref, k_ref, v_ref, o_ref):
    first = _head_masks(q_ref.shape[1:])
    nq = q_ref.shape[1]
    for bb in range(q_ref.shape[0]):
        q = q_ref[bb]
        qs = jnp.concatenate([jnp.where(first, q, jnp.zeros_like(q)), jnp.where(first, jnp.zeros_like(q), q)], 0)
        s = _dot_nt(qs, k_ref[bb]) * (NA_HEAD_DIM ** -0.5)
        m = jnp.max(s, -1, keepdims=True)
        p = jnp.exp(s - m)
        l = jnp.sum(p, -1, keepdims=True)
        o = _dot(p.astype(BF16), v_ref[bb]) / l
        o_ref[bb] = jnp.where(first, o[:nq], o[nq:]).astype(BF16)


def _context_attention(z, *, B, L):
    nblk = MIX_W // LANES
    nb = CTX_ATTN_BATCH

    def sec(s):
        return pl.BlockSpec((nb, L, LANES), lambda b, hp: (b, 0, s * nblk + hp))

    return pl.pallas_call(
        _cattn_body,
        grid=(B // nb, nblk),
        in_specs=[sec(5), sec(6), sec(7)],
        out_specs=pl.BlockSpec((nb, L, LANES), lambda b, hp: (b, 0, hp)),
        out_shape=jax.ShapeDtypeStruct((B, L, MIX_W), BF16),
        name="ctx_attention",
        compiler_params=_params("arbitrary", "arbitrary"),
    )(z, z, z)


def _na_chunks(rows):
    half = NA_KR // 2
    plan, kinds = [], []
    for r0 in range(0, rows, NA_CHUNK_ROWS):
        rs = [min(max(r - half, 0), rows - NA_KR) for r in range(r0, r0 + NA_CHUNK_ROWS)]
        ws = min(rs[0], rows - NA_WIN_ROWS)
        assert rs[-1] + NA_KR <= ws + NA_WIN_ROWS
        kind = tuple((r0 + n - ws, rs[n] - ws) for n in range(NA_CHUNK_ROWS))
        if kind not in kinds:
            kinds.append(kind)
        plan.append((ws, kinds.index(kind)))
    return plan, kinds


def _na_body(q_ref, k_ref, v_ref, kc_ref, vc_ref, tb_ref, o_ref, bias_scr, *, rows):
    scale = NA_HEAD_DIM ** -0.5
    nq = NA_CHUNK_ROWS * GRID_W
    plan, kinds = _na_chunks(rows)
    n_off = 2 * NA_KR - 1

    @pl.when(pl.program_id(1) == 0)
    def _():
        for t, kind in enumerate(kinds):
            for e in range(2):
                for n, (r_rel, rs_rel) in enumerate(kind):
                    for kj in range(NA_WIN_ROWS):
                        off = kj - r_rel + NA_KR - 1 if rs_rel <= kj < rs_rel + NA_KR else n_off
                        lo = (kj % 2) * GRID_W
                        bias_scr[t, e * nq + n * GRID_W:e * nq + (n + 1) * GRID_W, kj * GRID_W:(kj + 1) * GRID_W] = \
                            tb_ref[e, off, :, lo:lo + GRID_W]

    kctx = kc_ref[...].astype(BF16)
    vctx = vc_ref[...].astype(BF16)
    first = _head_masks((nq, LANES))
    for c, (ws, kind) in enumerate(plan):
        qc = q_ref[c * nq:(c + 1) * nq, :]
        qs = jnp.concatenate([jnp.where(first, qc, jnp.zeros_like(qc)),
                              jnp.where(first, jnp.zeros_like(qc), qc)], 0)
        kw = k_ref[ws * GRID_W:(ws + NA_WIN_ROWS) * GRID_W, :]
        vw = v_ref[ws * GRID_W:(ws + NA_WIN_ROWS) * GRID_W, :]
        s_loc = _dot_nt(qs, kw) * scale + bias_scr[kind]
        s_ctx = _dot_nt(qs, kctx) * scale
        m = jnp.maximum(jnp.max(s_loc, -1, keepdims=True), jnp.max(s_ctx, -1, keepdims=True))
        p_loc = jnp.exp(s_loc - m)
        p_ctx = jnp.exp(s_ctx - m)
        l = jnp.sum(p_loc, -1, keepdims=True) + jnp.sum(p_ctx, -1, keepdims=True)
        o = (_dot(p_loc.astype(BF16), vw) + _dot(p_ctx.astype(BF16), vctx)) / l
        o_ref[c * nq:(c + 1) * nq, :] = jnp.where(first, o[:nq], o[nq:]).astype(BF16)


def _neighbourhood_attention(z, cache_k, cache_v, blocks, *, B, L, layer):
    nblk = MIX_W // LANES
    rows = L // GRID_W
    Lc = cache_k.shape[2]
    _, kinds = _na_chunks(rows)

    def sec(s):
        return pl.BlockSpec((None, L, LANES), lambda hp, b: (b, 0, s * nblk + hp))

    ctx = pl.BlockSpec((None, None, Lc, LANES), lambda hp, b: (b, layer, 0, hp))
    return pl.pallas_call(
        functools.partial(_na_body, rows=rows),
        grid=(nblk, B),
        in_specs=[sec(5), sec(6), sec(7), ctx, ctx,
                  pl.BlockSpec((None, None, 2, 2 * NA_KR, GRID_W, 2 * GRID_W),
                               lambda hp, b: (layer, hp, 0, 0, 0, 0))],
        out_specs=pl.BlockSpec((None, L, LANES), lambda hp, b: (b, 0, hp)),
        out_shape=jax.ShapeDtypeStruct((B, L, MIX_W), BF16),
        scratch_shapes=[pltpu.VMEM((len(kinds), 2 * NA_CHUNK_ROWS * GRID_W, NA_WIN_ROWS * GRID_W), F32)],
        name="nbr_attention",
        compiler_params=_params("arbitrary", "arbitrary"),
    )(z, z, z, cache_k, cache_v, blocks)


def _na_bias_blocks(rpb):
    nr, nc = 2 * NA_KR - 1, 2 * NA_KW - 1
    qc = np.arange(GRID_W)
    kc = np.arange(GRID_W)
    ws = np.clip(qc - NA_KW // 2, 0, GRID_W - NA_KW)
    col_ok = (kc[None, :] >= ws[:, None]) & (kc[None, :] < ws[:, None] + NA_KW)
    coff = np.clip(kc[None, :] - qc[:, None] + NA_KW - 1, 0, nc - 1)
    sel_c = ((coff[None] == np.arange(nc)[:, None, None]) & col_ok[None]).astype(np.float32)
    sel_c = np.concatenate([sel_c, sel_c], -1)
    ok = np.concatenate([col_ok, col_ok], -1)[None] & (np.arange(nr + 1) < nr)[:, None, None]
    H = rpb.shape[0]
    rows = jnp.pad(rpb.astype(F32), ((0, 0), (0, 1), (0, 0)))
    t = jnp.einsum('hrc,cqk->hrqk', rows, sel_c, precision=lax.Precision.HIGHEST)
    return jnp.where(ok[None], t, NEG_INF).reshape(H // 2, 2, nr + 1, GRID_W, 2 * GRID_W)


def _merge_body(x_ref, p_ref, r_ref, u_ref, y_ref, n_ref, ga_ref, gb_ref, gc_ref,
                d_ref, wglu_ref, wbr_ref, wo_ref, lg_ref, lb_ref, o_ref,
                wglu_s, wbr_s, wo_s, *, L, row0, rstride):
    i = pl.program_id(0)
    tm = x_ref.shape[0]

    @pl.when(i == 0)
    def _():
        wglu_s[...] = wglu_ref[...].astype(BF16)
        wbr_s[...] = wbr_ref[...].astype(BF16)
        wo_s[...] = wo_ref[...].astype(BF16)

    row = row0 + rstride * ((i * tm) // L)
    g1 = _mod_row(p_ref, row, 2)

    y = d_ref[...] * u_ref[...].astype(F32) + y_ref[...]
    y = jax.nn.gelu(y)
    s_out = y * jax.nn.sigmoid(_dot(y.astype(BF16), wglu_s[...]))

    def gate(ref):
        return jax.nn.sigmoid(ref[...].astype(F32))

    merged = (gate(ga_ref) * _dot(r_ref[...], wbr_s[0])
              + gate(gb_ref) * _dot(s_out.astype(BF16), wbr_s[1])
              + gate(gc_ref) * _dot(n_ref[...], wbr_s[2]))
    m = _dot(merged.astype(BF16), wo_s[...])
    o_ref[...] = _layer_norm(DEEPNORM_ALPHA * x_ref[...] + g1 * m, lg_ref[...], lb_ref[...])


def _merge(x, p, z, r_out, y, n_out, ssm_d, w_glu, w_branch, w_o, ln_g, ln_b, *, layer, L, row0, rstride):
    T = x.shape[0]
    tm = MERGE_TILE
    assert L % tm == 0 or rstride == 0
    gate0 = 8 * MIX_W // D_MODEL

    def tok(w):
        return pl.BlockSpec((tm, w), lambda i: (i, 0))

    def full(shape):
        return pl.BlockSpec((None,) + shape, lambda i: (layer,) + (0,) * len(shape))

    body = functools.partial(_merge_body, L=L, row0=row0, rstride=rstride)
    return pl.pallas_call(
        body,
        grid=(T // tm,),
        in_specs=[tok(D_MODEL), full((N_PAD_ROWS, 6 * D_MODEL)), tok(MIX_W),
                  pl.BlockSpec((tm, MIX_W), lambda i: (i, SU_SECTION)), tok(MIX_W), tok(MIX_W),
                  pl.BlockSpec((tm, D_MODEL), lambda i: (i, gate0)),
                  pl.BlockSpec((tm, D_MODEL), lambda i: (i, gate0 + 1)),
                  pl.BlockSpec((tm, D_MODEL), lambda i: (i, gate0 + 2)),
                  full((1, MIX_W)), full((MIX_W, MIX_W)), full((3, MIX_W, D_MODEL)),
                  full((D_MODEL, D_MODEL)), full((1, D_MODEL)), full((1, D_MODEL))],
        out_specs=tok(D_MODEL),
        out_shape=jax.ShapeDtypeStruct((T, D_MODEL), F32),
        scratch_shapes=[pltpu.VMEM((MIX_W, MIX_W), BF16), pltpu.VMEM((3, MIX_W, D_MODEL), BF16),
                        pltpu.VMEM((D_MODEL, D_MODEL), BF16)],
        name="merge",
        compiler_params=_params("arbitrary"),
    )(x, p, r_out, z, y, n_out, z, z, z, ssm_d.reshape(DEPTH, 1, MIX_W), w_glu, w_branch, w_o,
      ln_g.reshape(DEPTH, 1, D_MODEL), ln_b.reshape(DEPTH, 1, D_MODEL))


def _ffn_body(x_ref, p_ref, wa_ref, wb_ref, cwa_ref, cwb_ref, cba_ref, cbb_ref, wd_ref, lg_ref, lb_ref,
              o_ref, h_scr, acc_scr, mp_scr, mn_scr, *, L, row0, rstride):
    i = pl.program_id(0)
    j = pl.program_id(1)
    tm = x_ref.shape[0]
    nb = tm // L

    @pl.when(j == 0)
    def _():
        for s in range(nb):
            row = row0 + rstride * (i * nb + s)
            sh = _mod_row(p_ref, row, 3)
            sc = _mod_row(p_ref, row, 4)
            h_scr[s * L:(s + 1) * L, :] = (x_ref[s * L:(s + 1) * L, :] * (1.0 + sc) + sh).astype(BF16)
        acc_scr[...] = jnp.zeros_like(acc_scr)
        t = lax.broadcasted_iota(jnp.int32, (tm, FF_TILE), 0) % L
        mp_scr[...] = (t != 0).astype(BF16)
        mn_scr[...] = (t != L - 1).astype(BF16)

    def conv(w_ref, cw_ref, cb_ref):
        zc = _dot(h_scr[...], w_ref[...].astype(BF16))
        zp = pltpu.roll(zc, 1, 0).astype(BF16) * mp_scr[...]
        zn = pltpu.roll(zc, tm - 1, 0).astype(BF16) * mn_scr[...]
        cw = cw_ref[...].astype(BF16)
        return zp * cw[0:1, :] + zc.astype(BF16) * cw[1:2, :] + zn * cw[2:3, :] + cb_ref[...].astype(BF16)

    a = conv(wa_ref, cwa_ref, cba_ref)
    b = conv(wb_ref, cwb_ref, cbb_ref)
    acc_scr[...] += _dot(jax.nn.gelu(a) * b, wd_ref[...].astype(BF16))

    @pl.when(j == pl.num_programs(1) - 1)
    def _():
        for s in range(nb):
            row = row0 + rstride * (i * nb + s)
            g2 = _mod_row(p_ref, row, 5)
            sl = slice(s * L, (s + 1) * L)
            o_ref[sl, :] = _layer_norm(DEEPNORM_ALPHA * x_ref[sl, :] + g2 * acc_scr[sl, :],
                                       lg_ref[...], lb_ref[...])


def _conv_ffn(x, p, w_up, conv_w, conv_b, w_down, ln_g, ln_b, *, layer, L, row0, rstride):
    T = x.shape[0]
    tm = TOKEN_TILE
    nff = D_FF // FF_TILE
    body = functools.partial(_ffn_body, L=L, row0=row0, rstride=rstride)
    conv_b = conv_b.reshape(DEPTH, 1, 2 * D_FF)
    return pl.pallas_call(
        body,
        grid=(T // tm, nff),
        in_specs=[pl.BlockSpec((tm, D_MODEL), lambda i, j: (i, 0)),
                  pl.BlockSpec((None, N_PAD_ROWS, 6 * D_MODEL), lambda i, j: (layer, 0, 0)),
                  pl.BlockSpec((None, D_MODEL, FF_TILE), lambda i, j: (layer, 0, j)),
                  pl.BlockSpec((None, D_MODEL, FF_TILE), lambda i, j: (layer, 0, nff + j)),
                  pl.BlockSpec((None, 3, FF_TILE), lambda i, j: (layer, 0, j)),
                  pl.BlockSpec((None, 3, FF_TILE), lambda i, j: (layer, 0, nff + j)),
                  pl.BlockSpec((None, 1, FF_TILE), lambda i, j: (layer, 0, j)),
                  pl.BlockSpec((None, 1, FF_TILE), lambda i, j: (layer, 0, nff + j)),
                  pl.BlockSpec((None, FF_TILE, D_MODEL), lambda i, j: (layer, j, 0)),
                  pl.BlockSpec((None, 1, D_MODEL), lambda i, j: (layer, 0, 0)),
                  pl.BlockSpec((None, 1, D_MODEL), lambda i, j: (layer, 0, 0))],
        out_specs=pl.BlockSpec((tm, D_MODEL), lambda i, j: (i, 0)),
        out_shape=jax.ShapeDtypeStruct((T, D_MODEL), F32),
        scratch_shapes=[pltpu.VMEM((tm, D_MODEL), BF16), pltpu.VMEM((tm, D_MODEL), F32),
                        pltpu.VMEM((tm, FF_TILE), BF16), pltpu.VMEM((tm, FF_TILE), BF16)],
        name="conv_ffn",
        compiler_params=_params("arbitrary", "arbitrary"),
    )(x, p, w_up, w_up, conv_w, conv_w, conv_b, conv_b, w_down, ln_g.reshape(DEPTH, 1, D_MODEL),
      ln_b.reshape(DEPTH, 1, D_MODEL))


def _s5_states_in(state_ssm, layer):
    B = state_ssm.shape[0]
    nlb = MIX_W // LANES
    h = state_ssm[:, layer].reshape(B, 2, nlb, S5_LBLK_GROUPS, SSM_STATE, 2)
    return jnp.transpose(h, (1, 2, 0, 5, 3, 4)).reshape(2, nlb, B, 2 * S5_LBLK_GROUPS * SSM_STATE)


def _s5_states_out(fin):
    nlb, B = fin.shape[1], fin.shape[2]
    h = fin.reshape(2, nlb, B, 2, S5_LBLK_GROUPS, SSM_STATE)
    return jnp.transpose(h, (2, 0, 1, 4, 5, 3)).reshape(B, 2, SSM_GROUPS, SSM_STATE, 2)


def _layer(x, p, lw, *, B, L, row0, rstride, latent, layer, extra):
    T = B * L
    log_gamma = lw['log_gamma']
    if latent:
        z, = _inproj(x, p, lw['w_in'], layer=layer, L=L, row0=row0, rstride=rstride)
        z3 = z.reshape(B, L, IN_COLS)
        r_out = _retention(z3, log_gamma, B=B, L=L, rope_tabs=extra['rope'], s0=extra['state_ret'],
                           layer=layer)[0]
        n_out = _neighbourhood_attention(z3, extra['cache_k'], extra['cache_v'], extra['bias'],
                                         B=B, L=L, layer=layer)
        y, _ = _s5(z, lw['s5'], _s5_states_in(extra['state_ssm'], layer), B=B, L=L, layer=layer)
        states = None
    else:
        ret_buf, k_buf, v_buf = extra
        z, k_buf, v_buf = _inproj(x, p, lw['w_in'], layer=layer, L=L, row0=row0, rstride=rstride,
                                  kv_bufs=(k_buf, v_buf))
        z3 = z.reshape(B, L, IN_COLS)
        r_out, ret_buf = _retention(z3, log_gamma, B=B, L=L, layer=layer, state_buf=ret_buf)
        n_out = _context_attention(z3, B=B, L=L)
        y, fin = _s5(z, lw['s5'], None, B=B, L=L, layer=layer)
        states = ((ret_buf, k_buf, v_buf), _s5_states_out(fin))
    x = _merge(x, p, z, r_out.reshape(T, MIX_W), y, n_out.reshape(T, MIX_W),
               lw['ssm_d'], lw['ssm_w_glu'], lw['w_branch'], lw['w_o'], lw['ln1_g'], lw['ln1_b'],
               layer=layer, L=L, row0=row0, rstride=rstride)
    x = _conv_ffn(x, p, lw['w_up'], lw['conv_w'], lw['conv_b'], lw['w_down'], lw['ln2_g'], lw['ln2_b'],
                  layer=layer, L=L, row0=row0, rstride=rstride)
    return x, states


def kernel(x_prompt, x_sample, state_ret, state_ssm, cache_na_k, cache_na_v, c, c_ctx, w_ada, b_ada, w_in, ret_decay, ssm_a_re, ssm_a_im, ssm_log_dt, ssm_b_re, ssm_b_im, ssm_c_re, ssm_c_im, ssm_d, ssm_w_glu, na_rpb, w_branch, w_o, ln1_g, ln1_b, w_up, conv_w, conv_b, w_down, ln2_g, ln2_b):
    B, L, _ = x_prompt.shape
    Bd, Ld, _ = x_sample.shape
    Lc = cache_na_k.shape[2]

    cond = jnp.concatenate([c_ctx[None, :], c, jnp.zeros((N_PAD_ROWS - 1 - Bd, D_MODEL), F32)], 0)
    p_all = _ada(cond, w_ada, b_ada)

    extra = dict(rope=_rope_tables(Ld), state_ret=state_ret, state_ssm=state_ssm,
                 cache_k=cache_na_k.reshape(Bd, DEPTH, Lc, MIX_W),
                 cache_v=cache_na_v.reshape(Bd, DEPTH, Lc, MIX_W),
                 bias=jax.vmap(_na_bias_blocks)(na_rpb))

    xp = x_prompt.reshape(B * L, D_MODEL)
    xs = x_sample.reshape(Bd * Ld, D_MODEL)
    bufs = (jnp.zeros((B, DEPTH, 2, N_RET_HEADS, RET_DK, RET_DK), F32),
            jnp.zeros((B, DEPTH, L, MIX_W), F32), jnp.zeros((B, DEPTH, L, MIX_W), F32))
    ssm_states = []
    lw = dict(w_in=w_in, log_gamma=jax.nn.log_sigmoid(ret_decay.astype(F32)), ssm_d=ssm_d, ssm_w_glu=ssm_w_glu,
              w_branch=w_branch, w_o=w_o, ln1_g=ln1_g, ln1_b=ln1_b, w_up=w_up,
              conv_w=conv_w, conv_b=conv_b, w_down=w_down, ln2_g=ln2_g, ln2_b=ln2_b,
              s5=jax.vmap(_s5_operators)(ssm_a_re, ssm_a_im, ssm_log_dt, ssm_b_re, ssm_b_im, ssm_c_re, ssm_c_im))
    for l in range(DEPTH):
        xp, (bufs, s_ssm) = _layer(xp, p_all, lw, B=B, L=L, row0=0, rstride=0, latent=False, layer=l, extra=bufs)
        ssm_states.append(s_ssm)
        xs, _ = _layer(xs, p_all, lw, B=Bd, L=Ld, row0=1, rstride=1, latent=True, layer=l, extra=extra)
    ret_buf, k_buf, v_buf = bufs
    heads = (B, DEPTH, L, NA_HEADS, NA_HEAD_DIM)
    return (xp.reshape(B, L, D_MODEL), xs.reshape(Bd, Ld, D_MODEL),
            ret_buf, jnp.stack(ssm_states, 1), k_buf.reshape(heads), v_buf.reshape(heads))
```
